```python
import math
import jax, jax.numpy as jnp
from jax import lax
import numpy as np

D_MODEL = 1024
BATCH = 8
SEQ = 4096
DEPTH = 1

PLE_DIM = 256
EPS = 1e-6
SSD_HEADS = 16
SSD_HEAD_DIM = 64
SSD_INNER = SSD_HEADS * SSD_HEAD_DIM
SSD_GROUPS = 2
SSD_STATE = 64
SSD_CONV = 4
SSD_CHUNK = 128
SSD_CONV_CH = SSD_INNER + 2 * SSD_GROUPS * SSD_STATE
ATT_HEADS = 16
ATT_HEAD_DIM = 64
ATT_INNER = ATT_HEADS * ATT_HEAD_DIM
MOBA_BLOCK = 256
MOBA_TOPK = 3
MOBA_QCHUNK = 16
ROPE_THETA = 10000.0
IN_SPLITS = (SSD_INNER, SSD_CONV_CH, SSD_HEADS, ATT_INNER, ATT_INNER, ATT_INNER, D_MODEL, D_MODEL)
IN_WIDTH = 7440
N_GROUPS = 4
EXPERTS_PER_GROUP = 8
TOPK_IN_GROUP = 2
D_EXPERT = 256

kernel_name = "hybrid_ssd_moba_hiermoe_block"


def rms_norm(x, w):
    xf = x.astype(jnp.float32)
    inv = lax.rsqrt(jnp.mean(xf * xf, axis=-1, keepdims=True) + EPS)
    return (xf * inv).astype(x.dtype) * w


def split_columns(proj):
    offs = np.cumsum(IN_SPLITS)[:-1].tolist()
    return jnp.split(proj, offs, axis=-1)


def rope(x, positions):
    half = x.shape[-1] // 2
    inv_freq = ROPE_THETA ** (-jnp.arange(half, dtype=jnp.float32) / half)
    ang = positions.astype(jnp.float32)[..., None] * inv_freq
    cos = jnp.cos(ang)[:, :, None, :].astype(x.dtype)
    sin = jnp.sin(ang)[:, :, None, :].astype(x.dtype)
    x1, x2 = x[..., :half], x[..., half:]
    return jnp.concatenate([x1 * cos - x2 * sin, x2 * cos + x1 * sin], axis=-1)


def causal_dwconv(u, w, b):
    out = lax.conv_general_dilated(
        u, w[:, None, :].astype(u.dtype), window_strides=(1,),
        padding=((SSD_CONV - 1, 0),), dimension_numbers=("NWC", "WIO", "NWC"),
        feature_group_count=u.shape[-1])
    return out + b


def ssd_mixer(z, xbc, dt_raw, conv_w, conv_b, dt_bias, a_log, d_skip, norm_w):
    b, L, _ = z.shape
    nc = L // SSD_CHUNK
    E = SSD_HEADS // SSD_GROUPS
    xbc = jax.nn.silu(causal_dwconv(xbc, conv_w, conv_b))
    xs, Bm, Cm = jnp.split(xbc, [SSD_INNER, SSD_INNER + SSD_GROUPS * SSD_STATE], axis=-1)
    dt = jax.nn.softplus(dt_raw.astype(jnp.float32) + dt_bias.astype(jnp.float32))
    A = -jnp.exp(a_log.astype(jnp.float32)).reshape(SSD_GROUPS, E)
    X = xs.reshape(b, nc, SSD_CHUNK, SSD_GROUPS, E, SSD_HEAD_DIM)
    Bc = Bm.reshape(b, nc, SSD_CHUNK, SSD_GROUPS, SSD_STATE)
    Cc = Cm.reshape(b, nc, SSD_CHUNK, SSD_GROUPS, SSD_STATE)
    dtc = dt.reshape(b, nc, SSD_CHUNK, SSD_GROUPS, E)
    Xdt = X * dtc[..., None].astype(X.dtype)
    dA = jnp.transpose(dtc * A, (0, 1, 3, 4, 2))
    A_cum = jnp.cumsum(dA, axis=-1)
    idx = jnp.arange(SSD_CHUNK)
    causal = idx[:, None] >= idx[None, :]
    Lmat = jnp.exp(jnp.where(causal, A_cum[..., :, None] - A_cum[..., None, :], -jnp.inf))
    CB = jnp.einsum("bclgn,bcsgn->bcgls", Cc, Bc)
    M = (CB[:, :, :, None] * Lmat).astype(X.dtype)
    y_diag = jnp.einsum("bcgels,bcsgep->bclgep", M, Xdt)
    decay_states = jnp.exp(A_cum[..., -1:] - A_cum).astype(X.dtype)
    states = jnp.einsum("bclgn,bcgel,bclgep->bcgepn", Bc, decay_states, Xdt)
    chunk_decay = jnp.exp(A_cum[..., -1])

    def step(h, inp):
        s_c, d_c = inp
        return h * d_c[..., None, None] + s_c, h

    h0 = jnp.zeros((b, SSD_GROUPS, E, SSD_HEAD_DIM, SSD_STATE), jnp.float32)
    _, prev = lax.scan(step, h0, (jnp.moveaxis(states, 1, 0).astype(jnp.float32),
                                  jnp.moveaxis(chunk_decay, 1, 0)))
    prev = jnp.moveaxis(prev, 0, 1).astype(X.dtype)
    y_off = jnp.einsum("bclgn,bcgepn,bcgel->bclgep", Cc, prev, jnp.exp(A_cum).astype(X.dtype))
    y = y_diag + y_off + X * d_skip.reshape(SSD_GROUPS, E)[:, :, None].astype(X.dtype)
    y = y.reshape(b, L, SSD_INNER)
    yg = (y * jax.nn.silu(z)).reshape(b, L, SSD_GROUPS, SSD_INNER // SSD_GROUPS)
    return rms_norm(yg, norm_w.reshape(SSD_GROUPS, -1)).reshape(b, L, SSD_INNER)


def moba_attention(q, k, v):
    b, L, H, Dh = q.shape
    Lp = -(-L // MOBA_BLOCK) * MOBA_BLOCK
    nb = Lp // MOBA_BLOCK
    n_sel = min(MOBA_TOPK, nb - 1)
    pad = ((0, 0), (0, Lp - L), (0, 0), (0, 0))
    qh = jnp.pad(q, pad).transpose(0, 2, 1, 3)
    k_blocks = jnp.pad(k, pad).transpose(0, 2, 1, 3).reshape(b, H, nb, MOBA_BLOCK, Dh)
    v_blocks = jnp.pad(v, pad).transpose(0, 2, 1, 3).reshape(b, H, nb, MOBA_BLOCK, Dh)
    k_mean = jnp.mean(k_blocks.astype(jnp.float32), axis=3).astype(q.dtype)
    scale = Dh ** -0.5
    bi = jnp.arange(b)[:, None, None, None]
    hi = jnp.arange(H)[None, :, None, None]
    kpos_in_blk = jnp.arange(MOBA_BLOCK)

    def chunk(c):
        q0 = c * MOBA_QCHUNK
        qc = lax.dynamic_slice_in_dim(qh, q0, MOBA_QCHUNK, axis=2)
        qpos = q0 + jnp.arange(MOBA_QCHUNK)
        own = q0 // MOBA_BLOCK
        own_idx = jnp.broadcast_to(own, (b, H, MOBA_QCHUNK, 1)).astype(jnp.int32)
        if n_sel > 0:
            gate = jnp.einsum("bhqd,bhnd->bhqn", qc, k_mean).astype(jnp.float32)
            gate = jnp.where(jnp.arange(nb) < own, gate, -jnp.inf)
            _, sel = lax.top_k(gate, n_sel)
            sel = sel.astype(jnp.int32)
            blk_idx = jnp.concatenate([sel, own_idx], axis=-1)
            slot_ok = jnp.concatenate([sel < own, jnp.ones(own_idx.shape, bool)], axis=-1)
        else:
            blk_idx = own_idx
            slot_ok = jnp.ones(own_idx.shape, bool)
        kg = k_blocks[bi, hi, blk_idx]
        vg = v_blocks[bi, hi, blk_idx]
        s = jnp.einsum("bhqd,bhqjkd->bhqjk", qc, kg).astype(jnp.float32) * scale
        kpos = blk_idx[..., None] * MOBA_BLOCK + kpos_in_blk
        mask = slot_ok[..., None] & (kpos <= qpos[:, None, None])
        s = jnp.where(mask, s, -jnp.inf)
        pr = jax.nn.softmax(s.reshape(b, H, MOBA_QCHUNK, -1), axis=-1).reshape(s.shape)
        return jnp.einsum("bhqjk,bhqjkd->bhqd", pr.astype(vg.dtype), vg)

    out = lax.map(chunk, jnp.arange(Lp // MOBA_QCHUNK))
    out = out.transpose(1, 0, 3, 2, 4).reshape(b, Lp, H, Dh)
    return out[:, :L]


def hybrid_mixer(h, positions, w_in, conv_w, conv_b, dt_bias, a_log, d_skip, ssd_norm_w,
                 w_ssd_out, w_attn_out, w_out):
    b, L, _ = h.shape
    z, xbc, dt_raw, q, k, v, g_ssd, g_att = split_columns(h @ w_in)
    y_ssd = ssd_mixer(z, xbc, dt_raw, conv_w, conv_b, dt_bias, a_log, d_skip, ssd_norm_w) @ w_ssd_out
    q = rope(q.reshape(b, L, ATT_HEADS, ATT_HEAD_DIM), positions)
    k = rope(k.reshape(b, L, ATT_HEADS, ATT_HEAD_DIM), positions)
    v = v.reshape(b, L, ATT_HEADS, ATT_HEAD_DIM)
    y_att = moba_attention(q, k, v).reshape(b, L, ATT_INNER) @ w_attn_out
    merged = jax.nn.sigmoid(g_ssd) * y_ssd + jax.nn.sigmoid(g_att) * y_att
    return merged @ w_out


def hier_moe(h, w_rg, b_rg, w_re, b_re, w_gate, w_up, w_down):
    b, L, _ = h.shape
    gl = (h @ w_rg).astype(jnp.float32) + b_rg.astype(jnp.float32)
    gp = jax.nn.softmax(gl, axis=-1)
    _, gi = lax.top_k(gl, 1)
    g_oh = jax.nn.one_hot(gi[..., 0], N_GROUPS, dtype=jnp.float32)
    g_w = jnp.sum(gp * g_oh, axis=-1)
    el = ((h @ w_re).astype(jnp.float32) + b_re.astype(jnp.float32)).reshape(b, L, N_GROUPS, EXPERTS_PER_GROUP)
    el_sel = jnp.einsum("blge,blg->ble", el, g_oh)
    top_v, top_i = lax.top_k(el_sel, TOPK_IN_GROUP)
    top_w = jax.nn.softmax(top_v, axis=-1) * g_w[..., None]
    e_w = jnp.sum(jax.nn.one_hot(top_i, EXPERTS_PER_GROUP, dtype=jnp.float32) * top_w[..., None], axis=-2)
    comb = (g_oh[..., None] * e_w[..., None, :]).astype(h.dtype)
    out = jnp.zeros_like(h)
    for g in range(N_GROUPS):
        hid = jax.nn.silu(jnp.einsum("bld,edf->blef", h, w_gate[g])) * jnp.einsum("bld,edf->blef", h, w_up[g])
        out = out + jnp.einsum("blef,ble,efd->bld", hid, comb[:, :, g], w_down[g])
    return out


def setup_inputs(seed: int = 0) -> dict:
    key = jax.random.key(seed)
    ks = jax.random.split(key, 32)
    f32 = jnp.float32

    def nrm(k, shape, fan_in):
        return jax.random.normal(k, shape, f32) * (fan_in ** -0.5)

    def gain(k, shape):
        return 1.0 + 0.05 * jax.random.normal(k, shape, f32)

    dt0 = jnp.exp(jax.random.uniform(ks[5], (DEPTH, SSD_HEADS), f32, math.log(1e-3), math.log(1e-1)))
    dt_bias = dt0 + jnp.log(-jnp.expm1(-dt0))
    return {
        "x": jax.random.normal(ks[0], (BATCH, SEQ, D_MODEL), f32),
        "p": jax.random.normal(ks[1], (DEPTH, BATCH, SEQ, PLE_DIM), f32),
        "positions": (jnp.arange(SEQ, dtype=jnp.int32)[None, :]
                      + jax.random.randint(ks[2], (BATCH, 1), 0, 1024, dtype=jnp.int32)),
        "attn_norm_w": gain(ks[3], (DEPTH, D_MODEL)),
        "w_in": nrm(ks[4], (DEPTH, D_MODEL, IN_WIDTH), D_MODEL),
        "conv_w": nrm(ks[6], (DEPTH, SSD_CONV, SSD_CONV_CH), SSD_CONV),
        "conv_b": 0.02 * jax.random.normal(ks[7], (DEPTH, SSD_CONV_CH), f32),
        "dt_bias": dt_bias,
        "a_log": jnp.log(jax.random.uniform(ks[8], (DEPTH, SSD_HEADS), f32, 1.0, 16.0)),
        "d_skip": 1.0 + 0.1 * jax.random.normal(ks[9], (DEPTH, SSD_HEADS), f32),
        "ssd_norm_w": gain(ks[10], (DEPTH, SSD_INNER)),
        "w_ssd_out": nrm(ks[11], (DEPTH, SSD_INNER, D_MODEL), SSD_INNER),
        "w_attn_out": nrm(ks[12], (DEPTH, ATT_INNER, D_MODEL), ATT_INNER),
        "w_out": nrm(ks[13], (DEPTH, D_MODEL, D_MODEL), D_MODEL),
        "moe_norm_w": gain(ks[14], (DEPTH, D_MODEL)),
        "w_router_group": nrm(ks[15], (DEPTH, D_MODEL, N_GROUPS), D_MODEL),
        "b_router_group": 0.01 * jax.random.normal(ks[16], (DEPTH, N_GROUPS), f32),
        "w_router_expert": nrm(ks[17], (DEPTH, D_MODEL, N_GROUPS * EXPERTS_PER_GROUP), D_MODEL),
        "b_router_expert": 0.01 * jax.random.normal(ks[18], (DEPTH, N_GROUPS * EXPERTS_PER_GROUP), f32),
        "w_exp_gate": nrm(ks[19], (DEPTH, N_GROUPS, EXPERTS_PER_GROUP, D_MODEL, D_EXPERT), D_MODEL),
        "w_exp_up": nrm(ks[20], (DEPTH, N_GROUPS, EXPERTS_PER_GROUP, D_MODEL, D_EXPERT), D_MODEL),
        "w_exp_down": nrm(ks[21], (DEPTH, N_GROUPS, EXPERTS_PER_GROUP, D_EXPERT, D_MODEL), D_EXPERT),
        "ple_norm_w": gain(ks[22], (DEPTH, D_MODEL)),
        "w_ple": nrm(ks[23], (DEPTH, PLE_DIM, D_MODEL), PLE_DIM),
        "w_ple_gate": nrm(ks[24], (DEPTH, D_MODEL, D_MODEL), D_MODEL),
        "final_norm_w": gain(ks[25], (D_MODEL,)),
    }


def reference(x, p, positions, attn_norm_w, w_in, conv_w, conv_b, dt_bias, a_log, d_skip,
              ssd_norm_w, w_ssd_out, w_attn_out, w_out, moe_norm_w, w_router_group,
              b_router_group, w_router_expert, b_router_expert, w_exp_gate, w_exp_up,
              w_exp_down, ple_norm_w, w_ple, w_ple_gate, final_norm_w):
    for i in range(DEPTH):
        h = rms_norm(x, attn_norm_w[i])
        x = x + hybrid_mixer(h, positions, w_in[i], conv_w[i], conv_b[i], dt_bias[i], a_log[i],
                             d_skip[i], ssd_norm_w[i], w_ssd_out[i], w_attn_out[i], w_out[i])
        h = rms_norm(x, moe_norm_w[i])
        x = x + hier_moe(h, w_router_group[i], b_router_group[i], w_router_expert[i],
                         b_router_expert[i], w_exp_gate[i], w_exp_up[i], w_exp_down[i])
        gate = jax.nn.sigmoid(rms_norm(x, ple_norm_w[i]) @ w_ple_gate[i])
        x = x + gate * (p[i] @ w_ple[i])
    return rms_norm(x, final_norm_w)
```

```python
import functools

import jax
import jax.numpy as jnp
import numpy as np
from jax import lax
from jax.experimental import pallas as pl
from jax.experimental.pallas import tpu as pltpu

F32 = jnp.float32
BF16 = jnp.bfloat16

EPS = 1e-6
D_MODEL = 1024
SSD_HEADS = 16
SSD_HEAD_DIM = 64
SSD_INNER = SSD_HEADS * SSD_HEAD_DIM
SSD_GROUPS = 2
SSD_STATE = 64
SSD_CONV = 4
SSD_CHUNK = 128
SSD_CONV_CH = SSD_INNER + 2 * SSD_GROUPS * SSD_STATE
ATT_HEADS = 16
ATT_HEAD_DIM = 64
ATT_INNER = ATT_HEADS * ATT_HEAD_DIM
MOBA_BLOCK = 256
MOBA_TOPK = 3
ROPE_THETA = 10000.0
IN_SPLITS = (SSD_INNER, SSD_CONV_CH, SSD_HEADS, ATT_INNER, ATT_INNER, ATT_INNER, D_MODEL, D_MODEL)
N_GROUPS = 4
EXPERTS_PER_GROUP = 8
N_EXPERTS = N_GROUPS * EXPERTS_PER_GROUP
D_EXPERT = 256
PLE_DIM = 256

LANES = 128
SUBLANES = 8
NEG_BIG = -1e30
VMEM_LIMIT = 56 * 1024 * 1024

TM_PROJ = 512
TM_MERGE = 512
TM_MOE = 256


def _dot(a, b):
    return jnp.dot(a, b, preferred_element_type=F32)


def _dot_nt(a, b):
    return lax.dot_general(a, b, (((1,), (1,)), ((), ())), preferred_element_type=F32)


def _dot_tn(a, b):
    return lax.dot_general(a, b, (((0,), (0,)), ((), ())), preferred_element_type=F32)


def _split3(a):
    hi = a.astype(BF16)
    r1 = a - hi.astype(F32)
    mid = r1.astype(BF16)
    lo = (r1 - mid.astype(F32)).astype(BF16)
    return hi, mid, lo


def _dot_exact_rhs(sel_bf16, a):
    hi, mid, lo = _split3(a)
    return _dot(sel_bf16, hi) + _dot(sel_bf16, mid) + _dot(sel_bf16, lo)


def _dot_exact_lhs(a, sel_bf16):
    hi, mid, lo = _split3(a)
    return _dot(hi, sel_bf16) + _dot(mid, sel_bf16) + _dot(lo, sel_bf16)


def _sigmoid(x):
    return 1.0 / (1.0 + jnp.exp(-x))


def _silu(x):
    return x * _sigmoid(x)


def _softplus(x):
    return jnp.maximum(x, 0.0) + jnp.log1p(jnp.exp(-jnp.abs(x)))


def _rms(x, w):
    inv = lax.rsqrt(jnp.mean(x * x, axis=-1, keepdims=True) + EPS)
    return (x * inv) * w


def _rope_tile(t, cos, sin_signed):
    half = ATT_HEAD_DIM // 2
    lane = lax.broadcasted_iota(jnp.int32, t.shape, 1)
    first = (lane % ATT_HEAD_DIM) < half
    partner = jnp.where(first, pltpu.roll(t, LANES - half, 1), pltpu.roll(t, half, 1))
    return t * cos + partner * sin_signed


def _in_proj_kernel(x_ref, nw_ref, cos_ref, sin_ref, w_ref, wdt_ref,
                    z_ref, xbc_ref, q_ref, k_ref, v_ref, gs_ref, ga_ref, dt_ref):
    h = _rms(x_ref[...], nw_ref[...]).astype(BF16)
    cos = cos_ref[...]
    sin = sin_ref[...]

    def proj(lo, width):
        return _dot(h, w_ref[:, lo:lo + width])

    o = 0
    z_ref[...] = proj(o, SSD_INNER).astype(BF16)
    o += SSD_INNER
    xbc_ref[...] = proj(o, SSD_CONV_CH).astype(BF16)
    o += SSD_CONV_CH
    scale = ATT_HEAD_DIM ** -0.5
    for ref, s in ((q_ref, scale), (k_ref, 1.0)):
        for g in range(ATT_INNER // LANES):
            t = proj(o + g * LANES, LANES)
            ref[:, g * LANES:(g + 1) * LANES] = (_rope_tile(t, cos, sin) * s).astype(BF16)
        o += ATT_INNER
    for ref in (v_ref, gs_ref, ga_ref):
        ref[...] = proj(o, D_MODEL).astype(BF16)
        o += D_MODEL
    dt_ref[...] = _dot(h, wdt_ref[...])[:, :SSD_HEADS]


def _in_proj(x2d, nw, cosf, sinf, w_big, w_dt):
    t_tokens = x2d.shape[0]
    tm = min(TM_PROJ, t_tokens)
    grid = (t_tokens // tm,)
    row = lambda width: pl.BlockSpec((tm, width), lambda i: (i, 0))
    whole = pl.BlockSpec(memory_space=pltpu.VMEM)
    out_shapes = [
        jax.ShapeDtypeStruct((t_tokens, SSD_INNER), BF16),
        jax.ShapeDtypeStruct((t_tokens, SSD_CONV_CH), BF16),
        jax.ShapeDtypeStruct((t_tokens, ATT_INNER), BF16),
        jax.ShapeDtypeStruct((t_tokens, ATT_INNER), BF16),
        jax.ShapeDtypeStruct((t_tokens, ATT_INNER), BF16),
        jax.ShapeDtypeStruct((t_tokens, D_MODEL), BF16),
        jax.ShapeDtypeStruct((t_tokens, D_MODEL), BF16),
        jax.ShapeDtypeStruct((t_tokens, SSD_HEADS), F32),
    ]
    out_specs = [row(SSD_INNER), row(SSD_CONV_CH), row(ATT_INNER), row(ATT_INNER), row(ATT_INNER),
                 row(D_MODEL), row(D_MODEL), row(SSD_HEADS)]
    return pl.pallas_call(
        _in_proj_kernel,
        grid=grid,
        in_specs=[row(D_MODEL), whole, row(LANES), row(LANES), whole, whole],
        out_specs=out_specs,
        out_shape=out_shapes,
        compiler_params=pltpu.CompilerParams(dimension_semantics=("arbitrary",), vmem_limit_bytes=VMEM_LIMIT),
        name="in_proj",
    )(x2d, nw, cosf, sinf, w_big, w_dt)


def _ssd_kernel(z_ref, xbc_ref, dt_ref, dtt_ref, cw_ref, cb_ref, dtb_ref, dtbc_ref, alog_ref, alogc_ref,
                dexp_ref, nw_ref, out_ref, state_ref, ubuf_ref):
    c = pl.program_id(1)
    cl = SSD_CHUNK
    heads_per_group = SSD_HEADS // SSD_GROUPS
    gw = heads_per_group * SSD_HEAD_DIM

    @pl.when(c == 0)
    def _():
        state_ref[...] = jnp.zeros_like(state_ref)
        ubuf_ref[0:SUBLANES, :] = jnp.zeros((SUBLANES, SSD_CONV_CH), F32)

    u = xbc_ref[...].astype(F32)
    ubuf_ref[SUBLANES:SUBLANES + cl, :] = u
    acc = cb_ref[...] + cw_ref[SSD_CONV - 1:SSD_CONV, :] * u
    for k in range(SSD_CONV - 1):
        shift = SSD_CONV - 1 - k
        acc = acc + cw_ref[k:k + 1, :] * ubuf_ref[pl.ds(SUBLANES - shift, cl), :]
    ubuf_ref[0:SUBLANES, :] = u[cl - SUBLANES:cl, :]
    xc = _silu(acc)
    xs = xc[:, :SSD_INNER]
    bm = xc[:, SSD_INNER:SSD_INNER + SSD_GROUPS * SSD_STATE].astype(BF16)
    cm = xc[:, SSD_INNER + SSD_GROUPS * SSD_STATE:].astype(BF16)

    dt = _softplus(dt_ref[...] + dtb_ref[...])
    da = dt * (-jnp.exp(alog_ref[...]))
    dtt = _softplus(dtt_ref[0] + dtbc_ref[...])
    dat = dtt * (-jnp.exp(alogc_ref[...]))
    ri = lax.broadcasted_iota(jnp.int32, (cl, cl), 0)
    ci = lax.broadcasted_iota(jnp.int32, (cl, cl), 1)
    causal = ri >= ci
    tril = causal.astype(BF16)
    triu = (ri <= ci).astype(BF16)
    acum = _dot_exact_rhs(tril, da)
    acumt = _dot_exact_lhs(dat, triu)

    hrow = lax.broadcasted_iota(jnp.int32, (SSD_HEADS, SSD_INNER), 0)
    hcol = lax.broadcasted_iota(jnp.int32, (SSD_HEADS, SSD_INNER), 1) // SSD_HEAD_DIM
    expand = (hrow == hcol).astype(BF16)

    a_last = acum[cl - 1:cl, :]
    dt_e = _dot_exact_lhs(dt, expand)
    dec_e = _dot_exact_lhs(jnp.exp(a_last - acum), expand)
    ea_e = _dot_exact_lhs(jnp.exp(acum), expand)
    xdt = xs * dt_e
    xdt_b = xdt.astype(BF16)
    xdtdec_b = (xdt * dec_e).astype(BF16)
    chunk_decay_t = jnp.exp(acumt[:, cl - 1:cl])

    y_groups = []
    for g in range(SSD_GROUPS):
        b_g = bm[:, g * SSD_STATE:(g + 1) * SSD_STATE]
        c_g = cm[:, g * SSD_STATE:(g + 1) * SSD_STATE]
        prev = state_ref[g * gw:(g + 1) * gw, :]
        cb = _dot_nt(c_g, b_g)
        y_off = _dot_nt(c_g, prev.astype(BF16)) * ea_e[:, g * gw:(g + 1) * gw]
        new_states = _dot_tn(xdtdec_b[:, g * gw:(g + 1) * gw], b_g)
        y_heads = []
        decay_rows = []
        for e in range(heads_per_group):
            hd = g * heads_per_group + e
            lmat = jnp.where(causal, jnp.exp(acum[:, hd:hd + 1] - acumt[hd:hd + 1, :]), 0.0)
            m = (cb * lmat).astype(BF16)
            y_heads.append(_dot(m, xdt_b[:, hd * SSD_HEAD_DIM:(hd + 1) * SSD_HEAD_DIM]))
            decay_rows.append(jnp.broadcast_to(chunk_decay_t[hd:hd + 1, :], (SSD_HEAD_DIM, SSD_STATE)))
        y_groups.append(jnp.concatenate(y_heads, axis=1) + y_off)
        state_ref[g * gw:(g + 1) * gw, :] = prev * jnp.concatenate(decay_rows, axis=0) + new_states

    y = jnp.concatenate(y_groups, axis=1) + xs * dexp_ref[...]
    yg = y * _silu(z_ref[...].astype(F32))
    nw = nw_ref[...]
    outs = [_rms(yg[:, g * gw:(g + 1) * gw], nw[:, g * gw:(g + 1) * gw]) for g in range(SSD_GROUPS)]
    out_ref[...] = jnp.concatenate(outs, axis=1).astype(BF16)


def _ssd(z, xbc, dt, dtt, conv_w, conv_b, dt_bias, a_log, d_skip, norm_w, batch):
    t_tokens = z.shape[0]
    nc = t_tokens // batch // SSD_CHUNK
    tok = lambda width: pl.BlockSpec((SSD_CHUNK, width), lambda b, c: (b * nc + c, 0))
    whole = pl.BlockSpec(memory_space=pltpu.VMEM)
    dexp = jnp.repeat(d_skip, SSD_HEAD_DIM)[None, :]
    return pl.pallas_call(
        _ssd_kernel,
        grid=(batch, nc),
        in_specs=[tok(SSD_INNER), tok(SSD_CONV_CH), tok(SSD_HEADS),
                  pl.BlockSpec((1, SSD_HEADS, SSD_CHUNK), lambda b, c: (b * nc + c, 0, 0)),
                  whole, whole, whole, whole, whole, whole, whole, whole],
        out_specs=tok(SSD_INNER),
        out_shape=jax.ShapeDtypeStruct((t_tokens, SSD_INNER), BF16),
        scratch_shapes=[pltpu.VMEM((SSD_INNER, SSD_STATE), F32),
                        pltpu.VMEM((SUBLANES + SSD_CHUNK, SSD_CONV_CH), F32)],
        compiler_params=pltpu.CompilerParams(dimension_semantics=("arbitrary", "arbitrary"),
                                             vmem_limit_bytes=VMEM_LIMIT),
        name="ssd",
    )(z, xbc, dt, dtt, conv_w, conv_b[None, :], dt_bias[None, :], dt_bias[:, None],
      a_log[None, :], a_log[:, None], dexp, norm_w[None, :])


def _moba_kernel(q_ref, k_ref, v_ref, o_ref, *, seq):
    nb = seq // MOBA_BLOCK
    blk = MOBA_BLOCK
    lane = lax.broadcasted_iota(jnp.int32, (blk, LANES), 1)
    head_of_lane = lane // ATT_HEAD_DIM
    row_i = lax.broadcasted_iota(jnp.int32, (blk, blk), 0)
    col_i = lax.broadcasted_iota(jnp.int32, (blk, blk), 1)

    pr = lax.broadcasted_iota(jnp.int32, (LANES, seq), 0)
    pc = lax.broadcasted_iota(jnp.int32, (LANES, seq), 1) // blk
    pool = jnp.where(pr == pc, 1.0 / blk, 0.0).astype(BF16)
    kmean = _dot(pool, k_ref[0])
    km_hi = kmean.astype(BF16)
    km_lo = (kmean - km_hi.astype(F32)).astype(BF16)

    def q_block(i, carry):
        q_i = q_ref[0, pl.ds(i * blk, blk), :]
        k_i = k_ref[0, pl.ds(i * blk, blk), :]
        v_i = v_ref[0, pl.ds(i * blk, blk), :]
        out = jnp.zeros((blk, LANES), F32)
        for a in range(2):
            mine = head_of_lane == a
            off = ATT_HEAD_DIM * (1 - a)
            qa = jnp.where(mine, q_i, jnp.zeros_like(q_i))
            gate = _dot_nt(qa, km_hi) + _dot_nt(qa, km_lo)
            gate = pltpu.roll(gate, off, 1) if off else gate
            bidx = lane - off
            valid = (bidx >= 0) & (bidx < i)
            inrange = (bidx >= 0) & (bidx < nb)
            g = jnp.where(valid, gate, -jnp.inf)
            sel = jnp.zeros((blk, LANES), jnp.bool_)
            for _ in range(min(MOBA_TOPK, nb - 1)):
                cand = inrange & jnp.logical_not(sel)
                gm = jnp.max(jnp.where(cand, g, -jnp.inf), axis=1, keepdims=True)
                hit = cand & (g == gm)
                first = jnp.min(jnp.where(hit, bidx, nb), axis=1, keepdims=True)
                sel = sel | (bidx == first)
            chosen = sel & valid
            flags = jnp.where(chosen, 0.0, NEG_BIG).astype(BF16)
            q_aug = jnp.where(mine, q_i, flags)

            s = _dot_nt(qa, k_i)
            s = jnp.where(col_i <= row_i, s, NEG_BIG)
            m0 = jnp.max(s, axis=1, keepdims=True)
            p = jnp.exp(s - m0)
            l0 = jnp.sum(p, axis=1, keepdims=True)
            acc0 = _dot(p.astype(BF16), v_i)

            def kv_block(j, st):
                m_prev, l_prev, acc_prev = st
                k_j = k_ref[0, pl.ds(j * blk, blk), :]
                v_j = v_ref[0, pl.ds(j * blk, blk), :]
                onehot = jnp.where(lane == off + j, 1.0, 0.0).astype(BF16)
                k_aug = jnp.where(mine, k_j, onehot)
                sj = _dot_nt(q_aug, k_aug)
                m_new = jnp.maximum(m_prev, jnp.max(sj, axis=1, keepdims=True))
                alpha = jnp.exp(m_prev - m_new)
                pj = jnp.exp(sj - m_new)
                l_new = alpha * l_prev + jnp.sum(pj, axis=1, keepdims=True)
                acc_new = alpha * acc_prev + _dot(pj.astype(BF16), v_j)
                return m_new, l_new, acc_new

            m_f, l_f, acc_f = lax.fori_loop(0, i, kv_block, (m0, l0, acc0))
            out = jnp.where(mine, acc_f / l_f, out)
        o_ref[0, pl.ds(i * blk, blk), :] = out.astype(BF16)
        return carry

    lax.fori_loop(0, nb, q_block, 0)


def _moba(q, k, v):
    batch, seq, _ = q.shape
    spec = pl.BlockSpec((1, seq, LANES), lambda b, hp: (b, 0, hp))
    return pl.pallas_call(
        functools.partial(_moba_kernel, seq=seq),
        grid=(batch, ATT_INNER // LANES),
        in_specs=[spec, spec, spec],
        out_specs=spec,
        out_shape=jax.ShapeDtypeStruct((batch, seq, ATT_INNER), BF16),
        compiler_params=pltpu.CompilerParams(dimension_semantics=("arbitrary", "arbitrary"),
                                             vmem_limit_bytes=VMEM_LIMIT),
        name="moba",
    )(q, k, v)


ROUTER_ROWS = 40


def _merge_kernel(x_ref, yn_ref, att_ref, gs_ref, ga_ref, wso_ref, wao_ref, wo_ref, nw_ref,
                  wr_hi_ref, wr_lo_ref, br_ref, x1_ref, h2_ref, route_ref):
    y_s = _dot(yn_ref[...], wso_ref[...])
    y_a = _dot(att_ref[...], wao_ref[...])
    merged = _sigmoid(gs_ref[...].astype(F32)) * y_s + _sigmoid(ga_ref[...].astype(F32)) * y_a
    x1 = x_ref[...] + _dot(merged.astype(BF16), wo_ref[...])
    x1_ref[...] = x1
    h2 = _rms(x1, nw_ref[...])
    h2_ref[...] = h2

    h_hi = h2.astype(BF16)
    h_lo = (h2 - h_hi.astype(F32)).astype(BF16)
    wr_hi = wr_hi_ref[...]
    logits = _dot_nt(wr_hi, h_hi) + _dot_nt(wr_hi, h_lo) + _dot_nt(wr_lo_ref[...], h_hi) + br_ref[...]
    gl = logits[0:N_GROUPS, :]
    gidx = lax.broadcasted_iota(jnp.int32, gl.shape, 0)
    gmax = jnp.max(gl, axis=0, keepdims=True)
    gi = jnp.min(jnp.where(gl == gmax, gidx, N_GROUPS), axis=0, keepdims=True)
    g_w = 1.0 / jnp.sum(jnp.exp(gl - gmax), axis=0, keepdims=True)
    el = jnp.zeros((EXPERTS_PER_GROUP, gl.shape[1]), F32)
    for g in range(N_GROUPS):
        lo = N_GROUPS + g * EXPERTS_PER_GROUP
        el = el + jnp.where(gi == g, logits[lo:lo + EXPERTS_PER_GROUP, :], 0.0)
    eidx = lax.broadcasted_iota(jnp.int32, el.shape, 0)
    v0 = jnp.max(el, axis=0, keepdims=True)
    i0 = jnp.min(jnp.where(el == v0, eidx, EXPERTS_PER_GROUP), axis=0, keepdims=True)
    rest = jnp.where(eidx == i0, -jnp.inf, el)
    v1 = jnp.max(rest, axis=0, keepdims=True)
    i1 = jnp.min(jnp.where((rest == v1) & (eidx != i0), eidx, EXPERTS_PER_GROUP), axis=0, keepdims=True)
    t = jnp.exp(v1 - v0)
    w0 = g_w / (1.0 + t)
    w1 = g_w * t / (1.0 + t)
    base = gi * EXPERTS_PER_GROUP
    route_ref[...] = jnp.concatenate(
        [(base + i0).astype(F32), (base + i1).astype(F32), w0, w1], axis=0)


def _merge(x2d, yn, att, gs, ga, wso, wao, wo, nw, wr_hi, wr_lo, br):
    t_tokens = x2d.shape[0]
    tm = min(TM_MERGE, t_tokens)
    row = lambda: pl.BlockSpec((tm, D_MODEL), lambda i: (i, 0))
    whole = pl.BlockSpec(memory_space=pltpu.VMEM)
    return pl.pallas_call(
        _merge_kernel,
        grid=(t_tokens // tm,),
        in_specs=[row(), row(), row(), row(), row(), whole, whole, whole, whole, whole, whole, whole],
        out_specs=[row(), row(), pl.BlockSpec((4, tm), lambda i: (0, i))],
        out_shape=[jax.ShapeDtypeStruct((t_tokens, D_MODEL), F32),
                   jax.ShapeDtypeStruct((t_tokens, D_MODEL), F32),
                   jax.ShapeDtypeStruct((4, t_tokens), F32)],
        compiler_params=pltpu.CompilerParams(dimension_semantics=("arbitrary",), vmem_limit_bytes=VMEM_LIMIT),
        name="merge",
    )(x2d, yn, att, gs, ga, wso, wao, wo, nw, wr_hi, wr_lo, br)


ROW_TILE = D_MODEL // LANES


def _gather_rows(src_hbm, idx_ref, idx_base, buf, sem, n_rows):
    def body(r, carry):
        tok = idx_ref[idx_base + r]
        pltpu.make_async_copy(src_hbm.at[pl.ds(tok * ROW_TILE, ROW_TILE), :],
                              buf.at[pl.ds(r * ROW_TILE, ROW_TILE), :], sem).start()
        return carry
    lax.fori_loop(0, n_rows, body, 0)


def _wait_rows(src_hbm, buf, sem, n_rows):
    pltpu.make_async_copy(src_hbm.at[pl.ds(0, n_rows * ROW_TILE), :], buf, sem).wait()


def _rows_as_matrix(buf, n_rows):
    return jnp.concatenate([buf[pl.ds(s, n_rows, stride=ROW_TILE), :] for s in range(ROW_TILE)], axis=1)


def _moe_kernel(texp_ref, ntile_ref, rtok_ref, h_hbm, rw_ref, wgu_ref, wd_ref, y_ref, buf0, buf1, sems):
    t = pl.program_id(0)
    nt = ntile_ref[0]
    bufs = (buf0, buf1)

    def start(tile, slot):
        _gather_rows(h_hbm, rtok_ref, tile * TM_MOE, bufs[slot], sems.at[slot], TM_MOE)

    @pl.when(t == 0)
    def _():
        start(0, 0)

    for slot in range(2):
        @pl.when((t % 2 == slot) & (t < nt))
        def _():
            @pl.when(t + 1 < nt)
            def _():
                start(t + 1, 1 - slot)
            _wait_rows(h_hbm, bufs[slot], sems.at[slot], TM_MOE)
            xt = _rows_as_matrix(bufs[slot], TM_MOE).astype(BF16)
            gu = _dot(xt, wgu_ref[0])
            hid = (_silu(gu[:, :D_EXPERT]) * gu[:, D_EXPERT:]).astype(BF16)
            y_ref[...] = _dot(hid, wd_ref[0]) * rw_ref[...]

    @pl.when(t >= nt)
    def _():
        y_ref[...] = jnp.zeros_like(y_ref)


def _moe(h2_tiles, tile_expert, n_tiles, row_token, row_w, wgu, wd, max_tiles):
    rows = max_tiles * TM_MOE
    grid_spec = pltpu.PrefetchScalarGridSpec(
        num_scalar_prefetch=3,
        grid=(max_tiles,),
        in_specs=[pl.BlockSpec(memory_space=pl.ANY),
                  pl.BlockSpec((TM_MOE, 1), lambda t, te, nt, rt: (t, 0)),
                  pl.BlockSpec((1, D_MODEL, 2 * D_EXPERT), lambda t, te, nt, rt: (te[t], 0, 0)),
                  pl.BlockSpec((1, D_EXPERT, D_MODEL), lambda t, te, nt, rt: (te[t], 0, 0))],
        out_specs=pl.BlockSpec((TM_MOE, D_MODEL), lambda t, te, nt, rt: (t, 0)),
        scratch_shapes=[pltpu.VMEM((TM_MOE * ROW_TILE, LANES), F32),
                        pltpu.VMEM((TM_MOE * ROW_TILE, LANES), F32),
                        pltpu.SemaphoreType.DMA((2,))],
    )
    return pl.pallas_call(
        _moe_kernel,
        grid_spec=grid_spec,
        out_shape=jax.ShapeDtypeStruct((rows, D_MODEL), F32),
        compiler_params=pltpu.CompilerParams(dimension_semantics=("arbitrary",), vmem_limit_bytes=VMEM_LIMIT),
        name="moe",
    )(tile_expert, n_tiles, row_token, h2_tiles, row_w, wgu, wd)


def _moe_plan(route, max_tiles):
    t_tokens = route.shape[1]
    rows = max_tiles * TM_MOE
    e_flat = route[0:2].astype(jnp.int32).reshape(-1)
    w_flat = route[2:4].reshape(-1)
    tok = jnp.tile(jnp.arange(t_tokens, dtype=jnp.int32), 2)
    onehot = (e_flat[:, None] == jnp.arange(N_EXPERTS, dtype=jnp.int32)[None, :]).astype(jnp.int32)
    csum = jnp.cumsum(onehot, axis=0)
    rank = jnp.sum(csum * onehot, axis=1) - 1
    counts = csum[-1]
    padded = ((counts + TM_MOE - 1) // TM_MOE) * TM_MOE
    ends = jnp.cumsum(padded)
    starts = ends - padded
    pos = jnp.sum(onehot * starts[None, :], axis=1) + rank
    row_token = jnp.zeros((rows,), jnp.int32).at[pos].set(tok)
    row_w = jnp.zeros((rows,), F32).at[pos].set(w_flat)
    tile_start = jnp.arange(max_tiles, dtype=jnp.int32) * TM_MOE
    tile_expert = jnp.sum((tile_start[:, None] >= ends[None, :]).astype(jnp.int32), axis=1)
    tile_expert = jnp.minimum(tile_expert, N_EXPERTS - 1)
    n_tiles = (ends[-1] // TM_MOE).astype(jnp.int32).reshape(1)
    return tile_expert, n_tiles, row_token, row_w[:, None], pos


def _tail_kernel(pos_ref, x1_ref, p_ref, y_hbm, pnw_ref, wpg_ref, wple_ref, fnw_ref, o_ref,
                 buf0, buf1, buf2, buf3, sems, *, tm, t_tokens):
    i = pl.program_id(0)
    n = pl.num_programs(0)
    bufs = ((buf0, buf1), (buf2, buf3))

    def start(tile, slot):
        for s in range(2):
            _gather_rows(y_hbm, pos_ref, s * t_tokens + tile * tm, bufs[slot][s], sems.at[slot, s], tm)

    @pl.when(i == 0)
    def _():
        start(0, 0)

    for slot in range(2):
        @pl.when(i % 2 == slot)
        def _():
            @pl.when(i + 1 < n)
            def _():
                start(i + 1, 1 - slot)
            moe = jnp.zeros((tm, D_MODEL), F32)
            for s in range(2):
                _wait_rows(y_hbm, bufs[slot][s], sems.at[slot, s], tm)
                moe = moe + _rows_as_matrix(bufs[slot][s], tm)
            x2 = x1_ref[...] + moe
            gate = _sigmoid(_dot(_rms(x2, pnw_ref[...]).astype(BF16), wpg_ref[...]))
            x3 = x2 + gate * _dot(p_ref[...].astype(BF16), wple_ref[...])
            o_ref[...] = _rms(x3, fnw_ref[...])


def _tail(pos, x1, p2d, y_tiles, pnw, wpg, wple, fnw):
    t_tokens = x1.shape[0]
    tm = min(TM_MERGE, t_tokens)
    grid_spec = pltpu.PrefetchScalarGridSpec(
        num_scalar_prefetch=1,
        grid=(t_tokens // tm,),
        in_specs=[pl.BlockSpec((tm, D_MODEL), lambda i, ps: (i, 0)),
                  pl.BlockSpec((tm, PLE_DIM), lambda i, ps: (i, 0)),
                  pl.BlockSpec(memory_space=pl.ANY),
                  pl.BlockSpec(memory_space=pltpu.VMEM),
                  pl.BlockSpec(memory_space=pltpu.VMEM),
                  pl.BlockSpec(memory_space=pltpu.VMEM),
                  pl.BlockSpec(memory_space=pltpu.VMEM)],
        out_specs=pl.BlockSpec((tm, D_MODEL), lambda i, ps: (i, 0)),
        scratch_shapes=[pltpu.VMEM((tm * ROW_TILE, LANES), F32) for _ in range(4)]
        + [pltpu.SemaphoreType.DMA((2, 2))],
    )
    return pl.pallas_call(
        functools.partial(_tail_kernel, tm=tm, t_tokens=t_tokens),
        grid_spec=grid_spec,
        out_shape=jax.ShapeDtypeStruct((t_tokens, D_MODEL), F32),
        compiler_params=pltpu.CompilerParams(dimension_semantics=("arbitrary",), vmem_limit_bytes=VMEM_LIMIT),
        name="tail",
    )(pos, x1, p2d, y_tiles, pnw, wpg, wple, fnw)


def _rope_tables(positions):
    half = ATT_HEAD_DIM // 2
    inv_freq = ROPE_THETA ** (-jnp.arange(half, dtype=F32) / half)
    ang = positions.astype(F32)[..., None] * inv_freq
    cos = jnp.cos(ang).reshape(-1, half)
    sin = jnp.sin(ang).reshape(-1, half)
    reps = LANES // ATT_HEAD_DIM
    cosf = jnp.tile(jnp.concatenate([cos, cos], axis=1), (1, reps))
    sinf = jnp.tile(jnp.concatenate([-sin, sin], axis=1), (1, reps))
    return cosf, sinf


def _layer(x2d, p2d, cosf, sinf, batch, attn_norm_w, w_in, conv_w, conv_b, dt_bias, a_log, d_skip, ssd_norm_w,
           w_ssd_out, w_attn_out, w_out, moe_norm_w, w_rg, b_rg, w_re, b_re, w_gate, w_up, w_down,
           ple_norm_w, w_ple, w_ple_gate, final_norm_w):
    t_tokens = x2d.shape[0]
    seq = t_tokens // batch
    offs = np.cumsum((0,) + IN_SPLITS)
    cols = lambda i: w_in[:, offs[i]:offs[i + 1]]
    w_big = jnp.concatenate([cols(0), cols(1), cols(3), cols(4), cols(5), cols(6), cols(7)], axis=1).astype(BF16)
    w_dt = jnp.pad(cols(2), ((0, 0), (0, LANES - SSD_HEADS))).astype(BF16)

    z, xbc, q, k, v, gs, ga, dt = _in_proj(x2d, attn_norm_w[None, :], cosf, sinf, w_big, w_dt)

    nchunks = t_tokens // SSD_CHUNK
    dtt = dt.reshape(nchunks, SSD_CHUNK, SSD_HEADS).transpose(0, 2, 1)
    yn = _ssd(z, xbc, dt, dtt, conv_w, conv_b, dt_bias, a_log, d_skip, ssd_norm_w, batch)

    shape3 = (batch, seq, ATT_INNER)
    att = _moba(q.reshape(shape3), k.reshape(shape3), v.reshape(shape3)).reshape(t_tokens, ATT_INNER)

    wr = jnp.concatenate([w_rg, w_re], axis=1).T
    wr = jnp.pad(wr, ((0, ROUTER_ROWS - wr.shape[0]), (0, 0)))
    wr_hi = wr.astype(BF16)
    wr_lo = (wr - wr_hi.astype(F32)).astype(BF16)
    br = jnp.pad(jnp.concatenate([b_rg, b_re]), (0, ROUTER_ROWS - N_GROUPS - N_EXPERTS))[:, None]
    x1, h2, route = _merge(x2d, yn, att, gs, ga, w_ssd_out.astype(BF16), w_attn_out.astype(BF16),
                           w_out.astype(BF16), moe_norm_w[None, :], wr_hi, wr_lo, br)

    max_tiles = (2 * t_tokens) // TM_MOE + N_EXPERTS
    tile_expert, n_tiles, row_token, row_w, pos = _moe_plan(route, max_tiles)
    wgu = jnp.concatenate([w_gate, w_up], axis=-1).reshape(N_EXPERTS, D_MODEL, 2 * D_EXPERT).astype(BF16)
    wd = w_down.reshape(N_EXPERTS, D_EXPERT, D_MODEL).astype(BF16)
    y_sorted = _moe(h2.reshape(t_tokens * ROW_TILE, LANES), tile_expert, n_tiles, row_token, row_w,
                    wgu, wd, max_tiles)

    return _tail(pos, x1, p2d, y_sorted.reshape(-1, LANES), ple_norm_w[None, :],
                 w_ple_gate.astype(BF16), w_ple.astype(BF16), final_norm_w[None, :])


def kernel(x, p, positions, attn_norm_w, w_in, conv_w, conv_b, dt_bias, a_log, d_skip, ssd_norm_w, w_ssd_out,
           w_attn_out, w_out, moe_norm_w, w_router_group, b_router_group, w_router_expert, b_router_expert,
           w_exp_gate, w_exp_up, w_exp_down, ple_norm_w, w_ple, w_ple_gate, final_norm_w):
    batch, seq, d = x.shape
    depth = p.shape[0]
    assert depth == 1, "the final norm is fused into the last layer's tail kernel"
    cosf, sinf = _rope_tables(positions)
    i = 0
    out = _layer(x.reshape(batch * seq, d), p[i].reshape(batch * seq, PLE_DIM), cosf, sinf, batch,
                 attn_norm_w[i], w_in[i], conv_w[i], conv_b[i], dt_bias[i], a_log[i], d_skip[i], ssd_norm_w[i],
                 w_ssd_out[i], w_attn_out[i], w_out[i], moe_norm_w[i], w_router_group[i], b_router_group[i],
                 w_router_expert[i], b_router_expert[i], w_exp_gate[i], w_exp_up[i], w_exp_down[i],
                 ple_norm_w[i], w_ple[i], w_ple_gate[i], final_norm_w)
    return out.reshape(batch, seq, d)
```

```python
import functools

import jax
import jax.numpy as jnp
import numpy as np
from jax import lax
from jax.experimental import pallas as pl
from jax.experimental.pallas import tpu as pltpu

F32 = jnp.float32
BF16 = jnp.bfloat16

EPS = 1e-6
D_MODEL = 1024
SSD_HEADS = 16
SSD_HEAD_DIM = 64
SSD_INNER = SSD_HEADS * SSD_HEAD_DIM
SSD_GROUPS = 2
SSD_STATE = 64
SSD_CONV = 4
SSD_CHUNK = 128
SSD_CONV_CH = SSD_INNER + 2 * SSD_GROUPS * SSD_STATE
ATT_HEADS = 16
ATT_HEAD_DIM = 64
ATT_INNER = ATT_HEADS * ATT_HEAD_DIM
MOBA_BLOCK = 256
MOBA_TOPK = 3
ROPE_THETA = 10000.0
IN_SPLITS = (SSD_INNER, SSD_CONV_CH, SSD_HEADS, ATT_INNER, ATT_INNER, ATT_INNER, D_MODEL, D_MODEL)
N_GROUPS = 4
EXPERTS_PER_GROUP = 8
N_EXPERTS = N_GROUPS * EXPERTS_PER_GROUP
D_EXPERT = 256
PLE_DIM = 256

LANES = 128
SUBLANES = 8
NEG_BIG = -1e30
VMEM_LIMIT = 56 * 1024 * 1024

TM_PROJ = 512
TM_MERGE = 512
TM_MOE = 256


def _dot(a, b):
    return jnp.dot(a, b, preferred_element_type=F32)


def _dot_nt(a, b):
    return lax.dot_general(a, b, (((1,), (1,)), ((), ())), preferred_element_type=F32)


def _dot_tn(a, b):
    return lax.dot_general(a, b, (((0,), (0,)), ((), ())), preferred_element_type=F32)


def _split3(a):
    hi = a.astype(BF16)
    r1 = a - hi.astype(F32)
    mid = r1.astype(BF16)
    lo = (r1 - mid.astype(F32)).astype(BF16)
    return hi, mid, lo


def _dot_exact_rhs(sel_bf16, a):
    hi, mid, lo = _split3(a)
    return _dot(sel_bf16, hi) + _dot(sel_bf16, mid) + _dot(sel_bf16, lo)


def _dot_exact_lhs(a, sel_bf16):
    hi, mid, lo = _split3(a)
    return _dot(hi, sel_bf16) + _dot(mid, sel_bf16) + _dot(lo, sel_bf16)


def _sigmoid(x):
    return 1.0 / (1.0 + jnp.exp(-x))


def _silu(x):
    return x * _sigmoid(x)


def _softplus(x):
    return jnp.maximum(x, 0.0) + jnp.log1p(jnp.exp(-jnp.abs(x)))


def _rms(x, w):
    inv = lax.rsqrt(jnp.mean(x * x, axis=-1, keepdims=True) + EPS)
    return (x * inv) * w


ROW_TILE = D_MODEL // LANES


def _store_rows_as_tiles(ref, mat):
    rows = mat.shape[0]
    for s in range(ROW_TILE):
        ref[pl.ds(s, rows, stride=ROW_TILE), :] = mat[:, s * LANES:(s + 1) * LANES]


def _rope_tile(t, cos, sin_signed):
    half = ATT_HEAD_DIM // 2
    lane = lax.broadcasted_iota(jnp.int32, t.shape, 1)
    first = (lane % ATT_HEAD_DIM) < half
    partner = jnp.where(first, pltpu.roll(t, LANES - half, 1), pltpu.roll(t, half, 1))
    return t * cos + partner * sin_signed


LOG2E = 1.4426950408889634
Q_SCALE = ATT_HEAD_DIM ** -0.5 * LOG2E


def _in_proj_kernel(x_ref, nw_ref, cos_ref, sin_ref, cost_ref, sint_ref, w_ref, wt_ref, wdt_ref,
                    z_ref, xbc_ref, k_ref, gs_ref, ga_ref, dt_ref, qt_ref, vt_ref, dtt_ref):
    h = _rms(x_ref[...], nw_ref[...]).astype(BF16)
    cos = cos_ref[...]
    sin = sin_ref[...]

    def proj(lo, width):
        return _dot(h, w_ref[:, lo:lo + width])

    o = 0
    z_ref[...] = proj(o, SSD_INNER).astype(BF16)
    o += SSD_INNER
    xbc_ref[...] = proj(o, SSD_CONV_CH).astype(BF16)
    o += SSD_CONV_CH
    for g in range(ATT_INNER // LANES):
        t = proj(o + g * LANES, LANES)
        k_ref[:, g * LANES:(g + 1) * LANES] = _rope_tile(t, cos, sin).astype(BF16)
    o += ATT_INNER
    for ref in (gs_ref, ga_ref):
        ref[...] = proj(o, D_MODEL).astype(BF16)
        o += D_MODEL
    dt_ref[...] = _dot(h, wdt_ref[...])[:, :SSD_HEADS]

    dtt_ref[...] = _dot_nt(wt_ref[2 * ATT_INNER:2 * ATT_INNER + SSD_HEADS, :], h)
    vt_ref[...] = _dot_nt(wt_ref[ATT_INNER:2 * ATT_INNER, :], h).astype(BF16)
    qt = _dot_nt(wt_ref[0:ATT_INNER, :], h)
    cost = cost_ref[...]
    sint = sint_ref[...]
    half = ATT_HEAD_DIM // 2
    for hd in range(ATT_HEADS):
        r0 = hd * ATT_HEAD_DIM
        x1 = qt[r0:r0 + half, :]
        x2 = qt[r0 + half:r0 + ATT_HEAD_DIM, :]
        qt_ref[r0:r0 + half, :] = ((x1 * cost - x2 * sint) * Q_SCALE).astype(BF16)
        qt_ref[r0 + half:r0 + ATT_HEAD_DIM, :] = ((x2 * cost + x1 * sint) * Q_SCALE).astype(BF16)


def _in_proj(x2d, nw, cosf, sinf, cost, sint, w_big, w_t, w_dt):
    t_tokens = x2d.shape[0]
    tm = min(TM_PROJ, t_tokens)
    grid = (t_tokens // tm,)
    row = lambda width: pl.BlockSpec((tm, width), lambda i: (i, 0))
    col = lambda height: pl.BlockSpec((height, tm), lambda i: (0, i))
    whole = pl.BlockSpec(memory_space=pltpu.VMEM)
    out_shapes = [
        jax.ShapeDtypeStruct((t_tokens, SSD_INNER), BF16),
        jax.ShapeDtypeStruct((t_tokens, SSD_CONV_CH), BF16),
        jax.ShapeDtypeStruct((t_tokens, ATT_INNER), BF16),
        jax.ShapeDtypeStruct((t_tokens, D_MODEL), BF16),
        jax.ShapeDtypeStruct((t_tokens, D_MODEL), BF16),
        jax.ShapeDtypeStruct((t_tokens, SSD_HEADS), F32),
        jax.ShapeDtypeStruct((ATT_INNER, t_tokens), BF16),
        jax.ShapeDtypeStruct((ATT_INNER, t_tokens), BF16),
        jax.ShapeDtypeStruct((SSD_HEADS, t_tokens), F32),
    ]
    out_specs = [row(SSD_INNER), row(SSD_CONV_CH), row(ATT_INNER), row(D_MODEL), row(D_MODEL), row(SSD_HEADS),
                 col(ATT_INNER), col(ATT_INNER), col(SSD_HEADS)]
    return pl.pallas_call(
        _in_proj_kernel,
        grid=grid,
        in_specs=[row(D_MODEL), whole, row(LANES), row(LANES), col(ATT_HEAD_DIM // 2), col(ATT_HEAD_DIM // 2),
                  whole, whole, whole],
        out_specs=out_specs,
        out_shape=out_shapes,
        compiler_params=pltpu.CompilerParams(dimension_semantics=("arbitrary",), vmem_limit_bytes=VMEM_LIMIT),
        name="in_proj",
    )(x2d, nw, cosf, sinf, cost, sint, w_big, w_t, w_dt)


def _ssd_kernel(z_ref, xbc_ref, dt_ref, dtt_ref, cw_ref, cb_ref, dtb_ref, dtbc_ref, alog_ref, alogc_ref,
                dexp_ref, nw_ref, out_ref, state_ref, ubuf_ref):
    c = pl.program_id(1)
    cl = SSD_CHUNK
    heads_per_group = SSD_HEADS // SSD_GROUPS
    gw = heads_per_group * SSD_HEAD_DIM

    @pl.when(c == 0)
    def _():
        state_ref[...] = jnp.zeros_like(state_ref)
        ubuf_ref[0:SUBLANES, :] = jnp.zeros((SUBLANES, SSD_CONV_CH), F32)

    u = xbc_ref[...].astype(F32)
    ubuf_ref[SUBLANES:SUBLANES + cl, :] = u
    acc = cb_ref[...] + cw_ref[SSD_CONV - 1:SSD_CONV, :] * u
    for k in range(SSD_CONV - 1):
        shift = SSD_CONV - 1 - k
        acc = acc + cw_ref[k:k + 1, :] * ubuf_ref[pl.ds(SUBLANES - shift, cl), :]
    ubuf_ref[0:SUBLANES, :] = u[cl - SUBLANES:cl, :]
    xc = _silu(acc)
    xs = xc[:, :SSD_INNER]
    bm = xc[:, SSD_INNER:SSD_INNER + SSD_GROUPS * SSD_STATE].astype(BF16)
    cm = xc[:, SSD_INNER + SSD_GROUPS * SSD_STATE:].astype(BF16)

    dt = _softplus(dt_ref[...] + dtb_ref[...])
    da = dt * (-jnp.exp(alog_ref[...]))
    dtt = _softplus(dtt_ref[...] + dtbc_ref[...])
    dat = dtt * (-jnp.exp(alogc_ref[...]))
    ri = lax.broadcasted_iota(jnp.int32, (cl, cl), 0)
    ci = lax.broadcasted_iota(jnp.int32, (cl, cl), 1)
    causal = ri >= ci
    tril = causal.astype(BF16)
    triu = (ri <= ci).astype(BF16)
    acum = _dot_exact_rhs(tril, da)
    acumt = _dot_exact_lhs(dat, triu)

    hrow = lax.broadcasted_iota(jnp.int32, (SSD_HEADS, SSD_INNER), 0)
    hcol = lax.broadcasted_iota(jnp.int32, (SSD_HEADS, SSD_INNER), 1) // SSD_HEAD_DIM
    expand = (hrow == hcol).astype(BF16)

    a_last = acum[cl - 1:cl, :]
    dt_e = _dot_exact_lhs(dt, expand)
    dec_e = _dot_exact_lhs(jnp.exp(a_last - acum), expand)
    ea_e = _dot_exact_lhs(jnp.exp(acum), expand)
    xdt = xs * dt_e
    xdt_b = xdt.astype(BF16)
    xdtdec_b = (xdt * dec_e).astype(BF16)
    chunk_decay_t = jnp.exp(acumt[:, cl - 1:cl])

    y_groups = []
    for g in range(SSD_GROUPS):
        b_g = bm[:, g * SSD_STATE:(g + 1) * SSD_STATE]
        c_g = cm[:, g * SSD_STATE:(g + 1) * SSD_STATE]
        prev = state_ref[g * gw:(g + 1) * gw, :]
        cb = _dot_nt(c_g, b_g)
        y_off = _dot_nt(c_g, prev.astype(BF16)) * ea_e[:, g * gw:(g + 1) * gw]
        new_states = _dot_tn(xdtdec_b[:, g * gw:(g + 1) * gw], b_g)
        y_heads = []
        decay_rows = []
        for e in range(heads_per_group):
            hd = g * heads_per_group + e
            lmat = jnp.where(causal, jnp.exp(acum[:, hd:hd + 1] - acumt[hd:hd + 1, :]), 0.0)
            m = (cb * lmat).astype(BF16)
            y_heads.append(_dot(m, xdt_b[:, hd * SSD_HEAD_DIM:(hd + 1) * SSD_HEAD_DIM]))
            decay_rows.append(jnp.broadcast_to(chunk_decay_t[hd:hd + 1, :], (SSD_HEAD_DIM, SSD_STATE)))
        y_groups.append(jnp.concatenate(y_heads, axis=1) + y_off)
        state_ref[g * gw:(g + 1) * gw, :] = prev * jnp.concatenate(decay_rows, axis=0) + new_states

    y = jnp.concatenate(y_groups, axis=1) + xs * dexp_ref[...]
    yg = y * _silu(z_ref[...].astype(F32))
    nw = nw_ref[...]
    outs = [_rms(yg[:, g * gw:(g + 1) * gw], nw[:, g * gw:(g + 1) * gw]) for g in range(SSD_GROUPS)]
    out_ref[...] = jnp.concatenate(outs, axis=1).astype(BF16)


def _ssd(z, xbc, dt, dtt, conv_w, conv_b, dt_bias, a_log, d_skip, norm_w, batch):
    t_tokens = z.shape[0]
    nc = t_tokens // batch // SSD_CHUNK
    tok = lambda width: pl.BlockSpec((SSD_CHUNK, width), lambda b, c: (b * nc + c, 0))
    whole = pl.BlockSpec(memory_space=pltpu.VMEM)
    dexp = jnp.repeat(d_skip, SSD_HEAD_DIM)[None, :]
    return pl.pallas_call(
        _ssd_kernel,
        grid=(batch, nc),
        in_specs=[tok(SSD_INNER), tok(SSD_CONV_CH), tok(SSD_HEADS),
                  pl.BlockSpec((SSD_HEADS, SSD_CHUNK), lambda b, c: (0, b * nc + c)),
                  whole, whole, whole, whole, whole, whole, whole, whole],
        out_specs=tok(SSD_INNER),
        out_shape=jax.ShapeDtypeStruct((t_tokens, SSD_INNER), BF16),
        scratch_shapes=[pltpu.VMEM((SSD_INNER, SSD_STATE), F32),
                        pltpu.VMEM((SUBLANES + SSD_CHUNK, SSD_CONV_CH), F32)],
        compiler_params=pltpu.CompilerParams(dimension_semantics=("arbitrary", "arbitrary"),
                                             vmem_limit_bytes=VMEM_LIMIT),
        name="ssd",
    )(z, xbc, dt, dtt, conv_w, conv_b[None, :], dt_bias[None, :], dt_bias[:, None],
      a_log[None, :], a_log[:, None], dexp, norm_w[None, :])


MOBA_SUPER = 2 * MOBA_BLOCK


MOBA_BLOCK_ROWS = 16
ONES_ROWS = 16


def _moba_kernel(qt_ref, k_ref, vt_ref, o_ref, kfull_ref, qaug_ref, sa_ref, sb_ref, m_ref, l_ref, acc_ref, *, seq):
    nb = seq // MOBA_BLOCK
    sb = MOBA_SUPER
    nsb = seq // sb
    hd = ATT_HEAD_DIM
    nbr = MOBA_BLOCK_ROWS

    for c in range(nsb):
        rows = slice(c * sb, (c + 1) * sb)
        kfull_ref[rows, 0:LANES] = k_ref[rows, :]
        rblk = lax.broadcasted_iota(jnp.int32, (sb, LANES), 0) // MOBA_BLOCK + c * (sb // MOBA_BLOCK)
        rlane = lax.broadcasted_iota(jnp.int32, (sb, LANES), 1)
        kfull_ref[rows, LANES:2 * LANES] = jnp.where(rblk == rlane, 1.0, 0.0).astype(BF16)

    pr = lax.broadcasted_iota(jnp.int32, (nbr, seq), 0)
    pc = lax.broadcasted_iota(jnp.int32, (nbr, seq), 1) // MOBA_BLOCK
    pool = jnp.where(pr == pc, 1.0 / MOBA_BLOCK, 0.0).astype(BF16)
    kmean = _dot(pool, k_ref[...])
    km_hi = kmean.astype(BF16)
    km_lo = (kmean - km_hi.astype(F32)).astype(BF16)

    head_of_row = lax.broadcasted_iota(jnp.int32, (LANES, sb), 0) // hd
    blk_row = lax.broadcasted_iota(jnp.int32, (nbr, sb), 0)
    q_local = lax.broadcasted_iota(jnp.int32, (nbr, sb), 1)
    inrange = blk_row < nb
    krow = lax.broadcasted_iota(jnp.int32, (sb, 2 * sb), 0)
    qcol = lax.broadcasted_iota(jnp.int32, (sb, 2 * sb), 1) % sb
    causal = krow <= qcol
    ones_rows = jnp.ones((ONES_ROWS, sb), BF16)
    flag_pad = jnp.zeros((LANES - nbr, sb), BF16)

    def scores_into(s_ref, kv):
        start = pl.multiple_of(kv * sb, sb)
        s_ref[...] = _dot(kfull_ref[pl.ds(start, sb), :], qaug_ref[...])

    def absorb(kv, s_ref, diagonal):
        start = pl.multiple_of(kv * sb, sb)
        s_t = s_ref[...]
        if diagonal:
            s_t = jnp.where(causal, s_t, NEG_BIG)
        m_prev = m_ref[...]
        m_new = jnp.maximum(m_prev, jnp.max(s_t, axis=0, keepdims=True))
        m_ref[...] = m_new
        p_b = jnp.exp2(s_t - m_new).astype(BF16)
        alpha = jnp.exp2(m_prev - m_new)
        for a in range(2):
            cols = slice(a * sb, (a + 1) * sb)
            rows = slice(a * hd, (a + 1) * hd)
            v_aug = jnp.concatenate([vt_ref[rows, pl.ds(start, sb)], ones_rows], axis=0)
            r = _dot(v_aug, p_b[:, cols])
            l_ref[a:a + 1, :] = alpha[:, cols] * l_ref[a:a + 1, :] + r[hd:hd + 1, :]
            acc_ref[rows, :] = alpha[:, cols] * acc_ref[rows, :] + r[0:hd, :]

    def q_super(i, carry):
        qstart = pl.multiple_of(i * sb, sb)
        qt_sb = qt_ref[:, pl.ds(qstart, sb)]
        own = 2 * i + jnp.where(q_local >= MOBA_BLOCK, 1, 0)
        valid = blk_row < own
        for a in range(2):
            qa = jnp.where(head_of_row == a, qt_sb, jnp.zeros_like(qt_sb))
            gate = _dot(km_hi, qa) + _dot(km_lo, qa)
            g = jnp.where(valid, gate, -jnp.inf)
            sel = jnp.zeros((nbr, sb), jnp.bool_)
            for _ in range(min(MOBA_TOPK, nb - 1)):
                cand = inrange & jnp.logical_not(sel)
                gm = jnp.max(jnp.where(cand, g, -jnp.inf), axis=0, keepdims=True)
                hit = cand & (g == gm)
                first = jnp.min(jnp.where(hit, blk_row, nb), axis=0, keepdims=True)
                sel = sel | (blk_row == first)
            chosen = (sel & valid) | (blk_row == own)
            flags = jnp.where(chosen, 0.0, NEG_BIG).astype(BF16)
            qaug_ref[:, a * sb:(a + 1) * sb] = jnp.concatenate([qa, flags, flag_pad], axis=0)

        m_ref[...] = jnp.full(m_ref.shape, NEG_BIG, F32)
        l_ref[...] = jnp.zeros_like(l_ref)
        acc_ref[...] = jnp.zeros_like(acc_ref)
        scores_into(sa_ref, 0)

        def two_steps(t, carry):
            scores_into(sb_ref, 2 * t + 1)
            absorb(2 * t, sa_ref, False)
            scores_into(sa_ref, 2 * t + 2)
            absorb(2 * t + 1, sb_ref, False)
            return carry

        lax.fori_loop(0, i // 2, two_steps, 0)

        @pl.when(i % 2 == 1)
        def _():
            scores_into(sb_ref, i)
            absorb(i - 1, sa_ref, False)
            absorb(i, sb_ref, True)

        @pl.when(i % 2 == 0)
        def _():
            absorb(i, sa_ref, True)

        out_t = jnp.concatenate([acc_ref[a * hd:(a + 1) * hd, :] / l_ref[a:a + 1, :] for a in range(2)], axis=0)
        o_ref[pl.ds(qstart, sb), :] = out_t.T.astype(BF16)
        return carry

    lax.fori_loop(0, nsb, q_super, 0)


def _moba(qt, k, vt, batch):
    t_tokens = k.shape[0]
    seq = t_tokens // batch
    assert seq % MOBA_SUPER == 0 and seq // MOBA_BLOCK <= MOBA_BLOCK_ROWS
    spec = pl.BlockSpec((seq, LANES), lambda b, hp: (b, hp))
    spec_t = pl.BlockSpec((LANES, seq), lambda b, hp: (hp, b))
    return pl.pallas_call(
        functools.partial(_moba_kernel, seq=seq),
        grid=(batch, ATT_INNER // LANES),
        in_specs=[spec_t, spec, spec_t],
        out_specs=spec,
        out_shape=jax.ShapeDtypeStruct((t_tokens, ATT_INNER), BF16),
        scratch_shapes=[pltpu.VMEM((seq, 2 * LANES), BF16),
                        pltpu.VMEM((2 * LANES, 2 * MOBA_SUPER), BF16),
                        pltpu.VMEM((MOBA_SUPER, 2 * MOBA_SUPER), F32),
                        pltpu.VMEM((MOBA_SUPER, 2 * MOBA_SUPER), F32),
                        pltpu.VMEM((1, 2 * MOBA_SUPER), F32),
                        pltpu.VMEM((SUBLANES, MOBA_SUPER), F32),
                        pltpu.VMEM((LANES, MOBA_SUPER), F32)],
        compiler_params=pltpu.CompilerParams(dimension_semantics=("arbitrary", "arbitrary"),
                                             vmem_limit_bytes=VMEM_LIMIT),
        name="moba",
    )(qt, k, vt)


ROUTER_ROWS = 40


def _merge_kernel(x_ref, yn_ref, att_ref, gs_ref, ga_ref, wso_ref, wao_ref, wo_ref, nw_ref,
                  wr_hi_ref, wr_lo_ref, br_ref, x1_ref, h2_ref, route_ref):
    y_s = _dot(yn_ref[...], wso_ref[...])
    y_a = _dot(att_ref[...], wao_ref[...])
    merged = _sigmoid(gs_ref[...].astype(F32)) * y_s + _sigmoid(ga_ref[...].astype(F32)) * y_a
    x1 = x_ref[...] + _dot(merged.astype(BF16), wo_ref[...])
    x1_ref[...] = x1
    h2 = _rms(x1, nw_ref[...])
    _store_rows_as_tiles(h2_ref, h2)

    h_hi = h2.astype(BF16)
    h_lo = (h2 - h_hi.astype(F32)).astype(BF16)
    wr_hi = wr_hi_ref[...]
    logits = _dot_nt(wr_hi, h_hi) + _dot_nt(wr_hi, h_lo) + _dot_nt(wr_lo_ref[...], h_hi) + br_ref[...]
    gl = logits[0:N_GROUPS, :]
    gidx = lax.broadcasted_iota(jnp.int32, gl.shape, 0)
    gmax = jnp.max(gl, axis=0, keepdims=True)
    gi = jnp.min(jnp.where(gl == gmax, gidx, N_GROUPS), axis=0, keepdims=True)
    g_w = 1.0 / jnp.sum(jnp.exp(gl - gmax), axis=0, keepdims=True)
    el = jnp.zeros((EXPERTS_PER_GROUP, gl.shape[1]), F32)
    for g in range(N_GROUPS):
        lo = N_GROUPS + g * EXPERTS_PER_GROUP
        el = el + jnp.where(gi == g, logits[lo:lo + EXPERTS_PER_GROUP, :], 0.0)
    eidx = lax.broadcasted_iota(jnp.int32, el.shape, 0)
    v0 = jnp.max(el, axis=0, keepdims=True)
    i0 = jnp.min(jnp.where(el == v0, eidx, EXPERTS_PER_GROUP), axis=0, keepdims=True)
    rest = jnp.where(eidx == i0, -jnp.inf, el)
    v1 = jnp.max(rest, axis=0, keepdims=True)
    i1 = jnp.min(jnp.where((rest == v1) & (eidx != i0), eidx, EXPERTS_PER_GROUP), axis=0, keepdims=True)
    t = jnp.exp(v1 - v0)
    w0 = g_w / (1.0 + t)
    w1 = g_w * t / (1.0 + t)
    base = gi * EXPERTS_PER_GROUP
    route_ref[...] = jnp.concatenate(
        [(base + i0).astype(F32), (base + i1).astype(F32), w0, w1], axis=0)


def _merge(x2d, yn, att, gs, ga, wso, wao, wo, nw, wr_hi, wr_lo, br):
    t_tokens = x2d.shape[0]
    tm = min(TM_MERGE, t_tokens)
    row = lambda: pl.BlockSpec((tm, D_MODEL), lambda i: (i, 0))
    whole = pl.BlockSpec(memory_space=pltpu.VMEM)
    return pl.pallas_call(
        _merge_kernel,
        grid=(t_tokens // tm,),
        in_specs=[row(), row(), row(), row(), row(), whole, whole, whole, whole, whole, whole, whole],
        out_specs=[row(), pl.BlockSpec((tm * ROW_TILE, LANES), lambda i: (i, 0)),
                   pl.BlockSpec((4, tm), lambda i: (0, i))],
        out_shape=[jax.ShapeDtypeStruct((t_tokens, D_MODEL), F32),
                   jax.ShapeDtypeStruct((t_tokens * ROW_TILE, LANES), F32),
                   jax.ShapeDtypeStruct((4, t_tokens), F32)],
        compiler_params=pltpu.CompilerParams(dimension_semantics=("arbitrary",), vmem_limit_bytes=VMEM_LIMIT),
        name="merge",
    )(x2d, yn, att, gs, ga, wso, wao, wo, nw, wr_hi, wr_lo, br)


def _gather_rows(src_hbm, idx_ref, idx_base, buf, sem, n_rows):
    def body(r, carry):
        tok = idx_ref[idx_base + r]
        pltpu.make_async_copy(src_hbm.at[pl.ds(tok * ROW_TILE, ROW_TILE), :],
                              buf.at[pl.ds(r * ROW_TILE, ROW_TILE), :], sem).start()
        return carry
    lax.fori_loop(0, n_rows, body, 0)


def _wait_rows(src_hbm, buf, sem, n_rows):
    pltpu.make_async_copy(src_hbm.at[pl.ds(0, n_rows * ROW_TILE), :], buf, sem).wait()


def _rows_as_matrix(buf, n_rows):
    return jnp.concatenate([buf[pl.ds(s, n_rows, stride=ROW_TILE), :] for s in range(ROW_TILE)], axis=1)


def _moe_kernel(texp_ref, ntile_ref, rtok_ref, h_hbm, rw_ref, wgu_ref, wd_ref, y_ref, buf0, buf1, sems):
    t = pl.program_id(0)
    nt = ntile_ref[0]
    bufs = (buf0, buf1)

    def start(tile, slot):
        _gather_rows(h_hbm, rtok_ref, tile * TM_MOE, bufs[slot], sems.at[slot], TM_MOE)

    @pl.when(t == 0)
    def _():
        start(0, 0)

    for slot in range(2):
        @pl.when((t % 2 == slot) & (t < nt))
        def _():
            @pl.when(t + 1 < nt)
            def _():
                start(t + 1, 1 - slot)
            _wait_rows(h_hbm, bufs[slot], sems.at[slot], TM_MOE)
            xt = _rows_as_matrix(bufs[slot], TM_MOE).astype(BF16)
            gu = _dot(xt, wgu_ref[0])
            hid = (_silu(gu[:, :D_EXPERT]) * gu[:, D_EXPERT:]).astype(BF16)
            _store_rows_as_tiles(y_ref, _dot(hid, wd_ref[0]) * rw_ref[...])

    @pl.when(t >= nt)
    def _():
        y_ref[...] = jnp.zeros_like(y_ref)


def _moe(h2_tiles, tile_expert, n_tiles, row_token, row_w, wgu, wd, max_tiles):
    rows = max_tiles * TM_MOE
    grid_spec = pltpu.PrefetchScalarGridSpec(
        num_scalar_prefetch=3,
        grid=(max_tiles,),
        in_specs=[pl.BlockSpec(memory_space=pl.ANY),
                  pl.BlockSpec((TM_MOE, 1), lambda t, te, nt, rt: (t, 0)),
                  pl.BlockSpec((1, D_MODEL, 2 * D_EXPERT), lambda t, te, nt, rt: (te[t], 0, 0)),
                  pl.BlockSpec((1, D_EXPERT, D_MODEL), lambda t, te, nt, rt: (te[t], 0, 0))],
        out_specs=pl.BlockSpec((TM_MOE * ROW_TILE, LANES), lambda t, te, nt, rt: (t, 0)),
        scratch_shapes=[pltpu.VMEM((TM_MOE * ROW_TILE, LANES), F32),
                        pltpu.VMEM((TM_MOE * ROW_TILE, LANES), F32),
                        pltpu.SemaphoreType.DMA((2,))],
    )
    return pl.pallas_call(
        _moe_kernel,
        grid_spec=grid_spec,
        out_shape=jax.ShapeDtypeStruct((rows * ROW_TILE, LANES), F32),
        compiler_params=pltpu.CompilerParams(dimension_semantics=("arbitrary",), vmem_limit_bytes=VMEM_LIMIT),
        name="moe",
    )(tile_expert, n_tiles, row_token, h2_tiles, row_w, wgu, wd)


def _moe_plan(route, max_tiles):
    t_tokens = route.shape[1]
    rows = max_tiles * TM_MOE
    e_flat = route[0:2].astype(jnp.int32).reshape(-1)
    w_flat = route[2:4].reshape(-1)
    tok = jnp.tile(jnp.arange(t_tokens, dtype=jnp.int32), 2)
    onehot = (e_flat[:, None] == jnp.arange(N_EXPERTS, dtype=jnp.int32)[None, :]).astype(jnp.int32)
    csum = jnp.cumsum(onehot, axis=0)
    rank = jnp.sum(csum * onehot, axis=1) - 1
    counts = csum[-1]
    padded = ((counts + TM_MOE - 1) // TM_MOE) * TM_MOE
    ends = jnp.cumsum(padded)
    starts = ends - padded
    pos = jnp.sum(onehot * starts[None, :], axis=1) + rank
    row_token = jnp.zeros((rows,), jnp.int32).at[pos].set(tok)
    row_w = jnp.zeros((rows,), F32).at[pos].set(w_flat)
    tile_start = jnp.arange(max_tiles, dtype=jnp.int32) * TM_MOE
    tile_expert = jnp.sum((tile_start[:, None] >= ends[None, :]).astype(jnp.int32), axis=1)
    tile_expert = jnp.minimum(tile_expert, N_EXPERTS - 1)
    n_tiles = (ends[-1] // TM_MOE).astype(jnp.int32).reshape(1)
    return tile_expert, n_tiles, row_token, row_w[:, None], pos


def _tail_kernel(pos_ref, x1_ref, p_ref, y_hbm, pnw_ref, wpg_ref, wple_ref, fnw_ref, o_ref,
                 buf0, buf1, buf2, buf3, sems, *, tm, t_tokens):
    i = pl.program_id(0)
    n = pl.num_programs(0)
    bufs = ((buf0, buf1), (buf2, buf3))

    def start(tile, slot):
        for s in range(2):
            _gather_rows(y_hbm, pos_ref, s * t_tokens + tile * tm, bufs[slot][s], sems.at[slot, s], tm)

    @pl.when(i == 0)
    def _():
        start(0, 0)

    for slot in range(2):
        @pl.when(i % 2 == slot)
        def _():
            @pl.when(i + 1 < n)
            def _():
                start(i + 1, 1 - slot)
            moe = jnp.zeros((tm, D_MODEL), F32)
            for s in range(2):
                _wait_rows(y_hbm, bufs[slot][s], sems.at[slot, s], tm)
                moe = moe + _rows_as_matrix(bufs[slot][s], tm)
            x2 = x1_ref[...] + moe
            gate = _sigmoid(_dot(_rms(x2, pnw_ref[...]).astype(BF16), wpg_ref[...]))
            x3 = x2 + gate * _dot(p_ref[...].astype(BF16), wple_ref[...])
            o_ref[...] = _rms(x3, fnw_ref[...])


def _tail(pos, x1, p2d, y_tiles, pnw, wpg, wple, fnw):
    t_tokens = x1.shape[0]
    tm = min(TM_MERGE, t_tokens)
    grid_spec = pltpu.PrefetchScalarGridSpec(
        num_scalar_prefetch=1,
        grid=(t_tokens // tm,),
        in_specs=[pl.BlockSpec((tm, D_MODEL), lambda i, ps: (i, 0)),
                  pl.BlockSpec((tm, PLE_DIM), lambda i, ps: (i, 0)),
                  pl.BlockSpec(memory_space=pl.ANY),
                  pl.BlockSpec(memory_space=pltpu.VMEM),
                  pl.BlockSpec(memory_space=pltpu.VMEM),
                  pl.BlockSpec(memory_space=pltpu.VMEM),
                  pl.BlockSpec(memory_space=pltpu.VMEM)],
        out_specs=pl.BlockSpec((tm, D_MODEL), lambda i, ps: (i, 0)),
        scratch_shapes=[pltpu.VMEM((tm * ROW_TILE, LANES), F32) for _ in range(4)]
        + [pltpu.SemaphoreType.DMA((2, 2))],
    )
    return pl.pallas_call(
        functools.partial(_tail_kernel, tm=tm, t_tokens=t_tokens),
        grid_spec=grid_spec,
        out_shape=jax.ShapeDtypeStruct((t_tokens, D_MODEL), F32),
        compiler_params=pltpu.CompilerParams(dimension_semantics=("arbitrary",), vmem_limit_bytes=VMEM_LIMIT),
        name="tail",
    )(pos, x1, p2d, y_tiles, pnw, wpg, wple, fnw)


def _rope_tables(positions):
    half = ATT_HEAD_DIM // 2
    inv_freq = ROPE_THETA ** (-jnp.arange(half, dtype=F32) / half)
    ang = positions.astype(F32)[..., None] * inv_freq
    cos = jnp.cos(ang).reshape(-1, half)
    sin = jnp.sin(ang).reshape(-1, half)
    reps = LANES // ATT_HEAD_DIM
    cosf = jnp.tile(jnp.concatenate([cos, cos], axis=1), (1, reps))
    sinf = jnp.tile(jnp.concatenate([-sin, sin], axis=1), (1, reps))
    return cosf, sinf, cos.T, sin.T


def _layer(x2d, p2d, rope_tables, batch, attn_norm_w, w_in, conv_w, conv_b, dt_bias, a_log, d_skip, ssd_norm_w,
           w_ssd_out, w_attn_out, w_out, moe_norm_w, w_rg, b_rg, w_re, b_re, w_gate, w_up, w_down,
           ple_norm_w, w_ple, w_ple_gate, final_norm_w):
    t_tokens = x2d.shape[0]
    offs = np.cumsum((0,) + IN_SPLITS)
    cols = lambda i: w_in[:, offs[i]:offs[i + 1]]
    w_big = jnp.concatenate([cols(0), cols(1), cols(4), cols(6), cols(7)], axis=1).astype(BF16)
    w_t = jnp.concatenate([cols(3), cols(5), cols(2)], axis=1).T.astype(BF16)
    w_dt = jnp.pad(cols(2), ((0, 0), (0, LANES - SSD_HEADS))).astype(BF16)

    z, xbc, k, gs, ga, dt, qt, vt, dtt = _in_proj(x2d, attn_norm_w[None, :], *rope_tables, w_big, w_t, w_dt)

    yn = _ssd(z, xbc, dt, dtt, conv_w, conv_b, dt_bias, a_log, d_skip, ssd_norm_w, batch)

    att = _moba(qt, k, vt, batch)

    wr = jnp.concatenate([w_rg, w_re], axis=1).T
    wr = jnp.pad(wr, ((0, ROUTER_ROWS - wr.shape[0]), (0, 0)))
    wr_hi = wr.astype(BF16)
    wr_lo = (wr - wr_hi.astype(F32)).astype(BF16)
    br = jnp.pad(jnp.concatenate([b_rg, b_re]), (0, ROUTER_ROWS - N_GROUPS - N_EXPERTS))[:, None]
    x1, h2, route = _merge(x2d, yn, att, gs, ga, w_ssd_out.astype(BF16), w_attn_out.astype(BF16),
                           w_out.astype(BF16), moe_norm_w[None, :], wr_hi, wr_lo, br)

    max_tiles = (2 * t_tokens) // TM_MOE + N_EXPERTS
    tile_expert, n_tiles, row_token, row_w, pos = _moe_plan(route, max_tiles)
    wgu = jnp.concatenate([w_gate, w_up], axis=-1).reshape(N_EXPERTS, D_MODEL, 2 * D_EXPERT).astype(BF16)
    wd = w_down.reshape(N_EXPERTS, D_EXPERT, D_MODEL).astype(BF16)
    y_sorted = _moe(h2, tile_expert, n_tiles, row_token, row_w, wgu, wd, max_tiles)

    return _tail(pos, x1, p2d, y_sorted, ple_norm_w[None, :],
                 w_ple_gate.astype(BF16), w_ple.astype(BF16), final_norm_w[None, :])


def kernel(x, p, positions, attn_norm_w, w_in, conv_w, conv_b, dt_bias, a_log, d_skip, ssd_norm_w, w_ssd_out,
           w_attn_out, w_out, moe_norm_w, w_router_group, b_router_group, w_router_expert, b_router_expert,
           w_exp_gate, w_exp_up, w_exp_down, ple_norm_w, w_ple, w_ple_gate, final_norm_w):
    batch, seq, d = x.shape
    depth = p.shape[0]
    assert depth == 1, "the final norm is fused into the last layer's tail kernel"
    rope_tables = _rope_tables(positions)
    i = 0
    out = _layer(x.reshape(batch * seq, d), p[i].reshape(batch * seq, PLE_DIM), rope_tables, batch,
                 attn_norm_w[i], w_in[i], conv_w[i], conv_b[i], dt_bias[i], a_log[i], d_skip[i], ssd_norm_w[i],
                 w_ssd_out[i], w_attn_out[i], w_out[i], moe_norm_w[i], w_router_group[i], b_router_group[i],
                 w_router_expert[i], b_router_expert[i], w_exp_gate[i], w_exp_up[i], w_exp_down[i],
                 ple_norm_w[i], w_ple[i], w_ple_gate[i], final_norm_w)
    return out.reshape(batch, seq, d)
```

```python
import functools

import jax
import jax.numpy as jnp
import numpy as np
from jax import lax
from jax.experimental import pallas as pl
from jax.experimental.pallas import tpu as pltpu

F32 = jnp.float32
BF16 = jnp.bfloat16

EPS = 1e-6
D_MODEL = 1024
SSD_HEADS = 16
SSD_HEAD_DIM = 64
SSD_INNER = SSD_HEADS * SSD_HEAD_DIM
SSD_GROUPS = 2
SSD_STATE = 64
SSD_CONV = 4
SSD_CHUNK = 128
SSD_CONV_CH = SSD_INNER + 2 * SSD_GROUPS * SSD_STATE
ATT_HEADS = 16
ATT_HEAD_DIM = 64
ATT_INNER = ATT_HEADS * ATT_HEAD_DIM
MOBA_BLOCK = 256
MOBA_TOPK = 3
ROPE_THETA = 10000.0
IN_SPLITS = (SSD_INNER, SSD_CONV_CH, SSD_HEADS, ATT_INNER, ATT_INNER, ATT_INNER, D_MODEL, D_MODEL)
N_GROUPS = 4
EXPERTS_PER_GROUP = 8
N_EXPERTS = N_GROUPS * EXPERTS_PER_GROUP
D_EXPERT = 256
PLE_DIM = 256

LANES = 128
SUBLANES = 8
NEG_BIG = -1e30
VMEM_LIMIT = 56 * 1024 * 1024

TM_PROJ = 512
TM_MERGE = 512
TM_MOE = 256


def _dot(a, b):
    return jnp.dot(a, b, preferred_element_type=F32)


def _dot_nt(a, b):
    return lax.dot_general(a, b, (((1,), (1,)), ((), ())), preferred_element_type=F32)


def _dot_tn(a, b):
    return lax.dot_general(a, b, (((0,), (0,)), ((), ())), preferred_element_type=F32)


def _split3(a):
    hi = a.astype(BF16)
    r1 = a - hi.astype(F32)
    mid = r1.astype(BF16)
    lo = (r1 - mid.astype(F32)).astype(BF16)
    return hi, mid, lo


def _dot_exact_rhs(sel_bf16, a):
    hi, mid, lo = _split3(a)
    return _dot(sel_bf16, hi) + _dot(sel_bf16, mid) + _dot(sel_bf16, lo)


def _dot_exact_lhs(a, sel_bf16):
    hi, mid, lo = _split3(a)
    return _dot(hi, sel_bf16) + _dot(mid, sel_bf16) + _dot(lo, sel_bf16)


def _sigmoid(x):
    return 1.0 / (1.0 + jnp.exp(-x))


def _silu(x):
    return x * _sigmoid(x)


def _softplus(x):
    return jnp.maximum(x, 0.0) + jnp.log1p(jnp.exp(-jnp.abs(x)))


def _rms(x, w):
    inv = lax.rsqrt(jnp.mean(x * x, axis=-1, keepdims=True) + EPS)
    return (x * inv) * w


ROW_TILE = D_MODEL // LANES


def _store_rows_as_tiles(ref, mat):
    rows = mat.shape[0]
    for s in range(ROW_TILE):
        ref[pl.ds(s, rows, stride=ROW_TILE), :] = mat[:, s * LANES:(s + 1) * LANES]


def _rope_tile(t, cos, sin_signed):
    half = ATT_HEAD_DIM // 2
    lane = lax.broadcasted_iota(jnp.int32, t.shape, 1)
    first = (lane % ATT_HEAD_DIM) < half
    partner = jnp.where(first, pltpu.roll(t, LANES - half, 1), pltpu.roll(t, half, 1))
    return t * cos + partner * sin_signed


LOG2E = 1.4426950408889634
Q_SCALE = ATT_HEAD_DIM ** -0.5 * LOG2E


def _in_proj_kernel(x_ref, nw_ref, cos_ref, sin_ref, cost_ref, sint_ref, w_ref, wt_ref, wdt_ref,
                    z_ref, xbc_ref, k_ref, gs_ref, ga_ref, dt_ref, qt_ref, vt_ref, dtt_ref):
    h = _rms(x_ref[...], nw_ref[...]).astype(BF16)
    cos = cos_ref[...]
    sin = sin_ref[...]

    def proj(lo, width):
        return _dot(h, w_ref[:, lo:lo + width])

    o = 0
    z_ref[...] = proj(o, SSD_INNER).astype(BF16)
    o += SSD_INNER
    xbc_ref[...] = proj(o, SSD_CONV_CH).astype(BF16)
    o += SSD_CONV_CH
    for g in range(ATT_INNER // LANES):
        t = proj(o + g * LANES, LANES)
        k_ref[:, g * LANES:(g + 1) * LANES] = _rope_tile(t, cos, sin).astype(BF16)
    o += ATT_INNER
    for ref in (gs_ref, ga_ref):
        ref[...] = proj(o, D_MODEL).astype(BF16)
        o += D_MODEL
    dt_ref[...] = _dot(h, wdt_ref[...])[:, :SSD_HEADS]

    dtt_ref[...] = _dot_nt(wt_ref[2 * ATT_INNER:2 * ATT_INNER + SSD_HEADS, :], h)
    vt_ref[...] = _dot_nt(wt_ref[ATT_INNER:2 * ATT_INNER, :], h).astype(BF16)
    qt = _dot_nt(wt_ref[0:ATT_INNER, :], h)
    cost = cost_ref[...]
    sint = sint_ref[...]
    half = ATT_HEAD_DIM // 2
    for hd in range(ATT_HEADS):
        r0 = hd * ATT_HEAD_DIM
        x1 = qt[r0:r0 + half, :]
        x2 = qt[r0 + half:r0 + ATT_HEAD_DIM, :]
        qt_ref[r0:r0 + half, :] = ((x1 * cost - x2 * sint) * Q_SCALE).astype(BF16)
        qt_ref[r0 + half:r0 + ATT_HEAD_DIM, :] = ((x2 * cost + x1 * sint) * Q_SCALE).astype(BF16)


def _in_proj(x2d, nw, cosf, sinf, cost, sint, w_big, w_t, w_dt):
    t_tokens = x2d.shape[0]
    tm = min(TM_PROJ, t_tokens)
    grid = (t_tokens // tm,)
    row = lambda width: pl.BlockSpec((tm, width), lambda i: (i, 0))
    col = lambda height: pl.BlockSpec((height, tm), lambda i: (0, i))
    whole = pl.BlockSpec(memory_space=pltpu.VMEM)
    out_shapes = [
        jax.ShapeDtypeStruct((t_tokens, SSD_INNER), BF16),
        jax.ShapeDtypeStruct((t_tokens, SSD_CONV_CH), BF16),
        jax.ShapeDtypeStruct((t_tokens, ATT_INNER), BF16),
        jax.ShapeDtypeStruct((t_tokens, D_MODEL), BF16),
        jax.ShapeDtypeStruct((t_tokens, D_MODEL), BF16),
        jax.ShapeDtypeStruct((t_tokens, SSD_HEADS), F32),
        jax.ShapeDtypeStruct((ATT_INNER, t_tokens), BF16),
        jax.ShapeDtypeStruct((ATT_INNER, t_tokens), BF16),
        jax.ShapeDtypeStruct((SSD_HEADS, t_tokens), F32),
    ]
    out_specs = [row(SSD_INNER), row(SSD_CONV_CH), row(ATT_INNER), row(D_MODEL), row(D_MODEL), row(SSD_HEADS),
                 col(ATT_INNER), col(ATT_INNER), col(SSD_HEADS)]
    return pl.pallas_call(
        _in_proj_kernel,
        grid=grid,
        in_specs=[row(D_MODEL), whole, row(LANES), row(LANES), col(ATT_HEAD_DIM // 2), col(ATT_HEAD_DIM // 2),
                  whole, whole, whole],
        out_specs=out_specs,
        out_shape=out_shapes,
        compiler_params=pltpu.CompilerParams(dimension_semantics=("arbitrary",), vmem_limit_bytes=VMEM_LIMIT),
        name="in_proj",
    )(x2d, nw, cosf, sinf, cost, sint, w_big, w_t, w_dt)


def _ssd_kernel(z_ref, xbc_ref, dt_ref, dtt_ref, cw_ref, cb_ref, dtb_ref, dtbc_ref, alog_ref, alogc_ref,
                dexp_ref, nw_ref, out_ref, state_ref, ubuf_ref):
    c = pl.program_id(1)
    cl = SSD_CHUNK
    heads_per_group = SSD_HEADS // SSD_GROUPS
    gw = heads_per_group * SSD_HEAD_DIM

    @pl.when(c == 0)
    def _():
        state_ref[...] = jnp.zeros_like(state_ref)
        ubuf_ref[0:SUBLANES, :] = jnp.zeros((SUBLANES, SSD_CONV_CH), F32)

    u = xbc_ref[...].astype(F32)
    ubuf_ref[SUBLANES:SUBLANES + cl, :] = u
    acc = cb_ref[...] + cw_ref[SSD_CONV - 1:SSD_CONV, :] * u
    for k in range(SSD_CONV - 1):
        shift = SSD_CONV - 1 - k
        acc = acc + cw_ref[k:k + 1, :] * ubuf_ref[pl.ds(SUBLANES - shift, cl), :]
    ubuf_ref[0:SUBLANES, :] = u[cl - SUBLANES:cl, :]
    xc = _silu(acc)
    xs = xc[:, :SSD_INNER]
    bm = xc[:, SSD_INNER:SSD_INNER + SSD_GROUPS * SSD_STATE].astype(BF16)
    cm = xc[:, SSD_INNER + SSD_GROUPS * SSD_STATE:].astype(BF16)

    dt = _softplus(dt_ref[...] + dtb_ref[...])
    da = dt * (-jnp.exp(alog_ref[...]))
    dtt = _softplus(dtt_ref[...] + dtbc_ref[...])
    dat = dtt * (-jnp.exp(alogc_ref[...]))
    ri = lax.broadcasted_iota(jnp.int32, (cl, cl), 0)
    ci = lax.broadcasted_iota(jnp.int32, (cl, cl), 1)
    causal = ri >= ci
    tril = causal.astype(BF16)
    triu = (ri <= ci).astype(BF16)
    acum = _dot_exact_rhs(tril, da)
    acumt = _dot_exact_lhs(dat, triu)

    hrow = lax.broadcasted_iota(jnp.int32, (SSD_HEADS, SSD_INNER), 0)
    hcol = lax.broadcasted_iota(jnp.int32, (SSD_HEADS, SSD_INNER), 1) // SSD_HEAD_DIM
    expand = (hrow == hcol).astype(BF16)

    a_last = acum[cl - 1:cl, :]
    dt_e = _dot_exact_lhs(dt, expand)
    dec_e = _dot_exact_lhs(jnp.exp(a_last - acum), expand)
    ea_e = _dot_exact_lhs(jnp.exp(acum), expand)
    xdt = xs * dt_e
    xdt_b = xdt.astype(BF16)
    xdtdec_b = (xdt * dec_e).astype(BF16)
    chunk_decay_t = jnp.exp(acumt[:, cl - 1:cl])

    y_groups = []
    for g in range(SSD_GROUPS):
        b_g = bm[:, g * SSD_STATE:(g + 1) * SSD_STATE]
        c_g = cm[:, g * SSD_STATE:(g + 1) * SSD_STATE]
        prev = state_ref[g * gw:(g + 1) * gw, :]
        cb = _dot_nt(c_g, b_g)
        y_off = _dot_nt(c_g, prev.astype(BF16)) * ea_e[:, g * gw:(g + 1) * gw]
        new_states = _dot_tn(xdtdec_b[:, g * gw:(g + 1) * gw], b_g)
        y_heads = []
        decay_rows = []
        for e in range(heads_per_group):
            hd = g * heads_per_group + e
            lmat = jnp.where(causal, jnp.exp(acum[:, hd:hd + 1] - acumt[hd:hd + 1, :]), 0.0)
            m = (cb * lmat).astype(BF16)
            y_heads.append(_dot(m, xdt_b[:, hd * SSD_HEAD_DIM:(hd + 1) * SSD_HEAD_DIM]))
            decay_rows.append(jnp.broadcast_to(chunk_decay_t[hd:hd + 1, :], (SSD_HEAD_DIM, SSD_STATE)))
        y_groups.append(jnp.concatenate(y_heads, axis=1) + y_off)
        state_ref[g * gw:(g + 1) * gw, :] = prev * jnp.concatenate(decay_rows, axis=0) + new_states

    y = jnp.concatenate(y_groups, axis=1) + xs * dexp_ref[...]
    yg = y * _silu(z_ref[...].astype(F32))
    nw = nw_ref[...]
    outs = [_rms(yg[:, g * gw:(g + 1) * gw], nw[:, g * gw:(g + 1) * gw]) for g in range(SSD_GROUPS)]
    out_ref[...] = jnp.concatenate(outs, axis=1).astype(BF16)


def _ssd(z, xbc, dt, dtt, conv_w, conv_b, dt_bias, a_log, d_skip, norm_w, batch):
    t_tokens = z.shape[0]
    nc = t_tokens // batch // SSD_CHUNK
    tok = lambda width: pl.BlockSpec((SSD_CHUNK, width), lambda b, c: (b * nc + c, 0))
    whole = pl.BlockSpec(memory_space=pltpu.VMEM)
    dexp = jnp.repeat(d_skip, SSD_HEAD_DIM)[None, :]
    return pl.pallas_call(
        _ssd_kernel,
        grid=(batch, nc),
        in_specs=[tok(SSD_INNER), tok(SSD_CONV_CH), tok(SSD_HEADS),
                  pl.BlockSpec((SSD_HEADS, SSD_CHUNK), lambda b, c: (0, b * nc + c)),
                  whole, whole, whole, whole, whole, whole, whole, whole],
        out_specs=tok(SSD_INNER),
        out_shape=jax.ShapeDtypeStruct((t_tokens, SSD_INNER), BF16),
        scratch_shapes=[pltpu.VMEM((SSD_INNER, SSD_STATE), F32),
                        pltpu.VMEM((SUBLANES + SSD_CHUNK, SSD_CONV_CH), F32)],
        compiler_params=pltpu.CompilerParams(dimension_semantics=("arbitrary", "arbitrary"),
                                             vmem_limit_bytes=VMEM_LIMIT),
        name="ssd",
    )(z, xbc, dt, dtt, conv_w, conv_b[None, :], dt_bias[None, :], dt_bias[:, None],
      a_log[None, :], a_log[:, None], dexp, norm_w[None, :])


MOBA_SUPER = 2 * MOBA_BLOCK


MOBA_BLOCK_ROWS = 16
ONES_ROWS = 16


def _moba_kernel(qt_ref, k_ref, vt_ref, o_ref, kfull_ref, qaug_ref, sa_ref, sb_ref, m_ref, l_ref, acc_ref, *, seq):
    nb = seq // MOBA_BLOCK
    sb = MOBA_SUPER
    nsb = seq // sb
    hd = ATT_HEAD_DIM
    nbr = MOBA_BLOCK_ROWS

    for c in range(nsb):
        rows = slice(c * sb, (c + 1) * sb)
        kfull_ref[rows, 0:LANES] = k_ref[rows, :]
        rblk = lax.broadcasted_iota(jnp.int32, (sb, LANES), 0) // MOBA_BLOCK + c * (sb // MOBA_BLOCK)
        rlane = lax.broadcasted_iota(jnp.int32, (sb, LANES), 1)
        kfull_ref[rows, LANES:2 * LANES] = jnp.where(rblk == rlane, 1.0, 0.0).astype(BF16)

    pr = lax.broadcasted_iota(jnp.int32, (nbr, seq), 0)
    pc = lax.broadcasted_iota(jnp.int32, (nbr, seq), 1) // MOBA_BLOCK
    pool = jnp.where(pr == pc, 1.0 / MOBA_BLOCK, 0.0).astype(BF16)
    kmean = _dot(pool, k_ref[...])
    km_hi = kmean.astype(BF16)
    km_lo = (kmean - km_hi.astype(F32)).astype(BF16)

    head_of_row = lax.broadcasted_iota(jnp.int32, (LANES, sb), 0) // hd
    blk_row = lax.broadcasted_iota(jnp.int32, (nbr, sb), 0)
    q_local = lax.broadcasted_iota(jnp.int32, (nbr, sb), 1)
    inrange = blk_row < nb
    krow = lax.broadcasted_iota(jnp.int32, (sb, 2 * sb), 0)
    qcol = lax.broadcasted_iota(jnp.int32, (sb, 2 * sb), 1) % sb
    causal = krow <= qcol
    ones_rows = jnp.ones((ONES_ROWS, sb), BF16)
    flag_pad = jnp.zeros((LANES - nbr, sb), BF16)

    def scores_into(s_ref, kv):
        start = pl.multiple_of(kv * sb, sb)
        s_ref[...] = _dot(kfull_ref[pl.ds(start, sb), :], qaug_ref[...])

    def absorb(kv, s_ref, diagonal):
        start = pl.multiple_of(kv * sb, sb)
        s_t = s_ref[...]
        if diagonal:
            s_t = jnp.where(causal, s_t, NEG_BIG)
        m_prev = m_ref[...]
        m_new = jnp.maximum(m_prev, jnp.max(s_t, axis=0, keepdims=True))
        m_ref[...] = m_new
        p_b = jnp.exp2(s_t - m_new).astype(BF16)
        alpha = jnp.exp2(m_prev - m_new)
        for a in range(2):
            cols = slice(a * sb, (a + 1) * sb)
            rows = slice(a * hd, (a + 1) * hd)
            v_aug = jnp.concatenate([vt_ref[rows, pl.ds(start, sb)], ones_rows], axis=0)
            r = _dot(v_aug, p_b[:, cols])
            l_ref[a:a + 1, :] = alpha[:, cols] * l_ref[a:a + 1, :] + r[hd:hd + 1, :]
            acc_ref[rows, :] = alpha[:, cols] * acc_ref[rows, :] + r[0:hd, :]

    def q_super(i, carry):
        qstart = pl.multiple_of(i * sb, sb)
        qt_sb = qt_ref[:, pl.ds(qstart, sb)]
        own = 2 * i + jnp.where(q_local >= MOBA_BLOCK, 1, 0)
        valid = blk_row < own
        for a in range(2):
            qa = jnp.where(head_of_row == a, qt_sb, jnp.zeros_like(qt_sb))
            gate = _dot(km_hi, qa) + _dot(km_lo, qa)
            g = jnp.where(valid, gate, -jnp.inf)
            sel = jnp.zeros((nbr, sb), jnp.bool_)
            for _ in range(min(MOBA_TOPK, nb - 1)):
                cand = inrange & jnp.logical_not(sel)
                gm = jnp.max(jnp.where(cand, g, -jnp.inf), axis=0, keepdims=True)
                hit = cand & (g == gm)
                first = jnp.min(jnp.where(hit, blk_row, nb), axis=0, keepdims=True)
                sel = sel | (blk_row == first)
            chosen = (sel & valid) | (blk_row == own)
            flags = jnp.where(chosen, 0.0, NEG_BIG).astype(BF16)
            qaug_ref[:, a * sb:(a + 1) * sb] = jnp.concatenate([qa, flags, flag_pad], axis=0)

        m_ref[...] = jnp.full(m_ref.shape, NEG_BIG, F32)
        l_ref[...] = jnp.zeros_like(l_ref)
        acc_ref[...] = jnp.zeros_like(acc_ref)
        scores_into(sa_ref, 0)

        def two_steps(t, carry):
            scores_into(sb_ref, 2 * t + 1)
            absorb(2 * t, sa_ref, False)
            scores_into(sa_ref, 2 * t + 2)
            absorb(2 * t + 1, sb_ref, False)
            return carry

        lax.fori_loop(0, i // 2, two_steps, 0)

        @pl.when(i % 2 == 1)
        def _():
            scores_into(sb_ref, i)
            absorb(i - 1, sa_ref, False)
            absorb(i, sb_ref, True)

        @pl.when(i % 2 == 0)
        def _():
            absorb(i, sa_ref, True)

        out_t = jnp.concatenate([acc_ref[a * hd:(a + 1) * hd, :] / l_ref[a:a + 1, :] for a in range(2)], axis=0)
        o_ref[pl.ds(qstart, sb), :] = out_t.T.astype(BF16)
        return carry

    lax.fori_loop(0, nsb, q_super, 0)


def _moba(qt, k, vt, batch):
    t_tokens = k.shape[0]
    seq = t_tokens // batch
    assert seq % MOBA_SUPER == 0 and seq // MOBA_BLOCK <= MOBA_BLOCK_ROWS
    spec = pl.BlockSpec((seq, LANES), lambda b, hp: (b, hp))
    spec_t = pl.BlockSpec((LANES, seq), lambda b, hp: (hp, b))
    return pl.pallas_call(
        functools.partial(_moba_kernel, seq=seq),
        grid=(batch, ATT_INNER // LANES),
        in_specs=[spec_t, spec, spec_t],
        out_specs=spec,
        out_shape=jax.ShapeDtypeStruct((t_tokens, ATT_INNER), BF16),
        scratch_shapes=[pltpu.VMEM((seq, 2 * LANES), BF16),
                        pltpu.VMEM((2 * LANES, 2 * MOBA_SUPER), BF16),
                        pltpu.VMEM((MOBA_SUPER, 2 * MOBA_SUPER), F32),
                        pltpu.VMEM((MOBA_SUPER, 2 * MOBA_SUPER), F32),
                        pltpu.VMEM((1, 2 * MOBA_SUPER), F32),
                        pltpu.VMEM((SUBLANES, MOBA_SUPER), F32),
                        pltpu.VMEM((LANES, MOBA_SUPER), F32)],
        compiler_params=pltpu.CompilerParams(dimension_semantics=("arbitrary", "arbitrary"),
                                             vmem_limit_bytes=VMEM_LIMIT),
        name="moba",
    )(qt, k, vt)


ROUTER_ROWS = 40


def _merge_kernel(x_ref, yn_ref, att_ref, gs_ref, ga_ref, wso_ref, wao_ref, wo_ref, nw_ref,
                  wr_hi_ref, wr_lo_ref, br_ref, x1_ref, h2_ref, route_ref):
    y_s = _dot(yn_ref[...], wso_ref[...])
    y_a = _dot(att_ref[...], wao_ref[...])
    merged = _sigmoid(gs_ref[...].astype(F32)) * y_s + _sigmoid(ga_ref[...].astype(F32)) * y_a
    x1 = x_ref[...] + _dot(merged.astype(BF16), wo_ref[...])
    x1_ref[...] = x1
    h2 = _rms(x1, nw_ref[...])
    _store_rows_as_tiles(h2_ref, h2)

    h_hi = h2.astype(BF16)
    h_lo = (h2 - h_hi.astype(F32)).astype(BF16)
    wr_hi = wr_hi_ref[...]
    logits = _dot_nt(wr_hi, h_hi) + _dot_nt(wr_hi, h_lo) + _dot_nt(wr_lo_ref[...], h_hi) + br_ref[...]
    gl = logits[0:N_GROUPS, :]
    gidx = lax.broadcasted_iota(jnp.int32, gl.shape, 0)
    gmax = jnp.max(gl, axis=0, keepdims=True)
    gi = jnp.min(jnp.where(gl == gmax, gidx, N_GROUPS), axis=0, keepdims=True)
    g_w = 1.0 / jnp.sum(jnp.exp(gl - gmax), axis=0, keepdims=True)
    el = jnp.zeros((EXPERTS_PER_GROUP, gl.shape[1]), F32)
    for g in range(N_GROUPS):
        lo = N_GROUPS + g * EXPERTS_PER_GROUP
        el = el + jnp.where(gi == g, logits[lo:lo + EXPERTS_PER_GROUP, :], 0.0)
    eidx = lax.broadcasted_iota(jnp.int32, el.shape, 0)
    v0 = jnp.max(el, axis=0, keepdims=True)
    i0 = jnp.min(jnp.where(el == v0, eidx, EXPERTS_PER_GROUP), axis=0, keepdims=True)
    rest = jnp.where(eidx == i0, -jnp.inf, el)
    v1 = jnp.max(rest, axis=0, keepdims=True)
    i1 = jnp.min(jnp.where((rest == v1) & (eidx != i0), eidx, EXPERTS_PER_GROUP), axis=0, keepdims=True)
    t = jnp.exp(v1 - v0)
    w0 = g_w / (1.0 + t)
    w1 = g_w * t / (1.0 + t)
    base = gi * EXPERTS_PER_GROUP
    route_ref[...] = jnp.concatenate(
        [(base + i0).astype(F32), (base + i1).astype(F32), w0, w1], axis=0)


def _merge(x2d, yn, att, gs, ga, wso, wao, wo, nw, wr_hi, wr_lo, br):
    t_tokens = x2d.shape[0]
    tm = min(TM_MERGE, t_tokens)
    row = lambda: pl.BlockSpec((tm, D_MODEL), lambda i: (i, 0))
    whole = pl.BlockSpec(memory_space=pltpu.VMEM)
    return pl.pallas_call(
        _merge_kernel,
        grid=(t_tokens // tm,),
        in_specs=[row(), row(), row(), row(), row(), whole, whole, whole, whole, whole, whole, whole],
        out_specs=[row(), pl.BlockSpec((tm * ROW_TILE, LANES), lambda i: (i, 0)),
                   pl.BlockSpec((4, tm), lambda i: (0, i))],
        out_shape=[jax.ShapeDtypeStruct((t_tokens, D_MODEL), F32),
                   jax.ShapeDtypeStruct((t_tokens * ROW_TILE, LANES), F32),
                   jax.ShapeDtypeStruct((4, t_tokens), F32)],
        compiler_params=pltpu.CompilerParams(dimension_semantics=("arbitrary",), vmem_limit_bytes=VMEM_LIMIT),
        name="merge",
    )(x2d, yn, att, gs, ga, wso, wao, wo, nw, wr_hi, wr_lo, br)


def _rows_as_matrix(buf, n_rows):
    return jnp.concatenate([buf[pl.ds(s, n_rows, stride=ROW_TILE), :] for s in range(ROW_TILE)], axis=1)


def _row_tile(ref, r):
    return ref.at[pl.ds(r * ROW_TILE, ROW_TILE), :]


def _moe_plan(route, max_tiles):
    e_flat = route[0:2].astype(jnp.int32).reshape(-1)
    onehot = (e_flat[:, None] == jnp.arange(N_EXPERTS, dtype=jnp.int32)[None, :]).astype(jnp.int32)
    csum = jnp.cumsum(onehot, axis=0)
    rank = jnp.sum(csum * onehot, axis=1) - 1
    counts = csum[-1]
    padded = ((counts + TM_MOE - 1) // TM_MOE) * TM_MOE
    ends = jnp.cumsum(padded)
    starts = ends - padded
    pos = jnp.sum(onehot * starts[None, :], axis=1) + rank
    tile_start = jnp.arange(max_tiles, dtype=jnp.int32) * TM_MOE
    tile_expert = jnp.sum((tile_start[:, None] >= ends[None, :]).astype(jnp.int32), axis=1)
    tile_expert = jnp.minimum(tile_expert, N_EXPERTS - 1)
    n_tiles = (ends[-1] // TM_MOE).astype(jnp.int32).reshape(1)
    return pos, starts + counts, ends, tile_expert, n_tiles


def _dispatch_kernel(pos_ref, padlo_ref, padhi_ref, ntile_ref, h2_ref, xs_hbm, rasg_ref, zeros_ref, sem, zsem,
                     *, tm, t_tokens, max_tiles):
    i = pl.program_id(0)
    dump = 2 * t_tokens
    tile_rows = TM_MOE * ROW_TILE

    @pl.when(i == 0)
    def _():
        zeros_ref[...] = jnp.zeros_like(zeros_ref)
        nt = ntile_ref[0]

        def pad_rows(e, n_rows):
            lo = padlo_ref[e]

            def one(r, c):
                rasg_ref[r] = dump + e * TM_MOE + (r - lo)
                pltpu.make_async_copy(_row_tile(zeros_ref, 0), _row_tile(xs_hbm, r), zsem).start()
                return c
            lax.fori_loop(lo, padhi_ref[e], one, 0)
            return n_rows + (padhi_ref[e] - lo)
        n_pad = lax.fori_loop(0, N_EXPERTS, pad_rows, 0)

        def spare_tile(tile, carry):
            pltpu.make_async_copy(zeros_ref, xs_hbm.at[pl.ds(tile * tile_rows, tile_rows), :], zsem).start()

            def mark(r, c):
                rasg_ref[r] = dump
                return c
            lax.fori_loop(tile * TM_MOE, (tile + 1) * TM_MOE, mark, 0)
            return carry
        lax.fori_loop(nt, max_tiles, spare_tile, 0)

        def drain_row(j, carry):
            pltpu.make_async_copy(_row_tile(zeros_ref, 0), _row_tile(xs_hbm, 0), zsem).wait()
            return carry
        lax.fori_loop(0, n_pad, drain_row, 0)

        def drain_tile(j, carry):
            pltpu.make_async_copy(zeros_ref, xs_hbm.at[pl.ds(0, tile_rows), :], zsem).wait()
            return carry
        lax.fori_loop(nt, max_tiles, drain_tile, 0)

    def body(r, carry):
        for s in range(2):
            a = s * t_tokens + i * tm + r
            p = pos_ref[a]
            rasg_ref[p] = a
            pltpu.make_async_copy(_row_tile(h2_ref, r), _row_tile(xs_hbm, p), sem).start()
        return carry
    lax.fori_loop(0, tm, body, 0)
    for _ in range(2):
        pltpu.make_async_copy(h2_ref, xs_hbm.at[pl.ds(0, tm * ROW_TILE), :], sem).wait()


def _dispatch(pos, pad_lo, pad_hi, n_tiles, h2_tiles, max_tiles):
    t_tokens = h2_tiles.shape[0] // ROW_TILE
    tm = min(TM_MERGE, t_tokens)
    rows = max_tiles * TM_MOE
    grid_spec = pltpu.PrefetchScalarGridSpec(
        num_scalar_prefetch=4,
        grid=(t_tokens // tm,),
        in_specs=[pl.BlockSpec((tm * ROW_TILE, LANES), lambda i, *_: (i, 0))],
        out_specs=[pl.BlockSpec(memory_space=pl.ANY), pl.BlockSpec(memory_space=pltpu.SMEM)],
        scratch_shapes=[pltpu.VMEM((TM_MOE * ROW_TILE, LANES), F32),
                        pltpu.SemaphoreType.DMA(()), pltpu.SemaphoreType.DMA(())],
    )
    return pl.pallas_call(
        functools.partial(_dispatch_kernel, tm=tm, t_tokens=t_tokens, max_tiles=max_tiles),
        grid_spec=grid_spec,
        out_shape=[jax.ShapeDtypeStruct((rows * ROW_TILE, LANES), F32),
                   jax.ShapeDtypeStruct((max_tiles * TM_MOE,), jnp.int32)],
        compiler_params=pltpu.CompilerParams(dimension_semantics=("arbitrary",), vmem_limit_bytes=VMEM_LIMIT),
        name="dispatch",
    )(pos, pad_lo, pad_hi, n_tiles, h2_tiles)


def _moe_kernel(texp_ref, ntile_ref, rasg_ref, xs_ref, wgu_ref, wd_ref, y_hbm, ybuf0, ybuf1, sems,
                *, max_tiles, dump_row):
    t = pl.program_id(0)
    nt = ntile_ref[0]
    ybufs = (ybuf0, ybuf1)

    def wait_slot(slot):
        pltpu.make_async_copy(ybufs[slot], y_hbm.at[pl.ds(0, TM_MOE * ROW_TILE), :], sems.at[slot]).wait()

    @pl.when(t == 0)
    def _():
        ybuf1[...] = jnp.zeros_like(ybuf1)
        for e in range(N_EXPERTS):
            pltpu.make_async_copy(
                ybuf1, y_hbm.at[pl.ds((dump_row + e * TM_MOE) * ROW_TILE, TM_MOE * ROW_TILE), :], sems.at[1]).start()
        for e in range(N_EXPERTS):
            wait_slot(1)

    for slot in range(2):
        @pl.when(t % 2 == slot)
        def _():
            @pl.when((t >= 2) & (t - 2 < nt))
            def _():
                wait_slot(slot)

            @pl.when(t < nt)
            def _():
                xt = _rows_as_matrix(xs_ref, TM_MOE).astype(BF16)
                gu = _dot(xt, wgu_ref[0])
                hid = (_silu(gu[:, :D_EXPERT]) * gu[:, D_EXPERT:]).astype(BF16)
                _store_rows_as_tiles(ybufs[slot], _dot(hid, wd_ref[0]))

                def body(r, carry):
                    a = rasg_ref[t * TM_MOE + r]
                    pltpu.make_async_copy(_row_tile(ybufs[slot], r), _row_tile(y_hbm, a), sems.at[slot]).start()
                    return carry
                lax.fori_loop(0, TM_MOE, body, 0)

            @pl.when(t == max_tiles - 1)
            def _():
                @pl.when((t >= 1) & (t - 1 < nt))
                def _():
                    wait_slot(1 - slot)

                @pl.when(t < nt)
                def _():
                    wait_slot(slot)


def _moe(xs_tiles, tile_expert, n_tiles, row_asg, wgu, wd, t_tokens, max_tiles):
    live = lambda t, nt: jnp.minimum(t, nt[0] - 1)
    grid_spec = pltpu.PrefetchScalarGridSpec(
        num_scalar_prefetch=3,
        grid=(max_tiles,),
        in_specs=[pl.BlockSpec((TM_MOE * ROW_TILE, LANES), lambda t, te, nt, ra: (live(t, nt), 0)),
                  pl.BlockSpec((1, D_MODEL, 2 * D_EXPERT), lambda t, te, nt, ra: (te[live(t, nt)], 0, 0)),
                  pl.BlockSpec((1, D_EXPERT, D_MODEL), lambda t, te, nt, ra: (te[live(t, nt)], 0, 0))],
        out_specs=pl.BlockSpec(memory_space=pl.ANY),
        scratch_shapes=[pltpu.VMEM((TM_MOE * ROW_TILE, LANES), F32),
                        pltpu.VMEM((TM_MOE * ROW_TILE, LANES), F32),
                        pltpu.SemaphoreType.DMA((2,))],
    )
    return pl.pallas_call(
        functools.partial(_moe_kernel, max_tiles=max_tiles, dump_row=2 * t_tokens),
        grid_spec=grid_spec,
        out_shape=jax.ShapeDtypeStruct(((2 * t_tokens + N_EXPERTS * TM_MOE) * ROW_TILE, LANES), F32),
        compiler_params=pltpu.CompilerParams(dimension_semantics=("arbitrary",), vmem_limit_bytes=VMEM_LIMIT),
        name="moe",
    )(tile_expert, n_tiles, row_asg, xs_tiles, wgu, wd)


def _tail_kernel(x1_ref, p_ref, y0_ref, y1_ref, w_ref, pnw_ref, wpg_ref, wple_ref, fnw_ref, o_ref, *, tm):
    w = w_ref[...]
    moe = w[:, 0:1] * _rows_as_matrix(y0_ref, tm) + w[:, 1:2] * _rows_as_matrix(y1_ref, tm)
    x2 = x1_ref[...] + moe
    gate = _sigmoid(_dot(_rms(x2, pnw_ref[...]).astype(BF16), wpg_ref[...]))
    x3 = x2 + gate * _dot(p_ref[...].astype(BF16), wple_ref[...])
    o_ref[...] = _rms(x3, fnw_ref[...])


def _tail(x1, p2d, y_tiles, w_tok, pnw, wpg, wple, fnw):
    t_tokens = x1.shape[0]
    tm = min(TM_MERGE, t_tokens)
    nblk = t_tokens // tm
    whole = pl.BlockSpec(memory_space=pltpu.VMEM)
    return pl.pallas_call(
        functools.partial(_tail_kernel, tm=tm),
        grid=(nblk,),
        in_specs=[pl.BlockSpec((tm, D_MODEL), lambda i: (i, 0)),
                  pl.BlockSpec((tm, PLE_DIM), lambda i: (i, 0)),
                  pl.BlockSpec((tm * ROW_TILE, LANES), lambda i: (i, 0)),
                  pl.BlockSpec((tm * ROW_TILE, LANES), lambda i: (nblk + i, 0)),
                  pl.BlockSpec((tm, 2), lambda i: (i, 0)),
                  whole, whole, whole, whole],
        out_specs=pl.BlockSpec((tm, D_MODEL), lambda i: (i, 0)),
        out_shape=jax.ShapeDtypeStruct((t_tokens, D_MODEL), F32),
        compiler_params=pltpu.CompilerParams(dimension_semantics=("arbitrary",), vmem_limit_bytes=VMEM_LIMIT),
        name="tail",
    )(x1, p2d, y_tiles, y_tiles, w_tok, pnw, wpg, wple, fnw)


def _rope_tables(positions):
    half = ATT_HEAD_DIM // 2
    inv_freq = ROPE_THETA ** (-jnp.arange(half, dtype=F32) / half)
    ang = positions.astype(F32)[..., None] * inv_freq
    cos = jnp.cos(ang).reshape(-1, half)
    sin = jnp.sin(ang).reshape(-1, half)
    reps = LANES // ATT_HEAD_DIM
    cosf = jnp.tile(jnp.concatenate([cos, cos], axis=1), (1, reps))
    sinf = jnp.tile(jnp.concatenate([-sin, sin], axis=1), (1, reps))
    return cosf, sinf, cos.T, sin.T


def _layer(x2d, p2d, rope_tables, batch, attn_norm_w, w_in, conv_w, conv_b, dt_bias, a_log, d_skip, ssd_norm_w,
           w_ssd_out, w_attn_out, w_out, moe_norm_w, w_rg, b_rg, w_re, b_re, w_gate, w_up, w_down,
           ple_norm_w, w_ple, w_ple_gate, final_norm_w):
    t_tokens = x2d.shape[0]
    offs = np.cumsum((0,) + IN_SPLITS)
    cols = lambda i: w_in[:, offs[i]:offs[i + 1]]
    w_big = jnp.concatenate([cols(0), cols(1), cols(4), cols(6), cols(7)], axis=1).astype(BF16)
    w_t = jnp.concatenate([cols(3), cols(5), cols(2)], axis=1).T.astype(BF16)
    w_dt = jnp.pad(cols(2), ((0, 0), (0, LANES - SSD_HEADS))).astype(BF16)

    z, xbc, k, gs, ga, dt, qt, vt, dtt = _in_proj(x2d, attn_norm_w[None, :], *rope_tables, w_big, w_t, w_dt)

    yn = _ssd(z, xbc, dt, dtt, conv_w, conv_b, dt_bias, a_log, d_skip, ssd_norm_w, batch)

    att = _moba(qt, k, vt, batch)

    wr = jnp.concatenate([w_rg, w_re], axis=1).T
    wr = jnp.pad(wr, ((0, ROUTER_ROWS - wr.shape[0]), (0, 0)))
    wr_hi = wr.astype(BF16)
    wr_lo = (wr - wr_hi.astype(F32)).astype(BF16)
    br = jnp.pad(jnp.concatenate([b_rg, b_re]), (0, ROUTER_ROWS - N_GROUPS - N_EXPERTS))[:, None]
    x1, h2, route = _merge(x2d, yn, att, gs, ga, w_ssd_out.astype(BF16), w_attn_out.astype(BF16),
                           w_out.astype(BF16), moe_norm_w[None, :], wr_hi, wr_lo, br)

    max_tiles = (2 * t_tokens) // TM_MOE + N_EXPERTS
    pos, pad_lo, pad_hi, tile_expert, n_tiles = _moe_plan(route, max_tiles)
    wgu = jnp.concatenate([w_gate, w_up], axis=-1).reshape(N_EXPERTS, D_MODEL, 2 * D_EXPERT).astype(BF16)
    wd = w_down.reshape(N_EXPERTS, D_EXPERT, D_MODEL).astype(BF16)
    x_sorted, row_asg = _dispatch(pos, pad_lo, pad_hi, n_tiles, h2, max_tiles)
    y_tok = _moe(x_sorted, tile_expert, n_tiles, row_asg, wgu, wd, t_tokens, max_tiles)

    return _tail(x1, p2d, y_tok, route[2:4].T, ple_norm_w[None, :],
                 w_ple_gate.astype(BF16), w_ple.astype(BF16), final_norm_w[None, :])


def kernel(x, p, positions, attn_norm_w, w_in, conv_w, conv_b, dt_bias, a_log, d_skip, ssd_norm_w, w_ssd_out,
           w_attn_out, w_out, moe_norm_w, w_router_group, b_router_group, w_router_expert, b_router_expert,
           w_exp_gate, w_exp_up, w_exp_down, ple_norm_w, w_ple, w_ple_gate, final_norm_w):
    batch, seq, d = x.shape
    depth = p.shape[0]
    assert depth == 1, "the final norm is fused into the last layer's tail kernel"
    rope_tables = _rope_tables(positions)
    i = 0
    out = _layer(x.reshape(batch * seq, d), p[i].reshape(batch * seq, PLE_DIM), rope_tables, batch,
                 attn_norm_w[i], w_in[i], conv_w[i], conv_b[i], dt_bias[i], a_log[i], d_skip[i], ssd_norm_w[i],
                 w_ssd_out[i], w_attn_out[i], w_out[i], moe_norm_w[i], w_router_group[i], b_router_group[i],
                 w_router_expert[i], b_router_expert[i], w_exp_gate[i], w_exp_up[i], w_exp_down[i],
                 ple_norm_w[i], w_ple[i], w_ple_gate[i], final_norm_w)
    return out.reshape(batch, seq, d)
```

```python
import functools

import jax
import jax.numpy as jnp
import numpy as np
from jax import lax
from jax.experimental import pallas as pl
from jax.experimental.pallas import tpu as pltpu

F32 = jnp.float32
BF16 = jnp.bfloat16

EPS = 1e-6
D_MODEL = 1024
SSD_HEADS = 16
SSD_HEAD_DIM = 64
SSD_INNER = SSD_HEADS * SSD_HEAD_DIM
SSD_GROUPS = 2
SSD_STATE = 64
SSD_CONV = 4
SSD_CHUNK = 128
SSD_CONV_CH = SSD_INNER + 2 * SSD_GROUPS * SSD_STATE
ATT_HEADS = 16
ATT_HEAD_DIM = 64
ATT_INNER = ATT_HEADS * ATT_HEAD_DIM
MOBA_BLOCK = 256
MOBA_TOPK = 3
ROPE_THETA = 10000.0
IN_SPLITS = (SSD_INNER, SSD_CONV_CH, SSD_HEADS, ATT_INNER, ATT_INNER, ATT_INNER, D_MODEL, D_MODEL)
N_GROUPS = 4
EXPERTS_PER_GROUP = 8
N_EXPERTS = N_GROUPS * EXPERTS_PER_GROUP
D_EXPERT = 256
PLE_DIM = 256

LANES = 128
SUBLANES = 8
NEG_BIG = -1e30
VMEM_LIMIT = 56 * 1024 * 1024

TM_PROJ = 512
TM_MERGE = 512
TM_MOE = 256


def _dot(a, b):
    return jnp.dot(a, b, preferred_element_type=F32)


def _dot_nt(a, b):
    return lax.dot_general(a, b, (((1,), (1,)), ((), ())), preferred_element_type=F32)


def _dot_tn(a, b):
    return lax.dot_general(a, b, (((0,), (0,)), ((), ())), preferred_element_type=F32)


def _split3(a):
    hi = a.astype(BF16)
    r1 = a - hi.astype(F32)
    mid = r1.astype(BF16)
    lo = (r1 - mid.astype(F32)).astype(BF16)
    return hi, mid, lo


def _dot_exact_rhs(sel_bf16, a):
    hi, mid, lo = _split3(a)
    return _dot(sel_bf16, hi) + _dot(sel_bf16, mid) + _dot(sel_bf16, lo)


def _dot_exact_lhs(a, sel_bf16):
    hi, mid, lo = _split3(a)
    return _dot(hi, sel_bf16) + _dot(mid, sel_bf16) + _dot(lo, sel_bf16)


def _sigmoid(x):
    return 1.0 / (1.0 + jnp.exp(-x))


def _silu(x):
    return x * _sigmoid(x)


def _softplus(x):
    return jnp.maximum(x, 0.0) + jnp.log1p(jnp.exp(-jnp.abs(x)))


def _rms(x, w):
    inv = lax.rsqrt(jnp.mean(x * x, axis=-1, keepdims=True) + EPS)
    return (x * inv) * w


ROW_TILE = D_MODEL // LANES


def _store_rows_as_tiles(ref, mat):
    rows = mat.shape[0]
    for s in range(ROW_TILE):
        ref[pl.ds(s, rows, stride=ROW_TILE), :] = mat[:, s * LANES:(s + 1) * LANES]


def _rope_tile(t, cos, sin_signed):
    half = ATT_HEAD_DIM // 2
    lane = lax.broadcasted_iota(jnp.int32, t.shape, 1)
    first = (lane % ATT_HEAD_DIM) < half
    partner = jnp.where(first, pltpu.roll(t, LANES - half, 1), pltpu.roll(t, half, 1))
    return t * cos + partner * sin_signed


LOG2E = 1.4426950408889634
Q_SCALE = ATT_HEAD_DIM ** -0.5 * LOG2E


def _in_proj_kernel(x_ref, nw_ref, cos_ref, sin_ref, cost_ref, sint_ref, w_ref, wt_ref, wdt_ref,
                    z_ref, xbc_ref, k_ref, gs_ref, ga_ref, dt_ref, qt_ref, vt_ref, dtt_ref):
    h = _rms(x_ref[...], nw_ref[...]).astype(BF16)
    cos = cos_ref[...]
    sin = sin_ref[...]

    def proj(lo, width):
        return _dot(h, w_ref[:, lo:lo + width])

    o = 0
    z_ref[...] = proj(o, SSD_INNER).astype(BF16)
    o += SSD_INNER
    xbc_ref[...] = proj(o, SSD_CONV_CH).astype(BF16)
    o += SSD_CONV_CH
    for g in range(ATT_INNER // LANES):
        t = proj(o + g * LANES, LANES)
        k_ref[:, g * LANES:(g + 1) * LANES] = _rope_tile(t, cos, sin).astype(BF16)
    o += ATT_INNER
    for ref in (gs_ref, ga_ref):
        ref[...] = proj(o, D_MODEL).astype(BF16)
        o += D_MODEL
    dt_ref[...] = _dot(h, wdt_ref[...])[:, :SSD_HEADS]

    dtt_ref[...] = _dot_nt(wt_ref[2 * ATT_INNER:2 * ATT_INNER + SSD_HEADS, :], h)
    vt_ref[...] = _dot_nt(wt_ref[ATT_INNER:2 * ATT_INNER, :], h).astype(BF16)
    qt = _dot_nt(wt_ref[0:ATT_INNER, :], h)
    cost = cost_ref[...]
    sint = sint_ref[...]
    half = ATT_HEAD_DIM // 2
    for hd in range(ATT_HEADS):
        r0 = hd * ATT_HEAD_DIM
        x1 = qt[r0:r0 + half, :]
        x2 = qt[r0 + half:r0 + ATT_HEAD_DIM, :]
        qt_ref[r0:r0 + half, :] = ((x1 * cost - x2 * sint) * Q_SCALE).astype(BF16)
        qt_ref[r0 + half:r0 + ATT_HEAD_DIM, :] = ((x2 * cost + x1 * sint) * Q_SCALE).astype(BF16)


def _in_proj(x2d, nw, cosf, sinf, cost, sint, w_big, w_t, w_dt):
    t_tokens = x2d.shape[0]
    tm = min(TM_PROJ, t_tokens)
    grid = (t_tokens // tm,)
    row = lambda width: pl.BlockSpec((tm, width), lambda i: (i, 0))
    col = lambda height: pl.BlockSpec((height, tm), lambda i: (0, i))
    whole = pl.BlockSpec(memory_space=pltpu.VMEM)
    out_shapes = [
        jax.ShapeDtypeStruct((t_tokens, SSD_INNER), BF16),
        jax.ShapeDtypeStruct((t_tokens, SSD_CONV_CH), BF16),
        jax.ShapeDtypeStruct((t_tokens, ATT_INNER), BF16),
        jax.ShapeDtypeStruct((t_tokens, D_MODEL), BF16),
        jax.ShapeDtypeStruct((t_tokens, D_MODEL), BF16),
        jax.ShapeDtypeStruct((t_tokens, SSD_HEADS), F32),
        jax.ShapeDtypeStruct((ATT_INNER, t_tokens), BF16),
        jax.ShapeDtypeStruct((ATT_INNER, t_tokens), BF16),
        jax.ShapeDtypeStruct((SSD_HEADS, t_tokens), F32),
    ]
    out_specs = [row(SSD_INNER), row(SSD_CONV_CH), row(ATT_INNER), row(D_MODEL), row(D_MODEL), row(SSD_HEADS),
                 col(ATT_INNER), col(ATT_INNER), col(SSD_HEADS)]
    return pl.pallas_call(
        _in_proj_kernel,
        grid=grid,
        in_specs=[row(D_MODEL), whole, row(LANES), row(LANES), col(ATT_HEAD_DIM // 2), col(ATT_HEAD_DIM // 2),
                  whole, whole, whole],
        out_specs=out_specs,
        out_shape=out_shapes,
        compiler_params=pltpu.CompilerParams(dimension_semantics=("arbitrary",), vmem_limit_bytes=VMEM_LIMIT),
        name="in_proj",
    )(x2d, nw, cosf, sinf, cost, sint, w_big, w_t, w_dt)


def _ssd_kernel(z_ref, xbc_ref, dt_ref, dtt_ref, cw_ref, cb_ref, dtb_ref, dtbc_ref, alog_ref, alogc_ref,
                dexp_ref, nw_ref, out_ref, state_ref, ubuf_ref):
    c = pl.program_id(1)
    cl = SSD_CHUNK
    heads_per_group = SSD_HEADS // SSD_GROUPS
    gw = heads_per_group * SSD_HEAD_DIM

    @pl.when(c == 0)
    def _():
        state_ref[...] = jnp.zeros_like(state_ref)
        ubuf_ref[0:SUBLANES, :] = jnp.zeros((SUBLANES, SSD_CONV_CH), F32)

    u = xbc_ref[...].astype(F32)
    ubuf_ref[SUBLANES:SUBLANES + cl, :] = u
    acc = cb_ref[...] + cw_ref[SSD_CONV - 1:SSD_CONV, :] * u
    for k in range(SSD_CONV - 1):
        shift = SSD_CONV - 1 - k
        acc = acc + cw_ref[k:k + 1, :] * ubuf_ref[pl.ds(SUBLANES - shift, cl), :]
    ubuf_ref[0:SUBLANES, :] = u[cl - SUBLANES:cl, :]
    xc = _silu(acc)
    xs = xc[:, :SSD_INNER]
    bm = xc[:, SSD_INNER:SSD_INNER + SSD_GROUPS * SSD_STATE].astype(BF16)
    cm = xc[:, SSD_INNER + SSD_GROUPS * SSD_STATE:].astype(BF16)

    dt = _softplus(dt_ref[...] + dtb_ref[...])
    da = dt * (-jnp.exp(alog_ref[...]))
    dtt = _softplus(dtt_ref[...] + dtbc_ref[...])
    dat = dtt * (-jnp.exp(alogc_ref[...]))
    ri = lax.broadcasted_iota(jnp.int32, (cl, cl), 0)
    ci = lax.broadcasted_iota(jnp.int32, (cl, cl), 1)
    causal = ri >= ci
    tril = causal.astype(BF16)
    triu = (ri <= ci).astype(BF16)
    acum = _dot_exact_rhs(tril, da)
    acumt = _dot_exact_lhs(dat, triu)

    hrow = lax.broadcasted_iota(jnp.int32, (SSD_HEADS, SSD_INNER), 0)
    hcol = lax.broadcasted_iota(jnp.int32, (SSD_HEADS, SSD_INNER), 1) // SSD_HEAD_DIM
    expand = (hrow == hcol).astype(BF16)

    a_last = acum[cl - 1:cl, :]
    dt_e = _dot_exact_lhs(dt, expand)
    dec_e = _dot_exact_lhs(jnp.exp(a_last - acum), expand)
    ea_e = _dot_exact_lhs(jnp.exp(acum), expand)
    xdt = xs * dt_e
    xdt_b = xdt.astype(BF16)
    xdtdec_b = (xdt * dec_e).astype(BF16)
    chunk_decay_t = jnp.exp(acumt[:, cl - 1:cl])

    y_groups = []
    for g in range(SSD_GROUPS):
        b_g = bm[:, g * SSD_STATE:(g + 1) * SSD_STATE]
        c_g = cm[:, g * SSD_STATE:(g + 1) * SSD_STATE]
        prev = state_ref[g * gw:(g + 1) * gw, :]
        cb = _dot_nt(c_g, b_g)
        y_off = _dot_nt(c_g, prev.astype(BF16)) * ea_e[:, g * gw:(g + 1) * gw]
        new_states = _dot_tn(xdtdec_b[:, g * gw:(g + 1) * gw], b_g)
        y_heads = []
        decay_rows = []
        for e in range(heads_per_group):
            hd = g * heads_per_group + e
            lmat = jnp.where(causal, jnp.exp(acum[:, hd:hd + 1] - acumt[hd:hd + 1, :]), 0.0)
            m = (cb * lmat).astype(BF16)
            y_heads.append(_dot(m, xdt_b[:, hd * SSD_HEAD_DIM:(hd + 1) * SSD_HEAD_DIM]))
            decay_rows.append(jnp.broadcast_to(chunk_decay_t[hd:hd + 1, :], (SSD_HEAD_DIM, SSD_STATE)))
        y_groups.append(jnp.concatenate(y_heads, axis=1) + y_off)
        state_ref[g * gw:(g + 1) * gw, :] = prev * jnp.concatenate(decay_rows, axis=0) + new_states

    y = jnp.concatenate(y_groups, axis=1) + xs * dexp_ref[...]
    yg = y * _silu(z_ref[...].astype(F32))
    nw = nw_ref[...]
    outs = [_rms(yg[:, g * gw:(g + 1) * gw], nw[:, g * gw:(g + 1) * gw]) for g in range(SSD_GROUPS)]
    out_ref[...] = jnp.concatenate(outs, axis=1).astype(BF16)


def _ssd(z, xbc, dt, dtt, conv_w, conv_b, dt_bias, a_log, d_skip, norm_w, batch):
    t_tokens = z.shape[0]
    nc = t_tokens // batch // SSD_CHUNK
    tok = lambda width: pl.BlockSpec((SSD_CHUNK, width), lambda b, c: (b * nc + c, 0))
    whole = pl.BlockSpec(memory_space=pltpu.VMEM)
    dexp = jnp.repeat(d_skip, SSD_HEAD_DIM)[None, :]
    return pl.pallas_call(
        _ssd_kernel,
        grid=(batch, nc),
        in_specs=[tok(SSD_INNER), tok(SSD_CONV_CH), tok(SSD_HEADS),
                  pl.BlockSpec((SSD_HEADS, SSD_CHUNK), lambda b, c: (0, b * nc + c)),
                  whole, whole, whole, whole, whole, whole, whole, whole],
        out_specs=tok(SSD_INNER),
        out_shape=jax.ShapeDtypeStruct((t_tokens, SSD_INNER), BF16),
        scratch_shapes=[pltpu.VMEM((SSD_INNER, SSD_STATE), F32),
                        pltpu.VMEM((SUBLANES + SSD_CHUNK, SSD_CONV_CH), F32)],
        compiler_params=pltpu.CompilerParams(dimension_semantics=("arbitrary", "arbitrary"),
                                             vmem_limit_bytes=VMEM_LIMIT),
        name="ssd",
    )(z, xbc, dt, dtt, conv_w, conv_b[None, :], dt_bias[None, :], dt_bias[:, None],
      a_log[None, :], a_log[:, None], dexp, norm_w[None, :])


MOBA_SUPER = 2 * MOBA_BLOCK


MOBA_BLOCK_ROWS = 16
ONES_ROWS = 16


def _moba_kernel(qt_ref, k_ref, vt_ref, o_ref, kfull_ref, qaug_ref, sa_ref, sb_ref, m_ref, l_ref, acc_ref, *, seq):
    nb = seq // MOBA_BLOCK
    sb = MOBA_SUPER
    nsb = seq // sb
    hd = ATT_HEAD_DIM
    nbr = MOBA_BLOCK_ROWS

    for c in range(nsb):
        rows = slice(c * sb, (c + 1) * sb)
        kfull_ref[rows, 0:LANES] = k_ref[rows, :]
        rblk = lax.broadcasted_iota(jnp.int32, (sb, LANES), 0) // MOBA_BLOCK + c * (sb // MOBA_BLOCK)
        rlane = lax.broadcasted_iota(jnp.int32, (sb, LANES), 1)
        kfull_ref[rows, LANES:2 * LANES] = jnp.where(rblk == rlane, 1.0, 0.0).astype(BF16)

    pr = lax.broadcasted_iota(jnp.int32, (nbr, seq), 0)
    pc = lax.broadcasted_iota(jnp.int32, (nbr, seq), 1) // MOBA_BLOCK
    pool = jnp.where(pr == pc, 1.0 / MOBA_BLOCK, 0.0).astype(BF16)
    kmean = _dot(pool, k_ref[...])
    km_hi = kmean.astype(BF16)
    km_lo = (kmean - km_hi.astype(F32)).astype(BF16)

    head_of_row = lax.broadcasted_iota(jnp.int32, (LANES, sb), 0) // hd
    blk_row = lax.broadcasted_iota(jnp.int32, (nbr, sb), 0)
    q_local = lax.broadcasted_iota(jnp.int32, (nbr, sb), 1)
    inrange = blk_row < nb
    krow = lax.broadcasted_iota(jnp.int32, (sb, 2 * sb), 0)
    qcol = lax.broadcasted_iota(jnp.int32, (sb, 2 * sb), 1) % sb
    causal = krow <= qcol
    ones_rows = jnp.ones((ONES_ROWS, sb), BF16)
    flag_pad = jnp.zeros((LANES - nbr, sb), BF16)

    def scores_into(s_ref, kv):
        start = pl.multiple_of(kv * sb, sb)
        s_ref[...] = _dot(kfull_ref[pl.ds(start, sb), :], qaug_ref[...])

    def absorb(kv, s_ref, diagonal):
        start = pl.multiple_of(kv * sb, sb)
        s_t = s_ref[...]
        if diagonal:
            s_t = jnp.where(causal, s_t, NEG_BIG)
        m_prev = m_ref[...]
        m_new = jnp.maximum(m_prev, jnp.max(s_t, axis=0, keepdims=True))
        m_ref[...] = m_new
        p_b = jnp.exp2(s_t - m_new).astype(BF16)
        alpha = jnp.exp2(m_prev - m_new)
        for a in range(2):
            cols = slice(a * sb, (a + 1) * sb)
            rows = slice(a * hd, (a + 1) * hd)
            v_aug = jnp.concatenate([vt_ref[rows, pl.ds(start, sb)], ones_rows], axis=0)
            r = _dot(v_aug, p_b[:, cols])
            l_ref[a:a + 1, :] = alpha[:, cols] * l_ref[a:a + 1, :] + r[hd:hd + 1, :]
            acc_ref[rows, :] = alpha[:, cols] * acc_ref[rows, :] + r[0:hd, :]

    def q_super(i, carry):
        qstart = pl.multiple_of(i * sb, sb)
        qt_sb = qt_ref[:, pl.ds(qstart, sb)]
        own = 2 * i + jnp.where(q_local >= MOBA_BLOCK, 1, 0)
        valid = blk_row < own
        for a in range(2):
            qa = jnp.where(head_of_row == a, qt_sb, jnp.zeros_like(qt_sb))
            gate = _dot(km_hi, qa) + _dot(km_lo, qa)
            g = jnp.where(valid, gate, -jnp.inf)
            sel = jnp.zeros((nbr, sb), jnp.bool_)
            for _ in range(min(MOBA_TOPK, nb - 1)):
                cand = inrange & jnp.logical_not(sel)
                gm = jnp.max(jnp.where(cand, g, -jnp.inf), axis=0, keepdims=True)
                hit = cand & (g == gm)
                first = jnp.min(jnp.where(hit, blk_row, nb), axis=0, keepdims=True)
                sel = sel | (blk_row == first)
            chosen = (sel & valid) | (blk_row == own)
            flags = jnp.where(chosen, 0.0, NEG_BIG).astype(BF16)
            qaug_ref[:, a * sb:(a + 1) * sb] = jnp.concatenate([qa, flags, flag_pad], axis=0)

        m_ref[...] = jnp.full(m_ref.shape, NEG_BIG, F32)
        l_ref[...] = jnp.zeros_like(l_ref)
        acc_ref[...] = jnp.zeros_like(acc_ref)
        scores_into(sa_ref, 0)

        def two_steps(t, carry):
            scores_into(sb_ref, 2 * t + 1)
            absorb(2 * t, sa_ref, False)
            scores_into(sa_ref, 2 * t + 2)
            absorb(2 * t + 1, sb_ref, False)
            return carry

        lax.fori_loop(0, i // 2, two_steps, 0)

        @pl.when(i % 2 == 1)
        def _():
            scores_into(sb_ref, i)
            absorb(i - 1, sa_ref, False)
            absorb(i, sb_ref, True)

        @pl.when(i % 2 == 0)
        def _():
            absorb(i, sa_ref, True)

        out_t = jnp.concatenate([acc_ref[a * hd:(a + 1) * hd, :] / l_ref[a:a + 1, :] for a in range(2)], axis=0)
        o_ref[pl.ds(qstart, sb), :] = out_t.T.astype(BF16)
        return carry

    lax.fori_loop(0, nsb, q_super, 0)


def _moba(qt, k, vt, batch):
    t_tokens = k.shape[0]
    seq = t_tokens // batch
    assert seq % MOBA_SUPER == 0 and seq // MOBA_BLOCK <= MOBA_BLOCK_ROWS
    spec = pl.BlockSpec((seq, LANES), lambda b, hp: (b, hp))
    spec_t = pl.BlockSpec((LANES, seq), lambda b, hp: (hp, b))
    return pl.pallas_call(
        functools.partial(_moba_kernel, seq=seq),
        grid=(batch, ATT_INNER // LANES),
        in_specs=[spec_t, spec, spec_t],
        out_specs=spec,
        out_shape=jax.ShapeDtypeStruct((t_tokens, ATT_INNER), BF16),
        scratch_shapes=[pltpu.VMEM((seq, 2 * LANES), BF16),
                        pltpu.VMEM((2 * LANES, 2 * MOBA_SUPER), BF16),
                        pltpu.VMEM((MOBA_SUPER, 2 * MOBA_SUPER), F32),
                        pltpu.VMEM((MOBA_SUPER, 2 * MOBA_SUPER), F32),
                        pltpu.VMEM((1, 2 * MOBA_SUPER), F32),
                        pltpu.VMEM((SUBLANES, MOBA_SUPER), F32),
                        pltpu.VMEM((LANES, MOBA_SUPER), F32)],
        compiler_params=pltpu.CompilerParams(dimension_semantics=("arbitrary", "arbitrary"),
                                             vmem_limit_bytes=VMEM_LIMIT),
        name="moba",
    )(qt, k, vt)


ROUTER_ROWS = 40


def _merge_kernel(x_ref, yn_ref, att_ref, gs_ref, ga_ref, wso_ref, wao_ref, wo_ref, nw_ref,
                  wr_hi_ref, wr_lo_ref, br_ref, x1_ref, h2_ref, route_ref):
    y_s = _dot(yn_ref[...], wso_ref[...])
    y_a = _dot(att_ref[...], wao_ref[...])
    merged = _sigmoid(gs_ref[...].astype(F32)) * y_s + _sigmoid(ga_ref[...].astype(F32)) * y_a
    x1 = x_ref[...] + _dot(merged.astype(BF16), wo_ref[...])
    x1_ref[...] = x1
    h2 = _rms(x1, nw_ref[...])
    h2_ref[...] = h2.astype(BF16)

    h_hi = h2.astype(BF16)
    h_lo = (h2 - h_hi.astype(F32)).astype(BF16)
    wr_hi = wr_hi_ref[...]
    logits = _dot_nt(wr_hi, h_hi) + _dot_nt(wr_hi, h_lo) + _dot_nt(wr_lo_ref[...], h_hi) + br_ref[...]
    gl = logits[0:N_GROUPS, :]
    gidx = lax.broadcasted_iota(jnp.int32, gl.shape, 0)
    gmax = jnp.max(gl, axis=0, keepdims=True)
    gi = jnp.min(jnp.where(gl == gmax, gidx, N_GROUPS), axis=0, keepdims=True)
    g_w = 1.0 / jnp.sum(jnp.exp(gl - gmax), axis=0, keepdims=True)
    el = jnp.zeros((EXPERTS_PER_GROUP, gl.shape[1]), F32)
    for g in range(N_GROUPS):
        lo = N_GROUPS + g * EXPERTS_PER_GROUP
        el = el + jnp.where(gi == g, logits[lo:lo + EXPERTS_PER_GROUP, :], 0.0)
    eidx = lax.broadcasted_iota(jnp.int32, el.shape, 0)
    v0 = jnp.max(el, axis=0, keepdims=True)
    i0 = jnp.min(jnp.where(el == v0, eidx, EXPERTS_PER_GROUP), axis=0, keepdims=True)
    rest = jnp.where(eidx == i0, -jnp.inf, el)
    v1 = jnp.max(rest, axis=0, keepdims=True)
    i1 = jnp.min(jnp.where((rest == v1) & (eidx != i0), eidx, EXPERTS_PER_GROUP), axis=0, keepdims=True)
    t = jnp.exp(v1 - v0)
    w0 = g_w / (1.0 + t)
    w1 = g_w * t / (1.0 + t)
    base = gi * EXPERTS_PER_GROUP
    route_ref[...] = jnp.concatenate(
        [(base + i0).astype(F32), (base + i1).astype(F32), w0, w1], axis=0)


def _merge(x2d, yn, att, gs, ga, wso, wao, wo, nw, wr_hi, wr_lo, br):
    t_tokens = x2d.shape[0]
    tm = min(TM_MERGE, t_tokens)
    row = lambda: pl.BlockSpec((tm, D_MODEL), lambda i: (i, 0))
    whole = pl.BlockSpec(memory_space=pltpu.VMEM)
    return pl.pallas_call(
        _merge_kernel,
        grid=(t_tokens // tm,),
        in_specs=[row(), row(), row(), row(), row(), whole, whole, whole, whole, whole, whole, whole],
        out_specs=[row(), row(), pl.BlockSpec((4, tm), lambda i: (0, i))],
        out_shape=[jax.ShapeDtypeStruct((t_tokens, D_MODEL), F32),
                   jax.ShapeDtypeStruct((t_tokens, D_MODEL), BF16),
                   jax.ShapeDtypeStruct((4, t_tokens), F32)],
        compiler_params=pltpu.CompilerParams(dimension_semantics=("arbitrary",), vmem_limit_bytes=VMEM_LIMIT),
        name="merge",
    )(x2d, yn, att, gs, ga, wso, wao, wo, nw, wr_hi, wr_lo, br)


def _rows_as_matrix(buf, n_rows):
    return jnp.concatenate([buf[pl.ds(s, n_rows, stride=ROW_TILE), :] for s in range(ROW_TILE)], axis=1)


def _rows(ref, first, n_rows):
    return ref.at[pl.ds(first * ROW_TILE, n_rows * ROW_TILE), :]


TM_TOK = 512
TILE_ASG = 2 * TM_TOK
RUN_BITS = 11
PAD_BITS = 8


def _for_each_piece(count, fn, bits=RUN_BITS):
    for bit in range(bits - 1, -1, -1):
        size = 1 << bit

        @pl.when((count & size) != 0)
        def _():
            fn((count >> (bit + 1)) << (bit + 1), size)


def _moe_plan(route, max_tiles):
    t_tokens = route.shape[1]
    ntok = t_tokens // TM_TOK
    experts = jnp.arange(N_EXPERTS, dtype=jnp.int32)
    e_loc = route[0:2].astype(jnp.int32).reshape(2, ntok, TM_TOK).transpose(1, 0, 2).reshape(ntok, TILE_ASG)
    onehot = (e_loc[:, :, None] == experts[None, None, :]).astype(jnp.int32)
    csum = jnp.cumsum(onehot, axis=1)
    lrank = jnp.sum(csum * onehot, axis=2) - 1
    cnt = csum[:, -1, :]
    off = jnp.cumsum(cnt, axis=1) - cnt
    counts = jnp.sum(cnt, axis=0)
    padded = ((counts + TM_MOE - 1) // TM_MOE) * TM_MOE
    ends = jnp.cumsum(padded)
    starts = ends - padded
    base = starts[None, :] + jnp.cumsum(cnt, axis=0) - cnt
    jloc = jnp.sum(onehot * off[:, None, :], axis=2) + lrank
    jloc = jloc.reshape(ntok, 2, TM_TOK).transpose(1, 0, 2).reshape(2, t_tokens)
    tile_start = jnp.arange(max_tiles, dtype=jnp.int32) * TM_MOE
    tile_expert = jnp.sum((tile_start[:, None] >= ends[None, :]).astype(jnp.int32), axis=1)
    tile_expert = jnp.minimum(tile_expert, N_EXPERTS - 1)
    n_tiles = (ends[-1] // TM_MOE).astype(jnp.int32).reshape(1)
    flat = lambda a: a.reshape(-1).astype(jnp.int32)
    return flat(cnt), flat(off), flat(base), jloc, starts + counts, ends, tile_expert, n_tiles


def _dispatch_kernel(cnt_ref, off_ref, base_ref, padlo_ref, padhi_ref, ntile_ref, h2_ref, jrow_ref, xs_hbm,
                     xbuf0, xbuf1, zeros_ref, sems, zsem, *, max_tiles):
    i = pl.program_id(0)
    n = pl.num_programs(0)
    xbufs = (xbuf0, xbuf1)

    def runs(tile, slot, wait):
        def per_expert(e, carry):
            k = tile * N_EXPERTS + e
            src, dst = off_ref[k], base_ref[k]

            def piece(o, size):
                cp = pltpu.make_async_copy(_rows(xbufs[slot], src + o, size), _rows(xs_hbm, dst + o, size),
                                           sems.at[slot])
                cp.wait() if wait else cp.start()
            _for_each_piece(cnt_ref[k], piece)
            return carry
        lax.fori_loop(0, N_EXPERTS, per_expert, 0)

    @pl.when(i == 0)
    def _():
        zeros_ref[...] = jnp.zeros_like(zeros_ref)
        nt = ntile_ref[0]
        for wait in (False, True):
            def pad(e, carry):
                lo = padlo_ref[e]

                def piece(o, size):
                    cp = pltpu.make_async_copy(_rows(zeros_ref, 0, size), _rows(xs_hbm, lo + o, size), zsem)
                    cp.wait() if wait else cp.start()
                _for_each_piece(padhi_ref[e] - lo, piece, bits=PAD_BITS)
                return carry
            lax.fori_loop(0, N_EXPERTS, pad, 0)

            def spare(tile, carry):
                cp = pltpu.make_async_copy(zeros_ref, _rows(xs_hbm, tile * TM_MOE, TM_MOE), zsem)
                cp.wait() if wait else cp.start()
                return carry
            lax.fori_loop(nt, max_tiles, spare, 0)

    for slot in range(2):
        @pl.when(i % 2 == slot)
        def _():
            @pl.when(i >= 2)
            def _():
                runs(i - 2, slot, True)
            row = lax.broadcasted_iota(jnp.int32, (TILE_ASG, TM_TOK), 0)
            j = jrow_ref[...]
            perm = ((row == j[0:1, :]) | (row == j[1:2, :])).astype(BF16)
            _store_rows_as_tiles(xbufs[slot], _dot(perm, h2_ref[...]))
            runs(i, slot, False)

            @pl.when(i == n - 1)
            def _():
                @pl.when(i >= 1)
                def _():
                    runs(i - 1, 1 - slot, True)
                runs(i, slot, True)


def _dispatch(plan, h2, max_tiles):
    cnt, off, base, jloc, pad_lo, pad_hi, _, n_tiles = plan
    t_tokens = h2.shape[0]
    grid_spec = pltpu.PrefetchScalarGridSpec(
        num_scalar_prefetch=6,
        grid=(t_tokens // TM_TOK,),
        in_specs=[pl.BlockSpec((TM_TOK, D_MODEL), lambda i, *_: (i, 0)),
                  pl.BlockSpec((2, TM_TOK), lambda i, *_: (0, i))],
        out_specs=pl.BlockSpec(memory_space=pl.ANY),
        scratch_shapes=[pltpu.VMEM((TILE_ASG * ROW_TILE, LANES), F32),
                        pltpu.VMEM((TILE_ASG * ROW_TILE, LANES), F32),
                        pltpu.VMEM((TM_MOE * ROW_TILE, LANES), F32),
                        pltpu.SemaphoreType.DMA((2,)), pltpu.SemaphoreType.DMA(())],
    )
    return pl.pallas_call(
        functools.partial(_dispatch_kernel, max_tiles=max_tiles),
        grid_spec=grid_spec,
        out_shape=jax.ShapeDtypeStruct((max_tiles * TM_MOE * ROW_TILE, LANES), F32),
        compiler_params=pltpu.CompilerParams(dimension_semantics=("arbitrary",), vmem_limit_bytes=VMEM_LIMIT),
        name="dispatch",
    )(cnt, off, base, pad_lo, pad_hi, n_tiles, h2, jloc)


def _moe_kernel(texp_ref, ntile_ref, xs_ref, wgu_ref, wd_ref, y_ref):
    t = pl.program_id(0)

    @pl.when(t < ntile_ref[0])
    def _():
        xt = _rows_as_matrix(xs_ref, TM_MOE).astype(BF16)
        gu = _dot(xt, wgu_ref[0])
        hid = (_silu(gu[:, :D_EXPERT]) * gu[:, D_EXPERT:]).astype(BF16)
        _store_rows_as_tiles(y_ref, _dot(hid, wd_ref[0]))

    @pl.when(t >= ntile_ref[0])
    def _():
        y_ref[...] = jnp.zeros_like(y_ref)


def _moe(xs_tiles, tile_expert, n_tiles, wgu, wd, max_tiles):
    live = lambda t, nt: jnp.minimum(t, nt[0] - 1)
    tile = lambda index: pl.BlockSpec((TM_MOE * ROW_TILE, LANES), index)
    grid_spec = pltpu.PrefetchScalarGridSpec(
        num_scalar_prefetch=2,
        grid=(max_tiles,),
        in_specs=[tile(lambda t, te, nt: (live(t, nt), 0)),
                  pl.BlockSpec((1, D_MODEL, 2 * D_EXPERT), lambda t, te, nt: (te[live(t, nt)], 0, 0)),
                  pl.BlockSpec((1, D_EXPERT, D_MODEL), lambda t, te, nt: (te[live(t, nt)], 0, 0))],
        out_specs=tile(lambda t, te, nt: (t, 0)),
    )
    return pl.pallas_call(
        _moe_kernel,
        grid_spec=grid_spec,
        out_shape=jax.ShapeDtypeStruct((max_tiles * TM_MOE * ROW_TILE, LANES), F32),
        compiler_params=pltpu.CompilerParams(dimension_semantics=("arbitrary",), vmem_limit_bytes=VMEM_LIMIT),
        name="moe",
    )(tile_expert, n_tiles, xs_tiles, wgu, wd)


def _tail_kernel(cnt_ref, off_ref, base_ref, x1_ref, p_ref, jw_ref, y_hbm, pnw_ref, wpg_ref, wple_ref, fnw_ref,
                 o_ref, ybuf0, ybuf1, sems):
    i = pl.program_id(0)
    n = pl.num_programs(0)
    ybufs = (ybuf0, ybuf1)

    def runs(tile, slot, wait):
        def per_expert(e, carry):
            k = tile * N_EXPERTS + e
            src, dst = base_ref[k], off_ref[k]

            def piece(o, size):
                cp = pltpu.make_async_copy(_rows(y_hbm, src + o, size), _rows(ybufs[slot], dst + o, size),
                                           sems.at[slot])
                cp.wait() if wait else cp.start()
            _for_each_piece(cnt_ref[k], piece)
            return carry
        lax.fori_loop(0, N_EXPERTS, per_expert, 0)

    @pl.when(i == 0)
    def _():
        runs(0, 0, False)

    for slot in range(2):
        @pl.when(i % 2 == slot)
        def _():
            @pl.when(i + 1 < n)
            def _():
                runs(i + 1, 1 - slot, False)
            runs(i, slot, True)
            y_loc = _rows_as_matrix(ybufs[slot], TILE_ASG).astype(BF16)
            jw = jw_ref[...]
            col = lax.broadcasted_iota(jnp.int32, (TM_TOK, TILE_ASG), 1)
            moe = jnp.zeros((TM_TOK, D_MODEL), F32)
            for s in range(2):
                pick = (col == jw[:, s:s + 1].astype(jnp.int32)).astype(BF16)
                moe = moe + jw[:, 2 + s:3 + s] * _dot(pick, y_loc)
            x2 = x1_ref[...] + moe
            gate = _sigmoid(_dot(_rms(x2, pnw_ref[...]).astype(BF16), wpg_ref[...]))
            x3 = x2 + gate * _dot(p_ref[...].astype(BF16), wple_ref[...])
            o_ref[...] = _rms(x3, fnw_ref[...])


def _tail(plan, x1, p2d, y_tiles, jw, pnw, wpg, wple, fnw):
    cnt, off, base = plan[0:3]
    t_tokens = x1.shape[0]
    whole = pl.BlockSpec(memory_space=pltpu.VMEM)
    grid_spec = pltpu.PrefetchScalarGridSpec(
        num_scalar_prefetch=3,
        grid=(t_tokens // TM_TOK,),
        in_specs=[pl.BlockSpec((TM_TOK, D_MODEL), lambda i, *_: (i, 0)),
                  pl.BlockSpec((TM_TOK, PLE_DIM), lambda i, *_: (i, 0)),
                  pl.BlockSpec((TM_TOK, 4), lambda i, *_: (i, 0)),
                  pl.BlockSpec(memory_space=pl.ANY),
                  whole, whole, whole, whole],
        out_specs=pl.BlockSpec((TM_TOK, D_MODEL), lambda i, *_: (i, 0)),
        scratch_shapes=[pltpu.VMEM((TILE_ASG * ROW_TILE, LANES), F32),
                        pltpu.VMEM((TILE_ASG * ROW_TILE, LANES), F32),
                        pltpu.SemaphoreType.DMA((2,))],
    )
    return pl.pallas_call(
        _tail_kernel,
        grid_spec=grid_spec,
        out_shape=jax.ShapeDtypeStruct((t_tokens, D_MODEL), F32),
        compiler_params=pltpu.CompilerParams(dimension_semantics=("arbitrary",), vmem_limit_bytes=VMEM_LIMIT),
        name="tail",
    )(cnt, off, base, x1, p2d, jw, y_tiles, pnw, wpg, wple, fnw)


def _rope_tables(positions):
    half = ATT_HEAD_DIM // 2
    inv_freq = ROPE_THETA ** (-jnp.arange(half, dtype=F32) / half)
    ang = positions.astype(F32)[..., None] * inv_freq
    cos = jnp.cos(ang).reshape(-1, half)
    sin = jnp.sin(ang).reshape(-1, half)
    reps = LANES // ATT_HEAD_DIM
    cosf = jnp.tile(jnp.concatenate([cos, cos], axis=1), (1, reps))
    sinf = jnp.tile(jnp.concatenate([-sin, sin], axis=1), (1, reps))
    return cosf, sinf, cos.T, sin.T


def _layer(x2d, p2d, rope_tables, batch, attn_norm_w, w_in, conv_w, conv_b, dt_bias, a_log, d_skip, ssd_norm_w,
           w_ssd_out, w_attn_out, w_out, moe_norm_w, w_rg, b_rg, w_re, b_re, w_gate, w_up, w_down,
           ple_norm_w, w_ple, w_ple_gate, final_norm_w):
    t_tokens = x2d.shape[0]
    offs = np.cumsum((0,) + IN_SPLITS)
    cols = lambda i: w_in[:, offs[i]:offs[i + 1]]
    w_big = jnp.concatenate([cols(0), cols(1), cols(4), cols(6), cols(7)], axis=1).astype(BF16)
    w_t = jnp.concatenate([cols(3), cols(5), cols(2)], axis=1).T.astype(BF16)
    w_dt = jnp.pad(cols(2), ((0, 0), (0, LANES - SSD_HEADS))).astype(BF16)

    z, xbc, k, gs, ga, dt, qt, vt, dtt = _in_proj(x2d, attn_norm_w[None, :], *rope_tables, w_big, w_t, w_dt)

    yn = _ssd(z, xbc, dt, dtt, conv_w, conv_b, dt_bias, a_log, d_skip, ssd_norm_w, batch)

    att = _moba(qt, k, vt, batch)

    wr = jnp.concatenate([w_rg, w_re], axis=1).T
    wr = jnp.pad(wr, ((0, ROUTER_ROWS - wr.shape[0]), (0, 0)))
    wr_hi = wr.astype(BF16)
    wr_lo = (wr - wr_hi.astype(F32)).astype(BF16)
    br = jnp.pad(jnp.concatenate([b_rg, b_re]), (0, ROUTER_ROWS - N_GROUPS - N_EXPERTS))[:, None]
    x1, h2, route = _merge(x2d, yn, att, gs, ga, w_ssd_out.astype(BF16), w_attn_out.astype(BF16),
                           w_out.astype(BF16), moe_norm_w[None, :], wr_hi, wr_lo, br)

    max_tiles = (2 * t_tokens) // TM_MOE + N_EXPERTS
    plan = _moe_plan(route, max_tiles)
    wgu = jnp.concatenate([w_gate, w_up], axis=-1).reshape(N_EXPERTS, D_MODEL, 2 * D_EXPERT).astype(BF16)
    wd = w_down.reshape(N_EXPERTS, D_EXPERT, D_MODEL).astype(BF16)
    x_sorted = _dispatch(plan, h2, max_tiles)
    y_sorted = _moe(x_sorted, plan[6], plan[7], wgu, wd, max_tiles)
    jw = jnp.concatenate([plan[3].astype(F32), route[2:4]], axis=0).T

    return _tail(plan, x1, p2d, y_sorted, jw, ple_norm_w[None, :],
                 w_ple_gate.astype(BF16), w_ple.astype(BF16), final_norm_w[None, :])


def kernel(x, p, positions, attn_norm_w, w_in, conv_w, conv_b, dt_bias, a_log, d_skip, ssd_norm_w, w_ssd_out,
           w_attn_out, w_out, moe_norm_w, w_router_group, b_router_group, w_router_expert, b_router_expert,
           w_exp_gate, w_exp_up, w_exp_down, ple_norm_w, w_ple, w_ple_gate, final_norm_w):
    batch, seq, d = x.shape
    depth = p.shape[0]
    assert depth == 1, "the final norm is fused into the last layer's tail kernel"
    rope_tables = _rope_tables(positions)
    i = 0
    out = _layer(x.reshape(batch * seq, d), p[i].reshape(batch * seq, PLE_DIM), rope_tables, batch,
                 attn_norm_w[i], w_in[i], conv_w[i], conv_b[i], dt_bias[i], a_log[i], d_skip[i], ssd_norm_w[i],
                 w_ssd_out[i], w_attn_out[i], w_out[i], moe_norm_w[i], w_router_group[i], b_router_group[i],
                 w_router_expert[i], b_router_expert[i], w_exp_gate[i], w_exp_up[i], w_exp_down[i],
                 ple_norm_w[i], w_ple[i], w_ple_gate[i], final_norm_w)
    return out.reshape(batch, seq, d)
```

```python
import functools

import jax
import jax.numpy as jnp
import numpy as np
from jax import lax
from jax.experimental import pallas as pl
from jax.experimental.pallas import tpu as pltpu

F32 = jnp.float32
BF16 = jnp.bfloat16

EPS = 1e-6
D_MODEL = 1024
SSD_HEADS = 16
SSD_HEAD_DIM = 64
SSD_INNER = SSD_HEADS * SSD_HEAD_DIM
SSD_GROUPS = 2
SSD_STATE = 64
SSD_CONV = 4
SSD_CHUNK = 128
SSD_CONV_CH = SSD_INNER + 2 * SSD_GROUPS * SSD_STATE
ATT_HEADS = 16
ATT_HEAD_DIM = 64
ATT_INNER = ATT_HEADS * ATT_HEAD_DIM
MOBA_BLOCK = 256
MOBA_TOPK = 3
ROPE_THETA = 10000.0
IN_SPLITS = (SSD_INNER, SSD_CONV_CH, SSD_HEADS, ATT_INNER, ATT_INNER, ATT_INNER, D_MODEL, D_MODEL)
N_GROUPS = 4
EXPERTS_PER_GROUP = 8
N_EXPERTS = N_GROUPS * EXPERTS_PER_GROUP
D_EXPERT = 256
PLE_DIM = 256

LANES = 128
SUBLANES = 8
NEG_BIG = -1e30
VMEM_LIMIT = 56 * 1024 * 1024

TM_PROJ = 512
TM_MERGE = 512
TM_MOE = 256


def _dot(a, b):
    return jnp.dot(a, b, preferred_element_type=F32)


def _dot_nt(a, b):
    return lax.dot_general(a, b, (((1,), (1,)), ((), ())), preferred_element_type=F32)


def _dot_tn(a, b):
    return lax.dot_general(a, b, (((0,), (0,)), ((), ())), preferred_element_type=F32)


def _split3(a):
    hi = a.astype(BF16)
    r1 = a - hi.astype(F32)
    mid = r1.astype(BF16)
    lo = (r1 - mid.astype(F32)).astype(BF16)
    return hi, mid, lo


def _dot_exact_rhs(sel_bf16, a):
    hi, mid, lo = _split3(a)
    return _dot(sel_bf16, hi) + _dot(sel_bf16, mid) + _dot(sel_bf16, lo)


def _dot_exact_lhs(a, sel_bf16):
    hi, mid, lo = _split3(a)
    return _dot(hi, sel_bf16) + _dot(mid, sel_bf16) + _dot(lo, sel_bf16)


def _sigmoid(x):
    return 1.0 / (1.0 + jnp.exp(-x))


def _silu(x):
    return x * _sigmoid(x)


def _softplus(x):
    return jnp.maximum(x, 0.0) + jnp.log1p(jnp.exp(-jnp.abs(x)))


def _rms(x, w):
    inv = lax.rsqrt(jnp.mean(x * x, axis=-1, keepdims=True) + EPS)
    return (x * inv) * w


ROW_TILE = D_MODEL // LANES


def _store_rows_as_tiles(ref, mat):
    rows = mat.shape[0]
    for s in range(ROW_TILE):
        ref[pl.ds(s, rows, stride=ROW_TILE), :] = mat[:, s * LANES:(s + 1) * LANES]


def _rope_tile(t, cos, sin_signed):
    half = ATT_HEAD_DIM // 2
    lane = lax.broadcasted_iota(jnp.int32, t.shape, 1)
    first = (lane % ATT_HEAD_DIM) < half
    partner = jnp.where(first, pltpu.roll(t, LANES - half, 1), pltpu.roll(t, half, 1))
    return t * cos + partner * sin_signed


LOG2E = 1.4426950408889634
Q_SCALE = ATT_HEAD_DIM ** -0.5 * LOG2E


def _in_proj_kernel(x_ref, nw_ref, cos_ref, sin_ref, cost_ref, sint_ref, w_ref, wt_ref, wdt_ref,
                    z_ref, xbc_ref, k_ref, gs_ref, ga_ref, dt_ref, qt_ref, vt_ref, dtt_ref):
    h = _rms(x_ref[...], nw_ref[...]).astype(BF16)
    cos = cos_ref[...]
    sin = sin_ref[...]

    def proj(lo, width):
        return _dot(h, w_ref[:, lo:lo + width])

    o = 0
    z_ref[...] = proj(o, SSD_INNER).astype(BF16)
    o += SSD_INNER
    xbc_ref[...] = proj(o, SSD_CONV_CH).astype(BF16)
    o += SSD_CONV_CH
    for g in range(ATT_INNER // LANES):
        t = proj(o + g * LANES, LANES)
        k_ref[:, g * LANES:(g + 1) * LANES] = _rope_tile(t, cos, sin).astype(BF16)
    o += ATT_INNER
    for ref in (gs_ref, ga_ref):
        ref[...] = proj(o, D_MODEL).astype(BF16)
        o += D_MODEL
    dt_ref[...] = _dot(h, wdt_ref[...])[:, :SSD_HEADS]

    dtt_ref[...] = _dot_nt(wt_ref[2 * ATT_INNER:2 * ATT_INNER + SSD_HEADS, :], h)
    vt_ref[...] = _dot_nt(wt_ref[ATT_INNER:2 * ATT_INNER, :], h).astype(BF16)
    qt = _dot_nt(wt_ref[0:ATT_INNER, :], h)
    cost = cost_ref[...]
    sint = sint_ref[...]
    half = ATT_HEAD_DIM // 2
    for hd in range(ATT_HEADS):
        r0 = hd * ATT_HEAD_DIM
        x1 = qt[r0:r0 + half, :]
        x2 = qt[r0 + half:r0 + ATT_HEAD_DIM, :]
        qt_ref[r0:r0 + half, :] = ((x1 * cost - x2 * sint) * Q_SCALE).astype(BF16)
        qt_ref[r0 + half:r0 + ATT_HEAD_DIM, :] = ((x2 * cost + x1 * sint) * Q_SCALE).astype(BF16)


def _in_proj(x2d, nw, cosf, sinf, cost, sint, w_big, w_t, w_dt):
    t_tokens = x2d.shape[0]
    tm = min(TM_PROJ, t_tokens)
    grid = (t_tokens // tm,)
    row = lambda width: pl.BlockSpec((tm, width), lambda i: (i, 0))
    col = lambda height: pl.BlockSpec((height, tm), lambda i: (0, i))
    whole = pl.BlockSpec(memory_space=pltpu.VMEM)
    out_shapes = [
        jax.ShapeDtypeStruct((t_tokens, SSD_INNER), BF16),
        jax.ShapeDtypeStruct((t_tokens, SSD_CONV_CH), BF16),
        jax.ShapeDtypeStruct((t_tokens, ATT_INNER), BF16),
        jax.ShapeDtypeStruct((t_tokens, D_MODEL), BF16),
        jax.ShapeDtypeStruct((t_tokens, D_MODEL), BF16),
        jax.ShapeDtypeStruct((t_tokens, SSD_HEADS), F32),
        jax.ShapeDtypeStruct((ATT_INNER, t_tokens), BF16),
        jax.ShapeDtypeStruct((ATT_INNER, t_tokens), BF16),
        jax.ShapeDtypeStruct((SSD_HEADS, t_tokens), F32),
    ]
    out_specs = [row(SSD_INNER), row(SSD_CONV_CH), row(ATT_INNER), row(D_MODEL), row(D_MODEL), row(SSD_HEADS),
                 col(ATT_INNER), col(ATT_INNER), col(SSD_HEADS)]
    return pl.pallas_call(
        _in_proj_kernel,
        grid=grid,
        in_specs=[row(D_MODEL), whole, row(LANES), row(LANES), col(ATT_HEAD_DIM // 2), col(ATT_HEAD_DIM // 2),
                  whole, whole, whole],
        out_specs=out_specs,
        out_shape=out_shapes,
        compiler_params=pltpu.CompilerParams(dimension_semantics=("arbitrary",), vmem_limit_bytes=VMEM_LIMIT),
        name="in_proj",
    )(x2d, nw, cosf, sinf, cost, sint, w_big, w_t, w_dt)


def _ssd_kernel(z_ref, xbc_ref, dt_ref, dtt_ref, cw_ref, cb_ref, dtb_ref, dtbc_ref, alog_ref, alogc_ref,
                dexp_ref, nw_ref, out_ref, state_ref, ubuf_ref):
    c = pl.program_id(1)
    cl = SSD_CHUNK
    heads_per_group = SSD_HEADS // SSD_GROUPS
    gw = heads_per_group * SSD_HEAD_DIM

    @pl.when(c == 0)
    def _():
        state_ref[...] = jnp.zeros_like(state_ref)
        ubuf_ref[0:SUBLANES, :] = jnp.zeros((SUBLANES, SSD_CONV_CH), F32)

    u = xbc_ref[...].astype(F32)
    ubuf_ref[SUBLANES:SUBLANES + cl, :] = u
    acc = cb_ref[...] + cw_ref[SSD_CONV - 1:SSD_CONV, :] * u
    for k in range(SSD_CONV - 1):
        shift = SSD_CONV - 1 - k
        acc = acc + cw_ref[k:k + 1, :] * ubuf_ref[pl.ds(SUBLANES - shift, cl), :]
    ubuf_ref[0:SUBLANES, :] = u[cl - SUBLANES:cl, :]
    xc = _silu(acc)
    xs = xc[:, :SSD_INNER]
    bm = xc[:, SSD_INNER:SSD_INNER + SSD_GROUPS * SSD_STATE].astype(BF16)
    cm = xc[:, SSD_INNER + SSD_GROUPS * SSD_STATE:].astype(BF16)

    dt = _softplus(dt_ref[...] + dtb_ref[...])
    da = dt * (-jnp.exp(alog_ref[...]))
    dtt = _softplus(dtt_ref[...] + dtbc_ref[...])
    dat = dtt * (-jnp.exp(alogc_ref[...]))
    ri = lax.broadcasted_iota(jnp.int32, (cl, cl), 0)
    ci = lax.broadcasted_iota(jnp.int32, (cl, cl), 1)
    causal = ri >= ci
    tril = causal.astype(BF16)
    triu = (ri <= ci).astype(BF16)
    acum = _dot_exact_rhs(tril, da)
    acumt = _dot_exact_lhs(dat, triu)

    hrow = lax.broadcasted_iota(jnp.int32, (SSD_HEADS, SSD_INNER), 0)
    hcol = lax.broadcasted_iota(jnp.int32, (SSD_HEADS, SSD_INNER), 1) // SSD_HEAD_DIM
    expand = (hrow == hcol).astype(BF16)

    a_last = acum[cl - 1:cl, :]
    dt_e = _dot_exact_lhs(dt, expand)
    dec_e = _dot_exact_lhs(jnp.exp(a_last - acum), expand)
    ea_e = _dot_exact_lhs(jnp.exp(acum), expand)
    xdt = xs * dt_e
    xdt_b = xdt.astype(BF16)
    xdtdec_b = (xdt * dec_e).astype(BF16)
    chunk_decay_t = jnp.exp(acumt[:, cl - 1:cl])

    y_groups = []
    for g in range(SSD_GROUPS):
        b_g = bm[:, g * SSD_STATE:(g + 1) * SSD_STATE]
        c_g = cm[:, g * SSD_STATE:(g + 1) * SSD_STATE]
        prev = state_ref[g * gw:(g + 1) * gw, :]
        cb = _dot_nt(c_g, b_g)
        y_off = _dot_nt(c_g, prev.astype(BF16)) * ea_e[:, g * gw:(g + 1) * gw]
        new_states = _dot_tn(xdtdec_b[:, g * gw:(g + 1) * gw], b_g)
        y_heads = []
        decay_rows = []
        for e in range(heads_per_group):
            hd = g * heads_per_group + e
            lmat = jnp.where(causal, jnp.exp(acum[:, hd:hd + 1] - acumt[hd:hd + 1, :]), 0.0)
            m = (cb * lmat).astype(BF16)
            y_heads.append(_dot(m, xdt_b[:, hd * SSD_HEAD_DIM:(hd + 1) * SSD_HEAD_DIM]))
            decay_rows.append(jnp.broadcast_to(chunk_decay_t[hd:hd + 1, :], (SSD_HEAD_DIM, SSD_STATE)))
        y_groups.append(jnp.concatenate(y_heads, axis=1) + y_off)
        state_ref[g * gw:(g + 1) * gw, :] = prev * jnp.concatenate(decay_rows, axis=0) + new_states

    y = jnp.concatenate(y_groups, axis=1) + xs * dexp_ref[...]
    yg = y * _silu(z_ref[...].astype(F32))
    nw = nw_ref[...]
    outs = [_rms(yg[:, g * gw:(g + 1) * gw], nw[:, g * gw:(g + 1) * gw]) for g in range(SSD_GROUPS)]
    out_ref[...] = jnp.concatenate(outs, axis=1).astype(BF16)


def _ssd(z, xbc, dt, dtt, conv_w, conv_b, dt_bias, a_log, d_skip, norm_w, batch):
    t_tokens = z.shape[0]
    nc = t_tokens // batch // SSD_CHUNK
    tok = lambda width: pl.BlockSpec((SSD_CHUNK, width), lambda b, c: (b * nc + c, 0))
    whole = pl.BlockSpec(memory_space=pltpu.VMEM)
    dexp = jnp.repeat(d_skip, SSD_HEAD_DIM)[None, :]
    return pl.pallas_call(
        _ssd_kernel,
        grid=(batch, nc),
        in_specs=[tok(SSD_INNER), tok(SSD_CONV_CH), tok(SSD_HEADS),
                  pl.BlockSpec((SSD_HEADS, SSD_CHUNK), lambda b, c: (0, b * nc + c)),
                  whole, whole, whole, whole, whole, whole, whole, whole],
        out_specs=tok(SSD_INNER),
        out_shape=jax.ShapeDtypeStruct((t_tokens, SSD_INNER), BF16),
        scratch_shapes=[pltpu.VMEM((SSD_INNER, SSD_STATE), F32),
                        pltpu.VMEM((SUBLANES + SSD_CHUNK, SSD_CONV_CH), F32)],
        compiler_params=pltpu.CompilerParams(dimension_semantics=("arbitrary", "arbitrary"),
                                             vmem_limit_bytes=VMEM_LIMIT),
        name="ssd",
    )(z, xbc, dt, dtt, conv_w, conv_b[None, :], dt_bias[None, :], dt_bias[:, None],
      a_log[None, :], a_log[:, None], dexp, norm_w[None, :])


MOBA_SUPER = 2 * MOBA_BLOCK


MOBA_BLOCK_ROWS = 16
ONES_ROWS = 16


def _moba_kernel(qt_ref, k_ref, vt_ref, o_ref, kfull_ref, qaug_ref, sa_ref, sb_ref, m_ref, l_ref, acc_ref, *, seq):
    nb = seq // MOBA_BLOCK
    sb = MOBA_SUPER
    nsb = seq // sb
    hd = ATT_HEAD_DIM
    nbr = MOBA_BLOCK_ROWS

    for c in range(nsb):
        rows = slice(c * sb, (c + 1) * sb)
        kfull_ref[rows, 0:LANES] = k_ref[rows, :]
        rblk = lax.broadcasted_iota(jnp.int32, (sb, LANES), 0) // MOBA_BLOCK + c * (sb // MOBA_BLOCK)
        rlane = lax.broadcasted_iota(jnp.int32, (sb, LANES), 1)
        kfull_ref[rows, LANES:2 * LANES] = jnp.where(rblk == rlane, 1.0, 0.0).astype(BF16)

    pr = lax.broadcasted_iota(jnp.int32, (nbr, seq), 0)
    pc = lax.broadcasted_iota(jnp.int32, (nbr, seq), 1) // MOBA_BLOCK
    pool = jnp.where(pr == pc, 1.0 / MOBA_BLOCK, 0.0).astype(BF16)
    kmean = _dot(pool, k_ref[...])
    km_hi = kmean.astype(BF16)
    km_lo = (kmean - km_hi.astype(F32)).astype(BF16)

    head_of_row = lax.broadcasted_iota(jnp.int32, (LANES, sb), 0) // hd
    blk_row = lax.broadcasted_iota(jnp.int32, (nbr, sb), 0)
    q_local = lax.broadcasted_iota(jnp.int32, (nbr, sb), 1)
    inrange = blk_row < nb
    krow = lax.broadcasted_iota(jnp.int32, (sb, 2 * sb), 0)
    qcol = lax.broadcasted_iota(jnp.int32, (sb, 2 * sb), 1) % sb
    causal = krow <= qcol
    ones_rows = jnp.ones((ONES_ROWS, sb), BF16)
    flag_pad = jnp.zeros((LANES - nbr, sb), BF16)

    def scores_into(s_ref, kv):
        start = pl.multiple_of(kv * sb, sb)
        s_ref[...] = _dot(kfull_ref[pl.ds(start, sb), :], qaug_ref[...])

    def absorb(kv, s_ref, diagonal):
        start = pl.multiple_of(kv * sb, sb)
        s_t = s_ref[...]
        if diagonal:
            s_t = jnp.where(causal, s_t, NEG_BIG)
        m_prev = m_ref[...]
        m_new = jnp.maximum(m_prev, jnp.max(s_t, axis=0, keepdims=True))
        m_ref[...] = m_new
        p_b = jnp.exp2(s_t - m_new).astype(BF16)
        alpha = jnp.exp2(m_prev - m_new)
        for a in range(2):
            cols = slice(a * sb, (a + 1) * sb)
            rows = slice(a * hd, (a + 1) * hd)
            v_aug = jnp.concatenate([vt_ref[rows, pl.ds(start, sb)], ones_rows], axis=0)
            r = _dot(v_aug, p_b[:, cols])
            l_ref[a:a + 1, :] = alpha[:, cols] * l_ref[a:a + 1, :] + r[hd:hd + 1, :]
            acc_ref[rows, :] = alpha[:, cols] * acc_ref[rows, :] + r[0:hd, :]

    def q_super(i, carry):
        qstart = pl.multiple_of(i * sb, sb)
        qt_sb = qt_ref[:, pl.ds(qstart, sb)]
        own = 2 * i + jnp.where(q_local >= MOBA_BLOCK, 1, 0)
        valid = blk_row < own
        for a in range(2):
            qa = jnp.where(head_of_row == a, qt_sb, jnp.zeros_like(qt_sb))
            gate = _dot(km_hi, qa) + _dot(km_lo, qa)
            g = jnp.where(valid, gate, -jnp.inf)
            sel = jnp.zeros((nbr, sb), jnp.bool_)
            for _ in range(min(MOBA_TOPK, nb - 1)):
                cand = inrange & jnp.logical_not(sel)
                gm = jnp.max(jnp.where(cand, g, -jnp.inf), axis=0, keepdims=True)
                hit = cand & (g == gm)
                first = jnp.min(jnp.where(hit, blk_row, nb), axis=0, keepdims=True)
                sel = sel | (blk_row == first)
            chosen = (sel & valid) | (blk_row == own)
            flags = jnp.where(chosen, 0.0, NEG_BIG).astype(BF16)
            qaug_ref[:, a * sb:(a + 1) * sb] = jnp.concatenate([qa, flags, flag_pad], axis=0)

        m_ref[...] = jnp.full(m_ref.shape, NEG_BIG, F32)
        l_ref[...] = jnp.zeros_like(l_ref)
        acc_ref[...] = jnp.zeros_like(acc_ref)
        scores_into(sa_ref, 0)

        def two_steps(t, carry):
            scores_into(sb_ref, 2 * t + 1)
            absorb(2 * t, sa_ref, False)
            scores_into(sa_ref, 2 * t + 2)
            absorb(2 * t + 1, sb_ref, False)
            return carry

        lax.fori_loop(0, i // 2, two_steps, 0)

        @pl.when(i % 2 == 1)
        def _():
            scores_into(sb_ref, i)
            absorb(i - 1, sa_ref, False)
            absorb(i, sb_ref, True)

        @pl.when(i % 2 == 0)
        def _():
            absorb(i, sa_ref, True)

        out_t = jnp.concatenate([acc_ref[a * hd:(a + 1) * hd, :] / l_ref[a:a + 1, :] for a in range(2)], axis=0)
        o_ref[pl.ds(qstart, sb), :] = out_t.T.astype(BF16)
        return carry

    lax.fori_loop(0, nsb, q_super, 0)


def _moba(qt, k, vt, batch):
    t_tokens = k.shape[0]
    seq = t_tokens // batch
    assert seq % MOBA_SUPER == 0 and seq // MOBA_BLOCK <= MOBA_BLOCK_ROWS
    spec = pl.BlockSpec((seq, LANES), lambda b, hp: (b, hp))
    spec_t = pl.BlockSpec((LANES, seq), lambda b, hp: (hp, b))
    return pl.pallas_call(
        functools.partial(_moba_kernel, seq=seq),
        grid=(batch, ATT_INNER // LANES),
        in_specs=[spec_t, spec, spec_t],
        out_specs=spec,
        out_shape=jax.ShapeDtypeStruct((t_tokens, ATT_INNER), BF16),
        scratch_shapes=[pltpu.VMEM((seq, 2 * LANES), BF16),
                        pltpu.VMEM((2 * LANES, 2 * MOBA_SUPER), BF16),
                        pltpu.VMEM((MOBA_SUPER, 2 * MOBA_SUPER), F32),
                        pltpu.VMEM((MOBA_SUPER, 2 * MOBA_SUPER), F32),
                        pltpu.VMEM((1, 2 * MOBA_SUPER), F32),
                        pltpu.VMEM((SUBLANES, MOBA_SUPER), F32),
                        pltpu.VMEM((LANES, MOBA_SUPER), F32)],
        compiler_params=pltpu.CompilerParams(dimension_semantics=("arbitrary", "arbitrary"),
                                             vmem_limit_bytes=VMEM_LIMIT),
        name="moba",
    )(qt, k, vt)


ROUTER_ROWS = 40


def _merge_kernel(x_ref, yn_ref, att_ref, gs_ref, ga_ref, wso_ref, wao_ref, wo_ref, nw_ref,
                  wr_hi_ref, wr_lo_ref, br_ref, x1_ref, h2_ref, route_ref):
    y_s = _dot(yn_ref[...], wso_ref[...])
    y_a = _dot(att_ref[...], wao_ref[...])
    merged = _sigmoid(gs_ref[...].astype(F32)) * y_s + _sigmoid(ga_ref[...].astype(F32)) * y_a
    x1 = x_ref[...] + _dot(merged.astype(BF16), wo_ref[...])
    x1_ref[...] = x1
    h2 = _rms(x1, nw_ref[...])
    h2_ref[...] = h2.astype(BF16)

    h_hi = h2.astype(BF16)
    h_lo = (h2 - h_hi.astype(F32)).astype(BF16)
    wr_hi = wr_hi_ref[...]
    logits = _dot_nt(wr_hi, h_hi) + _dot_nt(wr_hi, h_lo) + _dot_nt(wr_lo_ref[...], h_hi) + br_ref[...]
    gl = logits[0:N_GROUPS, :]
    gidx = lax.broadcasted_iota(jnp.int32, gl.shape, 0)
    gmax = jnp.max(gl, axis=0, keepdims=True)
    gi = jnp.min(jnp.where(gl == gmax, gidx, N_GROUPS), axis=0, keepdims=True)
    g_w = 1.0 / jnp.sum(jnp.exp(gl - gmax), axis=0, keepdims=True)
    el = jnp.zeros((EXPERTS_PER_GROUP, gl.shape[1]), F32)
    for g in range(N_GROUPS):
        lo = N_GROUPS + g * EXPERTS_PER_GROUP
        el = el + jnp.where(gi == g, logits[lo:lo + EXPERTS_PER_GROUP, :], 0.0)
    eidx = lax.broadcasted_iota(jnp.int32, el.shape, 0)
    v0 = jnp.max(el, axis=0, keepdims=True)
    i0 = jnp.min(jnp.where(el == v0, eidx, EXPERTS_PER_GROUP), axis=0, keepdims=True)
    rest = jnp.where(eidx == i0, -jnp.inf, el)
    v1 = jnp.max(rest, axis=0, keepdims=True)
    i1 = jnp.min(jnp.where((rest == v1) & (eidx != i0), eidx, EXPERTS_PER_GROUP), axis=0, keepdims=True)
    t = jnp.exp(v1 - v0)
    w0 = g_w / (1.0 + t)
    w1 = g_w * t / (1.0 + t)
    base = gi * EXPERTS_PER_GROUP
    route_ref[...] = jnp.concatenate(
        [(base + i0).astype(F32), (base + i1).astype(F32), w0, w1], axis=0)


def _merge(x2d, yn, att, gs, ga, wso, wao, wo, nw, wr_hi, wr_lo, br):
    t_tokens = x2d.shape[0]
    tm = min(TM_MERGE, t_tokens)
    row = lambda: pl.BlockSpec((tm, D_MODEL), lambda i: (i, 0))
    whole = pl.BlockSpec(memory_space=pltpu.VMEM)
    return pl.pallas_call(
        _merge_kernel,
        grid=(t_tokens // tm,),
        in_specs=[row(), row(), row(), row(), row(), whole, whole, whole, whole, whole, whole, whole],
        out_specs=[row(), row(), pl.BlockSpec((4, tm), lambda i: (0, i))],
        out_shape=[jax.ShapeDtypeStruct((t_tokens, D_MODEL), F32),
                   jax.ShapeDtypeStruct((t_tokens, D_MODEL), BF16),
                   jax.ShapeDtypeStruct((4, t_tokens), F32)],
        compiler_params=pltpu.CompilerParams(dimension_semantics=("arbitrary",), vmem_limit_bytes=VMEM_LIMIT),
        name="merge",
    )(x2d, yn, att, gs, ga, wso, wao, wo, nw, wr_hi, wr_lo, br)


def _rows_as_matrix(buf, n_rows):
    return jnp.concatenate([buf[pl.ds(s, n_rows, stride=ROW_TILE), :] for s in range(ROW_TILE)], axis=1)


def _rows(ref, first, n_rows):
    return ref.at[pl.ds(first * ROW_TILE, n_rows * ROW_TILE), :]


TM_TOK = 512
TILE_ASG = 2 * TM_TOK
RUN_BITS = 11
PAD_BITS = 8


def _for_each_piece(count, fn, bits=RUN_BITS):
    for bit in range(bits - 1, -1, -1):
        size = 1 << bit

        @pl.when((count & size) != 0)
        def _():
            fn((count >> (bit + 1)) << (bit + 1), size)


def _moe_plan(route, max_tiles):
    t_tokens = route.shape[1]
    ntok = t_tokens // TM_TOK
    experts = jnp.arange(N_EXPERTS, dtype=jnp.int32)
    e_loc = route[0:2].astype(jnp.int32).reshape(2, ntok, TM_TOK).transpose(1, 0, 2).reshape(ntok, TILE_ASG)
    onehot = (e_loc[:, :, None] == experts[None, None, :]).astype(jnp.int32)
    gsum = jnp.cumsum(onehot.reshape(ntok * TILE_ASG, N_EXPERTS), axis=0).reshape(ntok, TILE_ASG, N_EXPERTS)
    tile_end = gsum[:, -1, :]
    cnt = tile_end - jnp.concatenate([jnp.zeros_like(tile_end[:1]), tile_end[:-1]], axis=0)
    before = tile_end - cnt
    lrank = jnp.sum((gsum - before[:, None, :]) * onehot, axis=2) - 1
    off = jnp.cumsum(cnt, axis=1) - cnt
    counts = tile_end[-1]
    padded = ((counts + TM_MOE - 1) // TM_MOE) * TM_MOE
    ends = jnp.cumsum(padded)
    starts = ends - padded
    base = starts[None, :] + before
    jloc = jnp.sum(onehot * off[:, None, :], axis=2) + lrank
    jloc = jloc.reshape(ntok, 2, TM_TOK).transpose(1, 0, 2).reshape(2, t_tokens)
    tile_start = jnp.arange(max_tiles, dtype=jnp.int32) * TM_MOE
    tile_expert = jnp.sum((tile_start[:, None] >= ends[None, :]).astype(jnp.int32), axis=1)
    tile_expert = jnp.minimum(tile_expert, N_EXPERTS - 1)
    n_tiles = (ends[-1] // TM_MOE).astype(jnp.int32).reshape(1)
    flat = lambda a: a.reshape(-1).astype(jnp.int32)
    return flat(cnt), flat(off), flat(base), jloc, starts + counts, ends, tile_expert, n_tiles


def _dispatch_kernel(cnt_ref, off_ref, base_ref, padlo_ref, padhi_ref, ntile_ref, h2_ref, jrow_ref, xs_hbm,
                     xbuf0, xbuf1, zeros_ref, sems, zsem, *, max_tiles):
    i = pl.program_id(0)
    n = pl.num_programs(0)
    xbufs = (xbuf0, xbuf1)

    def runs(tile, slot, wait):
        def per_expert(e, carry):
            k = tile * N_EXPERTS + e
            src, dst = off_ref[k], base_ref[k]

            def piece(o, size):
                cp = pltpu.make_async_copy(_rows(xbufs[slot], src + o, size), _rows(xs_hbm, dst + o, size),
                                           sems.at[slot])
                cp.wait() if wait else cp.start()
            _for_each_piece(cnt_ref[k], piece)
            return carry
        lax.fori_loop(0, N_EXPERTS, per_expert, 0)

    @pl.when(i == 0)
    def _():
        zeros_ref[...] = jnp.zeros_like(zeros_ref)
        nt = ntile_ref[0]
        for wait in (False, True):
            def pad(e, carry):
                lo = padlo_ref[e]

                def piece(o, size):
                    cp = pltpu.make_async_copy(_rows(zeros_ref, 0, size), _rows(xs_hbm, lo + o, size), zsem)
                    cp.wait() if wait else cp.start()
                _for_each_piece(padhi_ref[e] - lo, piece, bits=PAD_BITS)
                return carry
            lax.fori_loop(0, N_EXPERTS, pad, 0)

            def spare(tile, carry):
                cp = pltpu.make_async_copy(zeros_ref, _rows(xs_hbm, tile * TM_MOE, TM_MOE), zsem)
                cp.wait() if wait else cp.start()
                return carry
            lax.fori_loop(nt, max_tiles, spare, 0)

    for slot in range(2):
        @pl.when(i % 2 == slot)
        def _():
            @pl.when(i >= 2)
            def _():
                runs(i - 2, slot, True)
            row = lax.broadcasted_iota(jnp.int32, (TILE_ASG, TM_TOK), 0)
            j = jrow_ref[...]
            perm = ((row == j[0:1, :]) | (row == j[1:2, :])).astype(BF16)
            _store_rows_as_tiles(xbufs[slot], _dot(perm, h2_ref[...]))
            runs(i, slot, False)

            @pl.when(i == n - 1)
            def _():
                @pl.when(i >= 1)
                def _():
                    runs(i - 1, 1 - slot, True)
                runs(i, slot, True)


def _dispatch(plan, h2, max_tiles):
    cnt, off, base, jloc, pad_lo, pad_hi, _, n_tiles = plan
    t_tokens = h2.shape[0]
    grid_spec = pltpu.PrefetchScalarGridSpec(
        num_scalar_prefetch=6,
        grid=(t_tokens // TM_TOK,),
        in_specs=[pl.BlockSpec((TM_TOK, D_MODEL), lambda i, *_: (i, 0)),
                  pl.BlockSpec((2, TM_TOK), lambda i, *_: (0, i))],
        out_specs=pl.BlockSpec(memory_space=pl.ANY),
        scratch_shapes=[pltpu.VMEM((TILE_ASG * ROW_TILE, LANES), F32),
                        pltpu.VMEM((TILE_ASG * ROW_TILE, LANES), F32),
                        pltpu.VMEM((TM_MOE * ROW_TILE, LANES), F32),
                        pltpu.SemaphoreType.DMA((2,)), pltpu.SemaphoreType.DMA(())],
    )
    return pl.pallas_call(
        functools.partial(_dispatch_kernel, max_tiles=max_tiles),
        grid_spec=grid_spec,
        out_shape=jax.ShapeDtypeStruct((max_tiles * TM_MOE * ROW_TILE, LANES), F32),
        compiler_params=pltpu.CompilerParams(dimension_semantics=("arbitrary",), vmem_limit_bytes=VMEM_LIMIT),
        name="dispatch",
    )(cnt, off, base, pad_lo, pad_hi, n_tiles, h2, jloc)


def _moe_kernel(texp_ref, ntile_ref, xs_ref, wgu_ref, wd_ref, y_ref):
    t = pl.program_id(0)

    @pl.when(t < ntile_ref[0])
    def _():
        xt = _rows_as_matrix(xs_ref, TM_MOE).astype(BF16)
        gu = _dot(xt, wgu_ref[0])
        hid = (_silu(gu[:, :D_EXPERT]) * gu[:, D_EXPERT:]).astype(BF16)
        _store_rows_as_tiles(y_ref, _dot(hid, wd_ref[0]))

    @pl.when(t >= ntile_ref[0])
    def _():
        y_ref[...] = jnp.zeros_like(y_ref)


def _moe(xs_tiles, tile_expert, n_tiles, wgu, wd, max_tiles):
    live = lambda t, nt: jnp.minimum(t, nt[0] - 1)
    tile = lambda index: pl.BlockSpec((TM_MOE * ROW_TILE, LANES), index)
    grid_spec = pltpu.PrefetchScalarGridSpec(
        num_scalar_prefetch=2,
        grid=(max_tiles,),
        in_specs=[tile(lambda t, te, nt: (live(t, nt), 0)),
                  pl.BlockSpec((1, D_MODEL, 2 * D_EXPERT), lambda t, te, nt: (te[live(t, nt)], 0, 0)),
                  pl.BlockSpec((1, D_EXPERT, D_MODEL), lambda t, te, nt: (te[live(t, nt)], 0, 0))],
        out_specs=tile(lambda t, te, nt: (t, 0)),
    )
    return pl.pallas_call(
        _moe_kernel,
        grid_spec=grid_spec,
        out_shape=jax.ShapeDtypeStruct((max_tiles * TM_MOE * ROW_TILE, LANES), F32),
        compiler_params=pltpu.CompilerParams(dimension_semantics=("arbitrary",), vmem_limit_bytes=VMEM_LIMIT),
        name="moe",
    )(tile_expert, n_tiles, xs_tiles, wgu, wd)


def _tail_kernel(cnt_ref, off_ref, base_ref, x1_ref, p_ref, jw_ref, y_hbm, pnw_ref, wpg_ref, wple_ref, fnw_ref,
                 o_ref, ybuf0, ybuf1, sems):
    i = pl.program_id(0)
    n = pl.num_programs(0)
    ybufs = (ybuf0, ybuf1)

    def runs(tile, slot, wait):
        def per_expert(e, carry):
            k = tile * N_EXPERTS + e
            src, dst = base_ref[k], off_ref[k]

            def piece(o, size):
                cp = pltpu.make_async_copy(_rows(y_hbm, src + o, size), _rows(ybufs[slot], dst + o, size),
                                           sems.at[slot])
                cp.wait() if wait else cp.start()
            _for_each_piece(cnt_ref[k], piece)
            return carry
        lax.fori_loop(0, N_EXPERTS, per_expert, 0)

    @pl.when(i == 0)
    def _():
        runs(0, 0, False)

    for slot in range(2):
        @pl.when(i % 2 == slot)
        def _():
            @pl.when(i + 1 < n)
            def _():
                runs(i + 1, 1 - slot, False)
            runs(i, slot, True)
            y_loc = _rows_as_matrix(ybufs[slot], TILE_ASG).astype(BF16)
            jw = jw_ref[...]
            col = lax.broadcasted_iota(jnp.int32, (TM_TOK, TILE_ASG), 1)
            moe = jnp.zeros((TM_TOK, D_MODEL), F32)
            for s in range(2):
                pick = (col == jw[:, s:s + 1].astype(jnp.int32)).astype(BF16)
                moe = moe + jw[:, 2 + s:3 + s] * _dot(pick, y_loc)
            x2 = x1_ref[...] + moe
            gate = _sigmoid(_dot(_rms(x2, pnw_ref[...]).astype(BF16), wpg_ref[...]))
            x3 = x2 + gate * _dot(p_ref[...].astype(BF16), wple_ref[...])
            o_ref[...] = _rms(x3, fnw_ref[...])


def _tail(plan, x1, p2d, y_tiles, jw, pnw, wpg, wple, fnw):
    cnt, off, base = plan[0:3]
    t_tokens = x1.shape[0]
    whole = pl.BlockSpec(memory_space=pltpu.VMEM)
    grid_spec = pltpu.PrefetchScalarGridSpec(
        num_scalar_prefetch=3,
        grid=(t_tokens // TM_TOK,),
        in_specs=[pl.BlockSpec((TM_TOK, D_MODEL), lambda i, *_: (i, 0)),
                  pl.BlockSpec((TM_TOK, PLE_DIM), lambda i, *_: (i, 0)),
                  pl.BlockSpec((TM_TOK, 4), lambda i, *_: (i, 0)),
                  pl.BlockSpec(memory_space=pl.ANY),
                  whole, whole, whole, whole],
        out_specs=pl.BlockSpec((TM_TOK, D_MODEL), lambda i, *_: (i, 0)),
        scratch_shapes=[pltpu.VMEM((TILE_ASG * ROW_TILE, LANES), F32),
                        pltpu.VMEM((TILE_ASG * ROW_TILE, LANES), F32),
                        pltpu.SemaphoreType.DMA((2,))],
    )
    return pl.pallas_call(
        _tail_kernel,
        grid_spec=grid_spec,
        out_shape=jax.ShapeDtypeStruct((t_tokens, D_MODEL), F32),
        compiler_params=pltpu.CompilerParams(dimension_semantics=("arbitrary",), vmem_limit_bytes=VMEM_LIMIT),
        name="tail",
    )(cnt, off, base, x1, p2d, jw, y_tiles, pnw, wpg, wple, fnw)


def _rope_tables(positions):
    half = ATT_HEAD_DIM // 2
    inv_freq = ROPE_THETA ** (-jnp.arange(half, dtype=F32) / half)
    ang = positions.astype(F32)[..., None] * inv_freq
    cos = jnp.cos(ang).reshape(-1, half)
    sin = jnp.sin(ang).reshape(-1, half)
    reps = LANES // ATT_HEAD_DIM
    cosf = jnp.tile(jnp.concatenate([cos, cos], axis=1), (1, reps))
    sinf = jnp.tile(jnp.concatenate([-sin, sin], axis=1), (1, reps))
    return cosf, sinf, cos.T, sin.T


def _layer(x2d, p2d, rope_tables, batch, attn_norm_w, w_in, conv_w, conv_b, dt_bias, a_log, d_skip, ssd_norm_w,
           w_ssd_out, w_attn_out, w_out, moe_norm_w, w_rg, b_rg, w_re, b_re, w_gate, w_up, w_down,
           ple_norm_w, w_ple, w_ple_gate, final_norm_w):
    t_tokens = x2d.shape[0]
    offs = np.cumsum((0,) + IN_SPLITS)
    cols = lambda i: w_in[:, offs[i]:offs[i + 1]]
    w_big = jnp.concatenate([cols(0), cols(1), cols(4), cols(6), cols(7)], axis=1).astype(BF16)
    w_t = jnp.concatenate([cols(3), cols(5), cols(2)], axis=1).T.astype(BF16)
    w_dt = jnp.pad(cols(2), ((0, 0), (0, LANES - SSD_HEADS))).astype(BF16)

    z, xbc, k, gs, ga, dt, qt, vt, dtt = _in_proj(x2d, attn_norm_w[None, :], *rope_tables, w_big, w_t, w_dt)

    yn = _ssd(z, xbc, dt, dtt, conv_w, conv_b, dt_bias, a_log, d_skip, ssd_norm_w, batch)

    att = _moba(qt, k, vt, batch)

    wr = jnp.concatenate([w_rg, w_re], axis=1).T
    wr = jnp.pad(wr, ((0, ROUTER_ROWS - wr.shape[0]), (0, 0)))
    wr_hi = wr.astype(BF16)
    wr_lo = (wr - wr_hi.astype(F32)).astype(BF16)
    br = jnp.pad(jnp.concatenate([b_rg, b_re]), (0, ROUTER_ROWS - N_GROUPS - N_EXPERTS))[:, None]
    x1, h2, route = _merge(x2d, yn, att, gs, ga, w_ssd_out.astype(BF16), w_attn_out.astype(BF16),
                           w_out.astype(BF16), moe_norm_w[None, :], wr_hi, wr_lo, br)

    max_tiles = (2 * t_tokens) // TM_MOE + N_EXPERTS
    plan = _moe_plan(route, max_tiles)
    wgu = jnp.concatenate([w_gate, w_up], axis=-1).reshape(N_EXPERTS, D_MODEL, 2 * D_EXPERT).astype(BF16)
    wd = w_down.reshape(N_EXPERTS, D_EXPERT, D_MODEL).astype(BF16)
    x_sorted = _dispatch(plan, h2, max_tiles)
    y_sorted = _moe(x_sorted, plan[6], plan[7], wgu, wd, max_tiles)
    jw = jnp.concatenate([plan[3].astype(F32), route[2:4]], axis=0).T

    return _tail(plan, x1, p2d, y_sorted, jw, ple_norm_w[None, :],
                 w_ple_gate.astype(BF16), w_ple.astype(BF16), final_norm_w[None, :])


def kernel(x, p, positions, attn_norm_w, w_in, conv_w, conv_b, dt_bias, a_log, d_skip, ssd_norm_w, w_ssd_out,
           w_attn_out, w_out, moe_norm_w, w_router_group, b_router_group, w_router_expert, b_router_expert,
           w_exp_gate, w_exp_up, w_exp_down, ple_norm_w, w_ple, w_ple_gate, final_norm_w):
    batch, seq, d = x.shape
    depth = p.shape[0]
    assert depth == 1, "the final norm is fused into the last layer's tail kernel"
    rope_tables = _rope_tables(positions)
    i = 0
    out = _layer(x.reshape(batch * seq, d), p[i].reshape(batch * seq, PLE_DIM), rope_tables, batch,
                 attn_norm_w[i], w_in[i], conv_w[i], conv_b[i], dt_bias[i], a_log[i], d_skip[i], ssd_norm_w[i],
                 w_ssd_out[i], w_attn_out[i], w_out[i], moe_norm_w[i], w_router_group[i], b_router_group[i],
                 w_router_expert[i], b_router_expert[i], w_exp_gate[i], w_exp_up[i], w_exp_down[i],
                 ple_norm_w[i], w_ple[i], w_ple_gate[i], final_norm_w)
    return out.reshape(batch, seq, d)
```

```python
import functools

import jax
import jax.numpy as jnp
import numpy as np
from jax import lax
from jax.experimental import pallas as pl
from jax.experimental.pallas import tpu as pltpu

F32 = jnp.float32
BF16 = jnp.bfloat16

EPS = 1e-6
D_MODEL = 1024
SSD_HEADS = 16
SSD_HEAD_DIM = 64
SSD_INNER = SSD_HEADS * SSD_HEAD_DIM
SSD_GROUPS = 2
SSD_STATE = 64
SSD_CONV = 4
SSD_CHUNK = 128
SSD_CONV_CH = SSD_INNER + 2 * SSD_GROUPS * SSD_STATE
CONV_TAIL = 16
ATT_HEADS = 16
ATT_HEAD_DIM = 64
ATT_INNER = ATT_HEADS * ATT_HEAD_DIM
MOBA_BLOCK = 256
MOBA_TOPK = 3
ROPE_THETA = 10000.0
IN_SPLITS = (SSD_INNER, SSD_CONV_CH, SSD_HEADS, ATT_INNER, ATT_INNER, ATT_INNER, D_MODEL, D_MODEL)
N_GROUPS = 4
EXPERTS_PER_GROUP = 8
N_EXPERTS = N_GROUPS * EXPERTS_PER_GROUP
D_EXPERT = 256
PLE_DIM = 256

LANES = 128
SUBLANES = 8
NEG_BIG = -1e30
LOG2E = 1.4426950408889634
VMEM_LIMIT = 56 * 1024 * 1024

TM_PROJ = 512
TM_MERGE = 512
TM_MOE = 256


def _dot(a, b):
    return jnp.dot(a, b, preferred_element_type=F32)


def _dot_nt(a, b):
    return lax.dot_general(a, b, (((1,), (1,)), ((), ())), preferred_element_type=F32)


def _dot_tn(a, b):
    return lax.dot_general(a, b, (((0,), (0,)), ((), ())), preferred_element_type=F32)


def _split3(a):
    hi = a.astype(BF16)
    r1 = a - hi.astype(F32)
    mid = r1.astype(BF16)
    lo = (r1 - mid.astype(F32)).astype(BF16)
    return hi, mid, lo


def _dot_exact_rhs(sel_bf16, a):
    return _dot(jnp.concatenate([sel_bf16] * 3, axis=1), jnp.concatenate(_split3(a), axis=0))


def _dot_exact_lhs(a, sel_bf16):
    return _dot(jnp.concatenate(_split3(a), axis=1), jnp.concatenate([sel_bf16] * 3, axis=0))


def _sigmoid(x):
    return 0.5 * jnp.tanh(0.5 * x) + 0.5


def _silu(x):
    return x * _sigmoid(x)


def _softplus(x):
    return jnp.maximum(x, 0.0) + jnp.log1p(jnp.exp(-jnp.abs(x)))


def _rms(x, w):
    inv = lax.rsqrt(jnp.mean(x * x, axis=-1, keepdims=True) + EPS)
    return (x * inv) * w


def _rope_tile(t, cos, sin_signed):
    half = ATT_HEAD_DIM // 2
    lane = lax.broadcasted_iota(jnp.int32, t.shape, 1)
    first = (lane % ATT_HEAD_DIM) < half
    partner = jnp.where(first, pltpu.roll(t, LANES - half, 1), pltpu.roll(t, half, 1))
    return t * cos + partner * sin_signed


Q_SCALE = ATT_HEAD_DIM ** -0.5 * LOG2E


def _in_proj_kernel(x_ref, nw_ref, cos_ref, sin_ref, cost_ref, sint_ref, w_ref, wt_ref, wdt_ref,
                    z_ref, xbc_ref, k_ref, gs_ref, ga_ref, dt_ref, qt_ref, vt_ref, dtt_ref):
    h = _rms(x_ref[...], nw_ref[...]).astype(BF16)
    cos = cos_ref[...]
    sin = sin_ref[...]

    def proj(lo, width):
        return _dot(h, w_ref[:, lo:lo + width])

    o = 0
    z_ref[...] = proj(o, SSD_INNER).astype(BF16)
    o += SSD_INNER
    xbc_ref[...] = proj(o, SSD_CONV_CH).astype(BF16)
    o += SSD_CONV_CH
    for g in range(ATT_INNER // LANES):
        t = proj(o + g * LANES, LANES)
        k_ref[:, g * LANES:(g + 1) * LANES] = _rope_tile(t, cos, sin).astype(BF16)
    o += ATT_INNER
    for ref in (gs_ref, ga_ref):
        ref[...] = proj(o, D_MODEL).astype(BF16)
        o += D_MODEL
    dt_ref[...] = _dot(h, wdt_ref[...])[:, :SSD_HEADS]

    dtt_ref[...] = _dot_nt(wt_ref[2 * ATT_INNER:2 * ATT_INNER + SSD_HEADS, :], h)
    vt_ref[...] = _dot_nt(wt_ref[ATT_INNER:2 * ATT_INNER, :], h).astype(BF16)
    qt = _dot_nt(wt_ref[0:ATT_INNER, :], h)
    cost = cost_ref[...]
    sint = sint_ref[...]
    half = ATT_HEAD_DIM // 2
    for hd in range(ATT_HEADS):
        r0 = hd * ATT_HEAD_DIM
        x1 = qt[r0:r0 + half, :]
        x2 = qt[r0 + half:r0 + ATT_HEAD_DIM, :]
        qt_ref[r0:r0 + half, :] = ((x1 * cost - x2 * sint) * Q_SCALE).astype(BF16)
        qt_ref[r0 + half:r0 + ATT_HEAD_DIM, :] = ((x2 * cost + x1 * sint) * Q_SCALE).astype(BF16)


def _in_proj(x2d, nw, cosf, sinf, cost, sint, w_big, w_t, w_dt):
    t_tokens = x2d.shape[0]
    tm = min(TM_PROJ, t_tokens)
    grid = (t_tokens // tm,)
    row = lambda width: pl.BlockSpec((tm, width), lambda i: (i, 0))
    col = lambda height: pl.BlockSpec((height, tm), lambda i: (0, i))
    whole = pl.BlockSpec(memory_space=pltpu.VMEM)
    out_shapes = [
        jax.ShapeDtypeStruct((t_tokens, SSD_INNER), BF16),
        jax.ShapeDtypeStruct((t_tokens, SSD_CONV_CH), BF16),
        jax.ShapeDtypeStruct((t_tokens, ATT_INNER), BF16),
        jax.ShapeDtypeStruct((t_tokens, D_MODEL), BF16),
        jax.ShapeDtypeStruct((t_tokens, D_MODEL), BF16),
        jax.ShapeDtypeStruct((t_tokens, SSD_HEADS), F32),
        jax.ShapeDtypeStruct((ATT_INNER, t_tokens), BF16),
        jax.ShapeDtypeStruct((ATT_INNER, t_tokens), BF16),
        jax.ShapeDtypeStruct((SSD_HEADS, t_tokens), F32),
    ]
    out_specs = [row(SSD_INNER), row(SSD_CONV_CH), row(ATT_INNER), row(D_MODEL), row(D_MODEL), row(SSD_HEADS),
                 col(ATT_INNER), col(ATT_INNER), col(SSD_HEADS)]
    return pl.pallas_call(
        _in_proj_kernel,
        grid=grid,
        in_specs=[row(D_MODEL), whole, row(LANES), row(LANES), col(ATT_HEAD_DIM // 2), col(ATT_HEAD_DIM // 2),
                  whole, whole, whole],
        out_specs=out_specs,
        out_shape=out_shapes,
        compiler_params=pltpu.CompilerParams(dimension_semantics=("arbitrary",), vmem_limit_bytes=VMEM_LIMIT),
        name="in_proj",
    )(x2d, nw, cosf, sinf, cost, sint, w_big, w_t, w_dt)


def _ssd_kernel(z_ref, xbc_ref, xtail_ref, dt_ref, dtt_ref, cw_ref, cb_ref, dtb_ref, dtbc_ref, alog_ref, alogc_ref,
                dexp_ref, nw_ref, out_ref, state_ref):
    c = pl.program_id(1)
    cl = SSD_CHUNK
    heads_per_group = SSD_HEADS // SSD_GROUPS
    gw = heads_per_group * SSD_HEAD_DIM

    @pl.when(c == 0)
    def _():
        state_ref[...] = jnp.zeros_like(state_ref)

    xbc = xbc_ref[...]
    tail = xtail_ref[...]
    tail = jnp.where(c == 0, jnp.zeros_like(tail), tail)
    u_ext = jnp.concatenate([tail, xbc], axis=0)
    n_shift = SSD_CONV - 1
    srow = lax.broadcasted_iota(jnp.int32, (n_shift * cl, CONV_TAIL + cl), 0)
    scol = lax.broadcasted_iota(jnp.int32, (n_shift * cl, CONV_TAIL + cl), 1)
    shift_sel = (scol == CONV_TAIL + srow % cl - (n_shift - srow // cl)).astype(BF16)
    shifted = _dot(shift_sel, u_ext)
    acc = cb_ref[...] + cw_ref[n_shift:SSD_CONV, :] * xbc.astype(F32)
    for k in range(n_shift):
        acc = acc + cw_ref[k:k + 1, :] * shifted[k * cl:(k + 1) * cl, :]
    xc = _silu(acc)
    xs = xc[:, :SSD_INNER]
    bm = xc[:, SSD_INNER:SSD_INNER + SSD_GROUPS * SSD_STATE].astype(BF16)
    cm = xc[:, SSD_INNER + SSD_GROUPS * SSD_STATE:].astype(BF16)

    dt = _softplus(dt_ref[...] + dtb_ref[...])
    da = dt * (-jnp.exp(alog_ref[...]))
    dtt = _softplus(dtt_ref[...] + dtbc_ref[...])
    dat = dtt * (-jnp.exp(alogc_ref[...]))
    ri = lax.broadcasted_iota(jnp.int32, (cl, cl), 0)
    ci = lax.broadcasted_iota(jnp.int32, (cl, cl), 1)
    causal = ri >= ci
    tril = causal.astype(BF16)
    triu = (ri <= ci).astype(BF16)
    acum = _dot_exact_rhs(tril, da)
    acumt = _dot_exact_lhs(dat, triu)

    hrow = lax.broadcasted_iota(jnp.int32, (SSD_HEADS, SSD_INNER), 0)
    hcol = lax.broadcasted_iota(jnp.int32, (SSD_HEADS, SSD_INNER), 1) // SSD_HEAD_DIM
    expand = (hrow == hcol).astype(BF16)

    a_last = acum[cl - 1:cl, :]
    dt_e = _dot_exact_lhs(dt, expand)
    dec_e = _dot_exact_lhs(jnp.exp(a_last - acum), expand)
    ea_e = _dot_exact_lhs(jnp.exp(acum), expand)
    xdt = xs * dt_e
    xdt_b = xdt.astype(BF16)
    xdtdec_b = (xdt * dec_e).astype(BF16)
    chunk_decay_t = jnp.exp(acumt[:, cl - 1:cl])

    y_groups = []
    for g in range(SSD_GROUPS):
        b_g = bm[:, g * SSD_STATE:(g + 1) * SSD_STATE]
        c_g = cm[:, g * SSD_STATE:(g + 1) * SSD_STATE]
        prev = state_ref[g * gw:(g + 1) * gw, :]
        cb = _dot_nt(c_g, b_g)
        y_off = _dot_nt(c_g, prev.astype(BF16)) * ea_e[:, g * gw:(g + 1) * gw]
        new_states = _dot_tn(xdtdec_b[:, g * gw:(g + 1) * gw], b_g)
        y_heads = []
        decay_rows = []
        for e in range(heads_per_group):
            hd = g * heads_per_group + e
            lmat = jnp.where(causal, jnp.exp(acum[:, hd:hd + 1] - acumt[hd:hd + 1, :]), 0.0)
            m = (cb * lmat).astype(BF16)
            y_heads.append(_dot(m, xdt_b[:, hd * SSD_HEAD_DIM:(hd + 1) * SSD_HEAD_DIM]))
            decay_rows.append(jnp.broadcast_to(chunk_decay_t[hd:hd + 1, :], (SSD_HEAD_DIM, SSD_STATE)))
        y_groups.append(jnp.concatenate(y_heads, axis=1) + y_off)
        state_ref[g * gw:(g + 1) * gw, :] = prev * jnp.concatenate(decay_rows, axis=0) + new_states

    y = jnp.concatenate(y_groups, axis=1) + xs * dexp_ref[...]
    yg = y * _silu(z_ref[...].astype(F32))
    nw = nw_ref[...]
    outs = [_rms(yg[:, g * gw:(g + 1) * gw], nw[:, g * gw:(g + 1) * gw]) for g in range(SSD_GROUPS)]
    out_ref[...] = jnp.concatenate(outs, axis=1).astype(BF16)


def _ssd(z, xbc, dt, dtt, conv_w, conv_b, dt_bias, a_log, d_skip, norm_w, batch):
    t_tokens = z.shape[0]
    nc = t_tokens // batch // SSD_CHUNK
    tok = lambda width: pl.BlockSpec((SSD_CHUNK, width), lambda b, c: (b * nc + c, 0))
    whole = pl.BlockSpec(memory_space=pltpu.VMEM)
    dexp = jnp.repeat(d_skip, SSD_HEAD_DIM)[None, :]
    tails_per_chunk = SSD_CHUNK // CONV_TAIL
    prev_tail = pl.BlockSpec((CONV_TAIL, SSD_CONV_CH),
                             lambda b, c: (jnp.maximum((b * nc + c) * tails_per_chunk - 1, 0), 0))
    return pl.pallas_call(
        _ssd_kernel,
        grid=(batch, nc),
        in_specs=[tok(SSD_INNER), tok(SSD_CONV_CH), prev_tail, tok(SSD_HEADS),
                  pl.BlockSpec((SSD_HEADS, SSD_CHUNK), lambda b, c: (0, b * nc + c)),
                  whole, whole, whole, whole, whole, whole, whole, whole],
        out_specs=tok(SSD_INNER),
        out_shape=jax.ShapeDtypeStruct((t_tokens, SSD_INNER), BF16),
        scratch_shapes=[pltpu.VMEM((SSD_INNER, SSD_STATE), F32)],
        compiler_params=pltpu.CompilerParams(dimension_semantics=("arbitrary", "arbitrary"),
                                             vmem_limit_bytes=VMEM_LIMIT),
        name="ssd",
    )(z, xbc, xbc, dt, dtt, conv_w, conv_b[None, :], dt_bias[None, :], dt_bias[:, None],
      a_log[None, :], a_log[:, None], dexp, norm_w[None, :])


MOBA_SUPER = 2 * MOBA_BLOCK


MOBA_BLOCK_ROWS = 16
ONES_ROWS = 16


def _moba_kernel(qt_ref, k_ref, vt_ref, o_ref, kfull_ref, qaug_ref, sa_ref, sb_ref, m_ref, l_ref, acc_ref, *, seq):
    nb = seq // MOBA_BLOCK
    sb = MOBA_SUPER
    nsb = seq // sb
    hd = ATT_HEAD_DIM
    nbr = MOBA_BLOCK_ROWS

    for c in range(nsb):
        rows = slice(c * sb, (c + 1) * sb)
        kfull_ref[rows, 0:LANES] = k_ref[rows, :]
        rblk = lax.broadcasted_iota(jnp.int32, (sb, LANES), 0) // MOBA_BLOCK + c * (sb // MOBA_BLOCK)
        rlane = lax.broadcasted_iota(jnp.int32, (sb, LANES), 1)
        kfull_ref[rows, LANES:2 * LANES] = jnp.where(rblk == rlane, 1.0, 0.0).astype(BF16)

    pr = lax.broadcasted_iota(jnp.int32, (nbr, seq), 0)
    pc = lax.broadcasted_iota(jnp.int32, (nbr, seq), 1) // MOBA_BLOCK
    pool = jnp.where(pr == pc, 1.0 / MOBA_BLOCK, 0.0).astype(BF16)
    kmean = _dot(pool, k_ref[...])
    km_hi = kmean.astype(BF16)
    km_lo = (kmean - km_hi.astype(F32)).astype(BF16)

    head_of_row = lax.broadcasted_iota(jnp.int32, (LANES, sb), 0) // hd
    blk_row = lax.broadcasted_iota(jnp.int32, (nbr, sb), 0)
    q_local = lax.broadcasted_iota(jnp.int32, (nbr, sb), 1)
    inrange = blk_row < nb
    krow = lax.broadcasted_iota(jnp.int32, (sb, 2 * sb), 0)
    qcol = lax.broadcasted_iota(jnp.int32, (sb, 2 * sb), 1) % sb
    causal = krow <= qcol
    ones_rows = jnp.ones((ONES_ROWS, sb), BF16)
    flag_pad = jnp.zeros((LANES - nbr, sb), BF16)

    def scores_into(s_ref, kv):
        start = pl.multiple_of(kv * sb, sb)
        s_ref[...] = _dot(kfull_ref[pl.ds(start, sb), :], qaug_ref[...])

    def absorb(kv, s_ref, diagonal):
        start = pl.multiple_of(kv * sb, sb)
        s_t = s_ref[...]
        if diagonal:
            s_t = jnp.where(causal, s_t, NEG_BIG)
        m_prev = m_ref[...]
        m_new = jnp.maximum(m_prev, jnp.max(s_t, axis=0, keepdims=True))
        m_ref[...] = m_new
        p_b = jnp.exp2(s_t - m_new).astype(BF16)
        alpha = jnp.exp2(m_prev - m_new)
        for a in range(2):
            cols = slice(a * sb, (a + 1) * sb)
            rows = slice(a * hd, (a + 1) * hd)
            v_aug = jnp.concatenate([vt_ref[rows, pl.ds(start, sb)], ones_rows], axis=0)
            r = _dot(v_aug, p_b[:, cols])
            l_ref[a:a + 1, :] = alpha[:, cols] * l_ref[a:a + 1, :] + r[hd:hd + 1, :]
            acc_ref[rows, :] = alpha[:, cols] * acc_ref[rows, :] + r[0:hd, :]

    def q_super(i, carry):
        qstart = pl.multiple_of(i * sb, sb)
        qt_sb = qt_ref[:, pl.ds(qstart, sb)]
        own = 2 * i + jnp.where(q_local >= MOBA_BLOCK, 1, 0)
        valid = blk_row < own
        for a in range(2):
            qa = jnp.where(head_of_row == a, qt_sb, jnp.zeros_like(qt_sb))
            gate = _dot(km_hi, qa) + _dot(km_lo, qa)
            g = jnp.where(valid, gate, -jnp.inf)
            sel = jnp.zeros((nbr, sb), jnp.bool_)
            for _ in range(min(MOBA_TOPK, nb - 1)):
                cand = inrange & jnp.logical_not(sel)
                gm = jnp.max(jnp.where(cand, g, -jnp.inf), axis=0, keepdims=True)
                hit = cand & (g == gm)
                first = jnp.min(jnp.where(hit, blk_row, nb), axis=0, keepdims=True)
                sel = sel | (blk_row == first)
            chosen = (sel & valid) | (blk_row == own)
            flags = jnp.where(chosen, 0.0, NEG_BIG).astype(BF16)
            qaug_ref[:, a * sb:(a + 1) * sb] = jnp.concatenate([qa, flags, flag_pad], axis=0)

        m_ref[...] = jnp.full(m_ref.shape, NEG_BIG, F32)
        l_ref[...] = jnp.zeros_like(l_ref)
        acc_ref[...] = jnp.zeros_like(acc_ref)
        scores_into(sa_ref, 0)

        def two_steps(t, carry):
            scores_into(sb_ref, 2 * t + 1)
            absorb(2 * t, sa_ref, False)
            scores_into(sa_ref, 2 * t + 2)
            absorb(2 * t + 1, sb_ref, False)
            return carry

        lax.fori_loop(0, i // 2, two_steps, 0)

        @pl.when(i % 2 == 1)
        def _():
            scores_into(sb_ref, i)
            absorb(i - 1, sa_ref, False)
            absorb(i, sb_ref, True)

        @pl.when(i % 2 == 0)
        def _():
            absorb(i, sa_ref, True)

        out_t = jnp.concatenate([acc_ref[a * hd:(a + 1) * hd, :] / l_ref[a:a + 1, :] for a in range(2)], axis=0)
        o_ref[pl.ds(qstart, sb), :] = out_t.T.astype(BF16)
        return carry

    lax.fori_loop(0, nsb, q_super, 0)


def _moba(qt, k, vt, batch):
    t_tokens = k.shape[0]
    seq = t_tokens // batch
    assert seq % MOBA_SUPER == 0 and seq // MOBA_BLOCK <= MOBA_BLOCK_ROWS
    spec = pl.BlockSpec((seq, LANES), lambda b, hp: (b, hp))
    spec_t = pl.BlockSpec((LANES, seq), lambda b, hp: (hp, b))
    return pl.pallas_call(
        functools.partial(_moba_kernel, seq=seq),
        grid=(batch, ATT_INNER // LANES),
        in_specs=[spec_t, spec, spec_t],
        out_specs=spec,
        out_shape=jax.ShapeDtypeStruct((t_tokens, ATT_INNER), BF16),
        scratch_shapes=[pltpu.VMEM((seq, 2 * LANES), BF16),
                        pltpu.VMEM((2 * LANES, 2 * MOBA_SUPER), BF16),
                        pltpu.VMEM((MOBA_SUPER, 2 * MOBA_SUPER), F32),
                        pltpu.VMEM((MOBA_SUPER, 2 * MOBA_SUPER), F32),
                        pltpu.VMEM((1, 2 * MOBA_SUPER), F32),
                        pltpu.VMEM((SUBLANES, MOBA_SUPER), F32),
                        pltpu.VMEM((LANES, MOBA_SUPER), F32)],
        compiler_params=pltpu.CompilerParams(dimension_semantics=("arbitrary", "arbitrary"),
                                             vmem_limit_bytes=VMEM_LIMIT),
        name="moba",
    )(qt, k, vt)


ROUTER_ROWS = 40


def _merge_kernel(x_ref, yn_ref, att_ref, gs_ref, ga_ref, wso_ref, wao_ref, wo_ref, nw_ref,
                  wr_hi_ref, wr_lo_ref, br_ref, x1_ref, h2_ref, route_ref):
    y_s = _dot(yn_ref[...], wso_ref[...])
    y_a = _dot(att_ref[...], wao_ref[...])
    merged = _sigmoid(gs_ref[...].astype(F32)) * y_s + _sigmoid(ga_ref[...].astype(F32)) * y_a
    x1 = x_ref[...] + _dot(merged.astype(BF16), wo_ref[...])
    x1_ref[...] = x1
    h2 = _rms(x1, nw_ref[...])
    h2_ref[...] = h2.astype(BF16)

    h_hi = h2.astype(BF16)
    h_lo = (h2 - h_hi.astype(F32)).astype(BF16)
    wr_hi = wr_hi_ref[...]
    logits = _dot_nt(wr_hi, h_hi) + _dot_nt(wr_hi, h_lo) + _dot_nt(wr_lo_ref[...], h_hi) + br_ref[...]
    gl = logits[0:N_GROUPS, :]
    gidx = lax.broadcasted_iota(jnp.int32, gl.shape, 0)
    gmax = jnp.max(gl, axis=0, keepdims=True)
    gi = jnp.min(jnp.where(gl == gmax, gidx, N_GROUPS), axis=0, keepdims=True)
    g_w = 1.0 / jnp.sum(jnp.exp(gl - gmax), axis=0, keepdims=True)
    el = jnp.zeros((EXPERTS_PER_GROUP, gl.shape[1]), F32)
    for g in range(N_GROUPS):
        lo = N_GROUPS + g * EXPERTS_PER_GROUP
        el = el + jnp.where(gi == g, logits[lo:lo + EXPERTS_PER_GROUP, :], 0.0)
    eidx = lax.broadcasted_iota(jnp.int32, el.shape, 0)
    v0 = jnp.max(el, axis=0, keepdims=True)
    i0 = jnp.min(jnp.where(el == v0, eidx, EXPERTS_PER_GROUP), axis=0, keepdims=True)
    rest = jnp.where(eidx == i0, -jnp.inf, el)
    v1 = jnp.max(rest, axis=0, keepdims=True)
    i1 = jnp.min(jnp.where((rest == v1) & (eidx != i0), eidx, EXPERTS_PER_GROUP), axis=0, keepdims=True)
    t = jnp.exp(v1 - v0)
    w0 = g_w / (1.0 + t)
    w1 = g_w * t / (1.0 + t)
    base = gi * EXPERTS_PER_GROUP
    route_ref[...] = jnp.concatenate(
        [(base + i0).astype(F32), (base + i1).astype(F32), w0, w1], axis=0)


def _merge(x2d, yn, att, gs, ga, wso, wao, wo, nw, wr_hi, wr_lo, br):
    t_tokens = x2d.shape[0]
    tm = min(TM_MERGE, t_tokens)
    row = lambda: pl.BlockSpec((tm, D_MODEL), lambda i: (i, 0))
    whole = pl.BlockSpec(memory_space=pltpu.VMEM)
    return pl.pallas_call(
        _merge_kernel,
        grid=(t_tokens // tm,),
        in_specs=[row(), row(), row(), row(), row(), whole, whole, whole, whole, whole, whole, whole],
        out_specs=[row(), row(), pl.BlockSpec((4, tm), lambda i: (0, i))],
        out_shape=[jax.ShapeDtypeStruct((t_tokens, D_MODEL), F32),
                   jax.ShapeDtypeStruct((t_tokens, D_MODEL), BF16),
                   jax.ShapeDtypeStruct((4, t_tokens), F32)],
        compiler_params=pltpu.CompilerParams(dimension_semantics=("arbitrary",), vmem_limit_bytes=VMEM_LIMIT),
        name="merge",
    )(x2d, yn, att, gs, ga, wso, wao, wo, nw, wr_hi, wr_lo, br)


TM_TOK = 512
TILE_ASG = 2 * TM_TOK
RUN_ALIGN = SUBLANES
LOC_ROWS = TILE_ASG + N_EXPERTS * RUN_ALIGN
ALIGN_BIT = RUN_ALIGN.bit_length() - 1
RUN_BITS = TILE_ASG.bit_length()
PAD_BITS = TM_MOE.bit_length() - 1


def _rows(ref, first, n_rows):
    start = first if isinstance(first, int) else pl.multiple_of(first, RUN_ALIGN)
    return ref.at[pl.ds(start, n_rows), :]


def _for_each_piece(count, fn, bits):
    for bit in range(bits - 1, ALIGN_BIT - 1, -1):
        size = 1 << bit

        @pl.when((count & size) != 0)
        def _():
            fn((count >> (bit + 1)) << (bit + 1), size)


def _moe_max_tiles(t_tokens):
    rows = 2 * t_tokens + (t_tokens // TM_TOK) * N_EXPERTS * (RUN_ALIGN - 1)
    return -(-rows // TM_MOE) + N_EXPERTS


def _moe_plan(route, max_tiles):
    t_tokens = route.shape[1]
    ntok = t_tokens // TM_TOK
    experts = jnp.arange(N_EXPERTS, dtype=jnp.int32)
    e_loc = route[0:2].astype(jnp.int32).reshape(2, ntok, TM_TOK).transpose(1, 0, 2).reshape(ntok, TILE_ASG)
    onehot = (e_loc[:, :, None] == experts[None, None, :]).astype(jnp.int32)
    gsum = jnp.cumsum(onehot.reshape(ntok * TILE_ASG, N_EXPERTS), axis=0).reshape(ntok, TILE_ASG, N_EXPERTS)
    tile_end = gsum[:, -1, :]
    before = jnp.concatenate([jnp.zeros_like(tile_end[:1]), tile_end[:-1]], axis=0)
    lrank = jnp.sum((gsum - before[:, None, :]) * onehot, axis=2) - 1
    cnt = ((tile_end - before + RUN_ALIGN - 1) // RUN_ALIGN) * RUN_ALIGN
    off = jnp.cumsum(cnt, axis=1) - cnt
    counts = jnp.sum(cnt, axis=0)
    padded = ((counts + TM_MOE - 1) // TM_MOE) * TM_MOE
    ends = jnp.cumsum(padded)
    starts = ends - padded
    base = starts[None, :] + jnp.cumsum(cnt, axis=0) - cnt
    jloc = jnp.sum(onehot * off[:, None, :], axis=2) + lrank
    jloc = jloc.reshape(ntok, 2, TM_TOK).transpose(1, 0, 2).reshape(2, t_tokens)
    tile_start = jnp.arange(max_tiles, dtype=jnp.int32) * TM_MOE
    tile_expert = jnp.sum((tile_start[:, None] >= ends[None, :]).astype(jnp.int32), axis=1)
    tile_expert = jnp.minimum(tile_expert, N_EXPERTS - 1)
    n_tiles = (ends[-1] // TM_MOE).astype(jnp.int32).reshape(1)
    flat = lambda a: a.reshape(-1).astype(jnp.int32)
    return flat(cnt), flat(off), flat(base), jloc, starts + counts, ends, tile_expert, n_tiles


def _dispatch_kernel(cnt_ref, off_ref, base_ref, padlo_ref, padhi_ref, ntile_ref, h2_ref, jrow_ref, xs_hbm,
                     xbuf0, xbuf1, zeros_ref, sems, zsem, *, max_tiles):
    i = pl.program_id(0)
    n = pl.num_programs(0)
    xbufs = (xbuf0, xbuf1)

    def runs(tile, slot, wait):
        def per_expert(e, carry):
            k = tile * N_EXPERTS + e
            src, dst = off_ref[k], base_ref[k]

            def piece(o, size):
                cp = pltpu.make_async_copy(_rows(xbufs[slot], src + o, size), _rows(xs_hbm, dst + o, size),
                                           sems.at[slot])
                cp.wait() if wait else cp.start()
            _for_each_piece(cnt_ref[k], piece, RUN_BITS)
            return carry
        lax.fori_loop(0, N_EXPERTS, per_expert, 0)

    @pl.when(i == 0)
    def _():
        zeros_ref[...] = jnp.zeros_like(zeros_ref)
        nt = ntile_ref[0]
        for wait in (False, True):
            def pad(e, carry):
                lo = padlo_ref[e]

                def piece(o, size):
                    cp = pltpu.make_async_copy(_rows(zeros_ref, 0, size), _rows(xs_hbm, lo + o, size), zsem)
                    cp.wait() if wait else cp.start()
                _for_each_piece(padhi_ref[e] - lo, piece, PAD_BITS)
                return carry
            lax.fori_loop(0, N_EXPERTS, pad, 0)

            def spare(tile, carry):
                cp = pltpu.make_async_copy(zeros_ref, _rows(xs_hbm, tile * TM_MOE, TM_MOE), zsem)
                cp.wait() if wait else cp.start()
                return carry
            lax.fori_loop(nt, max_tiles, spare, 0)

    for slot in range(2):
        @pl.when(i % 2 == slot)
        def _():
            @pl.when(i >= 2)
            def _():
                runs(i - 2, slot, True)
            row = lax.broadcasted_iota(jnp.int32, (LOC_ROWS, TM_TOK), 0)
            j = jrow_ref[...]
            perm = ((row == j[0:1, :]) | (row == j[1:2, :])).astype(BF16)
            xbufs[slot][...] = _dot(perm, h2_ref[...])
            runs(i, slot, False)

            @pl.when(i == n - 1)
            def _():
                @pl.when(i >= 1)
                def _():
                    runs(i - 1, 1 - slot, True)
                runs(i, slot, True)


def _dispatch(plan, h2, max_tiles):
    cnt, off, base, jloc, pad_lo, pad_hi, _, n_tiles = plan
    t_tokens = h2.shape[0]
    grid_spec = pltpu.PrefetchScalarGridSpec(
        num_scalar_prefetch=6,
        grid=(t_tokens // TM_TOK,),
        in_specs=[pl.BlockSpec((TM_TOK, D_MODEL), lambda i, *_: (i, 0)),
                  pl.BlockSpec((2, TM_TOK), lambda i, *_: (0, i))],
        out_specs=pl.BlockSpec(memory_space=pl.ANY),
        scratch_shapes=[pltpu.VMEM((LOC_ROWS, D_MODEL), F32),
                        pltpu.VMEM((LOC_ROWS, D_MODEL), F32),
                        pltpu.VMEM((TM_MOE, D_MODEL), F32),
                        pltpu.SemaphoreType.DMA((2,)), pltpu.SemaphoreType.DMA(())],
    )
    return pl.pallas_call(
        functools.partial(_dispatch_kernel, max_tiles=max_tiles),
        grid_spec=grid_spec,
        out_shape=jax.ShapeDtypeStruct((max_tiles * TM_MOE, D_MODEL), F32),
        compiler_params=pltpu.CompilerParams(dimension_semantics=("arbitrary",), vmem_limit_bytes=VMEM_LIMIT),
        name="dispatch",
    )(cnt, off, base, pad_lo, pad_hi, n_tiles, h2, jloc)


def _moe_kernel(texp_ref, ntile_ref, xs_ref, wgu_ref, wd_ref, y_ref):
    t = pl.program_id(0)

    @pl.when(t < ntile_ref[0])
    def _():
        gu = _dot(xs_ref[...].astype(BF16), wgu_ref[0])
        hid = (_silu(gu[:, :D_EXPERT]) * gu[:, D_EXPERT:]).astype(BF16)
        y_ref[...] = _dot(hid, wd_ref[0])

    @pl.when(t >= ntile_ref[0])
    def _():
        y_ref[...] = jnp.zeros_like(y_ref)


def _moe(xs, tile_expert, n_tiles, wgu, wd, max_tiles):
    live = lambda t, nt: jnp.minimum(t, nt[0] - 1)
    tile = lambda index: pl.BlockSpec((TM_MOE, D_MODEL), index)
    grid_spec = pltpu.PrefetchScalarGridSpec(
        num_scalar_prefetch=2,
        grid=(max_tiles,),
        in_specs=[tile(lambda t, te, nt: (live(t, nt), 0)),
                  pl.BlockSpec((1, D_MODEL, 2 * D_EXPERT), lambda t, te, nt: (te[live(t, nt)], 0, 0)),
                  pl.BlockSpec((1, D_EXPERT, D_MODEL), lambda t, te, nt: (te[live(t, nt)], 0, 0))],
        out_specs=tile(lambda t, te, nt: (t, 0)),
    )
    return pl.pallas_call(
        _moe_kernel,
        grid_spec=grid_spec,
        out_shape=jax.ShapeDtypeStruct((max_tiles * TM_MOE, D_MODEL), F32),
        compiler_params=pltpu.CompilerParams(dimension_semantics=("arbitrary",), vmem_limit_bytes=VMEM_LIMIT),
        name="moe",
    )(tile_expert, n_tiles, xs, wgu, wd)


def _tail_kernel(cnt_ref, off_ref, base_ref, x1_ref, p_ref, jw_ref, y_hbm, pnw_ref, wpg_ref, wple_ref, fnw_ref,
                 o_ref, ybuf0, ybuf1, sems):
    i = pl.program_id(0)
    n = pl.num_programs(0)
    ybufs = (ybuf0, ybuf1)

    def runs(tile, slot, wait):
        def per_expert(e, carry):
            k = tile * N_EXPERTS + e
            src, dst = base_ref[k], off_ref[k]

            def piece(o, size):
                cp = pltpu.make_async_copy(_rows(y_hbm, src + o, size), _rows(ybufs[slot], dst + o, size),
                                           sems.at[slot])
                cp.wait() if wait else cp.start()
            _for_each_piece(cnt_ref[k], piece, RUN_BITS)
            return carry
        lax.fori_loop(0, N_EXPERTS, per_expert, 0)

    @pl.when(i == 0)
    def _():
        ybuf0[...] = jnp.zeros_like(ybuf0)
        ybuf1[...] = jnp.zeros_like(ybuf1)
        runs(0, 0, False)

    for slot in range(2):
        @pl.when(i % 2 == slot)
        def _():
            @pl.when(i + 1 < n)
            def _():
                runs(i + 1, 1 - slot, False)
            runs(i, slot, True)
            y_loc = ybufs[slot][...].astype(BF16)
            jw = jw_ref[...]
            col = lax.broadcasted_iota(jnp.int32, (TM_TOK, LOC_ROWS), 1)
            moe = jnp.zeros((TM_TOK, D_MODEL), F32)
            for s in range(2):
                pick = (col == jw[:, s:s + 1].astype(jnp.int32)).astype(BF16)
                moe = moe + jw[:, 2 + s:3 + s] * _dot(pick, y_loc)
            x2 = x1_ref[...] + moe
            gate = _sigmoid(_dot(_rms(x2, pnw_ref[...]).astype(BF16), wpg_ref[...]))
            x3 = x2 + gate * _dot(p_ref[...].astype(BF16), wple_ref[...])
            o_ref[...] = _rms(x3, fnw_ref[...])


def _tail(plan, x1, p2d, y_sorted, jw, pnw, wpg, wple, fnw):
    cnt, off, base = plan[0:3]
    t_tokens = x1.shape[0]
    whole = pl.BlockSpec(memory_space=pltpu.VMEM)
    grid_spec = pltpu.PrefetchScalarGridSpec(
        num_scalar_prefetch=3,
        grid=(t_tokens // TM_TOK,),
        in_specs=[pl.BlockSpec((TM_TOK, D_MODEL), lambda i, *_: (i, 0)),
                  pl.BlockSpec((TM_TOK, PLE_DIM), lambda i, *_: (i, 0)),
                  pl.BlockSpec((TM_TOK, 4), lambda i, *_: (i, 0)),
                  pl.BlockSpec(memory_space=pl.ANY),
                  whole, whole, whole, whole],
        out_specs=pl.BlockSpec((TM_TOK, D_MODEL), lambda i, *_: (i, 0)),
        scratch_shapes=[pltpu.VMEM((LOC_ROWS, D_MODEL), F32),
                        pltpu.VMEM((LOC_ROWS, D_MODEL), F32),
                        pltpu.SemaphoreType.DMA((2,))],
    )
    return pl.pallas_call(
        _tail_kernel,
        grid_spec=grid_spec,
        out_shape=jax.ShapeDtypeStruct((t_tokens, D_MODEL), F32),
        compiler_params=pltpu.CompilerParams(dimension_semantics=("arbitrary",), vmem_limit_bytes=VMEM_LIMIT),
        name="tail",
    )(cnt, off, base, x1, p2d, jw, y_sorted, pnw, wpg, wple, fnw)


def _rope_tables(positions):
    half = ATT_HEAD_DIM // 2
    inv_freq = ROPE_THETA ** (-jnp.arange(half, dtype=F32) / half)
    ang = positions.astype(F32)[..., None] * inv_freq
    cos = jnp.cos(ang).reshape(-1, half)
    sin = jnp.sin(ang).reshape(-1, half)
    reps = LANES // ATT_HEAD_DIM
    cosf = jnp.tile(jnp.concatenate([cos, cos], axis=1), (1, reps))
    sinf = jnp.tile(jnp.concatenate([-sin, sin], axis=1), (1, reps))
    return cosf, sinf, cos.T, sin.T


def _layer(x2d, p2d, rope_tables, batch, attn_norm_w, w_in, conv_w, conv_b, dt_bias, a_log, d_skip, ssd_norm_w,
           w_ssd_out, w_attn_out, w_out, moe_norm_w, w_rg, b_rg, w_re, b_re, w_gate, w_up, w_down,
           ple_norm_w, w_ple, w_ple_gate, final_norm_w):
    t_tokens = x2d.shape[0]
    offs = np.cumsum((0,) + IN_SPLITS)
    cols = lambda i: w_in[:, offs[i]:offs[i + 1]]
    w_big = jnp.concatenate([cols(0), cols(1), cols(4), cols(6), cols(7)], axis=1).astype(BF16)
    w_t = jnp.concatenate([cols(3), cols(5), cols(2)], axis=1).T.astype(BF16)
    w_dt = jnp.pad(cols(2), ((0, 0), (0, LANES - SSD_HEADS))).astype(BF16)

    z, xbc, k, gs, ga, dt, qt, vt, dtt = _in_proj(x2d, attn_norm_w[None, :], *rope_tables, w_big, w_t, w_dt)

    yn = _ssd(z, xbc, dt, dtt, conv_w, conv_b, dt_bias, a_log, d_skip, ssd_norm_w, batch)

    att = _moba(qt, k, vt, batch)

    wr = jnp.concatenate([w_rg, w_re], axis=1).T
    wr = jnp.pad(wr, ((0, ROUTER_ROWS - wr.shape[0]), (0, 0)))
    wr_hi = wr.astype(BF16)
    wr_lo = (wr - wr_hi.astype(F32)).astype(BF16)
    br = jnp.pad(jnp.concatenate([b_rg, b_re]), (0, ROUTER_ROWS - N_GROUPS - N_EXPERTS))[:, None]
    x1, h2, route = _merge(x2d, yn, att, gs, ga, w_ssd_out.astype(BF16), w_attn_out.astype(BF16),
                           w_out.astype(BF16), moe_norm_w[None, :], wr_hi, wr_lo, br)

    max_tiles = _moe_max_tiles(t_tokens)
    plan = _moe_plan(route, max_tiles)
    wgu = jnp.concatenate([w_gate, w_up], axis=-1).reshape(N_EXPERTS, D_MODEL, 2 * D_EXPERT).astype(BF16)
    wd = w_down.reshape(N_EXPERTS, D_EXPERT, D_MODEL).astype(BF16)
    x_sorted = _dispatch(plan, h2, max_tiles)
    y_sorted = _moe(x_sorted, plan[6], plan[7], wgu, wd, max_tiles)
    jw = jnp.concatenate([plan[3].astype(F32), route[2:4]], axis=0).T

    return _tail(plan, x1, p2d, y_sorted, jw, ple_norm_w[None, :],
                 w_ple_gate.astype(BF16), w_ple.astype(BF16), final_norm_w[None, :])


def kernel(x, p, positions, attn_norm_w, w_in, conv_w, conv_b, dt_bias, a_log, d_skip, ssd_norm_w, w_ssd_out,
           w_attn_out, w_out, moe_norm_w, w_router_group, b_router_group, w_router_expert, b_router_expert,
           w_exp_gate, w_exp_up, w_exp_down, ple_norm_w, w_ple, w_ple_gate, final_norm_w):
    batch, seq, d = x.shape
    depth = p.shape[0]
    assert depth == 1, "the final norm is fused into the last layer's tail kernel"
    rope_tables = _rope_tables(positions)
    i = 0
    out = _layer(x.reshape(batch * seq, d), p[i].reshape(batch * seq, PLE_DIM), rope_tables, batch,
                 attn_norm_w[i], w_in[i], conv_w[i], conv_b[i], dt_bias[i], a_log[i], d_skip[i], ssd_norm_w[i],
                 w_ssd_out[i], w_attn_out[i], w_out[i], moe_norm_w[i], w_router_group[i], b_router_group[i],
                 w_router_expert[i], b_router_expert[i], w_exp_gate[i], w_exp_up[i], w_exp_down[i],
                 ple_norm_w[i], w_ple[i], w_ple_gate[i], final_norm_w)
    return out.reshape(batch, seq, d)
```

```python
import functools

import jax
import jax.numpy as jnp
import numpy as np
from jax import lax
from jax.experimental import pallas as pl
from jax.experimental.pallas import tpu as pltpu

F32 = jnp.float32
BF16 = jnp.bfloat16

EPS = 1e-6
D_MODEL = 1024
SSD_HEADS = 16
SSD_HEAD_DIM = 64
SSD_INNER = SSD_HEADS * SSD_HEAD_DIM
SSD_GROUPS = 2
SSD_STATE = 64
SSD_CONV = 4
SSD_CHUNK = 128
SSD_CONV_CH = SSD_INNER + 2 * SSD_GROUPS * SSD_STATE
CONV_TAIL = 16
ATT_HEADS = 16
ATT_HEAD_DIM = 64
ATT_INNER = ATT_HEADS * ATT_HEAD_DIM
MOBA_BLOCK = 256
MOBA_TOPK = 3
ROPE_THETA = 10000.0
IN_SPLITS = (SSD_INNER, SSD_CONV_CH, SSD_HEADS, ATT_INNER, ATT_INNER, ATT_INNER, D_MODEL, D_MODEL)
N_GROUPS = 4
EXPERTS_PER_GROUP = 8
N_EXPERTS = N_GROUPS * EXPERTS_PER_GROUP
D_EXPERT = 256
PLE_DIM = 256

LANES = 128
SUBLANES = 8
NEG_BIG = -1e30
LOG2E = 1.4426950408889634
VMEM_LIMIT = 56 * 1024 * 1024

TM_PROJ = 512
TM_MOE = 256
TM_TOK = 512
TILE_ASG = 2 * TM_TOK
RUN_ALIGN = SUBLANES
LOC_ROWS = TILE_ASG + N_EXPERTS * RUN_ALIGN
ALIGN_BIT = RUN_ALIGN.bit_length() - 1
RUN_BITS = TILE_ASG.bit_length()
PAD_BITS = TM_MOE.bit_length() - 1


def _dot(a, b):
    return jnp.dot(a, b, preferred_element_type=F32)


def _dot_nt(a, b):
    return lax.dot_general(a, b, (((1,), (1,)), ((), ())), preferred_element_type=F32)


def _dot_tn(a, b):
    return lax.dot_general(a, b, (((0,), (0,)), ((), ())), preferred_element_type=F32)


def _split3(a):
    hi = a.astype(BF16)
    r1 = a - hi.astype(F32)
    mid = r1.astype(BF16)
    lo = (r1 - mid.astype(F32)).astype(BF16)
    return hi, mid, lo


def _dot_exact_rhs(sel_bf16, a):
    return _dot(jnp.concatenate([sel_bf16] * 3, axis=1), jnp.concatenate(_split3(a), axis=0))


def _dot_exact_lhs(a, sel_bf16):
    return _dot(jnp.concatenate(_split3(a), axis=1), jnp.concatenate([sel_bf16] * 3, axis=0))


def _sigmoid(x):
    return 0.5 * jnp.tanh(0.5 * x) + 0.5


def _silu(x):
    return x * _sigmoid(x)


def _softplus(x):
    return jnp.maximum(x, 0.0) + jnp.log1p(jnp.exp(-jnp.abs(x)))


def _rms(x, w):
    inv = lax.rsqrt(jnp.mean(x * x, axis=-1, keepdims=True) + EPS)
    return (x * inv) * w


def _rope_tile(t, cos, sin_signed):
    half = ATT_HEAD_DIM // 2
    lane = lax.broadcasted_iota(jnp.int32, t.shape, 1)
    first = (lane % ATT_HEAD_DIM) < half
    partner = jnp.where(first, pltpu.roll(t, LANES - half, 1), pltpu.roll(t, half, 1))
    return t * cos + partner * sin_signed


Q_SCALE = ATT_HEAD_DIM ** -0.5 * LOG2E


def _in_proj_kernel(x_ref, nw_ref, cos_ref, sin_ref, cost_ref, sint_ref, w_ref, wt_ref, wdt_ref,
                    z_ref, xbc_ref, k_ref, gs_ref, ga_ref, dt_ref, qt_ref, vt_ref, dtt_ref):
    h = _rms(x_ref[...], nw_ref[...]).astype(BF16)
    cos = cos_ref[...]
    sin = sin_ref[...]

    def proj(lo, width):
        return _dot(h, w_ref[:, lo:lo + width])

    o = 0
    z_ref[...] = proj(o, SSD_INNER).astype(BF16)
    o += SSD_INNER
    xbc_ref[...] = proj(o, SSD_CONV_CH).astype(BF16)
    o += SSD_CONV_CH
    for g in range(ATT_INNER // LANES):
        t = proj(o + g * LANES, LANES)
        k_ref[:, g * LANES:(g + 1) * LANES] = _rope_tile(t, cos, sin).astype(BF16)
    o += ATT_INNER
    for ref in (gs_ref, ga_ref):
        ref[...] = proj(o, D_MODEL).astype(BF16)
        o += D_MODEL
    dt_ref[...] = _dot(h, wdt_ref[...])[:, :SSD_HEADS]

    dtt_ref[...] = _dot_nt(wt_ref[2 * ATT_INNER:2 * ATT_INNER + SSD_HEADS, :], h)
    vt_ref[...] = _dot_nt(wt_ref[ATT_INNER:2 * ATT_INNER, :], h).astype(BF16)
    qt = _dot_nt(wt_ref[0:ATT_INNER, :], h)
    cost = cost_ref[...]
    sint = sint_ref[...]
    half = ATT_HEAD_DIM // 2
    for hd in range(ATT_HEADS):
        r0 = hd * ATT_HEAD_DIM
        x1 = qt[r0:r0 + half, :]
        x2 = qt[r0 + half:r0 + ATT_HEAD_DIM, :]
        qt_ref[r0:r0 + half, :] = ((x1 * cost - x2 * sint) * Q_SCALE).astype(BF16)
        qt_ref[r0 + half:r0 + ATT_HEAD_DIM, :] = ((x2 * cost + x1 * sint) * Q_SCALE).astype(BF16)


def _in_proj(x2d, nw, cosf, sinf, cost, sint, w_big, w_t, w_dt):
    t_tokens = x2d.shape[0]
    tm = min(TM_PROJ, t_tokens)
    grid = (t_tokens // tm,)
    row = lambda width: pl.BlockSpec((tm, width), lambda i: (i, 0))
    col = lambda height: pl.BlockSpec((height, tm), lambda i: (0, i))
    whole = pl.BlockSpec(memory_space=pltpu.VMEM)
    out_shapes = [
        jax.ShapeDtypeStruct((t_tokens, SSD_INNER), BF16),
        jax.ShapeDtypeStruct((t_tokens, SSD_CONV_CH), BF16),
        jax.ShapeDtypeStruct((t_tokens, ATT_INNER), BF16),
        jax.ShapeDtypeStruct((t_tokens, D_MODEL), BF16),
        jax.ShapeDtypeStruct((t_tokens, D_MODEL), BF16),
        jax.ShapeDtypeStruct((t_tokens, SSD_HEADS), F32),
        jax.ShapeDtypeStruct((ATT_INNER, t_tokens), BF16),
        jax.ShapeDtypeStruct((ATT_INNER, t_tokens), BF16),
        jax.ShapeDtypeStruct((SSD_HEADS, t_tokens), F32),
    ]
    out_specs = [row(SSD_INNER), row(SSD_CONV_CH), row(ATT_INNER), row(D_MODEL), row(D_MODEL), row(SSD_HEADS),
                 col(ATT_INNER), col(ATT_INNER), col(SSD_HEADS)]
    return pl.pallas_call(
        _in_proj_kernel,
        grid=grid,
        in_specs=[row(D_MODEL), whole, row(LANES), row(LANES), col(ATT_HEAD_DIM // 2), col(ATT_HEAD_DIM // 2),
                  whole, whole, whole],
        out_specs=out_specs,
        out_shape=out_shapes,
        compiler_params=pltpu.CompilerParams(dimension_semantics=("arbitrary",), vmem_limit_bytes=VMEM_LIMIT),
        name="in_proj",
    )(x2d, nw, cosf, sinf, cost, sint, w_big, w_t, w_dt)


def _ssd_kernel(z_ref, xbc_ref, xtail_ref, dt_ref, dtt_ref, cw_ref, cb_ref, dtb_ref, dtbc_ref, alog_ref, alogc_ref,
                dexp_ref, nw_ref, out_ref, state_ref):
    c = pl.program_id(1)
    cl = SSD_CHUNK
    heads_per_group = SSD_HEADS // SSD_GROUPS
    gw = heads_per_group * SSD_HEAD_DIM

    @pl.when(c == 0)
    def _():
        state_ref[...] = jnp.zeros_like(state_ref)

    xbc = xbc_ref[...]
    tail = xtail_ref[...]
    tail = jnp.where(c == 0, jnp.zeros_like(tail), tail)
    u_ext = jnp.concatenate([tail, xbc], axis=0)
    n_shift = SSD_CONV - 1
    srow = lax.broadcasted_iota(jnp.int32, (n_shift * cl, CONV_TAIL + cl), 0)
    scol = lax.broadcasted_iota(jnp.int32, (n_shift * cl, CONV_TAIL + cl), 1)
    shift_sel = (scol == CONV_TAIL + srow % cl - (n_shift - srow // cl)).astype(BF16)
    shifted = _dot(shift_sel, u_ext)
    acc = cb_ref[...] + cw_ref[n_shift:SSD_CONV, :] * xbc.astype(F32)
    for k in range(n_shift):
        acc = acc + cw_ref[k:k + 1, :] * shifted[k * cl:(k + 1) * cl, :]
    xc = _silu(acc)
    xs = xc[:, :SSD_INNER]
    bm = xc[:, SSD_INNER:SSD_INNER + SSD_GROUPS * SSD_STATE].astype(BF16)
    cm = xc[:, SSD_INNER + SSD_GROUPS * SSD_STATE:].astype(BF16)

    dt = _softplus(dt_ref[...] + dtb_ref[...])
    da = dt * (-jnp.exp(alog_ref[...]))
    dtt = _softplus(dtt_ref[...] + dtbc_ref[...])
    dat = dtt * (-jnp.exp(alogc_ref[...]))
    ri = lax.broadcasted_iota(jnp.int32, (cl, cl), 0)
    ci = lax.broadcasted_iota(jnp.int32, (cl, cl), 1)
    causal = ri >= ci
    tril = causal.astype(BF16)
    triu = (ri <= ci).astype(BF16)
    acum = _dot_exact_rhs(tril, da)
    acumt = _dot_exact_lhs(dat, triu)

    hrow = lax.broadcasted_iota(jnp.int32, (SSD_HEADS, SSD_INNER), 0)
    hcol = lax.broadcasted_iota(jnp.int32, (SSD_HEADS, SSD_INNER), 1) // SSD_HEAD_DIM
    expand = (hrow == hcol).astype(BF16)

    a_last = acum[cl - 1:cl, :]
    dt_e = _dot_exact_lhs(dt, expand)
    dec_e = _dot_exact_lhs(jnp.exp(a_last - acum), expand)
    ea_e = _dot_exact_lhs(jnp.exp(acum), expand)
    xdt = xs * dt_e
    xdt_b = xdt.astype(BF16)
    xdtdec_b = (xdt * dec_e).astype(BF16)
    chunk_decay_t = jnp.exp(acumt[:, cl - 1:cl])

    y_groups = []
    for g in range(SSD_GROUPS):
        b_g = bm[:, g * SSD_STATE:(g + 1) * SSD_STATE]
        c_g = cm[:, g * SSD_STATE:(g + 1) * SSD_STATE]
        prev = state_ref[g * gw:(g + 1) * gw, :]
        cb = _dot_nt(c_g, b_g)
        y_off = _dot_nt(c_g, prev.astype(BF16)) * ea_e[:, g * gw:(g + 1) * gw]
        new_states = _dot_tn(xdtdec_b[:, g * gw:(g + 1) * gw], b_g)
        y_heads = []
        decay_rows = []
        for e in range(heads_per_group):
            hd = g * heads_per_group + e
            lmat = jnp.where(causal, jnp.exp(acum[:, hd:hd + 1] - acumt[hd:hd + 1, :]), 0.0)
            m = (cb * lmat).astype(BF16)
            y_heads.append(_dot(m, xdt_b[:, hd * SSD_HEAD_DIM:(hd + 1) * SSD_HEAD_DIM]))
            decay_rows.append(jnp.broadcast_to(chunk_decay_t[hd:hd + 1, :], (SSD_HEAD_DIM, SSD_STATE)))
        y_groups.append(jnp.concatenate(y_heads, axis=1) + y_off)
        state_ref[g * gw:(g + 1) * gw, :] = prev * jnp.concatenate(decay_rows, axis=0) + new_states

    y = jnp.concatenate(y_groups, axis=1) + xs * dexp_ref[...]
    yg = y * _silu(z_ref[...].astype(F32))
    nw = nw_ref[...]
    outs = [_rms(yg[:, g * gw:(g + 1) * gw], nw[:, g * gw:(g + 1) * gw]) for g in range(SSD_GROUPS)]
    out_ref[...] = jnp.concatenate(outs, axis=1).astype(BF16)


def _ssd(z, xbc, dt, dtt, conv_w, conv_b, dt_bias, a_log, d_skip, norm_w, batch):
    t_tokens = z.shape[0]
    nc = t_tokens // batch // SSD_CHUNK
    tok = lambda width: pl.BlockSpec((SSD_CHUNK, width), lambda b, c: (b * nc + c, 0))
    whole = pl.BlockSpec(memory_space=pltpu.VMEM)
    dexp = jnp.repeat(d_skip, SSD_HEAD_DIM)[None, :]
    tails_per_chunk = SSD_CHUNK // CONV_TAIL
    prev_tail = pl.BlockSpec((CONV_TAIL, SSD_CONV_CH),
                             lambda b, c: (jnp.maximum((b * nc + c) * tails_per_chunk - 1, 0), 0))
    return pl.pallas_call(
        _ssd_kernel,
        grid=(batch, nc),
        in_specs=[tok(SSD_INNER), tok(SSD_CONV_CH), prev_tail, tok(SSD_HEADS),
                  pl.BlockSpec((SSD_HEADS, SSD_CHUNK), lambda b, c: (0, b * nc + c)),
                  whole, whole, whole, whole, whole, whole, whole, whole],
        out_specs=tok(SSD_INNER),
        out_shape=jax.ShapeDtypeStruct((t_tokens, SSD_INNER), BF16),
        scratch_shapes=[pltpu.VMEM((SSD_INNER, SSD_STATE), F32)],
        compiler_params=pltpu.CompilerParams(dimension_semantics=("arbitrary", "arbitrary"),
                                             vmem_limit_bytes=VMEM_LIMIT),
        name="ssd",
    )(z, xbc, xbc, dt, dtt, conv_w, conv_b[None, :], dt_bias[None, :], dt_bias[:, None],
      a_log[None, :], a_log[:, None], dexp, norm_w[None, :])


MOBA_SUPER = 2 * MOBA_BLOCK


MOBA_BLOCK_ROWS = 16
ONES_ROWS = 16


def _moba_kernel(qt_ref, k_ref, vt_ref, o_ref, kfull_ref, qaug_ref, sa_ref, sb_ref, m_ref, l_ref, acc_ref, *, seq):
    nb = seq // MOBA_BLOCK
    sb = MOBA_SUPER
    nsb = seq // sb
    hd = ATT_HEAD_DIM
    nbr = MOBA_BLOCK_ROWS

    for c in range(nsb):
        rows = slice(c * sb, (c + 1) * sb)
        kfull_ref[rows, 0:LANES] = k_ref[rows, :]
        rblk = lax.broadcasted_iota(jnp.int32, (sb, LANES), 0) // MOBA_BLOCK + c * (sb // MOBA_BLOCK)
        rlane = lax.broadcasted_iota(jnp.int32, (sb, LANES), 1)
        kfull_ref[rows, LANES:2 * LANES] = jnp.where(rblk == rlane, 1.0, 0.0).astype(BF16)

    pr = lax.broadcasted_iota(jnp.int32, (nbr, seq), 0)
    pc = lax.broadcasted_iota(jnp.int32, (nbr, seq), 1) // MOBA_BLOCK
    pool = jnp.where(pr == pc, 1.0 / MOBA_BLOCK, 0.0).astype(BF16)
    kmean = _dot(pool, k_ref[...])
    km_hi = kmean.astype(BF16)
    km_lo = (kmean - km_hi.astype(F32)).astype(BF16)

    head_of_row = lax.broadcasted_iota(jnp.int32, (LANES, sb), 0) // hd
    blk_row = lax.broadcasted_iota(jnp.int32, (nbr, sb), 0)
    q_local = lax.broadcasted_iota(jnp.int32, (nbr, sb), 1)
    inrange = blk_row < nb
    krow = lax.broadcasted_iota(jnp.int32, (sb, 2 * sb), 0)
    qcol = lax.broadcasted_iota(jnp.int32, (sb, 2 * sb), 1) % sb
    causal = krow <= qcol
    ones_rows = jnp.ones((ONES_ROWS, sb), BF16)
    flag_pad = jnp.zeros((LANES - nbr, sb), BF16)

    def scores_into(s_ref, kv):
        start = pl.multiple_of(kv * sb, sb)
        s_ref[...] = _dot(kfull_ref[pl.ds(start, sb), :], qaug_ref[...])

    def absorb(kv, s_ref, diagonal):
        start = pl.multiple_of(kv * sb, sb)
        s_t = s_ref[...]
        if diagonal:
            s_t = jnp.where(causal, s_t, NEG_BIG)
        m_prev = m_ref[...]
        m_new = jnp.maximum(m_prev, jnp.max(s_t, axis=0, keepdims=True))
        m_ref[...] = m_new
        p_b = jnp.exp2(s_t - m_new).astype(BF16)
        alpha = jnp.exp2(m_prev - m_new)
        for a in range(2):
            cols = slice(a * sb, (a + 1) * sb)
            rows = slice(a * hd, (a + 1) * hd)
            v_aug = jnp.concatenate([vt_ref[rows, pl.ds(start, sb)], ones_rows], axis=0)
            r = _dot(v_aug, p_b[:, cols])
            l_ref[a:a + 1, :] = alpha[:, cols] * l_ref[a:a + 1, :] + r[hd:hd + 1, :]
            acc_ref[rows, :] = alpha[:, cols] * acc_ref[rows, :] + r[0:hd, :]

    def q_super(i, carry):
        qstart = pl.multiple_of(i * sb, sb)
        qt_sb = qt_ref[:, pl.ds(qstart, sb)]
        own = 2 * i + jnp.where(q_local >= MOBA_BLOCK, 1, 0)
        valid = blk_row < own
        for a in range(2):
            qa = jnp.where(head_of_row == a, qt_sb, jnp.zeros_like(qt_sb))
            gate = _dot(km_hi, qa) + _dot(km_lo, qa)
            g = jnp.where(valid, gate, -jnp.inf)
            sel = jnp.zeros((nbr, sb), jnp.bool_)
            for _ in range(min(MOBA_TOPK, nb - 1)):
                cand = inrange & jnp.logical_not(sel)
                gm = jnp.max(jnp.where(cand, g, -jnp.inf), axis=0, keepdims=True)
                hit = cand & (g == gm)
                first = jnp.min(jnp.where(hit, blk_row, nb), axis=0, keepdims=True)
                sel = sel | (blk_row == first)
            chosen = (sel & valid) | (blk_row == own)
            flags = jnp.where(chosen, 0.0, NEG_BIG).astype(BF16)
            qaug_ref[:, a * sb:(a + 1) * sb] = jnp.concatenate([qa, flags, flag_pad], axis=0)

        m_ref[...] = jnp.full(m_ref.shape, NEG_BIG, F32)
        l_ref[...] = jnp.zeros_like(l_ref)
        acc_ref[...] = jnp.zeros_like(acc_ref)
        scores_into(sa_ref, 0)

        def two_steps(t, carry):
            scores_into(sb_ref, 2 * t + 1)
            absorb(2 * t, sa_ref, False)
            scores_into(sa_ref, 2 * t + 2)
            absorb(2 * t + 1, sb_ref, False)
            return carry

        lax.fori_loop(0, i // 2, two_steps, 0)

        @pl.when(i % 2 == 1)
        def _():
            scores_into(sb_ref, i)
            absorb(i - 1, sa_ref, False)
            absorb(i, sb_ref, True)

        @pl.when(i % 2 == 0)
        def _():
            absorb(i, sa_ref, True)

        out_t = jnp.concatenate([acc_ref[a * hd:(a + 1) * hd, :] / l_ref[a:a + 1, :] for a in range(2)], axis=0)
        o_ref[pl.ds(qstart, sb), :] = out_t.T.astype(BF16)
        return carry

    lax.fori_loop(0, nsb, q_super, 0)


def _moba(qt, k, vt, batch):
    t_tokens = k.shape[0]
    seq = t_tokens // batch
    assert seq % MOBA_SUPER == 0 and seq // MOBA_BLOCK <= MOBA_BLOCK_ROWS
    spec = pl.BlockSpec((seq, LANES), lambda b, hp: (b, hp))
    spec_t = pl.BlockSpec((LANES, seq), lambda b, hp: (hp, b))
    return pl.pallas_call(
        functools.partial(_moba_kernel, seq=seq),
        grid=(batch, ATT_INNER // LANES),
        in_specs=[spec_t, spec, spec_t],
        out_specs=spec,
        out_shape=jax.ShapeDtypeStruct((t_tokens, ATT_INNER), BF16),
        scratch_shapes=[pltpu.VMEM((seq, 2 * LANES), BF16),
                        pltpu.VMEM((2 * LANES, 2 * MOBA_SUPER), BF16),
                        pltpu.VMEM((MOBA_SUPER, 2 * MOBA_SUPER), F32),
                        pltpu.VMEM((MOBA_SUPER, 2 * MOBA_SUPER), F32),
                        pltpu.VMEM((1, 2 * MOBA_SUPER), F32),
                        pltpu.VMEM((SUBLANES, MOBA_SUPER), F32),
                        pltpu.VMEM((LANES, MOBA_SUPER), F32)],
        compiler_params=pltpu.CompilerParams(dimension_semantics=("arbitrary", "arbitrary"),
                                             vmem_limit_bytes=VMEM_LIMIT),
        name="moba",
    )(qt, k, vt)


ROUTER_ROWS = 40


def _merge_kernel(x_ref, yn_ref, att_ref, gs_ref, ga_ref, wso_ref, wao_ref, wo_ref, nw_ref,
                  wr_hi_ref, wr_lo_ref, br_ref, upper_ref, x1_ref, h2_ref, route_ref, jloc_ref, cnt_ref):
    y_s = _dot(yn_ref[...], wso_ref[...])
    y_a = _dot(att_ref[...], wao_ref[...])
    merged = _sigmoid(gs_ref[...].astype(F32)) * y_s + _sigmoid(ga_ref[...].astype(F32)) * y_a
    x1 = x_ref[...] + _dot(merged.astype(BF16), wo_ref[...])
    x1_ref[...] = x1
    h2 = _rms(x1, nw_ref[...])
    h2_ref[...] = h2.astype(BF16)

    h_hi = h2.astype(BF16)
    h_lo = (h2 - h_hi.astype(F32)).astype(BF16)
    wr_hi = wr_hi_ref[...]
    logits = _dot_nt(wr_hi, h_hi) + _dot_nt(wr_hi, h_lo) + _dot_nt(wr_lo_ref[...], h_hi) + br_ref[...]
    gl = logits[0:N_GROUPS, :]
    gidx = lax.broadcasted_iota(jnp.int32, gl.shape, 0)
    gmax = jnp.max(gl, axis=0, keepdims=True)
    gi = jnp.min(jnp.where(gl == gmax, gidx, N_GROUPS), axis=0, keepdims=True)
    g_w = 1.0 / jnp.sum(jnp.exp(gl - gmax), axis=0, keepdims=True)
    el = jnp.zeros((EXPERTS_PER_GROUP, gl.shape[1]), F32)
    for g in range(N_GROUPS):
        lo = N_GROUPS + g * EXPERTS_PER_GROUP
        el = el + jnp.where(gi == g, logits[lo:lo + EXPERTS_PER_GROUP, :], 0.0)
    eidx = lax.broadcasted_iota(jnp.int32, el.shape, 0)
    v0 = jnp.max(el, axis=0, keepdims=True)
    i0 = jnp.min(jnp.where(el == v0, eidx, EXPERTS_PER_GROUP), axis=0, keepdims=True)
    rest = jnp.where(eidx == i0, -jnp.inf, el)
    v1 = jnp.max(rest, axis=0, keepdims=True)
    i1 = jnp.min(jnp.where((rest == v1) & (eidx != i0), eidx, EXPERTS_PER_GROUP), axis=0, keepdims=True)
    t = jnp.exp(v1 - v0)
    w0 = g_w / (1.0 + t)
    w1 = g_w * t / (1.0 + t)
    base = gi * EXPERTS_PER_GROUP
    route_ref[...] = jnp.concatenate(
        [(base + i0).astype(F32), (base + i1).astype(F32), w0, w1], axis=0)

    tm = gl.shape[1]
    eids = jnp.concatenate([base + i0, base + i1], axis=1)
    onehot = lax.broadcasted_iota(jnp.int32, (N_EXPERTS, 2 * tm), 0) == eids
    csum = _dot(onehot.astype(BF16), upper_ref[...])
    oh = onehot.astype(F32)
    lrank = jnp.sum(csum * oh, axis=0, keepdims=True) - 1.0
    cnt = jnp.floor((csum[:, 2 * tm - 1:2 * tm] + (RUN_ALIGN - 1)) * (1.0 / RUN_ALIGN))
    er = lax.broadcasted_iota(jnp.int32, (N_EXPERTS, N_EXPERTS), 0)
    ec = lax.broadcasted_iota(jnp.int32, (N_EXPERTS, N_EXPERTS), 1)
    before = _dot((ec < er).astype(BF16), jnp.broadcast_to(cnt, (N_EXPERTS, LANES)).astype(BF16))
    off = before[:, 0:1] * RUN_ALIGN
    jloc = jnp.sum(oh * off, axis=0, keepdims=True) + lrank
    jloc_ref[...] = jnp.concatenate([jloc[:, :tm], jloc[:, tm:]], axis=0).astype(jnp.int32)
    cnt_ref[...] = jnp.broadcast_to(cnt * RUN_ALIGN, (N_EXPERTS, LANES))


def _merge(x2d, yn, att, gs, ga, wso, wao, wo, nw, wr_hi, wr_lo, br):
    t_tokens = x2d.shape[0]
    tm = TM_TOK
    upper = jnp.triu(jnp.ones((TILE_ASG, TILE_ASG), BF16))
    row = lambda: pl.BlockSpec((tm, D_MODEL), lambda i: (i, 0))
    whole = pl.BlockSpec(memory_space=pltpu.VMEM)
    return pl.pallas_call(
        _merge_kernel,
        grid=(t_tokens // tm,),
        in_specs=[row(), row(), row(), row(), row(), whole, whole, whole, whole, whole, whole, whole, whole],
        out_specs=[row(), row(), pl.BlockSpec((4, tm), lambda i: (0, i)), pl.BlockSpec((2, tm), lambda i: (0, i)),
                   pl.BlockSpec((N_EXPERTS, LANES), lambda i: (i, 0))],
        out_shape=[jax.ShapeDtypeStruct((t_tokens, D_MODEL), F32),
                   jax.ShapeDtypeStruct((t_tokens, D_MODEL), BF16),
                   jax.ShapeDtypeStruct((4, t_tokens), F32),
                   jax.ShapeDtypeStruct((2, t_tokens), jnp.int32),
                   jax.ShapeDtypeStruct((t_tokens // tm * N_EXPERTS, LANES), F32)],
        compiler_params=pltpu.CompilerParams(dimension_semantics=("arbitrary",), vmem_limit_bytes=VMEM_LIMIT),
        name="merge",
    )(x2d, yn, att, gs, ga, wso, wao, wo, nw, wr_hi, wr_lo, br, upper)


def _rows(ref, first, n_rows):
    start = first if isinstance(first, int) else pl.multiple_of(first, RUN_ALIGN)
    return ref.at[pl.ds(start, n_rows), :]


def _for_each_piece(count, fn, bits):
    for bit in range(bits - 1, ALIGN_BIT - 1, -1):
        size = 1 << bit

        @pl.when((count & size) != 0)
        def _():
            fn((count >> (bit + 1)) << (bit + 1), size)


def _moe_max_tiles(t_tokens):
    rows = 2 * t_tokens + (t_tokens // TM_TOK) * N_EXPERTS * (RUN_ALIGN - 1)
    return -(-rows // TM_MOE) + N_EXPERTS


def _moe_plan(cnt_rows, max_tiles):
    cnt = cnt_rows[:, 0].astype(jnp.int32).reshape(-1, N_EXPERTS)
    off = jnp.cumsum(cnt, axis=1) - cnt
    counts = jnp.sum(cnt, axis=0)
    padded = ((counts + TM_MOE - 1) // TM_MOE) * TM_MOE
    ends = jnp.cumsum(padded)
    starts = ends - padded
    base = starts[None, :] + jnp.cumsum(cnt, axis=0) - cnt
    tile_start = jnp.arange(max_tiles, dtype=jnp.int32) * TM_MOE
    tile_expert = jnp.sum((tile_start[:, None] >= ends[None, :]).astype(jnp.int32), axis=1)
    tile_expert = jnp.minimum(tile_expert, N_EXPERTS - 1)
    n_tiles = (ends[-1] // TM_MOE).astype(jnp.int32).reshape(1)
    flat = lambda a: a.reshape(-1).astype(jnp.int32)
    return flat(cnt), flat(off), flat(base), starts + counts, ends, tile_expert, n_tiles


def _dispatch_kernel(cnt_ref, off_ref, base_ref, padlo_ref, padhi_ref, ntile_ref, h2_ref, jrow_ref, xs_hbm,
                     xbuf0, xbuf1, zeros_ref, sems, zsem, *, max_tiles):
    i = pl.program_id(0)
    n = pl.num_programs(0)
    xbufs = (xbuf0, xbuf1)

    def runs(tile, slot, wait):
        def per_expert(e, carry):
            k = tile * N_EXPERTS + e
            src, dst = off_ref[k], base_ref[k]

            def piece(o, size):
                cp = pltpu.make_async_copy(_rows(xbufs[slot], src + o, size), _rows(xs_hbm, dst + o, size),
                                           sems.at[slot])
                cp.wait() if wait else cp.start()
            _for_each_piece(cnt_ref[k], piece, RUN_BITS)
            return carry
        lax.fori_loop(0, N_EXPERTS, per_expert, 0)

    @pl.when(i == 0)
    def _():
        zeros_ref[...] = jnp.zeros_like(zeros_ref)
        nt = ntile_ref[0]
        for wait in (False, True):
            def pad(e, carry):
                lo = padlo_ref[e]

                def piece(o, size):
                    cp = pltpu.make_async_copy(_rows(zeros_ref, 0, size), _rows(xs_hbm, lo + o, size), zsem)
                    cp.wait() if wait else cp.start()
                _for_each_piece(padhi_ref[e] - lo, piece, PAD_BITS)
                return carry
            lax.fori_loop(0, N_EXPERTS, pad, 0)

            def spare(tile, carry):
                cp = pltpu.make_async_copy(zeros_ref, _rows(xs_hbm, tile * TM_MOE, TM_MOE), zsem)
                cp.wait() if wait else cp.start()
                return carry
            lax.fori_loop(nt, max_tiles, spare, 0)

    for slot in range(2):
        @pl.when(i % 2 == slot)
        def _():
            @pl.when(i >= 2)
            def _():
                runs(i - 2, slot, True)
            row = lax.broadcasted_iota(jnp.int32, (LOC_ROWS, TM_TOK), 0)
            j = jrow_ref[...]
            perm = ((row == j[0:1, :]) | (row == j[1:2, :])).astype(BF16)
            xbufs[slot][...] = _dot(perm, h2_ref[...])
            runs(i, slot, False)

            @pl.when(i == n - 1)
            def _():
                @pl.when(i >= 1)
                def _():
                    runs(i - 1, 1 - slot, True)
                runs(i, slot, True)


def _dispatch(plan, jloc, h2, max_tiles):
    cnt, off, base, pad_lo, pad_hi, _, n_tiles = plan
    t_tokens = h2.shape[0]
    grid_spec = pltpu.PrefetchScalarGridSpec(
        num_scalar_prefetch=6,
        grid=(t_tokens // TM_TOK,),
        in_specs=[pl.BlockSpec((TM_TOK, D_MODEL), lambda i, *_: (i, 0)),
                  pl.BlockSpec((2, TM_TOK), lambda i, *_: (0, i))],
        out_specs=pl.BlockSpec(memory_space=pl.ANY),
        scratch_shapes=[pltpu.VMEM((LOC_ROWS, D_MODEL), F32),
                        pltpu.VMEM((LOC_ROWS, D_MODEL), F32),
                        pltpu.VMEM((TM_MOE, D_MODEL), F32),
                        pltpu.SemaphoreType.DMA((2,)), pltpu.SemaphoreType.DMA(())],
    )
    return pl.pallas_call(
        functools.partial(_dispatch_kernel, max_tiles=max_tiles),
        grid_spec=grid_spec,
        out_shape=jax.ShapeDtypeStruct((max_tiles * TM_MOE, D_MODEL), F32),
        compiler_params=pltpu.CompilerParams(dimension_semantics=("arbitrary",), vmem_limit_bytes=VMEM_LIMIT),
        name="dispatch",
    )(cnt, off, base, pad_lo, pad_hi, n_tiles, h2, jloc)


def _moe_kernel(texp_ref, ntile_ref, xs_ref, wgu_ref, wd_ref, y_ref):
    t = pl.program_id(0)

    @pl.when(t < ntile_ref[0])
    def _():
        gu = _dot(xs_ref[...].astype(BF16), wgu_ref[0])
        hid = (_silu(gu[:, :D_EXPERT]) * gu[:, D_EXPERT:]).astype(BF16)
        y_ref[...] = _dot(hid, wd_ref[0])

    @pl.when(t >= ntile_ref[0])
    def _():
        y_ref[...] = jnp.zeros_like(y_ref)


def _moe(xs, tile_expert, n_tiles, wgu, wd, max_tiles):
    live = lambda t, nt: jnp.minimum(t, nt[0] - 1)
    tile = lambda index: pl.BlockSpec((TM_MOE, D_MODEL), index)
    grid_spec = pltpu.PrefetchScalarGridSpec(
        num_scalar_prefetch=2,
        grid=(max_tiles,),
        in_specs=[tile(lambda t, te, nt: (live(t, nt), 0)),
                  pl.BlockSpec((1, D_MODEL, 2 * D_EXPERT), lambda t, te, nt: (te[live(t, nt)], 0, 0)),
                  pl.BlockSpec((1, D_EXPERT, D_MODEL), lambda t, te, nt: (te[live(t, nt)], 0, 0))],
        out_specs=tile(lambda t, te, nt: (t, 0)),
    )
    return pl.pallas_call(
        _moe_kernel,
        grid_spec=grid_spec,
        out_shape=jax.ShapeDtypeStruct((max_tiles * TM_MOE, D_MODEL), F32),
        compiler_params=pltpu.CompilerParams(dimension_semantics=("arbitrary",), vmem_limit_bytes=VMEM_LIMIT),
        name="moe",
    )(tile_expert, n_tiles, xs, wgu, wd)


def _tail_kernel(cnt_ref, off_ref, base_ref, x1_ref, p_ref, jw_ref, y_hbm, pnw_ref, wpg_ref, wple_ref, fnw_ref,
                 o_ref, ybuf0, ybuf1, sems):
    i = pl.program_id(0)
    n = pl.num_programs(0)
    ybufs = (ybuf0, ybuf1)

    def runs(tile, slot, wait):
        def per_expert(e, carry):
            k = tile * N_EXPERTS + e
            src, dst = base_ref[k], off_ref[k]

            def piece(o, size):
                cp = pltpu.make_async_copy(_rows(y_hbm, src + o, size), _rows(ybufs[slot], dst + o, size),
                                           sems.at[slot])
                cp.wait() if wait else cp.start()
            _for_each_piece(cnt_ref[k], piece, RUN_BITS)
            return carry
        lax.fori_loop(0, N_EXPERTS, per_expert, 0)

    @pl.when(i == 0)
    def _():
        ybuf0[...] = jnp.zeros_like(ybuf0)
        ybuf1[...] = jnp.zeros_like(ybuf1)
        runs(0, 0, False)

    for slot in range(2):
        @pl.when(i % 2 == slot)
        def _():
            @pl.when(i + 1 < n)
            def _():
                runs(i + 1, 1 - slot, False)
            runs(i, slot, True)
            y_loc = ybufs[slot][...].astype(BF16)
            jw = jw_ref[...]
            col = lax.broadcasted_iota(jnp.int32, (TM_TOK, LOC_ROWS), 1)
            pick = sum(jnp.where(col == jw[:, s:s + 1].astype(jnp.int32), jw[:, 2 + s:3 + s], 0.0) for s in range(2))
            moe = _dot(pick.astype(BF16), y_loc)
            x2 = x1_ref[...] + moe
            gate = _sigmoid(_dot(_rms(x2, pnw_ref[...]).astype(BF16), wpg_ref[...]))
            x3 = x2 + gate * _dot(p_ref[...].astype(BF16), wple_ref[...])
            o_ref[...] = _rms(x3, fnw_ref[...])


def _tail(plan, x1, p2d, y_sorted, jw, pnw, wpg, wple, fnw):
    cnt, off, base = plan[0:3]
    t_tokens = x1.shape[0]
    whole = pl.BlockSpec(memory_space=pltpu.VMEM)
    grid_spec = pltpu.PrefetchScalarGridSpec(
        num_scalar_prefetch=3,
        grid=(t_tokens // TM_TOK,),
        in_specs=[pl.BlockSpec((TM_TOK, D_MODEL), lambda i, *_: (i, 0)),
                  pl.BlockSpec((TM_TOK, PLE_DIM), lambda i, *_: (i, 0)),
                  pl.BlockSpec((TM_TOK, 4), lambda i, *_: (i, 0)),
                  pl.BlockSpec(memory_space=pl.ANY),
                  whole, whole, whole, whole],
        out_specs=pl.BlockSpec((TM_TOK, D_MODEL), lambda i, *_: (i, 0)),
        scratch_shapes=[pltpu.VMEM((LOC_ROWS, D_MODEL), F32),
                        pltpu.VMEM((LOC_ROWS, D_MODEL), F32),
                        pltpu.SemaphoreType.DMA((2,))],
    )
    return pl.pallas_call(
        _tail_kernel,
        grid_spec=grid_spec,
        out_shape=jax.ShapeDtypeStruct((t_tokens, D_MODEL), F32),
        compiler_params=pltpu.CompilerParams(dimension_semantics=("arbitrary",), vmem_limit_bytes=VMEM_LIMIT),
        name="tail",
    )(cnt, off, base, x1, p2d, jw, y_sorted, pnw, wpg, wple, fnw)


def _rope_tables(positions):
    half = ATT_HEAD_DIM // 2
    inv_freq = ROPE_THETA ** (-jnp.arange(half, dtype=F32) / half)
    ang = positions.astype(F32)[..., None] * inv_freq
    cos = jnp.cos(ang).reshape(-1, half)
    sin = jnp.sin(ang).reshape(-1, half)
    reps = LANES // ATT_HEAD_DIM
    cosf = jnp.tile(jnp.concatenate([cos, cos], axis=1), (1, reps))
    sinf = jnp.tile(jnp.concatenate([-sin, sin], axis=1), (1, reps))
    return cosf, sinf, cos.T, sin.T


def _layer(x2d, p2d, rope_tables, batch, attn_norm_w, w_in, conv_w, conv_b, dt_bias, a_log, d_skip, ssd_norm_w,
           w_ssd_out, w_attn_out, w_out, moe_norm_w, w_rg, b_rg, w_re, b_re, w_gate, w_up, w_down,
           ple_norm_w, w_ple, w_ple_gate, final_norm_w):
    t_tokens = x2d.shape[0]
    offs = np.cumsum((0,) + IN_SPLITS)
    cols = lambda i: w_in[:, offs[i]:offs[i + 1]]
    w_big = jnp.concatenate([cols(0), cols(1), cols(4), cols(6), cols(7)], axis=1).astype(BF16)
    w_t = jnp.concatenate([cols(3), cols(5), cols(2)], axis=1).T.astype(BF16)
    w_dt = jnp.pad(cols(2), ((0, 0), (0, LANES - SSD_HEADS))).astype(BF16)

    z, xbc, k, gs, ga, dt, qt, vt, dtt = _in_proj(x2d, attn_norm_w[None, :], *rope_tables, w_big, w_t, w_dt)

    yn = _ssd(z, xbc, dt, dtt, conv_w, conv_b, dt_bias, a_log, d_skip, ssd_norm_w, batch)

    att = _moba(qt, k, vt, batch)

    wr = jnp.concatenate([w_rg, w_re], axis=1).T
    wr = jnp.pad(wr, ((0, ROUTER_ROWS - wr.shape[0]), (0, 0)))
    wr_hi = wr.astype(BF16)
    wr_lo = (wr - wr_hi.astype(F32)).astype(BF16)
    br = jnp.pad(jnp.concatenate([b_rg, b_re]), (0, ROUTER_ROWS - N_GROUPS - N_EXPERTS))[:, None]
    x1, h2, route, jloc, cnt_rows = _merge(x2d, yn, att, gs, ga, w_ssd_out.astype(BF16), w_attn_out.astype(BF16),
                           w_out.astype(BF16), moe_norm_w[None, :], wr_hi, wr_lo, br)

    max_tiles = _moe_max_tiles(t_tokens)
    plan = _moe_plan(cnt_rows, max_tiles)
    wgu = jnp.concatenate([w_gate, w_up], axis=-1).reshape(N_EXPERTS, D_MODEL, 2 * D_EXPERT).astype(BF16)
    wd = w_down.reshape(N_EXPERTS, D_EXPERT, D_MODEL).astype(BF16)
    x_sorted = _dispatch(plan, jloc, h2, max_tiles)
    y_sorted = _moe(x_sorted, plan[5], plan[6], wgu, wd, max_tiles)
    jw = jnp.concatenate([jloc.astype(F32), route[2:4]], axis=0).T

    return _tail(plan, x1, p2d, y_sorted, jw, ple_norm_w[None, :],
                 w_ple_gate.astype(BF16), w_ple.astype(BF16), final_norm_w[None, :])


def kernel(x, p, positions, attn_norm_w, w_in, conv_w, conv_b, dt_bias, a_log, d_skip, ssd_norm_w, w_ssd_out,
           w_attn_out, w_out, moe_norm_w, w_router_group, b_router_group, w_router_expert, b_router_expert,
           w_exp_gate, w_exp_up, w_exp_down, ple_norm_w, w_ple, w_ple_gate, final_norm_w):
    batch, seq, d = x.shape
    depth = p.shape[0]
    assert depth == 1, "the final norm is fused into the last layer's tail kernel"
    rope_tables = _rope_tables(positions)
    i = 0
    out = _layer(x.reshape(batch * seq, d), p[i].reshape(batch * seq, PLE_DIM), rope_tables, batch,
                 attn_norm_w[i], w_in[i], conv_w[i], conv_b[i], dt_bias[i], a_log[i], d_skip[i], ssd_norm_w[i],
                 w_ssd_out[i], w_attn_out[i], w_out[i], moe_norm_w[i], w_router_group[i], b_router_group[i],
                 w_router_expert[i], b_router_expert[i], w_exp_gate[i], w_exp_up[i], w_exp_down[i],
                 ple_norm_w[i], w_ple[i], w_ple_gate[i], final_norm_w)
    return out.reshape(batch, seq, d)
```

```python
import functools

import jax
import jax.numpy as jnp
import numpy as np
from jax import lax
from jax.experimental import pallas as pl
from jax.experimental.pallas import tpu as pltpu

F32 = jnp.float32
BF16 = jnp.bfloat16

EPS = 1e-6
D_MODEL = 1024
SSD_HEADS = 16
SSD_HEAD_DIM = 64
SSD_INNER = SSD_HEADS * SSD_HEAD_DIM
SSD_GROUPS = 2
SSD_STATE = 64
SSD_CONV = 4
SSD_CHUNK = 128
SSD_CONV_CH = SSD_INNER + 2 * SSD_GROUPS * SSD_STATE
CONV_TAIL = 16
ATT_HEADS = 16
ATT_HEAD_DIM = 64
ATT_INNER = ATT_HEADS * ATT_HEAD_DIM
MOBA_BLOCK = 256
MOBA_TOPK = 3
ROPE_THETA = 10000.0
IN_SPLITS = (SSD_INNER, SSD_CONV_CH, SSD_HEADS, ATT_INNER, ATT_INNER, ATT_INNER, D_MODEL, D_MODEL)
N_GROUPS = 4
EXPERTS_PER_GROUP = 8
N_EXPERTS = N_GROUPS * EXPERTS_PER_GROUP
D_EXPERT = 256
PLE_DIM = 256

LANES = 128
SUBLANES = 8
NEG_BIG = -1e30
LOG2E = 1.4426950408889634
VMEM_LIMIT = 56 * 1024 * 1024

TM_PROJ = 512
TM_MOE = 256
TM_TOK = 512
TILE_ASG = 2 * TM_TOK
RUN_ALIGN = 2 * SUBLANES
LOC_ROWS = TILE_ASG + N_EXPERTS * RUN_ALIGN
ALIGN_BIT = RUN_ALIGN.bit_length() - 1
RUN_BITS = TILE_ASG.bit_length()
PAD_BITS = TM_MOE.bit_length() - 1


def _dot(a, b):
    return jnp.dot(a, b, preferred_element_type=F32)


def _dot_nt(a, b):
    return lax.dot_general(a, b, (((1,), (1,)), ((), ())), preferred_element_type=F32)


def _dot_tn(a, b):
    return lax.dot_general(a, b, (((0,), (0,)), ((), ())), preferred_element_type=F32)


def _split3(a):
    hi = a.astype(BF16)
    r1 = a - hi.astype(F32)
    mid = r1.astype(BF16)
    lo = (r1 - mid.astype(F32)).astype(BF16)
    return hi, mid, lo


def _dot_exact_rhs(sel_bf16, a):
    return _dot(jnp.concatenate([sel_bf16] * 3, axis=1), jnp.concatenate(_split3(a), axis=0))


def _dot_exact_lhs(a, sel_bf16):
    return _dot(jnp.concatenate(_split3(a), axis=1), jnp.concatenate([sel_bf16] * 3, axis=0))


def _sigmoid(x):
    return 0.5 * jnp.tanh(0.5 * x) + 0.5


def _silu(x):
    return x * _sigmoid(x)


def _softplus(x):
    return jnp.maximum(x, 0.0) + jnp.log1p(jnp.exp(-jnp.abs(x)))


def _rms(x, w):
    inv = lax.rsqrt(jnp.mean(x * x, axis=-1, keepdims=True) + EPS)
    return (x * inv) * w


def _rope_tile(t, cos, sin_signed):
    half = ATT_HEAD_DIM // 2
    lane = lax.broadcasted_iota(jnp.int32, t.shape, 1)
    first = (lane % ATT_HEAD_DIM) < half
    partner = jnp.where(first, pltpu.roll(t, LANES - half, 1), pltpu.roll(t, half, 1))
    return t * cos + partner * sin_signed


Q_SCALE = ATT_HEAD_DIM ** -0.5 * LOG2E


def _in_proj_kernel(x_ref, nw_ref, cos_ref, sin_ref, cost_ref, sint_ref, w_ref, wt_ref, wdt_ref,
                    z_ref, xbc_ref, k_ref, gs_ref, ga_ref, dt_ref, qt_ref, vt_ref, dtt_ref):
    h = _rms(x_ref[...], nw_ref[...]).astype(BF16)
    half = ATT_HEAD_DIM // 2
    lane = lax.broadcasted_iota(jnp.int32, (half, LANES), 1)
    same = lane % half == lax.broadcasted_iota(jnp.int32, (half, LANES), 0)
    first = lane % ATT_HEAD_DIM < half
    cos = _dot_exact_lhs(cos_ref[...], same.astype(BF16))
    sin = _dot_exact_lhs(sin_ref[...], jnp.where(same, jnp.where(first, -1.0, 1.0), 0.0).astype(BF16))

    def proj(lo, width):
        return _dot(h, w_ref[:, lo:lo + width])

    o = 0
    z_ref[...] = proj(o, SSD_INNER).astype(BF16)
    o += SSD_INNER
    xbc_ref[...] = proj(o, SSD_CONV_CH).astype(BF16)
    o += SSD_CONV_CH
    k_lin = proj(o, ATT_INNER)
    for g in range(ATT_INNER // LANES):
        cols = slice(g * LANES, (g + 1) * LANES)
        k_ref[:, cols] = _rope_tile(k_lin[:, cols], cos, sin).astype(BF16)
    o += ATT_INNER
    for ref in (gs_ref, ga_ref):
        ref[...] = proj(o, D_MODEL).astype(BF16)
        o += D_MODEL
    dt_ref[...] = _dot(h, wdt_ref[...])[:, :SSD_HEADS]

    dtt_ref[...] = _dot_nt(wt_ref[2 * ATT_INNER:2 * ATT_INNER + SSD_HEADS, :], h)
    vt_ref[...] = _dot_nt(wt_ref[ATT_INNER:2 * ATT_INNER, :], h).astype(BF16)
    qt = _dot_nt(wt_ref[0:ATT_INNER, :], h)
    cost = cost_ref[...]
    sint = sint_ref[...]
    for hd in range(ATT_HEADS):
        r0 = hd * ATT_HEAD_DIM
        x1 = qt[r0:r0 + half, :]
        x2 = qt[r0 + half:r0 + ATT_HEAD_DIM, :]
        qt_ref[r0:r0 + half, :] = ((x1 * cost - x2 * sint) * Q_SCALE).astype(BF16)
        qt_ref[r0 + half:r0 + ATT_HEAD_DIM, :] = ((x2 * cost + x1 * sint) * Q_SCALE).astype(BF16)


def _in_proj(x2d, nw, cosf, sinf, cost, sint, w_big, w_t, w_dt):
    t_tokens = x2d.shape[0]
    tm = min(TM_PROJ, t_tokens)
    grid = (t_tokens // tm,)
    row = lambda width: pl.BlockSpec((tm, width), lambda i: (i, 0))
    col = lambda height: pl.BlockSpec((height, tm), lambda i: (0, i))
    whole = pl.BlockSpec(memory_space=pltpu.VMEM)
    out_shapes = [
        jax.ShapeDtypeStruct((t_tokens, SSD_INNER), BF16),
        jax.ShapeDtypeStruct((t_tokens, SSD_CONV_CH), BF16),
        jax.ShapeDtypeStruct((t_tokens, ATT_INNER), BF16),
        jax.ShapeDtypeStruct((t_tokens, D_MODEL), BF16),
        jax.ShapeDtypeStruct((t_tokens, D_MODEL), BF16),
        jax.ShapeDtypeStruct((t_tokens, SSD_HEADS), F32),
        jax.ShapeDtypeStruct((ATT_INNER, t_tokens), BF16),
        jax.ShapeDtypeStruct((ATT_INNER, t_tokens), BF16),
        jax.ShapeDtypeStruct((SSD_HEADS, t_tokens), F32),
    ]
    out_specs = [row(SSD_INNER), row(SSD_CONV_CH), row(ATT_INNER), row(D_MODEL), row(D_MODEL), row(SSD_HEADS),
                 col(ATT_INNER), col(ATT_INNER), col(SSD_HEADS)]
    return pl.pallas_call(
        _in_proj_kernel,
        grid=grid,
        in_specs=[row(D_MODEL), whole, row(ATT_HEAD_DIM // 2), row(ATT_HEAD_DIM // 2),
                  col(ATT_HEAD_DIM // 2), col(ATT_HEAD_DIM // 2),
                  whole, whole, whole],
        out_specs=out_specs,
        out_shape=out_shapes,
        compiler_params=pltpu.CompilerParams(dimension_semantics=("arbitrary",), vmem_limit_bytes=VMEM_LIMIT),
        name="in_proj",
    )(x2d, nw, cosf, sinf, cost, sint, w_big, w_t, w_dt)


def _ssd_kernel(z_ref, xbc_ref, xtail_ref, dt_ref, dtt_ref, cw_ref, cb_ref, dtb_ref, dtbc_ref, alog_ref, alogc_ref,
                dexp_ref, nw_ref, out_ref, state_ref):
    c = pl.program_id(1)
    cl = SSD_CHUNK
    heads_per_group = SSD_HEADS // SSD_GROUPS
    gw = heads_per_group * SSD_HEAD_DIM

    @pl.when(c == 0)
    def _():
        state_ref[...] = jnp.zeros_like(state_ref)

    xbc = xbc_ref[...]
    tail = xtail_ref[...]
    tail = jnp.where(c == 0, jnp.zeros_like(tail), tail)
    u_ext = jnp.concatenate([tail, xbc], axis=0)
    n_shift = SSD_CONV - 1
    srow = lax.broadcasted_iota(jnp.int32, (n_shift * cl, CONV_TAIL + cl), 0)
    scol = lax.broadcasted_iota(jnp.int32, (n_shift * cl, CONV_TAIL + cl), 1)
    shift_sel = (scol == CONV_TAIL + srow % cl - (n_shift - srow // cl)).astype(BF16)
    shifted = _dot(shift_sel, u_ext)
    acc = cb_ref[...] + cw_ref[n_shift:SSD_CONV, :] * xbc.astype(F32)
    for k in range(n_shift):
        acc = acc + cw_ref[k:k + 1, :] * shifted[k * cl:(k + 1) * cl, :]
    xc = _silu(acc)
    xs = xc[:, :SSD_INNER]
    bm = xc[:, SSD_INNER:SSD_INNER + SSD_GROUPS * SSD_STATE].astype(BF16)
    cm = xc[:, SSD_INNER + SSD_GROUPS * SSD_STATE:].astype(BF16)

    dt = _softplus(dt_ref[...] + dtb_ref[...])
    da = dt * (-jnp.exp(alog_ref[...]))
    dtt = _softplus(dtt_ref[...] + dtbc_ref[...])
    dat = dtt * (-jnp.exp(alogc_ref[...]))
    ri = lax.broadcasted_iota(jnp.int32, (cl, cl), 0)
    ci = lax.broadcasted_iota(jnp.int32, (cl, cl), 1)
    causal = ri >= ci
    tril = causal.astype(BF16)
    triu = (ri <= ci).astype(BF16)
    acum = _dot_exact_rhs(tril, da)
    acumt = _dot_exact_lhs(dat, triu)

    hrow = lax.broadcasted_iota(jnp.int32, (SSD_HEADS, SSD_INNER), 0)
    hcol = lax.broadcasted_iota(jnp.int32, (SSD_HEADS, SSD_INNER), 1) // SSD_HEAD_DIM
    expand = (hrow == hcol).astype(BF16)

    a_last = acum[cl - 1:cl, :]
    dt_e = _dot_exact_lhs(dt, expand)
    dec_e = _dot_exact_lhs(jnp.exp(a_last - acum), expand)
    ea_e = _dot_exact_lhs(jnp.exp(acum), expand)
    xdt = xs * dt_e
    xdt_b = xdt.astype(BF16)
    xdtdec_b = (xdt * dec_e).astype(BF16)
    chunk_decay_t = jnp.exp(acumt[:, cl - 1:cl])

    y_groups = []
    for g in range(SSD_GROUPS):
        b_g = bm[:, g * SSD_STATE:(g + 1) * SSD_STATE]
        c_g = cm[:, g * SSD_STATE:(g + 1) * SSD_STATE]
        prev = state_ref[g * gw:(g + 1) * gw, :]
        cb = _dot_nt(c_g, b_g)
        y_off = _dot_nt(c_g, prev.astype(BF16)) * ea_e[:, g * gw:(g + 1) * gw]
        new_states = _dot_tn(xdtdec_b[:, g * gw:(g + 1) * gw], b_g)
        y_heads = []
        decay_rows = []
        for e in range(heads_per_group):
            hd = g * heads_per_group + e
            lmat = jnp.where(causal, jnp.exp(acum[:, hd:hd + 1] - acumt[hd:hd + 1, :]), 0.0)
            m = (cb * lmat).astype(BF16)
            y_heads.append(_dot(m, xdt_b[:, hd * SSD_HEAD_DIM:(hd + 1) * SSD_HEAD_DIM]))
            decay_rows.append(jnp.broadcast_to(chunk_decay_t[hd:hd + 1, :], (SSD_HEAD_DIM, SSD_STATE)))
        y_groups.append(jnp.concatenate(y_heads, axis=1) + y_off)
        state_ref[g * gw:(g + 1) * gw, :] = prev * jnp.concatenate(decay_rows, axis=0) + new_states

    y = jnp.concatenate(y_groups, axis=1) + xs * dexp_ref[...]
    yg = y * _silu(z_ref[...].astype(F32))
    nw = nw_ref[...]
    outs = [_rms(yg[:, g * gw:(g + 1) * gw], nw[:, g * gw:(g + 1) * gw]) for g in range(SSD_GROUPS)]
    out_ref[...] = jnp.concatenate(outs, axis=1).astype(BF16)


def _ssd(z, xbc, dt, dtt, conv_w, conv_b, dt_bias, a_log, d_skip, norm_w, batch):
    t_tokens = z.shape[0]
    nc = t_tokens // batch // SSD_CHUNK
    tok = lambda width: pl.BlockSpec((SSD_CHUNK, width), lambda b, c: (b * nc + c, 0))
    whole = pl.BlockSpec(memory_space=pltpu.VMEM)
    dexp = jnp.repeat(d_skip, SSD_HEAD_DIM)[None, :]
    tails_per_chunk = SSD_CHUNK // CONV_TAIL
    prev_tail = pl.BlockSpec((CONV_TAIL, SSD_CONV_CH),
                             lambda b, c: (jnp.maximum((b * nc + c) * tails_per_chunk - 1, 0), 0))
    return pl.pallas_call(
        _ssd_kernel,
        grid=(batch, nc),
        in_specs=[tok(SSD_INNER), tok(SSD_CONV_CH), prev_tail, tok(SSD_HEADS),
                  pl.BlockSpec((SSD_HEADS, SSD_CHUNK), lambda b, c: (0, b * nc + c)),
                  whole, whole, whole, whole, whole, whole, whole, whole],
        out_specs=tok(SSD_INNER),
        out_shape=jax.ShapeDtypeStruct((t_tokens, SSD_INNER), BF16),
        scratch_shapes=[pltpu.VMEM((SSD_INNER, SSD_STATE), F32)],
        compiler_params=pltpu.CompilerParams(dimension_semantics=("arbitrary", "arbitrary"),
                                             vmem_limit_bytes=VMEM_LIMIT),
        name="ssd",
    )(z, xbc, xbc, dt, dtt, conv_w, conv_b[None, :], dt_bias[None, :], dt_bias[:, None],
      a_log[None, :], a_log[:, None], dexp, norm_w[None, :])


MOBA_SUPER = 2 * MOBA_BLOCK


MOBA_BLOCK_ROWS = 16
ONES_ROWS = 16


def _moba_kernel(qt_ref, k_ref, vt_ref, o_ref, kfull_ref, qaug_ref, sa_ref, sb_ref, m_ref, l_ref, acc_ref, *, seq):
    nb = seq // MOBA_BLOCK
    sb = MOBA_SUPER
    nsb = seq // sb
    hd = ATT_HEAD_DIM
    nbr = MOBA_BLOCK_ROWS

    for c in range(nsb):
        rows = slice(c * sb, (c + 1) * sb)
        kfull_ref[rows, 0:LANES] = k_ref[rows, :]
        rblk = lax.broadcasted_iota(jnp.int32, (sb, LANES), 0) // MOBA_BLOCK + c * (sb // MOBA_BLOCK)
        rlane = lax.broadcasted_iota(jnp.int32, (sb, LANES), 1)
        kfull_ref[rows, LANES:2 * LANES] = jnp.where(rblk == rlane, 1.0, 0.0).astype(BF16)

    pr = lax.broadcasted_iota(jnp.int32, (nbr, seq), 0)
    pc = lax.broadcasted_iota(jnp.int32, (nbr, seq), 1) // MOBA_BLOCK
    pool = jnp.where(pr == pc, 1.0 / MOBA_BLOCK, 0.0).astype(BF16)
    kmean = _dot(pool, k_ref[...])
    km_hi = kmean.astype(BF16)
    km_lo = (kmean - km_hi.astype(F32)).astype(BF16)

    head_of_row = lax.broadcasted_iota(jnp.int32, (LANES, sb), 0) // hd
    blk_row = lax.broadcasted_iota(jnp.int32, (nbr, sb), 0)
    q_local = lax.broadcasted_iota(jnp.int32, (nbr, sb), 1)
    inrange = blk_row < nb
    krow = lax.broadcasted_iota(jnp.int32, (sb, 2 * sb), 0)
    qcol = lax.broadcasted_iota(jnp.int32, (sb, 2 * sb), 1) % sb
    causal = krow <= qcol
    ones_rows = jnp.ones((ONES_ROWS, sb), BF16)
    flag_pad = jnp.zeros((LANES - nbr, sb), BF16)

    def scores_into(s_ref, kv):
        start = pl.multiple_of(kv * sb, sb)
        s_ref[...] = _dot(kfull_ref[pl.ds(start, sb), :], qaug_ref[...])

    def absorb(kv, s_ref, diagonal):
        start = pl.multiple_of(kv * sb, sb)
        s_t = s_ref[...]
        if diagonal:
            s_t = jnp.where(causal, s_t, NEG_BIG)
        m_prev = m_ref[...]
        m_new = jnp.maximum(m_prev, jnp.max(s_t, axis=0, keepdims=True))
        m_ref[...] = m_new
        p_b = jnp.exp2(s_t - m_new).astype(BF16)
        alpha = jnp.exp2(m_prev - m_new)
        for a in range(2):
            cols = slice(a * sb, (a + 1) * sb)
            rows = slice(a * hd, (a + 1) * hd)
            v_aug = jnp.concatenate([vt_ref[rows, pl.ds(start, sb)], ones_rows], axis=0)
            r = _dot(v_aug, p_b[:, cols])
            l_ref[a:a + 1, :] = alpha[:, cols] * l_ref[a:a + 1, :] + r[hd:hd + 1, :]
            acc_ref[rows, :] = alpha[:, cols] * acc_ref[rows, :] + r[0:hd, :]

    def q_super(i, carry):
        qstart = pl.multiple_of(i * sb, sb)
        qt_sb = qt_ref[:, pl.ds(qstart, sb)]
        own = 2 * i + jnp.where(q_local >= MOBA_BLOCK, 1, 0)
        valid = blk_row < own
        for a in range(2):
            qa = jnp.where(head_of_row == a, qt_sb, jnp.zeros_like(qt_sb))
            gate = _dot(km_hi, qa) + _dot(km_lo, qa)
            g = jnp.where(valid, gate, -jnp.inf)
            sel = jnp.zeros((nbr, sb), jnp.bool_)
            for _ in range(min(MOBA_TOPK, nb - 1)):
                cand = inrange & jnp.logical_not(sel)
                gm = jnp.max(jnp.where(cand, g, -jnp.inf), axis=0, keepdims=True)
                hit = cand & (g == gm)
                first = jnp.min(jnp.where(hit, blk_row, nb), axis=0, keepdims=True)
                sel = sel | (blk_row == first)
            chosen = (sel & valid) | (blk_row == own)
            flags = jnp.where(chosen, 0.0, NEG_BIG).astype(BF16)
            qaug_ref[:, a * sb:(a + 1) * sb] = jnp.concatenate([qa, flags, flag_pad], axis=0)

        m_ref[...] = jnp.full(m_ref.shape, NEG_BIG, F32)
        l_ref[...] = jnp.zeros_like(l_ref)
        acc_ref[...] = jnp.zeros_like(acc_ref)
        scores_into(sa_ref, 0)

        def two_steps(t, carry):
            scores_into(sb_ref, 2 * t + 1)
            absorb(2 * t, sa_ref, False)
            scores_into(sa_ref, 2 * t + 2)
            absorb(2 * t + 1, sb_ref, False)
            return carry

        lax.fori_loop(0, i // 2, two_steps, 0)

        @pl.when(i % 2 == 1)
        def _():
            scores_into(sb_ref, i)
            absorb(i - 1, sa_ref, False)
            absorb(i, sb_ref, True)

        @pl.when(i % 2 == 0)
        def _():
            absorb(i, sa_ref, True)

        out_t = jnp.concatenate([acc_ref[a * hd:(a + 1) * hd, :] / l_ref[a:a + 1, :] for a in range(2)], axis=0)
        o_ref[pl.ds(qstart, sb), :] = out_t.T.astype(BF16)
        return carry

    lax.fori_loop(0, nsb, q_super, 0)


def _moba(qt, k, vt, batch):
    t_tokens = k.shape[0]
    seq = t_tokens // batch
    assert seq % MOBA_SUPER == 0 and seq // MOBA_BLOCK <= MOBA_BLOCK_ROWS
    spec = pl.BlockSpec((seq, LANES), lambda b, hp: (b, hp))
    spec_t = pl.BlockSpec((LANES, seq), lambda b, hp: (hp, b))
    return pl.pallas_call(
        functools.partial(_moba_kernel, seq=seq),
        grid=(batch, ATT_INNER // LANES),
        in_specs=[spec_t, spec, spec_t],
        out_specs=spec,
        out_shape=jax.ShapeDtypeStruct((t_tokens, ATT_INNER), BF16),
        scratch_shapes=[pltpu.VMEM((seq, 2 * LANES), BF16),
                        pltpu.VMEM((2 * LANES, 2 * MOBA_SUPER), BF16),
                        pltpu.VMEM((MOBA_SUPER, 2 * MOBA_SUPER), F32),
                        pltpu.VMEM((MOBA_SUPER, 2 * MOBA_SUPER), F32),
                        pltpu.VMEM((1, 2 * MOBA_SUPER), F32),
                        pltpu.VMEM((SUBLANES, MOBA_SUPER), F32),
                        pltpu.VMEM((LANES, MOBA_SUPER), F32)],
        compiler_params=pltpu.CompilerParams(dimension_semantics=("arbitrary", "arbitrary"),
                                             vmem_limit_bytes=VMEM_LIMIT),
        name="moba",
    )(qt, k, vt)


ROUTER_ROWS = 40


def _merge_kernel(x_ref, yn_ref, att_ref, gs_ref, ga_ref, wso_ref, wao_ref, wo_ref, nw_ref,
                  wr_hi_ref, wr_lo_ref, br_ref, upper_ref, x1_ref, h2_ref, route_ref, jloc_ref, cnt_ref):
    y_s = _dot(yn_ref[...], wso_ref[...])
    y_a = _dot(att_ref[...], wao_ref[...])
    merged = _sigmoid(gs_ref[...].astype(F32)) * y_s + _sigmoid(ga_ref[...].astype(F32)) * y_a
    x1 = x_ref[...] + _dot(merged.astype(BF16), wo_ref[...])
    x1_ref[...] = x1
    h2 = _rms(x1, nw_ref[...])
    h2_ref[...] = h2.astype(BF16)

    h_hi = h2.astype(BF16)
    h_lo = (h2 - h_hi.astype(F32)).astype(BF16)
    wr_hi = wr_hi_ref[...]
    logits = _dot_nt(wr_hi, h_hi) + _dot_nt(wr_hi, h_lo) + _dot_nt(wr_lo_ref[...], h_hi) + br_ref[...]
    gl = logits[0:N_GROUPS, :]
    gidx = lax.broadcasted_iota(jnp.int32, gl.shape, 0)
    gmax = jnp.max(gl, axis=0, keepdims=True)
    gi = jnp.min(jnp.where(gl == gmax, gidx, N_GROUPS), axis=0, keepdims=True)
    g_w = 1.0 / jnp.sum(jnp.exp(gl - gmax), axis=0, keepdims=True)
    el = jnp.zeros((EXPERTS_PER_GROUP, gl.shape[1]), F32)
    for g in range(N_GROUPS):
        lo = N_GROUPS + g * EXPERTS_PER_GROUP
        el = el + jnp.where(gi == g, logits[lo:lo + EXPERTS_PER_GROUP, :], 0.0)
    eidx = lax.broadcasted_iota(jnp.int32, el.shape, 0)
    v0 = jnp.max(el, axis=0, keepdims=True)
    i0 = jnp.min(jnp.where(el == v0, eidx, EXPERTS_PER_GROUP), axis=0, keepdims=True)
    rest = jnp.where(eidx == i0, -jnp.inf, el)
    v1 = jnp.max(rest, axis=0, keepdims=True)
    i1 = jnp.min(jnp.where((rest == v1) & (eidx != i0), eidx, EXPERTS_PER_GROUP), axis=0, keepdims=True)
    t = jnp.exp(v1 - v0)
    w0 = g_w / (1.0 + t)
    w1 = g_w * t / (1.0 + t)
    base = gi * EXPERTS_PER_GROUP
    route_ref[...] = jnp.concatenate(
        [(base + i0).astype(F32), (base + i1).astype(F32), w0, w1], axis=0)

    tm = gl.shape[1]
    eids = jnp.concatenate([base + i0, base + i1], axis=1)
    onehot = lax.broadcasted_iota(jnp.int32, (N_EXPERTS, 2 * tm), 0) == eids
    csum = _dot(onehot.astype(BF16), upper_ref[...])
    oh = onehot.astype(F32)
    lrank = jnp.sum(csum * oh, axis=0, keepdims=True) - 1.0
    cnt = jnp.floor((csum[:, 2 * tm - 1:2 * tm] + (RUN_ALIGN - 1)) * (1.0 / RUN_ALIGN))
    er = lax.broadcasted_iota(jnp.int32, (N_EXPERTS, N_EXPERTS), 0)
    ec = lax.broadcasted_iota(jnp.int32, (N_EXPERTS, N_EXPERTS), 1)
    before = _dot((ec < er).astype(BF16), jnp.broadcast_to(cnt, (N_EXPERTS, LANES)).astype(BF16))
    off = before[:, 0:1] * RUN_ALIGN
    jloc = jnp.sum(oh * off, axis=0, keepdims=True) + lrank
    jloc_ref[...] = jnp.concatenate([jloc[:, :tm], jloc[:, tm:]], axis=0).astype(jnp.int32)
    cnt_ref[...] = jnp.broadcast_to(cnt * RUN_ALIGN, (N_EXPERTS, LANES))


def _merge(x2d, yn, att, gs, ga, wso, wao, wo, nw, wr_hi, wr_lo, br):
    t_tokens = x2d.shape[0]
    tm = TM_TOK
    upper = jnp.triu(jnp.ones((TILE_ASG, TILE_ASG), BF16))
    row = lambda: pl.BlockSpec((tm, D_MODEL), lambda i: (i, 0))
    whole = pl.BlockSpec(memory_space=pltpu.VMEM)
    return pl.pallas_call(
        _merge_kernel,
        grid=(t_tokens // tm,),
        in_specs=[row(), row(), row(), row(), row(), whole, whole, whole, whole, whole, whole, whole, whole],
        out_specs=[row(), row(), pl.BlockSpec((4, tm), lambda i: (0, i)), pl.BlockSpec((2, tm), lambda i: (0, i)),
                   pl.BlockSpec((N_EXPERTS, LANES), lambda i: (i, 0))],
        out_shape=[jax.ShapeDtypeStruct((t_tokens, D_MODEL), F32),
                   jax.ShapeDtypeStruct((t_tokens, D_MODEL), BF16),
                   jax.ShapeDtypeStruct((4, t_tokens), F32),
                   jax.ShapeDtypeStruct((2, t_tokens), jnp.int32),
                   jax.ShapeDtypeStruct((t_tokens // tm * N_EXPERTS, LANES), F32)],
        compiler_params=pltpu.CompilerParams(dimension_semantics=("arbitrary",), vmem_limit_bytes=VMEM_LIMIT),
        name="merge",
    )(x2d, yn, att, gs, ga, wso, wao, wo, nw, wr_hi, wr_lo, br, upper)


def _rows(ref, first, n_rows):
    start = first if isinstance(first, int) else pl.multiple_of(first, RUN_ALIGN)
    return ref.at[pl.ds(start, n_rows), :]


def _for_each_piece(count, fn, bits):
    for bit in range(bits - 1, ALIGN_BIT - 1, -1):
        size = 1 << bit

        @pl.when((count & size) != 0)
        def _():
            fn((count >> (bit + 1)) << (bit + 1), size)


def _moe_max_tiles(t_tokens):
    rows = 2 * t_tokens + (t_tokens // TM_TOK) * N_EXPERTS * (RUN_ALIGN - 1)
    return -(-rows // TM_MOE) + N_EXPERTS


def _moe_plan(cnt_rows, max_tiles):
    cnt = cnt_rows[:, 0].astype(jnp.int32).reshape(-1, N_EXPERTS)
    off = jnp.cumsum(cnt, axis=1) - cnt
    counts = jnp.sum(cnt, axis=0)
    padded = ((counts + TM_MOE - 1) // TM_MOE) * TM_MOE
    ends = jnp.cumsum(padded)
    starts = ends - padded
    base = starts[None, :] + jnp.cumsum(cnt, axis=0) - cnt
    tile_start = jnp.arange(max_tiles, dtype=jnp.int32) * TM_MOE
    tile_expert = jnp.sum((tile_start[:, None] >= ends[None, :]).astype(jnp.int32), axis=1)
    tile_expert = jnp.minimum(tile_expert, N_EXPERTS - 1)
    n_tiles = (ends[-1] // TM_MOE).astype(jnp.int32).reshape(1)
    flat = lambda a: a.reshape(-1).astype(jnp.int32)
    return flat(cnt), flat(off), flat(base), starts + counts, ends, tile_expert, n_tiles


def _dispatch_kernel(cnt_ref, off_ref, base_ref, padlo_ref, padhi_ref, ntile_ref, h2_ref, jrow_ref, xs_hbm,
                     xbuf0, xbuf1, zeros_ref, sems, zsem, *, max_tiles):
    i = pl.program_id(0)
    n = pl.num_programs(0)
    xbufs = (xbuf0, xbuf1)

    def runs(tile, slot, wait):
        def per_expert(e, carry):
            k = tile * N_EXPERTS + e
            src, dst = off_ref[k], base_ref[k]

            def piece(o, size):
                cp = pltpu.make_async_copy(_rows(xbufs[slot], src + o, size), _rows(xs_hbm, dst + o, size),
                                           sems.at[slot])
                cp.wait() if wait else cp.start()
            _for_each_piece(cnt_ref[k], piece, RUN_BITS)
            return carry
        lax.fori_loop(0, N_EXPERTS, per_expert, 0)

    @pl.when(i == 0)
    def _():
        zeros_ref[...] = jnp.zeros_like(zeros_ref)
        nt = ntile_ref[0]
        for wait in (False, True):
            def pad(e, carry):
                lo = padlo_ref[e]

                def piece(o, size):
                    cp = pltpu.make_async_copy(_rows(zeros_ref, 0, size), _rows(xs_hbm, lo + o, size), zsem)
                    cp.wait() if wait else cp.start()
                _for_each_piece(padhi_ref[e] - lo, piece, PAD_BITS)
                return carry
            lax.fori_loop(0, N_EXPERTS, pad, 0)

            def spare(tile, carry):
                cp = pltpu.make_async_copy(zeros_ref, _rows(xs_hbm, tile * TM_MOE, TM_MOE), zsem)
                cp.wait() if wait else cp.start()
                return carry
            lax.fori_loop(nt, max_tiles, spare, 0)

    for slot in range(2):
        @pl.when(i % 2 == slot)
        def _():
            @pl.when(i >= 2)
            def _():
                runs(i - 2, slot, True)
            row = lax.broadcasted_iota(jnp.int32, (LOC_ROWS, TM_TOK), 0)
            j = jrow_ref[...]
            perm = ((row == j[0:1, :]) | (row == j[1:2, :])).astype(BF16)
            xbufs[slot][...] = _dot(perm, h2_ref[...]).astype(BF16)
            runs(i, slot, False)

            @pl.when(i == n - 1)
            def _():
                @pl.when(i >= 1)
                def _():
                    runs(i - 1, 1 - slot, True)
                runs(i, slot, True)


def _dispatch(plan, jloc, h2, max_tiles):
    cnt, off, base, pad_lo, pad_hi, _, n_tiles = plan
    t_tokens = h2.shape[0]
    grid_spec = pltpu.PrefetchScalarGridSpec(
        num_scalar_prefetch=6,
        grid=(t_tokens // TM_TOK,),
        in_specs=[pl.BlockSpec((TM_TOK, D_MODEL), lambda i, *_: (i, 0)),
                  pl.BlockSpec((2, TM_TOK), lambda i, *_: (0, i))],
        out_specs=pl.BlockSpec(memory_space=pl.ANY),
        scratch_shapes=[pltpu.VMEM((LOC_ROWS, D_MODEL), BF16),
                        pltpu.VMEM((LOC_ROWS, D_MODEL), BF16),
                        pltpu.VMEM((TM_MOE, D_MODEL), BF16),
                        pltpu.SemaphoreType.DMA((2,)), pltpu.SemaphoreType.DMA(())],
    )
    return pl.pallas_call(
        functools.partial(_dispatch_kernel, max_tiles=max_tiles),
        grid_spec=grid_spec,
        out_shape=jax.ShapeDtypeStruct((max_tiles * TM_MOE, D_MODEL), BF16),
        compiler_params=pltpu.CompilerParams(dimension_semantics=("arbitrary",), vmem_limit_bytes=VMEM_LIMIT),
        name="dispatch",
    )(cnt, off, base, pad_lo, pad_hi, n_tiles, h2, jloc)


def _moe_kernel(texp_ref, ntile_ref, xs_ref, wgu_ref, wd_ref, y_ref):
    t = pl.program_id(0)

    @pl.when(t < ntile_ref[0])
    def _():
        gu = _dot(xs_ref[...], wgu_ref[0])
        hid = (_silu(gu[:, :D_EXPERT]) * gu[:, D_EXPERT:]).astype(BF16)
        y_ref[...] = _dot(hid, wd_ref[0]).astype(BF16)

    @pl.when(t >= ntile_ref[0])
    def _():
        y_ref[...] = jnp.zeros_like(y_ref)


def _moe(xs, tile_expert, n_tiles, wgu, wd, max_tiles):
    live = lambda t, nt: jnp.minimum(t, nt[0] - 1)
    tile = lambda index: pl.BlockSpec((TM_MOE, D_MODEL), index)
    grid_spec = pltpu.PrefetchScalarGridSpec(
        num_scalar_prefetch=2,
        grid=(max_tiles,),
        in_specs=[tile(lambda t, te, nt: (live(t, nt), 0)),
                  pl.BlockSpec((1, D_MODEL, 2 * D_EXPERT), lambda t, te, nt: (te[live(t, nt)], 0, 0)),
                  pl.BlockSpec((1, D_EXPERT, D_MODEL), lambda t, te, nt: (te[live(t, nt)], 0, 0))],
        out_specs=tile(lambda t, te, nt: (t, 0)),
    )
    return pl.pallas_call(
        _moe_kernel,
        grid_spec=grid_spec,
        out_shape=jax.ShapeDtypeStruct((max_tiles * TM_MOE, D_MODEL), BF16),
        compiler_params=pltpu.CompilerParams(dimension_semantics=("arbitrary",), vmem_limit_bytes=VMEM_LIMIT),
        name="moe",
    )(tile_expert, n_tiles, xs, wgu, wd)


def _tail_kernel(cnt_ref, off_ref, base_ref, x1_ref, p_ref, jw_ref, y_hbm, pnw_ref, wpg_ref, wple_ref, fnw_ref,
                 o_ref, ybuf0, ybuf1, sems):
    i = pl.program_id(0)
    n = pl.num_programs(0)
    ybufs = (ybuf0, ybuf1)

    def runs(tile, slot, wait):
        def per_expert(e, carry):
            k = tile * N_EXPERTS + e
            src, dst = base_ref[k], off_ref[k]

            def piece(o, size):
                cp = pltpu.make_async_copy(_rows(y_hbm, src + o, size), _rows(ybufs[slot], dst + o, size),
                                           sems.at[slot])
                cp.wait() if wait else cp.start()
            _for_each_piece(cnt_ref[k], piece, RUN_BITS)
            return carry
        lax.fori_loop(0, N_EXPERTS, per_expert, 0)

    @pl.when(i == 0)
    def _():
        ybuf0[...] = jnp.zeros_like(ybuf0)
        ybuf1[...] = jnp.zeros_like(ybuf1)
        runs(0, 0, False)

    for slot in range(2):
        @pl.when(i % 2 == slot)
        def _():
            @pl.when(i + 1 < n)
            def _():
                runs(i + 1, 1 - slot, False)
            runs(i, slot, True)
            y_loc = ybufs[slot][...]
            jw = jw_ref[...]
            col = lax.broadcasted_iota(jnp.int32, (TM_TOK, LOC_ROWS), 1)
            pick = sum(jnp.where(col == jw[:, s:s + 1].astype(jnp.int32), jw[:, 2 + s:3 + s], 0.0) for s in range(2))
            moe = _dot(pick.astype(BF16), y_loc)
            x2 = x1_ref[...] + moe
            gate = _sigmoid(_dot(_rms(x2, pnw_ref[...]).astype(BF16), wpg_ref[...]))
            x3 = x2 + gate * _dot(p_ref[...].astype(BF16), wple_ref[...])
            o_ref[...] = _rms(x3, fnw_ref[...])


def _tail(plan, x1, p2d, y_sorted, jw, pnw, wpg, wple, fnw):
    cnt, off, base = plan[0:3]
    t_tokens = x1.shape[0]
    whole = pl.BlockSpec(memory_space=pltpu.VMEM)
    grid_spec = pltpu.PrefetchScalarGridSpec(
        num_scalar_prefetch=3,
        grid=(t_tokens // TM_TOK,),
        in_specs=[pl.BlockSpec((TM_TOK, D_MODEL), lambda i, *_: (i, 0)),
                  pl.BlockSpec((TM_TOK, PLE_DIM), lambda i, *_: (i, 0)),
                  pl.BlockSpec((TM_TOK, 4), lambda i, *_: (i, 0)),
                  pl.BlockSpec(memory_space=pl.ANY),
                  whole, whole, whole, whole],
        out_specs=pl.BlockSpec((TM_TOK, D_MODEL), lambda i, *_: (i, 0)),
        scratch_shapes=[pltpu.VMEM((LOC_ROWS, D_MODEL), BF16),
                        pltpu.VMEM((LOC_ROWS, D_MODEL), BF16),
                        pltpu.SemaphoreType.DMA((2,))],
    )
    return pl.pallas_call(
        _tail_kernel,
        grid_spec=grid_spec,
        out_shape=jax.ShapeDtypeStruct((t_tokens, D_MODEL), F32),
        compiler_params=pltpu.CompilerParams(dimension_semantics=("arbitrary",), vmem_limit_bytes=VMEM_LIMIT),
        name="tail",
    )(cnt, off, base, x1, p2d, jw, y_sorted, pnw, wpg, wple, fnw)


def _rope_tables(positions):
    half = ATT_HEAD_DIM // 2
    inv_freq = ROPE_THETA ** (-jnp.arange(half, dtype=F32) / half)
    ang = positions.astype(F32)[..., None] * inv_freq
    cos = jnp.cos(ang).reshape(-1, half)
    sin = jnp.sin(ang).reshape(-1, half)
    return cos, sin, cos.T, sin.T


def _layer(x2d, p2d, rope_tables, batch, attn_norm_w, w_in, conv_w, conv_b, dt_bias, a_log, d_skip, ssd_norm_w,
           w_ssd_out, w_attn_out, w_out, moe_norm_w, w_rg, b_rg, w_re, b_re, w_gate, w_up, w_down,
           ple_norm_w, w_ple, w_ple_gate, final_norm_w):
    t_tokens = x2d.shape[0]
    offs = np.cumsum((0,) + IN_SPLITS)
    cols = lambda i: w_in[:, offs[i]:offs[i + 1]]
    w_big = jnp.concatenate([cols(0), cols(1), cols(4), cols(6), cols(7)], axis=1).astype(BF16)
    w_t = jnp.concatenate([cols(3), cols(5), cols(2)], axis=1).T.astype(BF16)
    w_dt = jnp.pad(cols(2), ((0, 0), (0, LANES - SSD_HEADS))).astype(BF16)

    z, xbc, k, gs, ga, dt, qt, vt, dtt = _in_proj(x2d, attn_norm_w[None, :], *rope_tables, w_big, w_t, w_dt)

    yn = _ssd(z, xbc, dt, dtt, conv_w, conv_b, dt_bias, a_log, d_skip, ssd_norm_w, batch)

    att = _moba(qt, k, vt, batch)

    wr = jnp.concatenate([w_rg, w_re], axis=1).T
    wr = jnp.pad(wr, ((0, ROUTER_ROWS - wr.shape[0]), (0, 0)))
    wr_hi = wr.astype(BF16)
    wr_lo = (wr - wr_hi.astype(F32)).astype(BF16)
    br = jnp.pad(jnp.concatenate([b_rg, b_re]), (0, ROUTER_ROWS - N_GROUPS - N_EXPERTS))[:, None]
    x1, h2, route, jloc, cnt_rows = _merge(x2d, yn, att, gs, ga, w_ssd_out.astype(BF16), w_attn_out.astype(BF16),
                           w_out.astype(BF16), moe_norm_w[None, :], wr_hi, wr_lo, br)

    max_tiles = _moe_max_tiles(t_tokens)
    plan = _moe_plan(cnt_rows, max_tiles)
    wgu = jnp.concatenate([w_gate, w_up], axis=-1).reshape(N_EXPERTS, D_MODEL, 2 * D_EXPERT).astype(BF16)
    wd = w_down.reshape(N_EXPERTS, D_EXPERT, D_MODEL).astype(BF16)
    x_sorted = _dispatch(plan, jloc, h2, max_tiles)
    y_sorted = _moe(x_sorted, plan[5], plan[6], wgu, wd, max_tiles)
    jw = jnp.concatenate([jloc.astype(F32), route[2:4]], axis=0).T

    return _tail(plan, x1, p2d, y_sorted, jw, ple_norm_w[None, :],
                 w_ple_gate.astype(BF16), w_ple.astype(BF16), final_norm_w[None, :])


def kernel(x, p, positions, attn_norm_w, w_in, conv_w, conv_b, dt_bias, a_log, d_skip, ssd_norm_w, w_ssd_out,
           w_attn_out, w_out, moe_norm_w, w_router_group, b_router_group, w_router_expert, b_router_expert,
           w_exp_gate, w_exp_up, w_exp_down, ple_norm_w, w_ple, w_ple_gate, final_norm_w):
    batch, seq, d = x.shape
    depth = p.shape[0]
    assert depth == 1, "the final norm is fused into the last layer's tail kernel"
    rope_tables = _rope_tables(positions)
    i = 0
    out = _layer(x.reshape(batch * seq, d), p[i].reshape(batch * seq, PLE_DIM), rope_tables, batch,
                 attn_norm_w[i], w_in[i], conv_w[i], conv_b[i], dt_bias[i], a_log[i], d_skip[i], ssd_norm_w[i],
                 w_ssd_out[i], w_attn_out[i], w_out[i], moe_norm_w[i], w_router_group[i], b_router_group[i],
                 w_router_expert[i], b_router_expert[i], w_exp_gate[i], w_exp_up[i], w_exp_down[i],
                 ple_norm_w[i], w_ple[i], w_ple_gate[i], final_norm_w)
    return out.reshape(batch, seq, d)
```

```python
import functools

import jax
import jax.numpy as jnp
import numpy as np
from jax import lax
from jax.experimental import pallas as pl
from jax.experimental.pallas import tpu as pltpu

F32 = jnp.float32
BF16 = jnp.bfloat16

EPS = 1e-6
D_MODEL = 1024
SSD_HEADS = 16
SSD_HEAD_DIM = 64
SSD_INNER = SSD_HEADS * SSD_HEAD_DIM
SSD_GROUPS = 2
SSD_STATE = 64
SSD_CONV = 4
SSD_CHUNK = 128
SSD_CONV_CH = SSD_INNER + 2 * SSD_GROUPS * SSD_STATE
CONV_TAIL = 16
ATT_HEADS = 16
ATT_HEAD_DIM = 64
ATT_INNER = ATT_HEADS * ATT_HEAD_DIM
MOBA_BLOCK = 256
MOBA_TOPK = 3
ROPE_THETA = 10000.0
IN_SPLITS = (SSD_INNER, SSD_CONV_CH, SSD_HEADS, ATT_INNER, ATT_INNER, ATT_INNER, D_MODEL, D_MODEL)
N_GROUPS = 4
EXPERTS_PER_GROUP = 8
N_EXPERTS = N_GROUPS * EXPERTS_PER_GROUP
D_EXPERT = 256
PLE_DIM = 256

LANES = 128
SUBLANES = 8
NEG_BIG = -1e30
LOG2E = 1.4426950408889634
VMEM_LIMIT = 56 * 1024 * 1024

TM_PROJ = 512
TM_MOE = 512
TM_TOK = 512
TILE_ASG = 2 * TM_TOK
RUN_ALIGN = 2 * SUBLANES
LOC_ROWS = TILE_ASG + N_EXPERTS * RUN_ALIGN
ALIGN_BIT = RUN_ALIGN.bit_length() - 1
RUN_BITS = TILE_ASG.bit_length()
PAD_BITS = TM_MOE.bit_length() - 1


def _dot(a, b):
    return jnp.dot(a, b, preferred_element_type=F32)


def _dot_nt(a, b):
    return lax.dot_general(a, b, (((1,), (1,)), ((), ())), preferred_element_type=F32)


def _dot_tn(a, b):
    return lax.dot_general(a, b, (((0,), (0,)), ((), ())), preferred_element_type=F32)


def _split3(a):
    hi = a.astype(BF16)
    r1 = a - hi.astype(F32)
    mid = r1.astype(BF16)
    lo = (r1 - mid.astype(F32)).astype(BF16)
    return hi, mid, lo


def _dot_exact_rhs(sel_bf16, a):
    return _dot(jnp.concatenate([sel_bf16] * 3, axis=1), jnp.concatenate(_split3(a), axis=0))


def _dot_exact_tn(a, sel_bf16):
    return _dot_tn(jnp.concatenate(_split3(a), axis=0), jnp.concatenate([sel_bf16] * 3, axis=0))


def _dot_exact_lhs(a, sel_bf16):
    return _dot(jnp.concatenate(_split3(a), axis=1), jnp.concatenate([sel_bf16] * 3, axis=0))


def _sigmoid(x):
    return 0.5 * jnp.tanh(0.5 * x) + 0.5


def _silu(x):
    return x * _sigmoid(x)


def _softplus(x):
    return jnp.maximum(x, 0.0) + jnp.log1p(jnp.exp(-jnp.abs(x)))


def _rms(x, w):
    inv = lax.rsqrt(jnp.mean(x * x, axis=-1, keepdims=True) + EPS)
    return (x * inv) * w


def _rope_tile(t, cos, sin_signed):
    half = ATT_HEAD_DIM // 2
    lane = lax.broadcasted_iota(jnp.int32, t.shape, 1)
    first = (lane % ATT_HEAD_DIM) < half
    partner = jnp.where(first, pltpu.roll(t, LANES - half, 1), pltpu.roll(t, half, 1))
    return t * cos + partner * sin_signed


Q_SCALE = ATT_HEAD_DIM ** -0.5 * LOG2E


def _in_proj_kernel(x_ref, nw_ref, cost_ref, sint_ref, w_ref, wt_ref, wdt_ref,
                    z_ref, xbc_ref, k_ref, gs_ref, ga_ref, dt_ref, qt_ref, vt_ref, dtt_ref):
    h = _rms(x_ref[...], nw_ref[...]).astype(BF16)
    half = ATT_HEAD_DIM // 2
    lane = lax.broadcasted_iota(jnp.int32, (half, LANES), 1)
    same = lane % half == lax.broadcasted_iota(jnp.int32, (half, LANES), 0)
    first = lane % ATT_HEAD_DIM < half
    cost = cost_ref[...]
    sint = sint_ref[...]
    cos = _dot_exact_tn(cost, same.astype(BF16))
    sin = _dot_exact_tn(sint, jnp.where(same, jnp.where(first, -1.0, 1.0), 0.0).astype(BF16))

    def proj(lo, width):
        return _dot(h, w_ref[:, lo:lo + width])

    o = 0
    z_ref[...] = proj(o, SSD_INNER).astype(BF16)
    o += SSD_INNER
    xbc_ref[...] = proj(o, SSD_CONV_CH).astype(BF16)
    o += SSD_CONV_CH
    k_lin = proj(o, ATT_INNER)
    for g in range(ATT_INNER // LANES):
        cols = slice(g * LANES, (g + 1) * LANES)
        k_ref[:, cols] = _rope_tile(k_lin[:, cols], cos, sin).astype(BF16)
    o += ATT_INNER
    for ref in (gs_ref, ga_ref):
        ref[...] = proj(o, D_MODEL).astype(BF16)
        o += D_MODEL
    dt_ref[...] = _dot(h, wdt_ref[...])[:, :SSD_HEADS]

    dtt_ref[...] = _dot_nt(wt_ref[2 * ATT_INNER:2 * ATT_INNER + SSD_HEADS, :], h)
    vt_ref[...] = _dot_nt(wt_ref[ATT_INNER:2 * ATT_INNER, :], h).astype(BF16)
    qt = _dot_nt(wt_ref[0:ATT_INNER, :], h)
    for hd in range(ATT_HEADS):
        r0 = hd * ATT_HEAD_DIM
        x1 = qt[r0:r0 + half, :]
        x2 = qt[r0 + half:r0 + ATT_HEAD_DIM, :]
        qt_ref[r0:r0 + half, :] = ((x1 * cost - x2 * sint) * Q_SCALE).astype(BF16)
        qt_ref[r0 + half:r0 + ATT_HEAD_DIM, :] = ((x2 * cost + x1 * sint) * Q_SCALE).astype(BF16)


def _in_proj(x2d, nw, cost, sint, w_big, w_t, w_dt):
    t_tokens = x2d.shape[0]
    tm = min(TM_PROJ, t_tokens)
    grid = (t_tokens // tm,)
    row = lambda width: pl.BlockSpec((tm, width), lambda i: (i, 0))
    col = lambda height: pl.BlockSpec((height, tm), lambda i: (0, i))
    whole = pl.BlockSpec(memory_space=pltpu.VMEM)
    out_shapes = [
        jax.ShapeDtypeStruct((t_tokens, SSD_INNER), BF16),
        jax.ShapeDtypeStruct((t_tokens, SSD_CONV_CH), BF16),
        jax.ShapeDtypeStruct((t_tokens, ATT_INNER), BF16),
        jax.ShapeDtypeStruct((t_tokens, D_MODEL), BF16),
        jax.ShapeDtypeStruct((t_tokens, D_MODEL), BF16),
        jax.ShapeDtypeStruct((t_tokens, SSD_HEADS), F32),
        jax.ShapeDtypeStruct((ATT_INNER, t_tokens), BF16),
        jax.ShapeDtypeStruct((ATT_INNER, t_tokens), BF16),
        jax.ShapeDtypeStruct((SSD_HEADS, t_tokens), F32),
    ]
    out_specs = [row(SSD_INNER), row(SSD_CONV_CH), row(ATT_INNER), row(D_MODEL), row(D_MODEL), row(SSD_HEADS),
                 col(ATT_INNER), col(ATT_INNER), col(SSD_HEADS)]
    return pl.pallas_call(
        _in_proj_kernel,
        grid=grid,
        in_specs=[row(D_MODEL), whole, col(ATT_HEAD_DIM // 2), col(ATT_HEAD_DIM // 2),
                  whole, whole, whole],
        out_specs=out_specs,
        out_shape=out_shapes,
        compiler_params=pltpu.CompilerParams(dimension_semantics=("arbitrary",), vmem_limit_bytes=VMEM_LIMIT),
        name="in_proj",
    )(x2d, nw, cost, sint, w_big, w_t, w_dt)


def _ssd_kernel(z_ref, xbc_ref, xtail_ref, dt_ref, dtt_ref, cw_ref, cb_ref, dtb_ref, dtbc_ref, alog_ref, alogc_ref,
                dexp_ref, nw_ref, out_ref, state_ref):
    c = pl.program_id(1)
    cl = SSD_CHUNK
    heads_per_group = SSD_HEADS // SSD_GROUPS
    gw = heads_per_group * SSD_HEAD_DIM

    @pl.when(c == 0)
    def _():
        state_ref[...] = jnp.zeros_like(state_ref)

    xbc = xbc_ref[...]
    tail = xtail_ref[...]
    tail = jnp.where(c == 0, jnp.zeros_like(tail), tail)
    u_ext = jnp.concatenate([tail, xbc], axis=0)
    n_shift = SSD_CONV - 1
    srow = lax.broadcasted_iota(jnp.int32, (n_shift * cl, CONV_TAIL + cl), 0)
    scol = lax.broadcasted_iota(jnp.int32, (n_shift * cl, CONV_TAIL + cl), 1)
    shift_sel = (scol == CONV_TAIL + srow % cl - (n_shift - srow // cl)).astype(BF16)
    shifted = _dot(shift_sel, u_ext)
    acc = cb_ref[...] + cw_ref[n_shift:SSD_CONV, :] * xbc.astype(F32)
    for k in range(n_shift):
        acc = acc + cw_ref[k:k + 1, :] * shifted[k * cl:(k + 1) * cl, :]
    xc = _silu(acc)
    xs = xc[:, :SSD_INNER]
    bm = xc[:, SSD_INNER:SSD_INNER + SSD_GROUPS * SSD_STATE].astype(BF16)
    cm = xc[:, SSD_INNER + SSD_GROUPS * SSD_STATE:].astype(BF16)

    dt = _softplus(dt_ref[...] + dtb_ref[...])
    da = dt * (-jnp.exp(alog_ref[...]))
    dtt = _softplus(dtt_ref[...] + dtbc_ref[...])
    dat = dtt * (-jnp.exp(alogc_ref[...]))
    ri = lax.broadcasted_iota(jnp.int32, (cl, cl), 0)
    ci = lax.broadcasted_iota(jnp.int32, (cl, cl), 1)
    causal = ri >= ci
    tril = causal.astype(BF16)
    triu = (ri <= ci).astype(BF16)
    acum = _dot_exact_rhs(tril, da)
    acumt = _dot_exact_lhs(dat, triu)

    hrow = lax.broadcasted_iota(jnp.int32, (SSD_HEADS, SSD_INNER), 0)
    hcol = lax.broadcasted_iota(jnp.int32, (SSD_HEADS, SSD_INNER), 1) // SSD_HEAD_DIM
    expand = (hrow == hcol).astype(BF16)

    a_last = acum[cl - 1:cl, :]
    dt_e = _dot_exact_lhs(dt, expand)
    dec_e = _dot_exact_lhs(jnp.exp(a_last - acum), expand)
    ea_e = _dot_exact_lhs(jnp.exp(acum), expand)
    xdt = xs * dt_e
    xdt_b = xdt.astype(BF16)
    xdtdec_b = (xdt * dec_e).astype(BF16)
    chunk_decay_t = jnp.exp(acumt[:, cl - 1:cl])

    y_groups = []
    for g in range(SSD_GROUPS):
        b_g = bm[:, g * SSD_STATE:(g + 1) * SSD_STATE]
        c_g = cm[:, g * SSD_STATE:(g + 1) * SSD_STATE]
        prev = state_ref[g * gw:(g + 1) * gw, :]
        cb = _dot_nt(c_g, b_g)
        y_off = _dot_nt(c_g, prev.astype(BF16)) * ea_e[:, g * gw:(g + 1) * gw]
        new_states = _dot_tn(xdtdec_b[:, g * gw:(g + 1) * gw], b_g)
        y_heads = []
        decay_rows = []
        for e in range(heads_per_group):
            hd = g * heads_per_group + e
            lmat = jnp.where(causal, jnp.exp(acum[:, hd:hd + 1] - acumt[hd:hd + 1, :]), 0.0)
            m = (cb * lmat).astype(BF16)
            y_heads.append(_dot(m, xdt_b[:, hd * SSD_HEAD_DIM:(hd + 1) * SSD_HEAD_DIM]))
            decay_rows.append(jnp.broadcast_to(chunk_decay_t[hd:hd + 1, :], (SSD_HEAD_DIM, SSD_STATE)))
        y_groups.append(jnp.concatenate(y_heads, axis=1) + y_off)
        state_ref[g * gw:(g + 1) * gw, :] = prev * jnp.concatenate(decay_rows, axis=0) + new_states

    y = jnp.concatenate(y_groups, axis=1) + xs * dexp_ref[...]
    yg = y * _silu(z_ref[...].astype(F32))
    nw = nw_ref[...]
    outs = [_rms(yg[:, g * gw:(g + 1) * gw], nw[:, g * gw:(g + 1) * gw]) for g in range(SSD_GROUPS)]
    out_ref[...] = jnp.concatenate(outs, axis=1).astype(BF16)


def _ssd(z, xbc, dt, dtt, conv_w, conv_b, dt_bias, a_log, d_skip, norm_w, batch):
    t_tokens = z.shape[0]
    nc = t_tokens // batch // SSD_CHUNK
    tok = lambda width: pl.BlockSpec((SSD_CHUNK, width), lambda b, c: (b * nc + c, 0))
    whole = pl.BlockSpec(memory_space=pltpu.VMEM)
    dexp = jnp.repeat(d_skip, SSD_HEAD_DIM)[None, :]
    tails_per_chunk = SSD_CHUNK // CONV_TAIL
    prev_tail = pl.BlockSpec((CONV_TAIL, SSD_CONV_CH),
                             lambda b, c: (jnp.maximum((b * nc + c) * tails_per_chunk - 1, 0), 0))
    return pl.pallas_call(
        _ssd_kernel,
        grid=(batch, nc),
        in_specs=[tok(SSD_INNER), tok(SSD_CONV_CH), prev_tail, tok(SSD_HEADS),
                  pl.BlockSpec((SSD_HEADS, SSD_CHUNK), lambda b, c: (0, b * nc + c)),
                  whole, whole, whole, whole, whole, whole, whole, whole],
        out_specs=tok(SSD_INNER),
        out_shape=jax.ShapeDtypeStruct((t_tokens, SSD_INNER), BF16),
        scratch_shapes=[pltpu.VMEM((SSD_INNER, SSD_STATE), F32)],
        compiler_params=pltpu.CompilerParams(dimension_semantics=("arbitrary", "arbitrary"),
                                             vmem_limit_bytes=VMEM_LIMIT),
        name="ssd",
    )(z, xbc, xbc, dt, dtt, conv_w, conv_b[None, :], dt_bias[None, :], dt_bias[:, None],
      a_log[None, :], a_log[:, None], dexp, norm_w[None, :])


MOBA_SUPER = 2 * MOBA_BLOCK


MOBA_BLOCK_ROWS = 16
ONES_ROWS = 16


def _moba_kernel(qt_ref, k_ref, vt_ref, o_ref, kfull_ref, qaug_ref, sa_ref, sb_ref, sc_ref, m_ref, l_ref, acc_ref,
                 *, seq):
    nb = seq // MOBA_BLOCK
    sb = MOBA_SUPER
    nsb = seq // sb
    hd = ATT_HEAD_DIM
    nbr = MOBA_BLOCK_ROWS

    for c in range(nsb):
        rows = slice(c * sb, (c + 1) * sb)
        kfull_ref[rows, 0:LANES] = k_ref[rows, :]
        rblk = lax.broadcasted_iota(jnp.int32, (sb, LANES), 0) // MOBA_BLOCK + c * (sb // MOBA_BLOCK)
        rlane = lax.broadcasted_iota(jnp.int32, (sb, LANES), 1)
        kfull_ref[rows, LANES:2 * LANES] = jnp.where(rblk == rlane, 1.0, 0.0).astype(BF16)

    pr = lax.broadcasted_iota(jnp.int32, (nbr, seq), 0)
    pc = lax.broadcasted_iota(jnp.int32, (nbr, seq), 1) // MOBA_BLOCK
    pool = jnp.where(pr == pc, 1.0 / MOBA_BLOCK, 0.0).astype(BF16)
    kmean = _dot(pool, k_ref[...])
    km_hi = kmean.astype(BF16)
    km_lo = (kmean - km_hi.astype(F32)).astype(BF16)

    head_of_row = lax.broadcasted_iota(jnp.int32, (LANES, sb), 0) // hd
    blk_row = lax.broadcasted_iota(jnp.int32, (nbr, sb), 0)
    q_local = lax.broadcasted_iota(jnp.int32, (nbr, sb), 1)
    inrange = blk_row < nb
    krow = lax.broadcasted_iota(jnp.int32, (sb, 2 * sb), 0)
    qcol = lax.broadcasted_iota(jnp.int32, (sb, 2 * sb), 1) % sb
    causal = krow <= qcol
    ones_rows = jnp.ones((ONES_ROWS, sb), BF16)
    flag_pad = jnp.zeros((LANES - nbr, sb), BF16)

    def scores_into(s_ref, q_sb, kv):
        start = pl.multiple_of(kv * sb, sb)
        s_ref[...] = _dot(kfull_ref[pl.ds(start, sb), :], qaug_ref[q_sb])

    def absorb(kv, s_ref, diagonal):
        start = pl.multiple_of(kv * sb, sb)
        s_t = s_ref[...]
        if diagonal:
            s_t = jnp.where(causal, s_t, NEG_BIG)
        m_prev = m_ref[...]
        m_new = jnp.maximum(m_prev, jnp.max(s_t, axis=0, keepdims=True))
        m_ref[...] = m_new
        p_b = jnp.exp2(s_t - m_new).astype(BF16)
        alpha = jnp.exp2(m_prev - m_new)
        for a in range(2):
            cols = slice(a * sb, (a + 1) * sb)
            rows = slice(a * hd, (a + 1) * hd)
            v_aug = jnp.concatenate([vt_ref[rows, pl.ds(start, sb)], ones_rows], axis=0)
            r = _dot(v_aug, p_b[:, cols])
            l_ref[a:a + 1, :] = alpha[:, cols] * l_ref[a:a + 1, :] + r[hd:hd + 1, :]
            acc_ref[rows, :] = alpha[:, cols] * acc_ref[rows, :] + r[0:hd, :]

    def build_q_aug(i, carry):
        qt_sb = qt_ref[:, pl.ds(pl.multiple_of(i * sb, sb), sb)]
        own = 2 * i + jnp.where(q_local >= MOBA_BLOCK, 1, 0)
        valid = blk_row < own
        for a in range(2):
            qa = jnp.where(head_of_row == a, qt_sb, jnp.zeros_like(qt_sb))
            gate = _dot(km_hi, qa) + _dot(km_lo, qa)
            g = jnp.where(valid, gate, -jnp.inf)
            sel = jnp.zeros((nbr, sb), jnp.bool_)
            for _ in range(min(MOBA_TOPK, nb - 1)):
                cand = inrange & jnp.logical_not(sel)
                gm = jnp.max(jnp.where(cand, g, -jnp.inf), axis=0, keepdims=True)
                hit = cand & (g == gm)
                first = jnp.min(jnp.where(hit, blk_row, nb), axis=0, keepdims=True)
                sel = sel | (blk_row == first)
            chosen = (sel & valid) | (blk_row == own)
            flags = jnp.where(chosen, 0.0, NEG_BIG).astype(BF16)
            qaug_ref[i, :, a * sb:(a + 1) * sb] = jnp.concatenate([qa, flags, flag_pad], axis=0)
        return carry

    lax.fori_loop(0, nsb, build_q_aug, 0)
    scores_into(sc_ref, 0, 0)

    def q_super(i, carry):
        m_ref[...] = jnp.full(m_ref.shape, NEG_BIG, F32)
        l_ref[...] = jnp.zeros_like(l_ref)
        acc_ref[...] = jnp.zeros_like(acc_ref)
        nxt = jnp.minimum(i + 1, nsb - 1)

        @pl.when(i == 0)
        def _():
            absorb(0, sc_ref, True)

        @pl.when(i == 0)
        def _():
            scores_into(sc_ref, nxt, 0)

        @pl.when(i > 0)
        def _():
            scores_into(sa_ref, i, 1)
            absorb(0, sc_ref, False)

            def two_steps(t, c):
                scores_into(sb_ref, i, 2 * t + 2)
                absorb(2 * t + 1, sa_ref, False)
                scores_into(sa_ref, i, 2 * t + 3)
                absorb(2 * t + 2, sb_ref, False)
                return c

            lax.fori_loop(0, (i - 1) // 2, two_steps, 0)

            @pl.when(i % 2 == 1)
            def _():
                scores_into(sc_ref, nxt, 0)
                absorb(i, sa_ref, True)

            @pl.when(i % 2 == 0)
            def _():
                scores_into(sb_ref, i, i)
                absorb(i - 1, sa_ref, False)
                scores_into(sc_ref, nxt, 0)
                absorb(i, sb_ref, True)

        out_t = jnp.concatenate([acc_ref[a * hd:(a + 1) * hd, :] / l_ref[a:a + 1, :] for a in range(2)], axis=0)
        o_ref[pl.ds(pl.multiple_of(i * sb, sb), sb), :] = out_t.T.astype(BF16)
        return carry

    lax.fori_loop(0, nsb, q_super, 0)


def _moba(qt, k, vt, batch):
    t_tokens = k.shape[0]
    seq = t_tokens // batch
    assert seq % MOBA_SUPER == 0 and seq // MOBA_BLOCK <= MOBA_BLOCK_ROWS
    spec = pl.BlockSpec((seq, LANES), lambda b, hp: (b, hp))
    spec_t = pl.BlockSpec((LANES, seq), lambda b, hp: (hp, b))
    return pl.pallas_call(
        functools.partial(_moba_kernel, seq=seq),
        grid=(batch, ATT_INNER // LANES),
        in_specs=[spec_t, spec, spec_t],
        out_specs=spec,
        out_shape=jax.ShapeDtypeStruct((t_tokens, ATT_INNER), BF16),
        scratch_shapes=[pltpu.VMEM((seq, 2 * LANES), BF16),
                        pltpu.VMEM((seq // MOBA_SUPER, 2 * LANES, 2 * MOBA_SUPER), BF16),
                        pltpu.VMEM((MOBA_SUPER, 2 * MOBA_SUPER), F32),
                        pltpu.VMEM((MOBA_SUPER, 2 * MOBA_SUPER), F32),
                        pltpu.VMEM((MOBA_SUPER, 2 * MOBA_SUPER), F32),
                        pltpu.VMEM((1, 2 * MOBA_SUPER), F32),
                        pltpu.VMEM((SUBLANES, MOBA_SUPER), F32),
                        pltpu.VMEM((LANES, MOBA_SUPER), F32)],
        compiler_params=pltpu.CompilerParams(dimension_semantics=("arbitrary", "arbitrary"),
                                             vmem_limit_bytes=VMEM_LIMIT),
        name="moba",
    )(qt, k, vt)


ROUTER_ROWS = 40


def _merge_kernel(x_ref, yn_ref, att_ref, gs_ref, ga_ref, wso_ref, wao_ref, wo_ref, nw_ref,
                  wr_hi_ref, wr_lo_ref, br_ref, upper_ref, x1_ref, h2_ref, route_ref, jloc_ref, cnt_ref):
    y_s = _dot(yn_ref[...], wso_ref[...])
    y_a = _dot(att_ref[...], wao_ref[...])
    merged = _sigmoid(gs_ref[...].astype(F32)) * y_s + _sigmoid(ga_ref[...].astype(F32)) * y_a
    x1 = x_ref[...] + _dot(merged.astype(BF16), wo_ref[...])
    x1_ref[...] = x1
    h2 = _rms(x1, nw_ref[...])
    h2_ref[...] = h2.astype(BF16)

    h_hi = h2.astype(BF16)
    h_lo = (h2 - h_hi.astype(F32)).astype(BF16)
    wr_hi = wr_hi_ref[...]
    logits = _dot_nt(wr_hi, h_hi) + _dot_nt(wr_hi, h_lo) + _dot_nt(wr_lo_ref[...], h_hi) + br_ref[...]
    gl = logits[0:N_GROUPS, :]
    gidx = lax.broadcasted_iota(jnp.int32, gl.shape, 0)
    gmax = jnp.max(gl, axis=0, keepdims=True)
    gi = jnp.min(jnp.where(gl == gmax, gidx, N_GROUPS), axis=0, keepdims=True)
    g_w = 1.0 / jnp.sum(jnp.exp(gl - gmax), axis=0, keepdims=True)
    el = jnp.zeros((EXPERTS_PER_GROUP, gl.shape[1]), F32)
    for g in range(N_GROUPS):
        lo = N_GROUPS + g * EXPERTS_PER_GROUP
        el = el + jnp.where(gi == g, logits[lo:lo + EXPERTS_PER_GROUP, :], 0.0)
    eidx = lax.broadcasted_iota(jnp.int32, el.shape, 0)
    v0 = jnp.max(el, axis=0, keepdims=True)
    i0 = jnp.min(jnp.where(el == v0, eidx, EXPERTS_PER_GROUP), axis=0, keepdims=True)
    rest = jnp.where(eidx == i0, -jnp.inf, el)
    v1 = jnp.max(rest, axis=0, keepdims=True)
    i1 = jnp.min(jnp.where((rest == v1) & (eidx != i0), eidx, EXPERTS_PER_GROUP), axis=0, keepdims=True)
    t = jnp.exp(v1 - v0)
    w0 = g_w / (1.0 + t)
    w1 = g_w * t / (1.0 + t)
    base = gi * EXPERTS_PER_GROUP
    route_ref[...] = jnp.concatenate(
        [(base + i0).astype(F32), (base + i1).astype(F32), w0, w1], axis=0)

    tm = gl.shape[1]
    eids = jnp.concatenate([base + i0, base + i1], axis=1)
    onehot = lax.broadcasted_iota(jnp.int32, (N_EXPERTS, 2 * tm), 0) == eids
    csum = _dot(onehot.astype(BF16), upper_ref[...])
    oh = onehot.astype(F32)
    lrank = jnp.sum(csum * oh, axis=0, keepdims=True) - 1.0
    cnt = jnp.floor((csum[:, 2 * tm - 1:2 * tm] + (RUN_ALIGN - 1)) * (1.0 / RUN_ALIGN))
    er = lax.broadcasted_iota(jnp.int32, (N_EXPERTS, N_EXPERTS), 0)
    ec = lax.broadcasted_iota(jnp.int32, (N_EXPERTS, N_EXPERTS), 1)
    before = _dot((ec < er).astype(BF16), jnp.broadcast_to(cnt, (N_EXPERTS, LANES)).astype(BF16))
    off = before[:, 0:1] * RUN_ALIGN
    jloc = jnp.sum(oh * off, axis=0, keepdims=True) + lrank
    jloc_ref[...] = jnp.concatenate([jloc[:, :tm], jloc[:, tm:]], axis=0).astype(jnp.int32)
    cnt_ref[...] = jnp.broadcast_to(cnt * RUN_ALIGN, (N_EXPERTS, LANES))


def _merge(x2d, yn, att, gs, ga, wso, wao, wo, nw, wr_hi, wr_lo, br):
    t_tokens = x2d.shape[0]
    tm = TM_TOK
    upper = jnp.triu(jnp.ones((TILE_ASG, TILE_ASG), BF16))
    row = lambda: pl.BlockSpec((tm, D_MODEL), lambda i: (i, 0))
    whole = pl.BlockSpec(memory_space=pltpu.VMEM)
    return pl.pallas_call(
        _merge_kernel,
        grid=(t_tokens // tm,),
        in_specs=[row(), row(), row(), row(), row(), whole, whole, whole, whole, whole, whole, whole, whole],
        out_specs=[row(), row(), pl.BlockSpec((4, tm), lambda i: (0, i)), pl.BlockSpec((2, tm), lambda i: (0, i)),
                   pl.BlockSpec((N_EXPERTS, LANES), lambda i: (i, 0))],
        out_shape=[jax.ShapeDtypeStruct((t_tokens, D_MODEL), F32),
                   jax.ShapeDtypeStruct((t_tokens, D_MODEL), BF16),
                   jax.ShapeDtypeStruct((4, t_tokens), F32),
                   jax.ShapeDtypeStruct((2, t_tokens), jnp.int32),
                   jax.ShapeDtypeStruct((t_tokens // tm * N_EXPERTS, LANES), F32)],
        compiler_params=pltpu.CompilerParams(dimension_semantics=("arbitrary",), vmem_limit_bytes=VMEM_LIMIT),
        name="merge",
    )(x2d, yn, att, gs, ga, wso, wao, wo, nw, wr_hi, wr_lo, br, upper)


def _rows(ref, first, n_rows):
    start = first if isinstance(first, int) else pl.multiple_of(first, RUN_ALIGN)
    return ref.at[pl.ds(start, n_rows), :]


def _for_each_piece(count, fn, bits):
    for bit in range(bits - 1, ALIGN_BIT - 1, -1):
        size = 1 << bit

        @pl.when((count & size) != 0)
        def _():
            fn((count >> (bit + 1)) << (bit + 1), size)


def _moe_max_tiles(t_tokens):
    rows = 2 * t_tokens + (t_tokens // TM_TOK) * N_EXPERTS * (RUN_ALIGN - 1)
    return -(-rows // TM_MOE) + N_EXPERTS


def _moe_plan(cnt_rows, max_tiles):
    cnt = cnt_rows[:, 0].astype(jnp.int32).reshape(-1, N_EXPERTS)
    off = jnp.cumsum(cnt, axis=1) - cnt
    counts = jnp.sum(cnt, axis=0)
    padded = ((counts + TM_MOE - 1) // TM_MOE) * TM_MOE
    ends = jnp.cumsum(padded)
    starts = ends - padded
    base = starts[None, :] + jnp.cumsum(cnt, axis=0) - cnt
    tile_start = jnp.arange(max_tiles, dtype=jnp.int32) * TM_MOE
    tile_expert = jnp.sum((tile_start[:, None] >= ends[None, :]).astype(jnp.int32), axis=1)
    tile_expert = jnp.minimum(tile_expert, N_EXPERTS - 1)
    n_tiles = (ends[-1] // TM_MOE).astype(jnp.int32).reshape(1)
    flat = lambda a: a.reshape(-1).astype(jnp.int32)
    return flat(cnt), flat(off), flat(base), starts + counts, ends, tile_expert, n_tiles


def _dispatch_kernel(cnt_ref, off_ref, base_ref, padlo_ref, padhi_ref, ntile_ref, h2_ref, jrow_ref, xs_hbm,
                     xbuf0, xbuf1, zeros_ref, sems, zsem, *, max_tiles):
    i = pl.program_id(0)
    n = pl.num_programs(0)
    xbufs = (xbuf0, xbuf1)

    def runs(tile, slot, wait):
        def per_expert(e, carry):
            k = tile * N_EXPERTS + e
            src, dst = off_ref[k], base_ref[k]

            def piece(o, size):
                cp = pltpu.make_async_copy(_rows(xbufs[slot], src + o, size), _rows(xs_hbm, dst + o, size),
                                           sems.at[slot])
                cp.wait() if wait else cp.start()
            _for_each_piece(cnt_ref[k], piece, RUN_BITS)
            return carry
        lax.fori_loop(0, N_EXPERTS, per_expert, 0)

    @pl.when(i == 0)
    def _():
        zeros_ref[...] = jnp.zeros_like(zeros_ref)
        nt = ntile_ref[0]
        for wait in (False, True):
            def pad(e, carry):
                lo = padlo_ref[e]

                def piece(o, size):
                    cp = pltpu.make_async_copy(_rows(zeros_ref, 0, size), _rows(xs_hbm, lo + o, size), zsem)
                    cp.wait() if wait else cp.start()
                _for_each_piece(padhi_ref[e] - lo, piece, PAD_BITS)
                return carry
            lax.fori_loop(0, N_EXPERTS, pad, 0)

            def spare(tile, carry):
                cp = pltpu.make_async_copy(zeros_ref, _rows(xs_hbm, tile * TM_MOE, TM_MOE), zsem)
                cp.wait() if wait else cp.start()
                return carry
            lax.fori_loop(nt, max_tiles, spare, 0)

    for slot in range(2):
        @pl.when(i % 2 == slot)
        def _():
            @pl.when(i >= 2)
            def _():
                runs(i - 2, slot, True)
            row = lax.broadcasted_iota(jnp.int32, (LOC_ROWS, TM_TOK), 0)
            j = jrow_ref[...]
            perm = ((row == j[0:1, :]) | (row == j[1:2, :])).astype(BF16)
            xbufs[slot][...] = _dot(perm, h2_ref[...]).astype(BF16)
            runs(i, slot, False)

            @pl.when(i == n - 1)
            def _():
                @pl.when(i >= 1)
                def _():
                    runs(i - 1, 1 - slot, True)
                runs(i, slot, True)


def _dispatch(plan, jloc, h2, max_tiles):
    cnt, off, base, pad_lo, pad_hi, _, n_tiles = plan
    t_tokens = h2.shape[0]
    grid_spec = pltpu.PrefetchScalarGridSpec(
        num_scalar_prefetch=6,
        grid=(t_tokens // TM_TOK,),
        in_specs=[pl.BlockSpec((TM_TOK, D_MODEL), lambda i, *_: (i, 0)),
                  pl.BlockSpec((2, TM_TOK), lambda i, *_: (0, i))],
        out_specs=pl.BlockSpec(memory_space=pl.ANY),
        scratch_shapes=[pltpu.VMEM((LOC_ROWS, D_MODEL), BF16),
                        pltpu.VMEM((LOC_ROWS, D_MODEL), BF16),
                        pltpu.VMEM((TM_MOE, D_MODEL), BF16),
                        pltpu.SemaphoreType.DMA((2,)), pltpu.SemaphoreType.DMA(())],
    )
    return pl.pallas_call(
        functools.partial(_dispatch_kernel, max_tiles=max_tiles),
        grid_spec=grid_spec,
        out_shape=jax.ShapeDtypeStruct((max_tiles * TM_MOE, D_MODEL), BF16),
        compiler_params=pltpu.CompilerParams(dimension_semantics=("arbitrary",), vmem_limit_bytes=VMEM_LIMIT),
        name="dispatch",
    )(cnt, off, base, pad_lo, pad_hi, n_tiles, h2, jloc)


def _moe_kernel(texp_ref, ntile_ref, xs_ref, wgu_ref, wd_ref, y_ref):
    t = pl.program_id(0)

    @pl.when(t < ntile_ref[0])
    def _():
        gu = _dot(xs_ref[...], wgu_ref[0])
        hid = (_silu(gu[:, :D_EXPERT]) * gu[:, D_EXPERT:]).astype(BF16)
        y_ref[...] = _dot(hid, wd_ref[0]).astype(BF16)

    @pl.when(t >= ntile_ref[0])
    def _():
        y_ref[...] = jnp.zeros_like(y_ref)


def _moe(xs, tile_expert, n_tiles, wgu, wd, max_tiles):
    live = lambda t, nt: jnp.minimum(t, nt[0] - 1)
    tile = lambda index: pl.BlockSpec((TM_MOE, D_MODEL), index)
    grid_spec = pltpu.PrefetchScalarGridSpec(
        num_scalar_prefetch=2,
        grid=(max_tiles,),
        in_specs=[tile(lambda t, te, nt: (live(t, nt), 0)),
                  pl.BlockSpec((1, D_MODEL, 2 * D_EXPERT), lambda t, te, nt: (te[live(t, nt)], 0, 0)),
                  pl.BlockSpec((1, D_EXPERT, D_MODEL), lambda t, te, nt: (te[live(t, nt)], 0, 0))],
        out_specs=tile(lambda t, te, nt: (t, 0)),
    )
    return pl.pallas_call(
        _moe_kernel,
        grid_spec=grid_spec,
        out_shape=jax.ShapeDtypeStruct((max_tiles * TM_MOE, D_MODEL), BF16),
        compiler_params=pltpu.CompilerParams(dimension_semantics=("arbitrary",), vmem_limit_bytes=VMEM_LIMIT),
        name="moe",
    )(tile_expert, n_tiles, xs, wgu, wd)


def _tail_kernel(cnt_ref, off_ref, base_ref, x1_ref, p_ref, jw_ref, y_hbm, pnw_ref, wpg_ref, wple_ref, fnw_ref,
                 o_ref, ybuf0, ybuf1, sems):
    i = pl.program_id(0)
    n = pl.num_programs(0)
    ybufs = (ybuf0, ybuf1)

    def runs(tile, slot, wait):
        def per_expert(e, carry):
            k = tile * N_EXPERTS + e
            src, dst = base_ref[k], off_ref[k]

            def piece(o, size):
                cp = pltpu.make_async_copy(_rows(y_hbm, src + o, size), _rows(ybufs[slot], dst + o, size),
                                           sems.at[slot])
                cp.wait() if wait else cp.start()
            _for_each_piece(cnt_ref[k], piece, RUN_BITS)
            return carry
        lax.fori_loop(0, N_EXPERTS, per_expert, 0)

    @pl.when(i == 0)
    def _():
        ybuf0[...] = jnp.zeros_like(ybuf0)
        ybuf1[...] = jnp.zeros_like(ybuf1)
        runs(0, 0, False)

    for slot in range(2):
        @pl.when(i % 2 == slot)
        def _():
            @pl.when(i + 1 < n)
            def _():
                runs(i + 1, 1 - slot, False)
            runs(i, slot, True)
            y_loc = ybufs[slot][...]
            jw = jw_ref[...]
            col = lax.broadcasted_iota(jnp.int32, (TM_TOK, LOC_ROWS), 1)
            pick = sum(jnp.where(col == jw[:, s:s + 1].astype(jnp.int32), jw[:, 2 + s:3 + s], 0.0) for s in range(2))
            moe = _dot(pick.astype(BF16), y_loc)
            x2 = x1_ref[...] + moe
            gate = _sigmoid(_dot(_rms(x2, pnw_ref[...]).astype(BF16), wpg_ref[...]))
            x3 = x2 + gate * _dot(p_ref[...].astype(BF16), wple_ref[...])
            o_ref[...] = _rms(x3, fnw_ref[...])


def _tail(plan, x1, p2d, y_sorted, jw, pnw, wpg, wple, fnw):
    cnt, off, base = plan[0:3]
    t_tokens = x1.shape[0]
    whole = pl.BlockSpec(memory_space=pltpu.VMEM)
    grid_spec = pltpu.PrefetchScalarGridSpec(
        num_scalar_prefetch=3,
        grid=(t_tokens // TM_TOK,),
        in_specs=[pl.BlockSpec((TM_TOK, D_MODEL), lambda i, *_: (i, 0)),
                  pl.BlockSpec((TM_TOK, PLE_DIM), lambda i, *_: (i, 0)),
                  pl.BlockSpec((TM_TOK, 4), lambda i, *_: (i, 0)),
                  pl.BlockSpec(memory_space=pl.ANY),
                  whole, whole, whole, whole],
        out_specs=pl.BlockSpec((TM_TOK, D_MODEL), lambda i, *_: (i, 0)),
        scratch_shapes=[pltpu.VMEM((LOC_ROWS, D_MODEL), BF16),
                        pltpu.VMEM((LOC_ROWS, D_MODEL), BF16),
                        pltpu.SemaphoreType.DMA((2,))],
    )
    return pl.pallas_call(
        _tail_kernel,
        grid_spec=grid_spec,
        out_shape=jax.ShapeDtypeStruct((t_tokens, D_MODEL), F32),
        compiler_params=pltpu.CompilerParams(dimension_semantics=("arbitrary",), vmem_limit_bytes=VMEM_LIMIT),
        name="tail",
    )(cnt, off, base, x1, p2d, jw, y_sorted, pnw, wpg, wple, fnw)


def _rope_tables(positions):
    half = ATT_HEAD_DIM // 2
    inv_freq = ROPE_THETA ** (-jnp.arange(half, dtype=F32) / half)
    ang = positions.astype(F32)[..., None] * inv_freq
    return jnp.cos(ang).reshape(-1, half).T, jnp.sin(ang).reshape(-1, half).T


def _layer(x2d, p2d, rope_tables, batch, attn_norm_w, w_in, conv_w, conv_b, dt_bias, a_log, d_skip, ssd_norm_w,
           w_ssd_out, w_attn_out, w_out, moe_norm_w, w_rg, b_rg, w_re, b_re, w_gate, w_up, w_down,
           ple_norm_w, w_ple, w_ple_gate, final_norm_w):
    t_tokens = x2d.shape[0]
    offs = np.cumsum((0,) + IN_SPLITS)
    cols = lambda i: w_in[:, offs[i]:offs[i + 1]]
    w_big = jnp.concatenate([cols(0), cols(1), cols(4), cols(6), cols(7)], axis=1).astype(BF16)
    w_t = jnp.concatenate([cols(3), cols(5), cols(2)], axis=1).T.astype(BF16)
    w_dt = jnp.pad(cols(2), ((0, 0), (0, LANES - SSD_HEADS))).astype(BF16)

    z, xbc, k, gs, ga, dt, qt, vt, dtt = _in_proj(x2d, attn_norm_w[None, :], *rope_tables, w_big, w_t, w_dt)

    yn = _ssd(z, xbc, dt, dtt, conv_w, conv_b, dt_bias, a_log, d_skip, ssd_norm_w, batch)

    att = _moba(qt, k, vt, batch)

    wr = jnp.concatenate([w_rg, w_re], axis=1).T
    wr = jnp.pad(wr, ((0, ROUTER_ROWS - wr.shape[0]), (0, 0)))
    wr_hi = wr.astype(BF16)
    wr_lo = (wr - wr_hi.astype(F32)).astype(BF16)
    br = jnp.pad(jnp.concatenate([b_rg, b_re]), (0, ROUTER_ROWS - N_GROUPS - N_EXPERTS))[:, None]
    x1, h2, route, jloc, cnt_rows = _merge(x2d, yn, att, gs, ga, w_ssd_out.astype(BF16), w_attn_out.astype(BF16),
                           w_out.astype(BF16), moe_norm_w[None, :], wr_hi, wr_lo, br)

    max_tiles = _moe_max_tiles(t_tokens)
    plan = _moe_plan(cnt_rows, max_tiles)
    wgu = jnp.concatenate([w_gate, w_up], axis=-1).reshape(N_EXPERTS, D_MODEL, 2 * D_EXPERT).astype(BF16)
    wd = w_down.reshape(N_EXPERTS, D_EXPERT, D_MODEL).astype(BF16)
    x_sorted = _dispatch(plan, jloc, h2, max_tiles)
    y_sorted = _moe(x_sorted, plan[5], plan[6], wgu, wd, max_tiles)
    jw = jnp.concatenate([jloc.astype(F32), route[2:4]], axis=0).T

    return _tail(plan, x1, p2d, y_sorted, jw, ple_norm_w[None, :],
                 w_ple_gate.astype(BF16), w_ple.astype(BF16), final_norm_w[None, :])


def kernel(x, p, positions, attn_norm_w, w_in, conv_w, conv_b, dt_bias, a_log, d_skip, ssd_norm_w, w_ssd_out,
           w_attn_out, w_out, moe_norm_w, w_router_group, b_router_group, w_router_expert, b_router_expert,
           w_exp_gate, w_exp_up, w_exp_down, ple_norm_w, w_ple, w_ple_gate, final_norm_w):
    batch, seq, d = x.shape
    depth = p.shape[0]
    assert depth == 1, "the final norm is fused into the last layer's tail kernel"
    rope_tables = _rope_tables(positions)
    i = 0
    out = _layer(x.reshape(batch * seq, d), p[i].reshape(batch * seq, PLE_DIM), rope_tables, batch,
                 attn_norm_w[i], w_in[i], conv_w[i], conv_b[i], dt_bias[i], a_log[i], d_skip[i], ssd_norm_w[i],
                 w_ssd_out[i], w_attn_out[i], w_out[i], moe_norm_w[i], w_router_group[i], b_router_group[i],
                 w_router_expert[i], b_router_expert[i], w_exp_gate[i], w_exp_up[i], w_exp_down[i],
                 ple_norm_w[i], w_ple[i], w_ple_gate[i], final_norm_w)
    return out.reshape(batch, seq, d)
```

```python
import functools

import jax
import jax.numpy as jnp
import numpy as np
from jax import lax
from jax.experimental import pallas as pl
from jax.experimental.pallas import tpu as pltpu

F32 = jnp.float32
BF16 = jnp.bfloat16

EPS = 1e-6
D_MODEL = 1024
SSD_HEADS = 16
SSD_HEAD_DIM = 64
SSD_INNER = SSD_HEADS * SSD_HEAD_DIM
SSD_GROUPS = 2
SSD_STATE = 64
SSD_CONV = 4
SSD_CHUNK = 128
SSD_CONV_CH = SSD_INNER + 2 * SSD_GROUPS * SSD_STATE
CONV_TAIL = 16
ATT_HEADS = 16
ATT_HEAD_DIM = 64
ATT_INNER = ATT_HEADS * ATT_HEAD_DIM
MOBA_BLOCK = 256
MOBA_TOPK = 3
ROPE_THETA = 10000.0
IN_SPLITS = (SSD_INNER, SSD_CONV_CH, SSD_HEADS, ATT_INNER, ATT_INNER, ATT_INNER, D_MODEL, D_MODEL)
N_GROUPS = 4
EXPERTS_PER_GROUP = 8
N_EXPERTS = N_GROUPS * EXPERTS_PER_GROUP
D_EXPERT = 256
PLE_DIM = 256

LANES = 128
SUBLANES = 8
NEG_BIG = -1e30
LOG2E = 1.4426950408889634
VMEM_LIMIT = 56 * 1024 * 1024

TM_PROJ = 512
TM_MOE = 512
TM_TOK = 512
TILE_ASG = 2 * TM_TOK
RUN_ALIGN = 2 * SUBLANES
LOC_ROWS = TILE_ASG + N_EXPERTS * RUN_ALIGN
ALIGN_BIT = RUN_ALIGN.bit_length() - 1
RUN_BITS = TILE_ASG.bit_length()
PAD_BITS = TM_MOE.bit_length() - 1


def _dot(a, b):
    return jnp.dot(a, b, preferred_element_type=F32)


def _dot_nt(a, b):
    return lax.dot_general(a, b, (((1,), (1,)), ((), ())), preferred_element_type=F32)


def _dot_tn(a, b):
    return lax.dot_general(a, b, (((0,), (0,)), ((), ())), preferred_element_type=F32)


def _split3(a):
    hi = a.astype(BF16)
    r1 = a - hi.astype(F32)
    mid = r1.astype(BF16)
    lo = (r1 - mid.astype(F32)).astype(BF16)
    return hi, mid, lo


def _dot_exact_rhs(sel_bf16, a):
    return _dot(jnp.concatenate([sel_bf16] * 3, axis=1), jnp.concatenate(_split3(a), axis=0))


def _dot_exact_tn(a, sel_bf16):
    return _dot_tn(jnp.concatenate(_split3(a), axis=0), jnp.concatenate([sel_bf16] * 3, axis=0))


def _dot_exact_lhs(a, sel_bf16):
    return _dot(jnp.concatenate(_split3(a), axis=1), jnp.concatenate([sel_bf16] * 3, axis=0))


def _sigmoid(x):
    return 0.5 * jnp.tanh(0.5 * x) + 0.5


def _silu(x):
    return x * _sigmoid(x)


def _softplus(x):
    return jnp.maximum(x, 0.0) + jnp.log1p(jnp.exp(-jnp.abs(x)))


def _rms(x, w):
    inv = lax.rsqrt(jnp.mean(x * x, axis=-1, keepdims=True) + EPS)
    return (x * inv) * w


def _rope_tile(t, cos, sin_signed):
    half = ATT_HEAD_DIM // 2
    lane = lax.broadcasted_iota(jnp.int32, t.shape, 1)
    first = (lane % ATT_HEAD_DIM) < half
    partner = jnp.where(first, pltpu.roll(t, LANES - half, 1), pltpu.roll(t, half, 1))
    return t * cos + partner * sin_signed


Q_SCALE = ATT_HEAD_DIM ** -0.5 * LOG2E


def _in_proj_kernel(x_ref, nw_ref, cost_ref, sint_ref, w_ref, wt_ref, wdt_ref,
                    z_ref, xbc_ref, k_ref, gs_ref, ga_ref, dt_ref, qt_ref, vt_ref, dtt_ref):
    h = _rms(x_ref[...], nw_ref[...]).astype(BF16)
    half = ATT_HEAD_DIM // 2
    lane = lax.broadcasted_iota(jnp.int32, (half, LANES), 1)
    same = lane % half == lax.broadcasted_iota(jnp.int32, (half, LANES), 0)
    first = lane % ATT_HEAD_DIM < half
    cost = cost_ref[...]
    sint = sint_ref[...]
    cos = _dot_exact_tn(cost, same.astype(BF16))
    sin = _dot_exact_tn(sint, jnp.where(same, jnp.where(first, -1.0, 1.0), 0.0).astype(BF16))

    def proj(lo, width):
        return _dot(h, w_ref[:, lo:lo + width])

    o = 0
    z_ref[...] = proj(o, SSD_INNER).astype(BF16)
    o += SSD_INNER
    xbc_ref[...] = proj(o, SSD_CONV_CH).astype(BF16)
    o += SSD_CONV_CH
    k_lin = proj(o, ATT_INNER)
    for g in range(ATT_INNER // LANES):
        cols = slice(g * LANES, (g + 1) * LANES)
        k_ref[:, cols] = _rope_tile(k_lin[:, cols], cos, sin).astype(BF16)
    o += ATT_INNER
    for ref in (gs_ref, ga_ref):
        ref[...] = proj(o, D_MODEL).astype(BF16)
        o += D_MODEL
    dt_ref[...] = _dot(h, wdt_ref[...])[:, :SSD_HEADS]

    dtt_ref[...] = _dot_nt(wt_ref[2 * ATT_INNER:2 * ATT_INNER + SSD_HEADS, :], h)
    vt_ref[...] = _dot_nt(wt_ref[ATT_INNER:2 * ATT_INNER, :], h).astype(BF16)
    qt = _dot_nt(wt_ref[0:ATT_INNER, :], h)
    for hd in range(ATT_HEADS):
        r0 = hd * ATT_HEAD_DIM
        x1 = qt[r0:r0 + half, :]
        x2 = qt[r0 + half:r0 + ATT_HEAD_DIM, :]
        qt_ref[r0:r0 + half, :] = ((x1 * cost - x2 * sint) * Q_SCALE).astype(BF16)
        qt_ref[r0 + half:r0 + ATT_HEAD_DIM, :] = ((x2 * cost + x1 * sint) * Q_SCALE).astype(BF16)


def _in_proj(x2d, nw, cost, sint, w_big, w_t, w_dt):
    t_tokens = x2d.shape[0]
    tm = min(TM_PROJ, t_tokens)
    grid = (t_tokens // tm,)
    row = lambda width: pl.BlockSpec((tm, width), lambda i: (i, 0))
    col = lambda height: pl.BlockSpec((height, tm), lambda i: (0, i))
    whole = pl.BlockSpec(memory_space=pltpu.VMEM)
    out_shapes = [
        jax.ShapeDtypeStruct((t_tokens, SSD_INNER), BF16),
        jax.ShapeDtypeStruct((t_tokens, SSD_CONV_CH), BF16),
        jax.ShapeDtypeStruct((t_tokens, ATT_INNER), BF16),
        jax.ShapeDtypeStruct((t_tokens, D_MODEL), BF16),
        jax.ShapeDtypeStruct((t_tokens, D_MODEL), BF16),
        jax.ShapeDtypeStruct((t_tokens, SSD_HEADS), F32),
        jax.ShapeDtypeStruct((ATT_INNER, t_tokens), BF16),
        jax.ShapeDtypeStruct((ATT_INNER, t_tokens), BF16),
        jax.ShapeDtypeStruct((SSD_HEADS, t_tokens), F32),
    ]
    out_specs = [row(SSD_INNER), row(SSD_CONV_CH), row(ATT_INNER), row(D_MODEL), row(D_MODEL), row(SSD_HEADS),
                 col(ATT_INNER), col(ATT_INNER), col(SSD_HEADS)]
    return pl.pallas_call(
        _in_proj_kernel,
        grid=grid,
        in_specs=[row(D_MODEL), whole, col(ATT_HEAD_DIM // 2), col(ATT_HEAD_DIM // 2),
                  whole, whole, whole],
        out_specs=out_specs,
        out_shape=out_shapes,
        compiler_params=pltpu.CompilerParams(dimension_semantics=("arbitrary",), vmem_limit_bytes=VMEM_LIMIT),
        name="in_proj",
    )(x2d, nw, cost, sint, w_big, w_t, w_dt)


def _ssd_kernel(z_ref, xbc_ref, xtail_ref, dt_ref, dtt_ref, cw_ref, cb_ref, dtb_ref, dtbc_ref, alog_ref, alogc_ref,
                dexp_ref, nw_ref, out_ref, state_ref):
    c = pl.program_id(1)
    cl = SSD_CHUNK
    heads_per_group = SSD_HEADS // SSD_GROUPS
    gw = heads_per_group * SSD_HEAD_DIM

    @pl.when(c == 0)
    def _():
        state_ref[...] = jnp.zeros_like(state_ref)

    xbc = xbc_ref[...]
    tail = xtail_ref[...]
    tail = jnp.where(c == 0, jnp.zeros_like(tail), tail)
    u_ext = jnp.concatenate([tail, xbc], axis=0)
    n_shift = SSD_CONV - 1
    srow = lax.broadcasted_iota(jnp.int32, (n_shift * cl, CONV_TAIL + cl), 0)
    scol = lax.broadcasted_iota(jnp.int32, (n_shift * cl, CONV_TAIL + cl), 1)
    shift_sel = (scol == CONV_TAIL + srow % cl - (n_shift - srow // cl)).astype(BF16)
    shifted = _dot(shift_sel, u_ext)
    acc = cb_ref[...] + cw_ref[n_shift:SSD_CONV, :] * xbc.astype(F32)
    for k in range(n_shift):
        acc = acc + cw_ref[k:k + 1, :] * shifted[k * cl:(k + 1) * cl, :]
    xc = _silu(acc)
    xs = xc[:, :SSD_INNER]
    bm = xc[:, SSD_INNER:SSD_INNER + SSD_GROUPS * SSD_STATE].astype(BF16)
    cm = xc[:, SSD_INNER + SSD_GROUPS * SSD_STATE:].astype(BF16)

    dt = _softplus(dt_ref[...] + dtb_ref[...])
    da = dt * (-jnp.exp(alog_ref[...]))
    dtt = _softplus(dtt_ref[...] + dtbc_ref[...])
    dat = dtt * (-jnp.exp(alogc_ref[...]))
    ri = lax.broadcasted_iota(jnp.int32, (cl, cl), 0)
    ci = lax.broadcasted_iota(jnp.int32, (cl, cl), 1)
    causal = ri >= ci
    tril = causal.astype(BF16)
    triu = (ri <= ci).astype(BF16)
    acum = _dot_exact_rhs(tril, da)
    acumt = _dot_exact_lhs(dat, triu)

    hrow = lax.broadcasted_iota(jnp.int32, (SSD_HEADS, SSD_INNER), 0)
    hcol = lax.broadcasted_iota(jnp.int32, (SSD_HEADS, SSD_INNER), 1) // SSD_HEAD_DIM
    expand = (hrow == hcol).astype(BF16)

    a_last = acum[cl - 1:cl, :]
    dt_e = _dot_exact_lhs(dt, expand)
    dec_e = _dot_exact_lhs(jnp.exp(a_last - acum), expand)
    ea_e = _dot_exact_lhs(jnp.exp(acum), expand)
    xdt = xs * dt_e
    xdt_b = xdt.astype(BF16)
    xdtdec_b = (xdt * dec_e).astype(BF16)
    chunk_decay_t = jnp.exp(acumt[:, cl - 1:cl])

    y_groups = []
    for g in range(SSD_GROUPS):
        b_g = bm[:, g * SSD_STATE:(g + 1) * SSD_STATE]
        c_g = cm[:, g * SSD_STATE:(g + 1) * SSD_STATE]
        prev = state_ref[g * gw:(g + 1) * gw, :]
        cb = _dot_nt(c_g, b_g)
        y_off = _dot_nt(c_g, prev.astype(BF16)) * ea_e[:, g * gw:(g + 1) * gw]
        new_states = _dot_tn(xdtdec_b[:, g * gw:(g + 1) * gw], b_g)
        y_heads = []
        decay_rows = []
        for e in range(heads_per_group):
            hd = g * heads_per_group + e
            lmat = jnp.where(causal, jnp.exp(acum[:, hd:hd + 1] - acumt[hd:hd + 1, :]), 0.0)
            m = (cb * lmat).astype(BF16)
            y_heads.append(_dot(m, xdt_b[:, hd * SSD_HEAD_DIM:(hd + 1) * SSD_HEAD_DIM]))
            decay_rows.append(jnp.broadcast_to(chunk_decay_t[hd:hd + 1, :], (SSD_HEAD_DIM, SSD_STATE)))
        y_groups.append(jnp.concatenate(y_heads, axis=1) + y_off)
        state_ref[g * gw:(g + 1) * gw, :] = prev * jnp.concatenate(decay_rows, axis=0) + new_states

    y = jnp.concatenate(y_groups, axis=1) + xs * dexp_ref[...]
    yg = y * _silu(z_ref[...].astype(F32))
    nw = nw_ref[...]
    outs = [_rms(yg[:, g * gw:(g + 1) * gw], nw[:, g * gw:(g + 1) * gw]) for g in range(SSD_GROUPS)]
    out_ref[...] = jnp.concatenate(outs, axis=1).astype(BF16)


def _ssd(z, xbc, dt, dtt, conv_w, conv_b, dt_bias, a_log, d_skip, norm_w, batch):
    t_tokens = z.shape[0]
    nc = t_tokens // batch // SSD_CHUNK
    tok = lambda width: pl.BlockSpec((SSD_CHUNK, width), lambda b, c: (b * nc + c, 0))
    whole = pl.BlockSpec(memory_space=pltpu.VMEM)
    dexp = jnp.repeat(d_skip, SSD_HEAD_DIM)[None, :]
    tails_per_chunk = SSD_CHUNK // CONV_TAIL
    prev_tail = pl.BlockSpec((CONV_TAIL, SSD_CONV_CH),
                             lambda b, c: (jnp.maximum((b * nc + c) * tails_per_chunk - 1, 0), 0))
    return pl.pallas_call(
        _ssd_kernel,
        grid=(batch, nc),
        in_specs=[tok(SSD_INNER), tok(SSD_CONV_CH), prev_tail, tok(SSD_HEADS),
                  pl.BlockSpec((SSD_HEADS, SSD_CHUNK), lambda b, c: (0, b * nc + c)),
                  whole, whole, whole, whole, whole, whole, whole, whole],
        out_specs=tok(SSD_INNER),
        out_shape=jax.ShapeDtypeStruct((t_tokens, SSD_INNER), BF16),
        scratch_shapes=[pltpu.VMEM((SSD_INNER, SSD_STATE), F32)],
        compiler_params=pltpu.CompilerParams(dimension_semantics=("arbitrary", "arbitrary"),
                                             vmem_limit_bytes=VMEM_LIMIT),
        name="ssd",
    )(z, xbc, xbc, dt, dtt, conv_w, conv_b[None, :], dt_bias[None, :], dt_bias[:, None],
      a_log[None, :], a_log[:, None], dexp, norm_w[None, :])


MOBA_SUPER = 2 * MOBA_BLOCK


MOBA_BLOCK_ROWS = 16
ONES_ROWS = 16


def _moba_kernel(qt_ref, k_ref, vt_ref, o_ref, kfull_ref, qaug_ref, sa_ref, sb_ref, sc_ref, m_ref, l_ref, acc_ref,
                 *, seq):
    nb = seq // MOBA_BLOCK
    sb = MOBA_SUPER
    nsb = seq // sb
    hd = ATT_HEAD_DIM
    nbr = MOBA_BLOCK_ROWS

    for c in range(nsb):
        rows = slice(c * sb, (c + 1) * sb)
        kfull_ref[rows, 0:LANES] = k_ref[rows, :]
        rblk = lax.broadcasted_iota(jnp.int32, (sb, LANES), 0) // MOBA_BLOCK + c * (sb // MOBA_BLOCK)
        rlane = lax.broadcasted_iota(jnp.int32, (sb, LANES), 1)
        kfull_ref[rows, LANES:2 * LANES] = jnp.where(rblk == rlane, 1.0, 0.0).astype(BF16)

    pr = lax.broadcasted_iota(jnp.int32, (nbr, seq), 0)
    pc = lax.broadcasted_iota(jnp.int32, (nbr, seq), 1) // MOBA_BLOCK
    pool = jnp.where(pr == pc, 1.0 / MOBA_BLOCK, 0.0).astype(BF16)
    kmean = _dot(pool, k_ref[...])
    km_hi = kmean.astype(BF16)
    km_lo = (kmean - km_hi.astype(F32)).astype(BF16)

    head_of_row = lax.broadcasted_iota(jnp.int32, (LANES, sb), 0) // hd
    blk_row = lax.broadcasted_iota(jnp.int32, (nbr, sb), 0)
    q_local = lax.broadcasted_iota(jnp.int32, (nbr, sb), 1)
    inrange = blk_row < nb
    krow = lax.broadcasted_iota(jnp.int32, (sb, 2 * sb), 0)
    qcol = lax.broadcasted_iota(jnp.int32, (sb, 2 * sb), 1) % sb
    causal = krow <= qcol
    ones_rows = jnp.ones((ONES_ROWS, sb), BF16)
    flag_pad = jnp.zeros((LANES - nbr, sb), BF16)

    def scores_into(s_ref, q_sb, kv):
        start = pl.multiple_of(kv * sb, sb)
        s_ref[...] = _dot(kfull_ref[pl.ds(start, sb), :], qaug_ref[q_sb])

    def absorb(kv, s_ref, diagonal):
        start = pl.multiple_of(kv * sb, sb)
        s_t = s_ref[...]
        if diagonal:
            s_t = jnp.where(causal, s_t, NEG_BIG)
        m_prev = m_ref[...]
        m_new = jnp.maximum(m_prev, jnp.max(s_t, axis=0, keepdims=True))
        m_ref[...] = m_new
        p_b = jnp.exp2(s_t - m_new).astype(BF16)
        alpha = jnp.exp2(m_prev - m_new)
        for a in range(2):
            cols = slice(a * sb, (a + 1) * sb)
            rows = slice(a * hd, (a + 1) * hd)
            v_aug = jnp.concatenate([vt_ref[rows, pl.ds(start, sb)], ones_rows], axis=0)
            r = _dot(v_aug, p_b[:, cols])
            l_ref[a:a + 1, :] = alpha[:, cols] * l_ref[a:a + 1, :] + r[hd:hd + 1, :]
            acc_ref[rows, :] = alpha[:, cols] * acc_ref[rows, :] + r[0:hd, :]

    def build_q_aug(i, carry):
        qt_sb = qt_ref[:, pl.ds(pl.multiple_of(i * sb, sb), sb)]
        own = 2 * i + jnp.where(q_local >= MOBA_BLOCK, 1, 0)
        valid = blk_row < own
        for a in range(2):
            qa = jnp.where(head_of_row == a, qt_sb, jnp.zeros_like(qt_sb))
            gate = _dot(km_hi, qa) + _dot(km_lo, qa)
            g = jnp.where(valid, gate, -jnp.inf)
            sel = jnp.zeros((nbr, sb), jnp.bool_)
            for _ in range(min(MOBA_TOPK, nb - 1)):
                cand = inrange & jnp.logical_not(sel)
                gm = jnp.max(jnp.where(cand, g, -jnp.inf), axis=0, keepdims=True)
                hit = cand & (g == gm)
                first = jnp.min(jnp.where(hit, blk_row, nb), axis=0, keepdims=True)
                sel = sel | (blk_row == first)
            chosen = (sel & valid) | (blk_row == own)
            flags = jnp.where(chosen, 0.0, NEG_BIG).astype(BF16)
            qaug_ref[i, :, a * sb:(a + 1) * sb] = jnp.concatenate([qa, flags, flag_pad], axis=0)
        return carry

    lax.fori_loop(0, nsb, build_q_aug, 0)
    scores_into(sc_ref, 0, 0)

    def q_super(i, carry):
        m_ref[...] = jnp.full(m_ref.shape, NEG_BIG, F32)
        l_ref[...] = jnp.zeros_like(l_ref)
        acc_ref[...] = jnp.zeros_like(acc_ref)
        nxt = jnp.minimum(i + 1, nsb - 1)

        @pl.when(i == 0)
        def _():
            absorb(0, sc_ref, True)

        @pl.when(i == 0)
        def _():
            scores_into(sc_ref, nxt, 0)

        @pl.when(i > 0)
        def _():
            scores_into(sa_ref, i, 1)
            absorb(0, sc_ref, False)

            def two_steps(t, c):
                scores_into(sb_ref, i, 2 * t + 2)
                absorb(2 * t + 1, sa_ref, False)
                scores_into(sa_ref, i, 2 * t + 3)
                absorb(2 * t + 2, sb_ref, False)
                return c

            lax.fori_loop(0, (i - 1) // 2, two_steps, 0)

            @pl.when(i % 2 == 1)
            def _():
                scores_into(sc_ref, nxt, 0)
                absorb(i, sa_ref, True)

            @pl.when(i % 2 == 0)
            def _():
                scores_into(sb_ref, i, i)
                absorb(i - 1, sa_ref, False)
                scores_into(sc_ref, nxt, 0)
                absorb(i, sb_ref, True)

        out_t = jnp.concatenate([acc_ref[a * hd:(a + 1) * hd, :] / l_ref[a:a + 1, :] for a in range(2)], axis=0)
        o_ref[pl.ds(pl.multiple_of(i * sb, sb), sb), :] = out_t.T.astype(BF16)
        return carry

    lax.fori_loop(0, nsb, q_super, 0)


def _moba(qt, k, vt, batch):
    t_tokens = k.shape[0]
    seq = t_tokens // batch
    assert seq % MOBA_SUPER == 0 and seq // MOBA_BLOCK <= MOBA_BLOCK_ROWS
    spec = pl.BlockSpec((seq, LANES), lambda b, hp: (b, hp))
    spec_t = pl.BlockSpec((LANES, seq), lambda b, hp: (hp, b))
    return pl.pallas_call(
        functools.partial(_moba_kernel, seq=seq),
        grid=(batch, ATT_INNER // LANES),
        in_specs=[spec_t, spec, spec_t],
        out_specs=spec,
        out_shape=jax.ShapeDtypeStruct((t_tokens, ATT_INNER), BF16),
        scratch_shapes=[pltpu.VMEM((seq, 2 * LANES), BF16),
                        pltpu.VMEM((seq // MOBA_SUPER, 2 * LANES, 2 * MOBA_SUPER), BF16),
                        pltpu.VMEM((MOBA_SUPER, 2 * MOBA_SUPER), F32),
                        pltpu.VMEM((MOBA_SUPER, 2 * MOBA_SUPER), F32),
                        pltpu.VMEM((MOBA_SUPER, 2 * MOBA_SUPER), F32),
                        pltpu.VMEM((1, 2 * MOBA_SUPER), F32),
                        pltpu.VMEM((SUBLANES, MOBA_SUPER), F32),
                        pltpu.VMEM((LANES, MOBA_SUPER), F32)],
        compiler_params=pltpu.CompilerParams(dimension_semantics=("arbitrary", "arbitrary"),
                                             vmem_limit_bytes=VMEM_LIMIT),
        name="moba",
    )(qt, k, vt)


ROUTER_ROWS = 40


def _merge_kernel(x_ref, yn_ref, att_ref, gs_ref, ga_ref, wso_ref, wao_ref, wo_ref, nw_ref,
                  wr_hi_ref, wr_lo_ref, br_ref, upper_ref, x1_ref, h2_ref, route_ref, jloc_ref, cnt_ref):
    y_s = _dot(yn_ref[...], wso_ref[...])
    y_a = _dot(att_ref[...], wao_ref[...])
    merged = _sigmoid(gs_ref[...].astype(F32)) * y_s + _sigmoid(ga_ref[...].astype(F32)) * y_a
    x1 = x_ref[...] + _dot(merged.astype(BF16), wo_ref[...])
    x1_ref[...] = x1
    h2 = _rms(x1, nw_ref[...])
    h2_ref[...] = h2.astype(BF16)

    h_hi = h2.astype(BF16)
    h_lo = (h2 - h_hi.astype(F32)).astype(BF16)
    wr_hi = wr_hi_ref[...]
    logits = _dot_nt(wr_hi, h_hi) + _dot_nt(wr_hi, h_lo) + _dot_nt(wr_lo_ref[...], h_hi) + br_ref[...]
    gl = logits[0:N_GROUPS, :]
    gidx = lax.broadcasted_iota(jnp.int32, gl.shape, 0)
    gmax = jnp.max(gl, axis=0, keepdims=True)
    gi = jnp.min(jnp.where(gl == gmax, gidx, N_GROUPS), axis=0, keepdims=True)
    g_w = 1.0 / jnp.sum(jnp.exp(gl - gmax), axis=0, keepdims=True)
    el = jnp.zeros((EXPERTS_PER_GROUP, gl.shape[1]), F32)
    for g in range(N_GROUPS):
        lo = N_GROUPS + g * EXPERTS_PER_GROUP
        el = el + jnp.where(gi == g, logits[lo:lo + EXPERTS_PER_GROUP, :], 0.0)
    eidx = lax.broadcasted_iota(jnp.int32, el.shape, 0)
    v0 = jnp.max(el, axis=0, keepdims=True)
    i0 = jnp.min(jnp.where(el == v0, eidx, EXPERTS_PER_GROUP), axis=0, keepdims=True)
    rest = jnp.where(eidx == i0, -jnp.inf, el)
    v1 = jnp.max(rest, axis=0, keepdims=True)
    i1 = jnp.min(jnp.where((rest == v1) & (eidx != i0), eidx, EXPERTS_PER_GROUP), axis=0, keepdims=True)
    t = jnp.exp(v1 - v0)
    w0 = g_w / (1.0 + t)
    w1 = g_w * t / (1.0 + t)
    base = gi * EXPERTS_PER_GROUP
    route_ref[...] = jnp.concatenate(
        [(base + i0).astype(F32), (base + i1).astype(F32), w0, w1], axis=0)

    tm = gl.shape[1]
    eids = jnp.concatenate([base + i0, base + i1], axis=1)
    onehot = lax.broadcasted_iota(jnp.int32, (N_EXPERTS, 2 * tm), 0) == eids
    csum = _dot(onehot.astype(BF16), upper_ref[...])
    oh = onehot.astype(F32)
    lrank = jnp.sum(csum * oh, axis=0, keepdims=True) - 1.0
    cnt = jnp.floor((csum[:, 2 * tm - 1:2 * tm] + (RUN_ALIGN - 1)) * (1.0 / RUN_ALIGN))
    er = lax.broadcasted_iota(jnp.int32, (N_EXPERTS, N_EXPERTS), 0)
    ec = lax.broadcasted_iota(jnp.int32, (N_EXPERTS, N_EXPERTS), 1)
    before = _dot((ec < er).astype(BF16), jnp.broadcast_to(cnt, (N_EXPERTS, LANES)).astype(BF16))
    off = before[:, 0:1] * RUN_ALIGN
    jloc = jnp.sum(oh * off, axis=0, keepdims=True) + lrank
    jloc_ref[...] = jnp.concatenate([jloc[:, :tm], jloc[:, tm:]], axis=0).astype(jnp.int32)
    cnt_ref[...] = jnp.broadcast_to(cnt * RUN_ALIGN, (N_EXPERTS, LANES))


def _merge(x2d, yn, att, gs, ga, wso, wao, wo, nw, wr_hi, wr_lo, br):
    t_tokens = x2d.shape[0]
    tm = TM_TOK
    upper = jnp.triu(jnp.ones((TILE_ASG, TILE_ASG), BF16))
    row = lambda: pl.BlockSpec((tm, D_MODEL), lambda i: (i, 0))
    whole = pl.BlockSpec(memory_space=pltpu.VMEM)
    return pl.pallas_call(
        _merge_kernel,
        grid=(t_tokens // tm,),
        in_specs=[row(), row(), row(), row(), row(), whole, whole, whole, whole, whole, whole, whole, whole],
        out_specs=[row(), row(), pl.BlockSpec((4, tm), lambda i: (0, i)), pl.BlockSpec((2, tm), lambda i: (0, i)),
                   pl.BlockSpec((N_EXPERTS, LANES), lambda i: (i, 0))],
        out_shape=[jax.ShapeDtypeStruct((t_tokens, D_MODEL), F32),
                   jax.ShapeDtypeStruct((t_tokens, D_MODEL), BF16),
                   jax.ShapeDtypeStruct((4, t_tokens), F32),
                   jax.ShapeDtypeStruct((2, t_tokens), jnp.int32),
                   jax.ShapeDtypeStruct((t_tokens // tm * N_EXPERTS, LANES), F32)],
        compiler_params=pltpu.CompilerParams(dimension_semantics=("arbitrary",), vmem_limit_bytes=VMEM_LIMIT),
        name="merge",
    )(x2d, yn, att, gs, ga, wso, wao, wo, nw, wr_hi, wr_lo, br, upper)


def _rows(ref, first, n_rows):
    start = first if isinstance(first, int) else pl.multiple_of(first, RUN_ALIGN)
    return ref.at[pl.ds(start, n_rows), :]


def _for_each_piece(count, fn, bits):
    for bit in range(bits - 1, ALIGN_BIT - 1, -1):
        size = 1 << bit

        @pl.when((count & size) != 0)
        def _():
            fn((count >> (bit + 1)) << (bit + 1), size)


def _moe_max_tiles(t_tokens):
    rows = 2 * t_tokens + (t_tokens // TM_TOK) * N_EXPERTS * (RUN_ALIGN - 1)
    return -(-rows // TM_MOE) + N_EXPERTS


def _moe_plan(cnt_rows, max_tiles):
    cnt = cnt_rows[:, 0].astype(jnp.int32).reshape(-1, N_EXPERTS)
    off = jnp.cumsum(cnt, axis=1) - cnt
    counts = jnp.sum(cnt, axis=0)
    padded = ((counts + TM_MOE - 1) // TM_MOE) * TM_MOE
    ends = jnp.cumsum(padded)
    starts = ends - padded
    base = starts[None, :] + jnp.cumsum(cnt, axis=0) - cnt
    tile_start = jnp.arange(max_tiles, dtype=jnp.int32) * TM_MOE
    tile_expert = jnp.sum((tile_start[:, None] >= ends[None, :]).astype(jnp.int32), axis=1)
    tile_expert = jnp.minimum(tile_expert, N_EXPERTS - 1)
    n_tiles = (ends[-1] // TM_MOE).astype(jnp.int32).reshape(1)
    flat = lambda a: a.reshape(-1).astype(jnp.int32)
    return flat(cnt), flat(off), flat(base), starts + counts, ends, tile_expert, n_tiles


def _dispatch_kernel(cnt_ref, off_ref, base_ref, padlo_ref, padhi_ref, ntile_ref, h2_ref, jrow_ref, xs_hbm,
                     xbuf0, xbuf1, zeros_ref, sems, zsem, *, max_tiles):
    i = pl.program_id(0)
    n = pl.num_programs(0)
    xbufs = (xbuf0, xbuf1)

    def runs(tile, slot, wait):
        def per_expert(e, carry):
            k = tile * N_EXPERTS + e
            src, dst = off_ref[k], base_ref[k]

            def piece(o, size):
                cp = pltpu.make_async_copy(_rows(xbufs[slot], src + o, size), _rows(xs_hbm, dst + o, size),
                                           sems.at[slot])
                cp.wait() if wait else cp.start()
            _for_each_piece(cnt_ref[k], piece, RUN_BITS)
            return carry
        lax.fori_loop(0, N_EXPERTS, per_expert, 0)

    @pl.when(i == 0)
    def _():
        zeros_ref[...] = jnp.zeros_like(zeros_ref)
        nt = ntile_ref[0]
        for wait in (False, True):
            def pad(e, carry):
                lo = padlo_ref[e]

                def piece(o, size):
                    cp = pltpu.make_async_copy(_rows(zeros_ref, 0, size), _rows(xs_hbm, lo + o, size), zsem)
                    cp.wait() if wait else cp.start()
                _for_each_piece(padhi_ref[e] - lo, piece, PAD_BITS)
                return carry
            lax.fori_loop(0, N_EXPERTS, pad, 0)

            def spare(tile, carry):
                cp = pltpu.make_async_copy(zeros_ref, _rows(xs_hbm, tile * TM_MOE, TM_MOE), zsem)
                cp.wait() if wait else cp.start()
                return carry
            lax.fori_loop(nt, max_tiles, spare, 0)

    for slot in range(2):
        @pl.when(i % 2 == slot)
        def _():
            @pl.when(i >= 2)
            def _():
                runs(i - 2, slot, True)
            row = lax.broadcasted_iota(jnp.int32, (LOC_ROWS, TM_TOK), 0)
            j = jrow_ref[...]
            perm = ((row == j[0:1, :]) | (row == j[1:2, :])).astype(BF16)
            xbufs[slot][...] = _dot(perm, h2_ref[...]).astype(BF16)
            runs(i, slot, False)

            @pl.when(i == n - 1)
            def _():
                @pl.when(i >= 1)
                def _():
                    runs(i - 1, 1 - slot, True)
                runs(i, slot, True)


def _dispatch(plan, jloc, h2, max_tiles):
    cnt, off, base, pad_lo, pad_hi, _, n_tiles = plan
    t_tokens = h2.shape[0]
    grid_spec = pltpu.PrefetchScalarGridSpec(
        num_scalar_prefetch=6,
        grid=(t_tokens // TM_TOK,),
        in_specs=[pl.BlockSpec((TM_TOK, D_MODEL), lambda i, *_: (i, 0)),
                  pl.BlockSpec((2, TM_TOK), lambda i, *_: (0, i))],
        out_specs=pl.BlockSpec(memory_space=pl.ANY),
        scratch_shapes=[pltpu.VMEM((LOC_ROWS, D_MODEL), BF16),
                        pltpu.VMEM((LOC_ROWS, D_MODEL), BF16),
                        pltpu.VMEM((TM_MOE, D_MODEL), BF16),
                        pltpu.SemaphoreType.DMA((2,)), pltpu.SemaphoreType.DMA(())],
    )
    return pl.pallas_call(
        functools.partial(_dispatch_kernel, max_tiles=max_tiles),
        grid_spec=grid_spec,
        out_shape=jax.ShapeDtypeStruct((max_tiles * TM_MOE, D_MODEL), BF16),
        compiler_params=pltpu.CompilerParams(dimension_semantics=("arbitrary",), vmem_limit_bytes=VMEM_LIMIT),
        name="dispatch",
    )(cnt, off, base, pad_lo, pad_hi, n_tiles, h2, jloc)


def _moe_kernel(texp_ref, ntile_ref, xs_ref, wg_ref, wu_ref, wd_ref, y_ref, wgu_s, wd_s):
    t = pl.program_id(0)
    live = t < ntile_ref[0]

    @pl.when(live & ((t == 0) | (texp_ref[t] != texp_ref[jnp.maximum(t - 1, 0)])))
    def _():
        wgu_s[:, 0:D_EXPERT] = wg_ref[0].astype(BF16)
        wgu_s[:, D_EXPERT:2 * D_EXPERT] = wu_ref[0].astype(BF16)
        wd_s[...] = wd_ref[0].astype(BF16)

    @pl.when(live)
    def _():
        gu = _dot(xs_ref[...], wgu_s[...])
        hid = (_silu(gu[:, :D_EXPERT]) * gu[:, D_EXPERT:]).astype(BF16)
        y_ref[...] = _dot(hid, wd_s[...]).astype(BF16)

    @pl.when(jnp.logical_not(live))
    def _():
        y_ref[...] = jnp.zeros_like(y_ref)


def _moe(xs, tile_expert, n_tiles, w_gate, w_up, w_down, max_tiles):
    live = lambda t, nt: jnp.minimum(t, nt[0] - 1)
    tile = lambda index: pl.BlockSpec((TM_MOE, D_MODEL), index)
    expert = lambda shape: pl.BlockSpec((1,) + shape, lambda t, te, nt: (te[live(t, nt)], 0, 0))
    grid_spec = pltpu.PrefetchScalarGridSpec(
        num_scalar_prefetch=2,
        grid=(max_tiles,),
        in_specs=[tile(lambda t, te, nt: (live(t, nt), 0)),
                  expert((D_MODEL, D_EXPERT)), expert((D_MODEL, D_EXPERT)), expert((D_EXPERT, D_MODEL))],
        out_specs=tile(lambda t, te, nt: (t, 0)),
        scratch_shapes=[pltpu.VMEM((D_MODEL, 2 * D_EXPERT), BF16), pltpu.VMEM((D_EXPERT, D_MODEL), BF16)],
    )
    return pl.pallas_call(
        _moe_kernel,
        grid_spec=grid_spec,
        out_shape=jax.ShapeDtypeStruct((max_tiles * TM_MOE, D_MODEL), BF16),
        compiler_params=pltpu.CompilerParams(dimension_semantics=("arbitrary",), vmem_limit_bytes=VMEM_LIMIT),
        name="moe",
    )(tile_expert, n_tiles, xs, w_gate, w_up, w_down)


def _tail_kernel(cnt_ref, off_ref, base_ref, x1_ref, p_ref, jw_ref, y_hbm, pnw_ref, wpg_ref, wple_ref, fnw_ref,
                 o_ref, ybuf0, ybuf1, sems):
    i = pl.program_id(0)
    n = pl.num_programs(0)
    ybufs = (ybuf0, ybuf1)

    def runs(tile, slot, wait):
        def per_expert(e, carry):
            k = tile * N_EXPERTS + e
            src, dst = base_ref[k], off_ref[k]

            def piece(o, size):
                cp = pltpu.make_async_copy(_rows(y_hbm, src + o, size), _rows(ybufs[slot], dst + o, size),
                                           sems.at[slot])
                cp.wait() if wait else cp.start()
            _for_each_piece(cnt_ref[k], piece, RUN_BITS)
            return carry
        lax.fori_loop(0, N_EXPERTS, per_expert, 0)

    @pl.when(i == 0)
    def _():
        ybuf0[...] = jnp.zeros_like(ybuf0)
        ybuf1[...] = jnp.zeros_like(ybuf1)
        runs(0, 0, False)

    for slot in range(2):
        @pl.when(i % 2 == slot)
        def _():
            @pl.when(i + 1 < n)
            def _():
                runs(i + 1, 1 - slot, False)
            runs(i, slot, True)
            y_loc = ybufs[slot][...]
            jw = jw_ref[...]
            col = lax.broadcasted_iota(jnp.int32, (TM_TOK, LOC_ROWS), 1)
            pick = sum(jnp.where(col == jw[:, s:s + 1].astype(jnp.int32), jw[:, 2 + s:3 + s], 0.0) for s in range(2))
            moe = _dot(pick.astype(BF16), y_loc)
            x2 = x1_ref[...] + moe
            gate = _sigmoid(_dot(_rms(x2, pnw_ref[...]).astype(BF16), wpg_ref[...]))
            x3 = x2 + gate * _dot(p_ref[...].astype(BF16), wple_ref[...])
            o_ref[...] = _rms(x3, fnw_ref[...])


def _tail(plan, x1, p2d, y_sorted, jw, pnw, wpg, wple, fnw):
    cnt, off, base = plan[0:3]
    t_tokens = x1.shape[0]
    whole = pl.BlockSpec(memory_space=pltpu.VMEM)
    grid_spec = pltpu.PrefetchScalarGridSpec(
        num_scalar_prefetch=3,
        grid=(t_tokens // TM_TOK,),
        in_specs=[pl.BlockSpec((TM_TOK, D_MODEL), lambda i, *_: (i, 0)),
                  pl.BlockSpec((TM_TOK, PLE_DIM), lambda i, *_: (i, 0)),
                  pl.BlockSpec((TM_TOK, 4), lambda i, *_: (i, 0)),
                  pl.BlockSpec(memory_space=pl.ANY),
                  whole, whole, whole, whole],
        out_specs=pl.BlockSpec((TM_TOK, D_MODEL), lambda i, *_: (i, 0)),
        scratch_shapes=[pltpu.VMEM((LOC_ROWS, D_MODEL), BF16),
                        pltpu.VMEM((LOC_ROWS, D_MODEL), BF16),
                        pltpu.SemaphoreType.DMA((2,))],
    )
    return pl.pallas_call(
        _tail_kernel,
        grid_spec=grid_spec,
        out_shape=jax.ShapeDtypeStruct((t_tokens, D_MODEL), F32),
        compiler_params=pltpu.CompilerParams(dimension_semantics=("arbitrary",), vmem_limit_bytes=VMEM_LIMIT),
        name="tail",
    )(cnt, off, base, x1, p2d, jw, y_sorted, pnw, wpg, wple, fnw)


def _rope_tables(positions):
    half = ATT_HEAD_DIM // 2
    inv_freq = ROPE_THETA ** (-jnp.arange(half, dtype=F32) / half)
    ang = positions.astype(F32)[..., None] * inv_freq
    return jnp.cos(ang).reshape(-1, half).T, jnp.sin(ang).reshape(-1, half).T


def _layer(x2d, p2d, rope_tables, batch, attn_norm_w, w_in, conv_w, conv_b, dt_bias, a_log, d_skip, ssd_norm_w,
           w_ssd_out, w_attn_out, w_out, moe_norm_w, w_rg, b_rg, w_re, b_re, w_gate, w_up, w_down,
           ple_norm_w, w_ple, w_ple_gate, final_norm_w):
    t_tokens = x2d.shape[0]
    offs = np.cumsum((0,) + IN_SPLITS)
    cols = lambda i: w_in[:, offs[i]:offs[i + 1]]
    w_big = jnp.concatenate([cols(0), cols(1), cols(4), cols(6), cols(7)], axis=1).astype(BF16)
    w_t = jnp.concatenate([cols(3), cols(5), cols(2)], axis=1).T.astype(BF16)
    w_dt = jnp.pad(cols(2), ((0, 0), (0, LANES - SSD_HEADS))).astype(BF16)

    z, xbc, k, gs, ga, dt, qt, vt, dtt = _in_proj(x2d, attn_norm_w[None, :], *rope_tables, w_big, w_t, w_dt)

    yn = _ssd(z, xbc, dt, dtt, conv_w, conv_b, dt_bias, a_log, d_skip, ssd_norm_w, batch)

    att = _moba(qt, k, vt, batch)

    wr = jnp.concatenate([w_rg, w_re], axis=1).T
    wr = jnp.pad(wr, ((0, ROUTER_ROWS - wr.shape[0]), (0, 0)))
    wr_hi = wr.astype(BF16)
    wr_lo = (wr - wr_hi.astype(F32)).astype(BF16)
    br = jnp.pad(jnp.concatenate([b_rg, b_re]), (0, ROUTER_ROWS - N_GROUPS - N_EXPERTS))[:, None]
    x1, h2, route, jloc, cnt_rows = _merge(x2d, yn, att, gs, ga, w_ssd_out.astype(BF16), w_attn_out.astype(BF16),
                           w_out.astype(BF16), moe_norm_w[None, :], wr_hi, wr_lo, br)

    max_tiles = _moe_max_tiles(t_tokens)
    plan = _moe_plan(cnt_rows, max_tiles)
    x_sorted = _dispatch(plan, jloc, h2, max_tiles)
    y_sorted = _moe(x_sorted, plan[5], plan[6], w_gate.reshape(N_EXPERTS, D_MODEL, D_EXPERT),
                    w_up.reshape(N_EXPERTS, D_MODEL, D_EXPERT), w_down.reshape(N_EXPERTS, D_EXPERT, D_MODEL), max_tiles)
    jw = jnp.concatenate([jloc.astype(F32), route[2:4]], axis=0).T

    return _tail(plan, x1, p2d, y_sorted, jw, ple_norm_w[None, :],
                 w_ple_gate.astype(BF16), w_ple.astype(BF16), final_norm_w[None, :])


def kernel(x, p, positions, attn_norm_w, w_in, conv_w, conv_b, dt_bias, a_log, d_skip, ssd_norm_w, w_ssd_out,
           w_attn_out, w_out, moe_norm_w, w_router_group, b_router_group, w_router_expert, b_router_expert,
           w_exp_gate, w_exp_up, w_exp_down, ple_norm_w, w_ple, w_ple_gate, final_norm_w):
    batch, seq, d = x.shape
    depth = p.shape[0]
    assert depth == 1, "the final norm is fused into the last layer's tail kernel"
    rope_tables = _rope_tables(positions)
    i = 0
    out = _layer(x.reshape(batch * seq, d), p[i].reshape(batch * seq, PLE_DIM), rope_tables, batch,
                 attn_norm_w[i], w_in[i], conv_w[i], conv_b[i], dt_bias[i], a_log[i], d_skip[i], ssd_norm_w[i],
                 w_ssd_out[i], w_attn_out[i], w_out[i], moe_norm_w[i], w_router_group[i], b_router_group[i],
                 w_router_expert[i], b_router_expert[i], w_exp_gate[i], w_exp_up[i], w_exp_down[i],
                 ple_norm_w[i], w_ple[i], w_ple_gate[i], final_norm_w)
    return out.reshape(batch, seq, d)
```

```python
import functools

import jax
import jax.numpy as jnp
import numpy as np
from jax import lax
from jax.experimental import pallas as pl
from jax.experimental.pallas import tpu as pltpu

F32 = jnp.float32
BF16 = jnp.bfloat16

EPS = 1e-6
D_MODEL = 1024
SSD_HEADS = 16
SSD_HEAD_DIM = 64
SSD_INNER = SSD_HEADS * SSD_HEAD_DIM
SSD_GROUPS = 2
SSD_STATE = 64
SSD_CONV = 4
SSD_CHUNK = 128
SSD_CONV_CH = SSD_INNER + 2 * SSD_GROUPS * SSD_STATE
CONV_TAIL = 16
ATT_HEADS = 16
ATT_HEAD_DIM = 64
ATT_INNER = ATT_HEADS * ATT_HEAD_DIM
MOBA_BLOCK = 256
MOBA_TOPK = 3
ROPE_THETA = 10000.0
IN_SPLITS = (SSD_INNER, SSD_CONV_CH, SSD_HEADS, ATT_INNER, ATT_INNER, ATT_INNER, D_MODEL, D_MODEL)
N_GROUPS = 4
EXPERTS_PER_GROUP = 8
N_EXPERTS = N_GROUPS * EXPERTS_PER_GROUP
D_EXPERT = 256
PLE_DIM = 256

LANES = 128
SUBLANES = 8
NEG_BIG = -1e30
LOG2E = 1.4426950408889634
VMEM_LIMIT = 56 * 1024 * 1024

TM_PROJ = 512
TM_MOE = 512
TM_TOK = 512
TILE_ASG = 2 * TM_TOK
RUN_ALIGN = 2 * SUBLANES
LOC_ROWS = TILE_ASG + N_EXPERTS * RUN_ALIGN
ALIGN_BIT = RUN_ALIGN.bit_length() - 1
PAD_BITS = TM_MOE.bit_length() - 1
EXTRA_BITS = (LOC_ROWS - TILE_ASG).bit_length()


def _dot(a, b):
    return jnp.dot(a, b, preferred_element_type=F32)


def _dot_nt(a, b):
    return lax.dot_general(a, b, (((1,), (1,)), ((), ())), preferred_element_type=F32)


def _dot_tn(a, b):
    return lax.dot_general(a, b, (((0,), (0,)), ((), ())), preferred_element_type=F32)


def _split3(a):
    hi = a.astype(BF16)
    r1 = a - hi.astype(F32)
    mid = r1.astype(BF16)
    lo = (r1 - mid.astype(F32)).astype(BF16)
    return hi, mid, lo


def _dot_exact_rhs(sel_bf16, a):
    return _dot(jnp.concatenate([sel_bf16] * 3, axis=1), jnp.concatenate(_split3(a), axis=0))


def _dot_exact_tn(a, sel_bf16):
    return _dot_tn(jnp.concatenate(_split3(a), axis=0), jnp.concatenate([sel_bf16] * 3, axis=0))


def _dot_exact_lhs(a, sel_bf16):
    return _dot(jnp.concatenate(_split3(a), axis=1), jnp.concatenate([sel_bf16] * 3, axis=0))


def _sigmoid(x):
    return 0.5 * jnp.tanh(0.5 * x) + 0.5


def _silu(x):
    return x * _sigmoid(x)


def _softplus(x):
    return jnp.maximum(x, 0.0) + jnp.log1p(jnp.exp(-jnp.abs(x)))


def _rms(x, w):
    inv = lax.rsqrt(jnp.mean(x * x, axis=-1, keepdims=True) + EPS)
    return (x * inv) * w


def _rope_tile(t, cos, sin_signed):
    half = ATT_HEAD_DIM // 2
    lane = lax.broadcasted_iota(jnp.int32, t.shape, 1)
    first = (lane % ATT_HEAD_DIM) < half
    partner = jnp.where(first, pltpu.roll(t, LANES - half, 1), pltpu.roll(t, half, 1))
    return t * cos + partner * sin_signed


Q_SCALE = ATT_HEAD_DIM ** -0.5 * LOG2E


def _in_proj_kernel(x_ref, nw_ref, cost_ref, sint_ref, w_ref, wt_ref, wdt_ref,
                    z_ref, xbc_ref, k_ref, gs_ref, ga_ref, dt_ref, qt_ref, vt_ref, dtt_ref):
    h = _rms(x_ref[...], nw_ref[...]).astype(BF16)
    half = ATT_HEAD_DIM // 2
    lane = lax.broadcasted_iota(jnp.int32, (half, LANES), 1)
    same = lane % half == lax.broadcasted_iota(jnp.int32, (half, LANES), 0)
    first = lane % ATT_HEAD_DIM < half
    cost = cost_ref[...]
    sint = sint_ref[...]
    cos = _dot_exact_tn(cost, same.astype(BF16))
    sin = _dot_exact_tn(sint, jnp.where(same, jnp.where(first, -1.0, 1.0), 0.0).astype(BF16))

    def proj(lo, width):
        return _dot(h, w_ref[:, lo:lo + width])

    o = 0
    z_ref[...] = proj(o, SSD_INNER).astype(BF16)
    o += SSD_INNER
    xbc_ref[...] = proj(o, SSD_CONV_CH).astype(BF16)
    o += SSD_CONV_CH
    k_lin = proj(o, ATT_INNER)
    for g in range(ATT_INNER // LANES):
        cols = slice(g * LANES, (g + 1) * LANES)
        k_ref[:, cols] = _rope_tile(k_lin[:, cols], cos, sin).astype(BF16)
    o += ATT_INNER
    for ref in (gs_ref, ga_ref):
        ref[...] = proj(o, D_MODEL).astype(BF16)
        o += D_MODEL
    dt_ref[...] = _dot(h, wdt_ref[...])[:, :SSD_HEADS]

    dtt_ref[...] = _dot_nt(wt_ref[2 * ATT_INNER:2 * ATT_INNER + SSD_HEADS, :], h)
    vt_ref[...] = _dot_nt(wt_ref[ATT_INNER:2 * ATT_INNER, :], h).astype(BF16)
    qt = _dot_nt(wt_ref[0:ATT_INNER, :], h)
    for hd in range(ATT_HEADS):
        r0 = hd * ATT_HEAD_DIM
        x1 = qt[r0:r0 + half, :]
        x2 = qt[r0 + half:r0 + ATT_HEAD_DIM, :]
        qt_ref[r0:r0 + half, :] = ((x1 * cost - x2 * sint) * Q_SCALE).astype(BF16)
        qt_ref[r0 + half:r0 + ATT_HEAD_DIM, :] = ((x2 * cost + x1 * sint) * Q_SCALE).astype(BF16)


def _in_proj(x2d, nw, cost, sint, w_big, w_t, w_dt):
    t_tokens = x2d.shape[0]
    tm = min(TM_PROJ, t_tokens)
    grid = (t_tokens // tm,)
    row = lambda width: pl.BlockSpec((tm, width), lambda i: (i, 0))
    col = lambda height: pl.BlockSpec((height, tm), lambda i: (0, i))
    whole = pl.BlockSpec(memory_space=pltpu.VMEM)
    out_shapes = [
        jax.ShapeDtypeStruct((t_tokens, SSD_INNER), BF16),
        jax.ShapeDtypeStruct((t_tokens, SSD_CONV_CH), BF16),
        jax.ShapeDtypeStruct((t_tokens, ATT_INNER), BF16),
        jax.ShapeDtypeStruct((t_tokens, D_MODEL), BF16),
        jax.ShapeDtypeStruct((t_tokens, D_MODEL), BF16),
        jax.ShapeDtypeStruct((t_tokens, SSD_HEADS), F32),
        jax.ShapeDtypeStruct((ATT_INNER, t_tokens), BF16),
        jax.ShapeDtypeStruct((ATT_INNER, t_tokens), BF16),
        jax.ShapeDtypeStruct((SSD_HEADS, t_tokens), F32),
    ]
    out_specs = [row(SSD_INNER), row(SSD_CONV_CH), row(ATT_INNER), row(D_MODEL), row(D_MODEL), row(SSD_HEADS),
                 col(ATT_INNER), col(ATT_INNER), col(SSD_HEADS)]
    return pl.pallas_call(
        _in_proj_kernel,
        grid=grid,
        in_specs=[row(D_MODEL), whole, col(ATT_HEAD_DIM // 2), col(ATT_HEAD_DIM // 2),
                  whole, whole, whole],
        out_specs=out_specs,
        out_shape=out_shapes,
        compiler_params=pltpu.CompilerParams(dimension_semantics=("arbitrary",), vmem_limit_bytes=VMEM_LIMIT),
        name="in_proj",
    )(x2d, nw, cost, sint, w_big, w_t, w_dt)


def _ssd_kernel(z_ref, xbc_ref, xtail_ref, dt_ref, dtt_ref, cw_ref, cb_ref, dtb_ref, dtbc_ref, alog_ref, alogc_ref,
                dexp_ref, nw_ref, out_ref, state_ref):
    c = pl.program_id(1)
    cl = SSD_CHUNK
    heads_per_group = SSD_HEADS // SSD_GROUPS
    gw = heads_per_group * SSD_HEAD_DIM

    @pl.when(c == 0)
    def _():
        state_ref[...] = jnp.zeros_like(state_ref)

    xbc = xbc_ref[...]
    tail = xtail_ref[...]
    tail = jnp.where(c == 0, jnp.zeros_like(tail), tail)
    u_ext = jnp.concatenate([tail, xbc], axis=0)
    n_shift = SSD_CONV - 1
    srow = lax.broadcasted_iota(jnp.int32, (n_shift * cl, CONV_TAIL + cl), 0)
    scol = lax.broadcasted_iota(jnp.int32, (n_shift * cl, CONV_TAIL + cl), 1)
    shift_sel = (scol == CONV_TAIL + srow % cl - (n_shift - srow // cl)).astype(BF16)
    shifted = _dot(shift_sel, u_ext)
    acc = cb_ref[...] + cw_ref[n_shift:SSD_CONV, :] * xbc.astype(F32)
    for k in range(n_shift):
        acc = acc + cw_ref[k:k + 1, :] * shifted[k * cl:(k + 1) * cl, :]
    xc = _silu(acc)
    xs = xc[:, :SSD_INNER]
    bm = xc[:, SSD_INNER:SSD_INNER + SSD_GROUPS * SSD_STATE].astype(BF16)
    cm = xc[:, SSD_INNER + SSD_GROUPS * SSD_STATE:].astype(BF16)

    dt = _softplus(dt_ref[...] + dtb_ref[...])
    da = dt * (-jnp.exp(alog_ref[...]))
    dtt = _softplus(dtt_ref[...] + dtbc_ref[...])
    dat = dtt * (-jnp.exp(alogc_ref[...]))
    ri = lax.broadcasted_iota(jnp.int32, (cl, cl), 0)
    ci = lax.broadcasted_iota(jnp.int32, (cl, cl), 1)
    causal = ri >= ci
    tril = causal.astype(BF16)
    triu = (ri <= ci).astype(BF16)
    acum = _dot_exact_rhs(tril, da)
    acumt = _dot_exact_lhs(dat, triu)

    hrow = lax.broadcasted_iota(jnp.int32, (SSD_HEADS, SSD_INNER), 0)
    hcol = lax.broadcasted_iota(jnp.int32, (SSD_HEADS, SSD_INNER), 1) // SSD_HEAD_DIM
    expand = (hrow == hcol).astype(BF16)

    a_last = acum[cl - 1:cl, :]
    dt_e = _dot_exact_lhs(dt, expand)
    dec_e = _dot_exact_lhs(jnp.exp(a_last - acum), expand)
    ea_e = _dot_exact_lhs(jnp.exp(acum), expand)
    xdt = xs * dt_e
    xdt_b = xdt.astype(BF16)
    xdtdec_b = (xdt * dec_e).astype(BF16)
    chunk_decay_t = jnp.exp(acumt[:, cl - 1:cl])

    y_groups = []
    for g in range(SSD_GROUPS):
        b_g = bm[:, g * SSD_STATE:(g + 1) * SSD_STATE]
        c_g = cm[:, g * SSD_STATE:(g + 1) * SSD_STATE]
        prev = state_ref[g * gw:(g + 1) * gw, :]
        cb = _dot_nt(c_g, b_g)
        y_off = _dot_nt(c_g, prev.astype(BF16)) * ea_e[:, g * gw:(g + 1) * gw]
        new_states = _dot_tn(xdtdec_b[:, g * gw:(g + 1) * gw], b_g)
        y_heads = []
        decay_rows = []
        for e in range(heads_per_group):
            hd = g * heads_per_group + e
            lmat = jnp.where(causal, jnp.exp(acum[:, hd:hd + 1] - acumt[hd:hd + 1, :]), 0.0)
            m = (cb * lmat).astype(BF16)
            y_heads.append(_dot(m, xdt_b[:, hd * SSD_HEAD_DIM:(hd + 1) * SSD_HEAD_DIM]))
            decay_rows.append(jnp.broadcast_to(chunk_decay_t[hd:hd + 1, :], (SSD_HEAD_DIM, SSD_STATE)))
        y_groups.append(jnp.concatenate(y_heads, axis=1) + y_off)
        state_ref[g * gw:(g + 1) * gw, :] = prev * jnp.concatenate(decay_rows, axis=0) + new_states

    y = jnp.concatenate(y_groups, axis=1) + xs * dexp_ref[...]
    yg = y * _silu(z_ref[...].astype(F32))
    nw = nw_ref[...]
    outs = [_rms(yg[:, g * gw:(g + 1) * gw], nw[:, g * gw:(g + 1) * gw]) for g in range(SSD_GROUPS)]
    out_ref[...] = jnp.concatenate(outs, axis=1).astype(BF16)


def _ssd(z, xbc, dt, dtt, conv_w, conv_b, dt_bias, a_log, d_skip, norm_w, batch):
    t_tokens = z.shape[0]
    nc = t_tokens // batch // SSD_CHUNK
    tok = lambda width: pl.BlockSpec((SSD_CHUNK, width), lambda b, c: (b * nc + c, 0))
    whole = pl.BlockSpec(memory_space=pltpu.VMEM)
    dexp = jnp.repeat(d_skip, SSD_HEAD_DIM)[None, :]
    tails_per_chunk = SSD_CHUNK // CONV_TAIL
    prev_tail = pl.BlockSpec((CONV_TAIL, SSD_CONV_CH),
                             lambda b, c: (jnp.maximum((b * nc + c) * tails_per_chunk - 1, 0), 0))
    return pl.pallas_call(
        _ssd_kernel,
        grid=(batch, nc),
        in_specs=[tok(SSD_INNER), tok(SSD_CONV_CH), prev_tail, tok(SSD_HEADS),
                  pl.BlockSpec((SSD_HEADS, SSD_CHUNK), lambda b, c: (0, b * nc + c)),
                  whole, whole, whole, whole, whole, whole, whole, whole],
        out_specs=tok(SSD_INNER),
        out_shape=jax.ShapeDtypeStruct((t_tokens, SSD_INNER), BF16),
        scratch_shapes=[pltpu.VMEM((SSD_INNER, SSD_STATE), F32)],
        compiler_params=pltpu.CompilerParams(dimension_semantics=("arbitrary", "arbitrary"),
                                             vmem_limit_bytes=VMEM_LIMIT),
        name="ssd",
    )(z, xbc, xbc, dt, dtt, conv_w, conv_b[None, :], dt_bias[None, :], dt_bias[:, None],
      a_log[None, :], a_log[:, None], dexp, norm_w[None, :])


MOBA_SUPER = 2 * MOBA_BLOCK


MOBA_BLOCK_ROWS = 16
ONES_ROWS = 16


def _moba_kernel(qt_ref, k_ref, vt_ref, o_ref, kfull_ref, qaug_ref, sa_ref, sb_ref, sc_ref, m_ref, l_ref, acc_ref,
                 *, seq):
    nb = seq // MOBA_BLOCK
    sb = MOBA_SUPER
    nsb = seq // sb
    hd = ATT_HEAD_DIM
    nbr = MOBA_BLOCK_ROWS

    for c in range(nsb):
        rows = slice(c * sb, (c + 1) * sb)
        kfull_ref[rows, 0:LANES] = k_ref[rows, :]
        rblk = lax.broadcasted_iota(jnp.int32, (sb, LANES), 0) // MOBA_BLOCK + c * (sb // MOBA_BLOCK)
        rlane = lax.broadcasted_iota(jnp.int32, (sb, LANES), 1)
        kfull_ref[rows, LANES:2 * LANES] = jnp.where(rblk == rlane, 1.0, 0.0).astype(BF16)

    pr = lax.broadcasted_iota(jnp.int32, (nbr, seq), 0)
    pc = lax.broadcasted_iota(jnp.int32, (nbr, seq), 1) // MOBA_BLOCK
    pool = jnp.where(pr == pc, 1.0 / MOBA_BLOCK, 0.0).astype(BF16)
    kmean = _dot(pool, k_ref[...])
    km_hi = kmean.astype(BF16)
    km_lo = (kmean - km_hi.astype(F32)).astype(BF16)

    head_of_row = lax.broadcasted_iota(jnp.int32, (LANES, sb), 0) // hd
    blk_row = lax.broadcasted_iota(jnp.int32, (nbr, sb), 0)
    q_local = lax.broadcasted_iota(jnp.int32, (nbr, sb), 1)
    inrange = blk_row < nb
    krow = lax.broadcasted_iota(jnp.int32, (sb, 2 * sb), 0)
    qcol = lax.broadcasted_iota(jnp.int32, (sb, 2 * sb), 1) % sb
    causal = krow <= qcol
    ones_rows = jnp.ones((ONES_ROWS, sb), BF16)
    flag_pad = jnp.zeros((LANES - nbr, sb), BF16)

    def scores_into(s_ref, q_sb, kv):
        start = pl.multiple_of(kv * sb, sb)
        s_ref[...] = _dot(kfull_ref[pl.ds(start, sb), :], qaug_ref[q_sb])

    def absorb(kv, s_ref, diagonal):
        start = pl.multiple_of(kv * sb, sb)
        s_t = s_ref[...]
        if diagonal:
            s_t = jnp.where(causal, s_t, NEG_BIG)
        m_prev = m_ref[...]
        m_new = jnp.maximum(m_prev, jnp.max(s_t, axis=0, keepdims=True))
        m_ref[...] = m_new
        p_b = jnp.exp2(s_t - m_new).astype(BF16)
        alpha = jnp.exp2(m_prev - m_new)
        for a in range(2):
            cols = slice(a * sb, (a + 1) * sb)
            rows = slice(a * hd, (a + 1) * hd)
            v_aug = jnp.concatenate([vt_ref[rows, pl.ds(start, sb)], ones_rows], axis=0)
            r = _dot(v_aug, p_b[:, cols])
            l_ref[a:a + 1, :] = alpha[:, cols] * l_ref[a:a + 1, :] + r[hd:hd + 1, :]
            acc_ref[rows, :] = alpha[:, cols] * acc_ref[rows, :] + r[0:hd, :]

    def build_q_aug(i, carry):
        qt_sb = qt_ref[:, pl.ds(pl.multiple_of(i * sb, sb), sb)]
        own = 2 * i + jnp.where(q_local >= MOBA_BLOCK, 1, 0)
        valid = blk_row < own
        for a in range(2):
            qa = jnp.where(head_of_row == a, qt_sb, jnp.zeros_like(qt_sb))
            gate = _dot(km_hi, qa) + _dot(km_lo, qa)
            g = jnp.where(valid, gate, -jnp.inf)
            sel = jnp.zeros((nbr, sb), jnp.bool_)
            for _ in range(min(MOBA_TOPK, nb - 1)):
                cand = inrange & jnp.logical_not(sel)
                gm = jnp.max(jnp.where(cand, g, -jnp.inf), axis=0, keepdims=True)
                hit = cand & (g == gm)
                first = jnp.min(jnp.where(hit, blk_row, nb), axis=0, keepdims=True)
                sel = sel | (blk_row == first)
            chosen = (sel & valid) | (blk_row == own)
            flags = jnp.where(chosen, 0.0, NEG_BIG).astype(BF16)
            qaug_ref[i, :, a * sb:(a + 1) * sb] = jnp.concatenate([qa, flags, flag_pad], axis=0)
        return carry

    lax.fori_loop(0, nsb, build_q_aug, 0)
    scores_into(sc_ref, 0, 0)

    def q_super(i, carry):
        m_ref[...] = jnp.full(m_ref.shape, NEG_BIG, F32)
        l_ref[...] = jnp.zeros_like(l_ref)
        acc_ref[...] = jnp.zeros_like(acc_ref)
        nxt = jnp.minimum(i + 1, nsb - 1)

        @pl.when(i == 0)
        def _():
            absorb(0, sc_ref, True)

        @pl.when(i == 0)
        def _():
            scores_into(sc_ref, nxt, 0)

        @pl.when(i > 0)
        def _():
            scores_into(sa_ref, i, 1)
            absorb(0, sc_ref, False)

            def two_steps(t, c):
                scores_into(sb_ref, i, 2 * t + 2)
                absorb(2 * t + 1, sa_ref, False)
                scores_into(sa_ref, i, 2 * t + 3)
                absorb(2 * t + 2, sb_ref, False)
                return c

            lax.fori_loop(0, (i - 1) // 2, two_steps, 0)

            @pl.when(i % 2 == 1)
            def _():
                scores_into(sc_ref, nxt, 0)
                absorb(i, sa_ref, True)

            @pl.when(i % 2 == 0)
            def _():
                scores_into(sb_ref, i, i)
                absorb(i - 1, sa_ref, False)
                scores_into(sc_ref, nxt, 0)
                absorb(i, sb_ref, True)

        out_t = jnp.concatenate([acc_ref[a * hd:(a + 1) * hd, :] / l_ref[a:a + 1, :] for a in range(2)], axis=0)
        o_ref[pl.ds(pl.multiple_of(i * sb, sb), sb), :] = out_t.T.astype(BF16)
        return carry

    lax.fori_loop(0, nsb, q_super, 0)


def _moba(qt, k, vt, batch):
    t_tokens = k.shape[0]
    seq = t_tokens // batch
    assert seq % MOBA_SUPER == 0 and seq // MOBA_BLOCK <= MOBA_BLOCK_ROWS
    spec = pl.BlockSpec((seq, LANES), lambda b, hp: (b, hp))
    spec_t = pl.BlockSpec((LANES, seq), lambda b, hp: (hp, b))
    return pl.pallas_call(
        functools.partial(_moba_kernel, seq=seq),
        grid=(batch, ATT_INNER // LANES),
        in_specs=[spec_t, spec, spec_t],
        out_specs=spec,
        out_shape=jax.ShapeDtypeStruct((t_tokens, ATT_INNER), BF16),
        scratch_shapes=[pltpu.VMEM((seq, 2 * LANES), BF16),
                        pltpu.VMEM((seq // MOBA_SUPER, 2 * LANES, 2 * MOBA_SUPER), BF16),
                        pltpu.VMEM((MOBA_SUPER, 2 * MOBA_SUPER), F32),
                        pltpu.VMEM((MOBA_SUPER, 2 * MOBA_SUPER), F32),
                        pltpu.VMEM((MOBA_SUPER, 2 * MOBA_SUPER), F32),
                        pltpu.VMEM((1, 2 * MOBA_SUPER), F32),
                        pltpu.VMEM((SUBLANES, MOBA_SUPER), F32),
                        pltpu.VMEM((LANES, MOBA_SUPER), F32)],
        compiler_params=pltpu.CompilerParams(dimension_semantics=("arbitrary", "arbitrary"),
                                             vmem_limit_bytes=VMEM_LIMIT),
        name="moba",
    )(qt, k, vt)


ROUTER_ROWS = 40


def _merge_kernel(x_ref, yn_ref, att_ref, gs_ref, ga_ref, wso_ref, wao_ref, wo_ref, nw_ref,
                  wr_hi_ref, wr_lo_ref, br_ref, upper_ref, x1_ref, h2_ref, route_ref, jloc_ref, cnt_ref):
    y_s = _dot(yn_ref[...], wso_ref[...])
    y_a = _dot(att_ref[...], wao_ref[...])
    merged = _sigmoid(gs_ref[...].astype(F32)) * y_s + _sigmoid(ga_ref[...].astype(F32)) * y_a
    x1 = x_ref[...] + _dot(merged.astype(BF16), wo_ref[...])
    x1_ref[...] = x1
    h2 = _rms(x1, nw_ref[...])
    h2_ref[...] = h2.astype(BF16)

    h_hi = h2.astype(BF16)
    h_lo = (h2 - h_hi.astype(F32)).astype(BF16)
    wr_hi = wr_hi_ref[...]
    logits = _dot_nt(wr_hi, h_hi) + _dot_nt(wr_hi, h_lo) + _dot_nt(wr_lo_ref[...], h_hi) + br_ref[...]
    gl = logits[0:N_GROUPS, :]
    gidx = lax.broadcasted_iota(jnp.int32, gl.shape, 0)
    gmax = jnp.max(gl, axis=0, keepdims=True)
    gi = jnp.min(jnp.where(gl == gmax, gidx, N_GROUPS), axis=0, keepdims=True)
    g_w = 1.0 / jnp.sum(jnp.exp(gl - gmax), axis=0, keepdims=True)
    el = jnp.zeros((EXPERTS_PER_GROUP, gl.shape[1]), F32)
    for g in range(N_GROUPS):
        lo = N_GROUPS + g * EXPERTS_PER_GROUP
        el = el + jnp.where(gi == g, logits[lo:lo + EXPERTS_PER_GROUP, :], 0.0)
    eidx = lax.broadcasted_iota(jnp.int32, el.shape, 0)
    v0 = jnp.max(el, axis=0, keepdims=True)
    i0 = jnp.min(jnp.where(el == v0, eidx, EXPERTS_PER_GROUP), axis=0, keepdims=True)
    rest = jnp.where(eidx == i0, -jnp.inf, el)
    v1 = jnp.max(rest, axis=0, keepdims=True)
    i1 = jnp.min(jnp.where((rest == v1) & (eidx != i0), eidx, EXPERTS_PER_GROUP), axis=0, keepdims=True)
    t = jnp.exp(v1 - v0)
    w0 = g_w / (1.0 + t)
    w1 = g_w * t / (1.0 + t)
    base = gi * EXPERTS_PER_GROUP
    route_ref[...] = jnp.concatenate(
        [(base + i0).astype(F32), (base + i1).astype(F32), w0, w1], axis=0)

    tm = gl.shape[1]
    eids = jnp.concatenate([base + i0, base + i1], axis=1)
    onehot = lax.broadcasted_iota(jnp.int32, (N_EXPERTS, 2 * tm), 0) == eids
    csum = _dot(onehot.astype(BF16), upper_ref[...])
    oh = onehot.astype(F32)
    lrank = jnp.sum(csum * oh, axis=0, keepdims=True) - 1.0
    cnt = jnp.floor((csum[:, 2 * tm - 1:2 * tm] + (RUN_ALIGN - 1)) * (1.0 / RUN_ALIGN))
    er = lax.broadcasted_iota(jnp.int32, (N_EXPERTS, N_EXPERTS), 0)
    ec = lax.broadcasted_iota(jnp.int32, (N_EXPERTS, N_EXPERTS), 1)
    before = _dot((ec < er).astype(BF16), jnp.broadcast_to(cnt, (N_EXPERTS, LANES)).astype(BF16))
    off = before[:, 0:1] * RUN_ALIGN
    jloc = jnp.sum(oh * off, axis=0, keepdims=True) + lrank
    jloc_ref[...] = jnp.concatenate([jloc[:, :tm], jloc[:, tm:]], axis=0).astype(jnp.int32)
    cnt_ref[...] = jnp.broadcast_to(cnt * RUN_ALIGN, (N_EXPERTS, LANES))


def _merge(x2d, yn, att, gs, ga, wso, wao, wo, nw, wr_hi, wr_lo, br):
    t_tokens = x2d.shape[0]
    tm = TM_TOK
    upper = jnp.triu(jnp.ones((TILE_ASG, TILE_ASG), BF16))
    row = lambda: pl.BlockSpec((tm, D_MODEL), lambda i: (i, 0))
    whole = pl.BlockSpec(memory_space=pltpu.VMEM)
    return pl.pallas_call(
        _merge_kernel,
        grid=(t_tokens // tm,),
        in_specs=[row(), row(), row(), row(), row(), whole, whole, whole, whole, whole, whole, whole, whole],
        out_specs=[row(), row(), pl.BlockSpec((4, tm), lambda i: (0, i)), pl.BlockSpec((2, tm), lambda i: (0, i)),
                   pl.BlockSpec((N_EXPERTS, LANES), lambda i: (i, 0))],
        out_shape=[jax.ShapeDtypeStruct((t_tokens, D_MODEL), F32),
                   jax.ShapeDtypeStruct((t_tokens, D_MODEL), BF16),
                   jax.ShapeDtypeStruct((4, t_tokens), F32),
                   jax.ShapeDtypeStruct((2, t_tokens), jnp.int32),
                   jax.ShapeDtypeStruct((t_tokens // tm * N_EXPERTS, LANES), F32)],
        compiler_params=pltpu.CompilerParams(dimension_semantics=("arbitrary",), vmem_limit_bytes=VMEM_LIMIT),
        name="merge",
    )(x2d, yn, att, gs, ga, wso, wao, wo, nw, wr_hi, wr_lo, br, upper)


def _rows(ref, first, n_rows):
    start = first if isinstance(first, int) else pl.multiple_of(first, RUN_ALIGN)
    return ref.at[pl.ds(start, n_rows), :]


def _for_each_piece(count, fn, bits):
    for bit in range(bits - 1, ALIGN_BIT - 1, -1):
        size = 1 << bit

        @pl.when((count & size) != 0)
        def _():
            fn((count >> (bit + 1)) << (bit + 1), size)


def _run_copies(cnt_ref, src_ref, dst_ref, tile, src, dst, sem):
    def per_expert(e, carry):
        k = tile * N_EXPERTS + e
        s0, d0 = src_ref[k], dst_ref[k]

        def unit(u, c):
            pltpu.make_async_copy(_rows(src, s0 + u * RUN_ALIGN, RUN_ALIGN), _rows(dst, d0 + u * RUN_ALIGN, RUN_ALIGN),
                                  sem).start()
            return c
        lax.fori_loop(0, cnt_ref[k] >> ALIGN_BIT, unit, 0)
        return carry
    lax.fori_loop(0, N_EXPERTS, per_expert, 0)


def _wait_run_copies(tot_ref, tile, src, dst, sem):
    pltpu.make_async_copy(_rows(src, 0, TILE_ASG), _rows(dst, 0, TILE_ASG), sem).wait()

    def piece(o, size):
        pltpu.make_async_copy(_rows(src, 0, size), _rows(dst, 0, size), sem).wait()
    _for_each_piece(tot_ref[tile] - TILE_ASG, piece, EXTRA_BITS)


def _moe_max_tiles(t_tokens):
    rows = 2 * t_tokens + (t_tokens // TM_TOK) * N_EXPERTS * (RUN_ALIGN - 1)
    return -(-rows // TM_MOE) + N_EXPERTS


def _moe_plan(cnt_rows, max_tiles):
    cnt = cnt_rows[:, 0].astype(jnp.int32).reshape(-1, N_EXPERTS)
    off = jnp.cumsum(cnt, axis=1) - cnt
    counts = jnp.sum(cnt, axis=0)
    padded = ((counts + TM_MOE - 1) // TM_MOE) * TM_MOE
    ends = jnp.cumsum(padded)
    starts = ends - padded
    base = starts[None, :] + jnp.cumsum(cnt, axis=0) - cnt
    tile_start = jnp.arange(max_tiles, dtype=jnp.int32) * TM_MOE
    tile_expert = jnp.sum((tile_start[:, None] >= ends[None, :]).astype(jnp.int32), axis=1)
    tile_expert = jnp.minimum(tile_expert, N_EXPERTS - 1)
    n_tiles = (ends[-1] // TM_MOE).astype(jnp.int32).reshape(1)
    flat = lambda a: a.reshape(-1).astype(jnp.int32)
    return flat(cnt), flat(off), flat(base), flat(jnp.sum(cnt, axis=1)), starts + counts, ends, tile_expert, n_tiles


def _dispatch_kernel(cnt_ref, off_ref, base_ref, tot_ref, padlo_ref, padhi_ref, ntile_ref, h2_ref, jrow_ref, xs_hbm,
                     xbuf0, xbuf1, zeros_ref, sems, zsem, *, max_tiles):
    i = pl.program_id(0)
    n = pl.num_programs(0)
    xbufs = (xbuf0, xbuf1)

    def wait_runs(tile, slot):
        _wait_run_copies(tot_ref, tile, xbufs[slot], xs_hbm, sems.at[slot])

    @pl.when(i == 0)
    def _():
        zeros_ref[...] = jnp.zeros_like(zeros_ref)
        nt = ntile_ref[0]
        for wait in (False, True):
            def pad(e, carry):
                lo = padlo_ref[e]

                def piece(o, size):
                    cp = pltpu.make_async_copy(_rows(zeros_ref, 0, size), _rows(xs_hbm, lo + o, size), zsem)
                    cp.wait() if wait else cp.start()
                _for_each_piece(padhi_ref[e] - lo, piece, PAD_BITS)
                return carry
            lax.fori_loop(0, N_EXPERTS, pad, 0)

            def spare(tile, carry):
                cp = pltpu.make_async_copy(zeros_ref, _rows(xs_hbm, tile * TM_MOE, TM_MOE), zsem)
                cp.wait() if wait else cp.start()
                return carry
            lax.fori_loop(nt, max_tiles, spare, 0)

    for slot in range(2):
        @pl.when(i % 2 == slot)
        def _():
            @pl.when(i >= 2)
            def _():
                wait_runs(i - 2, slot)
            row = lax.broadcasted_iota(jnp.int32, (LOC_ROWS, TM_TOK), 0)
            j = jrow_ref[...]
            perm = ((row == j[0:1, :]) | (row == j[1:2, :])).astype(BF16)
            xbufs[slot][...] = _dot(perm, h2_ref[...]).astype(BF16)
            _run_copies(cnt_ref, off_ref, base_ref, i, xbufs[slot], xs_hbm, sems.at[slot])

            @pl.when(i == n - 1)
            def _():
                @pl.when(i >= 1)
                def _():
                    wait_runs(i - 1, 1 - slot)
                wait_runs(i, slot)


def _dispatch(plan, jloc, h2, max_tiles):
    cnt, off, base, tot, pad_lo, pad_hi, _, n_tiles = plan
    t_tokens = h2.shape[0]
    grid_spec = pltpu.PrefetchScalarGridSpec(
        num_scalar_prefetch=7,
        grid=(t_tokens // TM_TOK,),
        in_specs=[pl.BlockSpec((TM_TOK, D_MODEL), lambda i, *_: (i, 0)),
                  pl.BlockSpec((2, TM_TOK), lambda i, *_: (0, i))],
        out_specs=pl.BlockSpec(memory_space=pl.ANY),
        scratch_shapes=[pltpu.VMEM((LOC_ROWS, D_MODEL), BF16),
                        pltpu.VMEM((LOC_ROWS, D_MODEL), BF16),
                        pltpu.VMEM((TM_MOE, D_MODEL), BF16),
                        pltpu.SemaphoreType.DMA((2,)), pltpu.SemaphoreType.DMA(())],
    )
    return pl.pallas_call(
        functools.partial(_dispatch_kernel, max_tiles=max_tiles),
        grid_spec=grid_spec,
        out_shape=jax.ShapeDtypeStruct((max_tiles * TM_MOE, D_MODEL), BF16),
        compiler_params=pltpu.CompilerParams(dimension_semantics=("arbitrary",), vmem_limit_bytes=VMEM_LIMIT),
        name="dispatch",
    )(cnt, off, base, tot, pad_lo, pad_hi, n_tiles, h2, jloc)


def _moe_kernel(texp_ref, ntile_ref, xs_ref, wg_ref, wu_ref, wd_ref, y_ref, wgu_s, wd_s):
    t = pl.program_id(0)
    live = t < ntile_ref[0]

    @pl.when(live & ((t == 0) | (texp_ref[t] != texp_ref[jnp.maximum(t - 1, 0)])))
    def _():
        wgu_s[:, 0:D_EXPERT] = wg_ref[0].astype(BF16)
        wgu_s[:, D_EXPERT:2 * D_EXPERT] = wu_ref[0].astype(BF16)
        wd_s[...] = wd_ref[0].astype(BF16)

    @pl.when(live)
    def _():
        gu = _dot(xs_ref[...], wgu_s[...])
        hid = (_silu(gu[:, :D_EXPERT]) * gu[:, D_EXPERT:]).astype(BF16)
        y_ref[...] = _dot(hid, wd_s[...]).astype(BF16)

    @pl.when(jnp.logical_not(live))
    def _():
        y_ref[...] = jnp.zeros_like(y_ref)


def _moe(xs, tile_expert, n_tiles, w_gate, w_up, w_down, max_tiles):
    live = lambda t, nt: jnp.minimum(t, nt[0] - 1)
    tile = lambda index: pl.BlockSpec((TM_MOE, D_MODEL), index)
    expert = lambda shape: pl.BlockSpec((1,) + shape, lambda t, te, nt: (te[live(t, nt)], 0, 0))
    grid_spec = pltpu.PrefetchScalarGridSpec(
        num_scalar_prefetch=2,
        grid=(max_tiles,),
        in_specs=[tile(lambda t, te, nt: (live(t, nt), 0)),
                  expert((D_MODEL, D_EXPERT)), expert((D_MODEL, D_EXPERT)), expert((D_EXPERT, D_MODEL))],
        out_specs=tile(lambda t, te, nt: (t, 0)),
        scratch_shapes=[pltpu.VMEM((D_MODEL, 2 * D_EXPERT), BF16), pltpu.VMEM((D_EXPERT, D_MODEL), BF16)],
    )
    return pl.pallas_call(
        _moe_kernel,
        grid_spec=grid_spec,
        out_shape=jax.ShapeDtypeStruct((max_tiles * TM_MOE, D_MODEL), BF16),
        compiler_params=pltpu.CompilerParams(dimension_semantics=("arbitrary",), vmem_limit_bytes=VMEM_LIMIT),
        name="moe",
    )(tile_expert, n_tiles, xs, w_gate, w_up, w_down)


def _tail_kernel(cnt_ref, off_ref, base_ref, tot_ref, x1_ref, p_ref, jw_ref, y_hbm, pnw_ref, wpg_ref, wple_ref,
                 fnw_ref, o_ref, ybuf0, ybuf1, sems):
    i = pl.program_id(0)
    n = pl.num_programs(0)
    ybufs = (ybuf0, ybuf1)

    def fetch_runs(tile, slot):
        _run_copies(cnt_ref, base_ref, off_ref, tile, y_hbm, ybufs[slot], sems.at[slot])

    @pl.when(i == 0)
    def _():
        ybuf0[...] = jnp.zeros_like(ybuf0)
        ybuf1[...] = jnp.zeros_like(ybuf1)
        fetch_runs(0, 0)

    for slot in range(2):
        @pl.when(i % 2 == slot)
        def _():
            @pl.when(i + 1 < n)
            def _():
                fetch_runs(i + 1, 1 - slot)
            _wait_run_copies(tot_ref, i, y_hbm, ybufs[slot], sems.at[slot])
            y_loc = ybufs[slot][...]
            jw = jw_ref[...]
            col = lax.broadcasted_iota(jnp.int32, (TM_TOK, LOC_ROWS), 1)
            pick = sum(jnp.where(col == jw[:, s:s + 1].astype(jnp.int32), jw[:, 2 + s:3 + s], 0.0) for s in range(2))
            moe = _dot(pick.astype(BF16), y_loc)
            x2 = x1_ref[...] + moe
            gate = _sigmoid(_dot(_rms(x2, pnw_ref[...]).astype(BF16), wpg_ref[...]))
            x3 = x2 + gate * _dot(p_ref[...].astype(BF16), wple_ref[...])
            o_ref[...] = _rms(x3, fnw_ref[...])


def _tail(plan, x1, p2d, y_sorted, jw, pnw, wpg, wple, fnw):
    cnt, off, base, tot = plan[0:4]
    t_tokens = x1.shape[0]
    whole = pl.BlockSpec(memory_space=pltpu.VMEM)
    grid_spec = pltpu.PrefetchScalarGridSpec(
        num_scalar_prefetch=4,
        grid=(t_tokens // TM_TOK,),
        in_specs=[pl.BlockSpec((TM_TOK, D_MODEL), lambda i, *_: (i, 0)),
                  pl.BlockSpec((TM_TOK, PLE_DIM), lambda i, *_: (i, 0)),
                  pl.BlockSpec((TM_TOK, 4), lambda i, *_: (i, 0)),
                  pl.BlockSpec(memory_space=pl.ANY),
                  whole, whole, whole, whole],
        out_specs=pl.BlockSpec((TM_TOK, D_MODEL), lambda i, *_: (i, 0)),
        scratch_shapes=[pltpu.VMEM((LOC_ROWS, D_MODEL), BF16),
                        pltpu.VMEM((LOC_ROWS, D_MODEL), BF16),
                        pltpu.SemaphoreType.DMA((2,))],
    )
    return pl.pallas_call(
        _tail_kernel,
        grid_spec=grid_spec,
        out_shape=jax.ShapeDtypeStruct((t_tokens, D_MODEL), F32),
        compiler_params=pltpu.CompilerParams(dimension_semantics=("arbitrary",), vmem_limit_bytes=VMEM_LIMIT),
        name="tail",
    )(cnt, off, base, tot, x1, p2d, jw, y_sorted, pnw, wpg, wple, fnw)


def _rope_tables(positions):
    half = ATT_HEAD_DIM // 2
    inv_freq = ROPE_THETA ** (-jnp.arange(half, dtype=F32) / half)
    ang = positions.astype(F32)[..., None] * inv_freq
    return jnp.cos(ang).reshape(-1, half).T, jnp.sin(ang).reshape(-1, half).T


def _layer(x2d, p2d, rope_tables, batch, attn_norm_w, w_in, conv_w, conv_b, dt_bias, a_log, d_skip, ssd_norm_w,
           w_ssd_out, w_attn_out, w_out, moe_norm_w, w_rg, b_rg, w_re, b_re, w_gate, w_up, w_down,
           ple_norm_w, w_ple, w_ple_gate, final_norm_w):
    t_tokens = x2d.shape[0]
    offs = np.cumsum((0,) + IN_SPLITS)
    cols = lambda i: w_in[:, offs[i]:offs[i + 1]]
    w_big = jnp.concatenate([cols(0), cols(1), cols(4), cols(6), cols(7)], axis=1).astype(BF16)
    w_t = jnp.concatenate([cols(3), cols(5), cols(2)], axis=1).T.astype(BF16)
    w_dt = jnp.pad(cols(2), ((0, 0), (0, LANES - SSD_HEADS))).astype(BF16)

    z, xbc, k, gs, ga, dt, qt, vt, dtt = _in_proj(x2d, attn_norm_w[None, :], *rope_tables, w_big, w_t, w_dt)

    yn = _ssd(z, xbc, dt, dtt, conv_w, conv_b, dt_bias, a_log, d_skip, ssd_norm_w, batch)

    att = _moba(qt, k, vt, batch)

    wr = jnp.concatenate([w_rg, w_re], axis=1).T
    wr = jnp.pad(wr, ((0, ROUTER_ROWS - wr.shape[0]), (0, 0)))
    wr_hi = wr.astype(BF16)
    wr_lo = (wr - wr_hi.astype(F32)).astype(BF16)
    br = jnp.pad(jnp.concatenate([b_rg, b_re]), (0, ROUTER_ROWS - N_GROUPS - N_EXPERTS))[:, None]
    x1, h2, route, jloc, cnt_rows = _merge(x2d, yn, att, gs, ga, w_ssd_out.astype(BF16), w_attn_out.astype(BF16),
                           w_out.astype(BF16), moe_norm_w[None, :], wr_hi, wr_lo, br)

    max_tiles = _moe_max_tiles(t_tokens)
    plan = _moe_plan(cnt_rows, max_tiles)
    x_sorted = _dispatch(plan, jloc, h2, max_tiles)
    y_sorted = _moe(x_sorted, plan[6], plan[7], w_gate.reshape(N_EXPERTS, D_MODEL, D_EXPERT),
                    w_up.reshape(N_EXPERTS, D_MODEL, D_EXPERT), w_down.reshape(N_EXPERTS, D_EXPERT, D_MODEL), max_tiles)
    jw = jnp.concatenate([jloc.astype(F32), route[2:4]], axis=0).T

    return _tail(plan, x1, p2d, y_sorted, jw, ple_norm_w[None, :],
                 w_ple_gate.astype(BF16), w_ple.astype(BF16), final_norm_w[None, :])


def kernel(x, p, positions, attn_norm_w, w_in, conv_w, conv_b, dt_bias, a_log, d_skip, ssd_norm_w, w_ssd_out,
           w_attn_out, w_out, moe_norm_w, w_router_group, b_router_group, w_router_expert, b_router_expert,
           w_exp_gate, w_exp_up, w_exp_down, ple_norm_w, w_ple, w_ple_gate, final_norm_w):
    batch, seq, d = x.shape
    depth = p.shape[0]
    assert depth == 1, "the final norm is fused into the last layer's tail kernel"
    rope_tables = _rope_tables(positions)
    i = 0
    out = _layer(x.reshape(batch * seq, d), p[i].reshape(batch * seq, PLE_DIM), rope_tables, batch,
                 attn_norm_w[i], w_in[i], conv_w[i], conv_b[i], dt_bias[i], a_log[i], d_skip[i], ssd_norm_w[i],
                 w_ssd_out[i], w_attn_out[i], w_out[i], moe_norm_w[i], w_router_group[i], b_router_group[i],
                 w_router_expert[i], b_router_expert[i], w_exp_gate[i], w_exp_up[i], w_exp_down[i],
                 ple_norm_w[i], w_ple[i], w_ple_gate[i], final_norm_w)
    return out.reshape(batch, seq, d)
```

```python
import functools

import jax
import jax.numpy as jnp
import numpy as np
from jax import lax
from jax.experimental import pallas as pl
from jax.experimental.pallas import tpu as pltpu

F32 = jnp.float32
BF16 = jnp.bfloat16

EPS = 1e-6
D_MODEL = 1024
SSD_HEADS = 16
SSD_HEAD_DIM = 64
SSD_INNER = SSD_HEADS * SSD_HEAD_DIM
SSD_GROUPS = 2
SSD_STATE = 64
SSD_CONV = 4
SSD_CHUNK = 128
SSD_CONV_CH = SSD_INNER + 2 * SSD_GROUPS * SSD_STATE
CONV_TAIL = 16
ATT_HEADS = 16
ATT_HEAD_DIM = 64
ATT_INNER = ATT_HEADS * ATT_HEAD_DIM
MOBA_BLOCK = 256
MOBA_TOPK = 3
ROPE_THETA = 10000.0
IN_SPLITS = (SSD_INNER, SSD_CONV_CH, SSD_HEADS, ATT_INNER, ATT_INNER, ATT_INNER, D_MODEL, D_MODEL)
N_GROUPS = 4
EXPERTS_PER_GROUP = 8
N_EXPERTS = N_GROUPS * EXPERTS_PER_GROUP
D_EXPERT = 256
PLE_DIM = 256

LANES = 128
SUBLANES = 8
NEG_BIG = -1e30
LOG2E = 1.4426950408889634
VMEM_LIMIT = 56 * 1024 * 1024

TM_PROJ = 512
TM_MOE = 512
TM_TOK = 512
TILE_ASG = 2 * TM_TOK
RUN_ALIGN = 2 * SUBLANES
LOC_ROWS = TILE_ASG + N_EXPERTS * RUN_ALIGN
ALIGN_BIT = RUN_ALIGN.bit_length() - 1
PAD_BITS = TM_MOE.bit_length() - 1
EXTRA_BITS = (LOC_ROWS - TILE_ASG).bit_length()
UNITS = LOC_ROWS // RUN_ALIGN


def _dot(a, b):
    return jnp.dot(a, b, preferred_element_type=F32)


def _dot_nt(a, b):
    return lax.dot_general(a, b, (((1,), (1,)), ((), ())), preferred_element_type=F32)


def _dot_tn(a, b):
    return lax.dot_general(a, b, (((0,), (0,)), ((), ())), preferred_element_type=F32)


def _split3(a):
    hi = a.astype(BF16)
    r1 = a - hi.astype(F32)
    mid = r1.astype(BF16)
    lo = (r1 - mid.astype(F32)).astype(BF16)
    return hi, mid, lo


def _dot_exact_rhs(sel_bf16, a):
    return _dot(jnp.concatenate([sel_bf16] * 3, axis=1), jnp.concatenate(_split3(a), axis=0))


def _dot_exact_tn(a, sel_bf16):
    return _dot_tn(jnp.concatenate(_split3(a), axis=0), jnp.concatenate([sel_bf16] * 3, axis=0))


def _dot_exact_lhs(a, sel_bf16):
    return _dot(jnp.concatenate(_split3(a), axis=1), jnp.concatenate([sel_bf16] * 3, axis=0))


def _sigmoid(x):
    return 0.5 * jnp.tanh(0.5 * x) + 0.5


def _silu(x):
    return x * _sigmoid(x)


def _softplus(x):
    return jnp.maximum(x, 0.0) + jnp.log1p(jnp.exp(-jnp.abs(x)))


def _rms(x, w):
    inv = lax.rsqrt(jnp.mean(x * x, axis=-1, keepdims=True) + EPS)
    return (x * inv) * w


def _rope_tile(t, cos, sin_signed):
    half = ATT_HEAD_DIM // 2
    lane = lax.broadcasted_iota(jnp.int32, t.shape, 1)
    first = (lane % ATT_HEAD_DIM) < half
    partner = jnp.where(first, pltpu.roll(t, LANES - half, 1), pltpu.roll(t, half, 1))
    return t * cos + partner * sin_signed


Q_SCALE = ATT_HEAD_DIM ** -0.5 * LOG2E


def _in_proj_kernel(x_ref, nw_ref, cost_ref, sint_ref, w_ref, wt_ref, wdt_ref,
                    z_ref, xbc_ref, k_ref, gs_ref, ga_ref, dt_ref, qt_ref, vt_ref, dtt_ref):
    h = _rms(x_ref[...], nw_ref[...]).astype(BF16)
    half = ATT_HEAD_DIM // 2
    lane = lax.broadcasted_iota(jnp.int32, (half, LANES), 1)
    same = lane % half == lax.broadcasted_iota(jnp.int32, (half, LANES), 0)
    first = lane % ATT_HEAD_DIM < half
    cost = cost_ref[...]
    sint = sint_ref[...]
    cos = _dot_exact_tn(cost, same.astype(BF16))
    sin = _dot_exact_tn(sint, jnp.where(same, jnp.where(first, -1.0, 1.0), 0.0).astype(BF16))

    def proj(lo, width):
        return _dot(h, w_ref[:, lo:lo + width])

    o = 0
    z_ref[...] = proj(o, SSD_INNER).astype(BF16)
    o += SSD_INNER
    xbc_ref[...] = proj(o, SSD_CONV_CH).astype(BF16)
    o += SSD_CONV_CH
    k_lin = proj(o, ATT_INNER)
    for g in range(ATT_INNER // LANES):
        cols = slice(g * LANES, (g + 1) * LANES)
        k_ref[:, cols] = _rope_tile(k_lin[:, cols], cos, sin).astype(BF16)
    o += ATT_INNER
    for ref in (gs_ref, ga_ref):
        ref[...] = proj(o, D_MODEL).astype(BF16)
        o += D_MODEL
    dt_ref[...] = _dot(h, wdt_ref[...])[:, :SSD_HEADS]

    dtt_ref[...] = _dot_nt(wt_ref[2 * ATT_INNER:2 * ATT_INNER + SSD_HEADS, :], h)
    vt_ref[...] = _dot_nt(wt_ref[ATT_INNER:2 * ATT_INNER, :], h).astype(BF16)
    qt = _dot_nt(wt_ref[0:ATT_INNER, :], h)
    for hd in range(ATT_HEADS):
        r0 = hd * ATT_HEAD_DIM
        x1 = qt[r0:r0 + half, :]
        x2 = qt[r0 + half:r0 + ATT_HEAD_DIM, :]
        qt_ref[r0:r0 + half, :] = ((x1 * cost - x2 * sint) * Q_SCALE).astype(BF16)
        qt_ref[r0 + half:r0 + ATT_HEAD_DIM, :] = ((x2 * cost + x1 * sint) * Q_SCALE).astype(BF16)


def _in_proj(x2d, nw, cost, sint, w_big, w_t, w_dt):
    t_tokens = x2d.shape[0]
    tm = min(TM_PROJ, t_tokens)
    grid = (t_tokens // tm,)
    row = lambda width: pl.BlockSpec((tm, width), lambda i: (i, 0))
    col = lambda height: pl.BlockSpec((height, tm), lambda i: (0, i))
    whole = pl.BlockSpec(memory_space=pltpu.VMEM)
    out_shapes = [
        jax.ShapeDtypeStruct((t_tokens, SSD_INNER), BF16),
        jax.ShapeDtypeStruct((t_tokens, SSD_CONV_CH), BF16),
        jax.ShapeDtypeStruct((t_tokens, ATT_INNER), BF16),
        jax.ShapeDtypeStruct((t_tokens, D_MODEL), BF16),
        jax.ShapeDtypeStruct((t_tokens, D_MODEL), BF16),
        jax.ShapeDtypeStruct((t_tokens, SSD_HEADS), F32),
        jax.ShapeDtypeStruct((ATT_INNER, t_tokens), BF16),
        jax.ShapeDtypeStruct((ATT_INNER, t_tokens), BF16),
        jax.ShapeDtypeStruct((SSD_HEADS, t_tokens), F32),
    ]
    out_specs = [row(SSD_INNER), row(SSD_CONV_CH), row(ATT_INNER), row(D_MODEL), row(D_MODEL), row(SSD_HEADS),
                 col(ATT_INNER), col(ATT_INNER), col(SSD_HEADS)]
    return pl.pallas_call(
        _in_proj_kernel,
        grid=grid,
        in_specs=[row(D_MODEL), whole, col(ATT_HEAD_DIM // 2), col(ATT_HEAD_DIM // 2),
                  whole, whole, whole],
        out_specs=out_specs,
        out_shape=out_shapes,
        compiler_params=pltpu.CompilerParams(dimension_semantics=("arbitrary",), vmem_limit_bytes=VMEM_LIMIT),
        name="in_proj",
    )(x2d, nw, cost, sint, w_big, w_t, w_dt)


def _ssd_kernel(z_ref, xbc_ref, xtail_ref, dt_ref, dtt_ref, cw_ref, cb_ref, dtb_ref, dtbc_ref, alog_ref, alogc_ref,
                dexp_ref, nw_ref, out_ref, state_ref):
    c = pl.program_id(1)
    cl = SSD_CHUNK
    heads_per_group = SSD_HEADS // SSD_GROUPS
    gw = heads_per_group * SSD_HEAD_DIM

    @pl.when(c == 0)
    def _():
        state_ref[...] = jnp.zeros_like(state_ref)

    xbc = xbc_ref[...]
    tail = xtail_ref[...]
    tail = jnp.where(c == 0, jnp.zeros_like(tail), tail)
    u_ext = jnp.concatenate([tail, xbc], axis=0)
    n_shift = SSD_CONV - 1
    srow = lax.broadcasted_iota(jnp.int32, (n_shift * cl, CONV_TAIL + cl), 0)
    scol = lax.broadcasted_iota(jnp.int32, (n_shift * cl, CONV_TAIL + cl), 1)
    shift_sel = (scol == CONV_TAIL + srow % cl - (n_shift - srow // cl)).astype(BF16)
    shifted = _dot(shift_sel, u_ext)
    acc = cb_ref[...] + cw_ref[n_shift:SSD_CONV, :] * xbc.astype(F32)
    for k in range(n_shift):
        acc = acc + cw_ref[k:k + 1, :] * shifted[k * cl:(k + 1) * cl, :]
    xc = _silu(acc)
    xs = xc[:, :SSD_INNER]
    bm = xc[:, SSD_INNER:SSD_INNER + SSD_GROUPS * SSD_STATE].astype(BF16)
    cm = xc[:, SSD_INNER + SSD_GROUPS * SSD_STATE:].astype(BF16)

    dt = _softplus(dt_ref[...] + dtb_ref[...])
    da = dt * (-jnp.exp(alog_ref[...]))
    dtt = _softplus(dtt_ref[...] + dtbc_ref[...])
    dat = dtt * (-jnp.exp(alogc_ref[...]))
    ri = lax.broadcasted_iota(jnp.int32, (cl, cl), 0)
    ci = lax.broadcasted_iota(jnp.int32, (cl, cl), 1)
    causal = ri >= ci
    tril = causal.astype(BF16)
    triu = (ri <= ci).astype(BF16)
    acum = _dot_exact_rhs(tril, da)
    acumt = _dot_exact_lhs(dat, triu)

    hrow = lax.broadcasted_iota(jnp.int32, (SSD_HEADS, SSD_INNER), 0)
    hcol = lax.broadcasted_iota(jnp.int32, (SSD_HEADS, SSD_INNER), 1) // SSD_HEAD_DIM
    expand = (hrow == hcol).astype(BF16)

    a_last = acum[cl - 1:cl, :]
    dt_e = _dot_exact_lhs(dt, expand)
    dec_e = _dot_exact_lhs(jnp.exp(a_last - acum), expand)
    ea_e = _dot_exact_lhs(jnp.exp(acum), expand)
    xdt = xs * dt_e
    xdt_b = xdt.astype(BF16)
    xdtdec_b = (xdt * dec_e).astype(BF16)
    chunk_decay_t = jnp.exp(acumt[:, cl - 1:cl])

    y_groups = []
    for g in range(SSD_GROUPS):
        b_g = bm[:, g * SSD_STATE:(g + 1) * SSD_STATE]
        c_g = cm[:, g * SSD_STATE:(g + 1) * SSD_STATE]
        prev = state_ref[g * gw:(g + 1) * gw, :]
        cb = _dot_nt(c_g, b_g)
        y_off = _dot_nt(c_g, prev.astype(BF16)) * ea_e[:, g * gw:(g + 1) * gw]
        new_states = _dot_tn(xdtdec_b[:, g * gw:(g + 1) * gw], b_g)
        y_heads = []
        decay_rows = []
        for e in range(heads_per_group):
            hd = g * heads_per_group + e
            lmat = jnp.where(causal, jnp.exp(acum[:, hd:hd + 1] - acumt[hd:hd + 1, :]), 0.0)
            m = (cb * lmat).astype(BF16)
            y_heads.append(_dot(m, xdt_b[:, hd * SSD_HEAD_DIM:(hd + 1) * SSD_HEAD_DIM]))
            decay_rows.append(jnp.broadcast_to(chunk_decay_t[hd:hd + 1, :], (SSD_HEAD_DIM, SSD_STATE)))
        y_groups.append(jnp.concatenate(y_heads, axis=1) + y_off)
        state_ref[g * gw:(g + 1) * gw, :] = prev * jnp.concatenate(decay_rows, axis=0) + new_states

    y = jnp.concatenate(y_groups, axis=1) + xs * dexp_ref[...]
    yg = y * _silu(z_ref[...].astype(F32))
    nw = nw_ref[...]
    outs = [_rms(yg[:, g * gw:(g + 1) * gw], nw[:, g * gw:(g + 1) * gw]) for g in range(SSD_GROUPS)]
    out_ref[...] = jnp.concatenate(outs, axis=1).astype(BF16)


def _ssd(z, xbc, dt, dtt, conv_w, conv_b, dt_bias, a_log, d_skip, norm_w, batch):
    t_tokens = z.shape[0]
    nc = t_tokens // batch // SSD_CHUNK
    tok = lambda width: pl.BlockSpec((SSD_CHUNK, width), lambda b, c: (b * nc + c, 0))
    whole = pl.BlockSpec(memory_space=pltpu.VMEM)
    dexp = jnp.repeat(d_skip, SSD_HEAD_DIM)[None, :]
    tails_per_chunk = SSD_CHUNK // CONV_TAIL
    prev_tail = pl.BlockSpec((CONV_TAIL, SSD_CONV_CH),
                             lambda b, c: (jnp.maximum((b * nc + c) * tails_per_chunk - 1, 0), 0))
    return pl.pallas_call(
        _ssd_kernel,
        grid=(batch, nc),
        in_specs=[tok(SSD_INNER), tok(SSD_CONV_CH), prev_tail, tok(SSD_HEADS),
                  pl.BlockSpec((SSD_HEADS, SSD_CHUNK), lambda b, c: (0, b * nc + c)),
                  whole, whole, whole, whole, whole, whole, whole, whole],
        out_specs=tok(SSD_INNER),
        out_shape=jax.ShapeDtypeStruct((t_tokens, SSD_INNER), BF16),
        scratch_shapes=[pltpu.VMEM((SSD_INNER, SSD_STATE), F32)],
        compiler_params=pltpu.CompilerParams(dimension_semantics=("arbitrary", "arbitrary"),
                                             vmem_limit_bytes=VMEM_LIMIT),
        name="ssd",
    )(z, xbc, xbc, dt, dtt, conv_w, conv_b[None, :], dt_bias[None, :], dt_bias[:, None],
      a_log[None, :], a_log[:, None], dexp, norm_w[None, :])


MOBA_SUPER = 2 * MOBA_BLOCK


MOBA_BLOCK_ROWS = 16
ONES_ROWS = 16


def _moba_kernel(qt_ref, k_ref, vt_ref, o_ref, kfull_ref, qaug_ref, sa_ref, sb_ref, sc_ref, m_ref, l_ref, acc_ref,
                 *, seq):
    nb = seq // MOBA_BLOCK
    sb = MOBA_SUPER
    nsb = seq // sb
    hd = ATT_HEAD_DIM
    nbr = MOBA_BLOCK_ROWS

    for c in range(nsb):
        rows = slice(c * sb, (c + 1) * sb)
        kfull_ref[rows, 0:LANES] = k_ref[rows, :]
        rblk = lax.broadcasted_iota(jnp.int32, (sb, LANES), 0) // MOBA_BLOCK + c * (sb // MOBA_BLOCK)
        rlane = lax.broadcasted_iota(jnp.int32, (sb, LANES), 1)
        kfull_ref[rows, LANES:2 * LANES] = jnp.where(rblk == rlane, 1.0, 0.0).astype(BF16)

    pr = lax.broadcasted_iota(jnp.int32, (nbr, seq), 0)
    pc = lax.broadcasted_iota(jnp.int32, (nbr, seq), 1) // MOBA_BLOCK
    pool = jnp.where(pr == pc, 1.0 / MOBA_BLOCK, 0.0).astype(BF16)
    kmean = _dot(pool, k_ref[...])
    km_hi = kmean.astype(BF16)
    km_lo = (kmean - km_hi.astype(F32)).astype(BF16)

    head_of_row = lax.broadcasted_iota(jnp.int32, (LANES, sb), 0) // hd
    blk_row = lax.broadcasted_iota(jnp.int32, (nbr, sb), 0)
    q_local = lax.broadcasted_iota(jnp.int32, (nbr, sb), 1)
    inrange = blk_row < nb
    krow = lax.broadcasted_iota(jnp.int32, (sb, 2 * sb), 0)
    qcol = lax.broadcasted_iota(jnp.int32, (sb, 2 * sb), 1) % sb
    causal = krow <= qcol
    ones_rows = jnp.ones((ONES_ROWS, sb), BF16)
    flag_pad = jnp.zeros((LANES - nbr, sb), BF16)

    def scores_into(s_ref, q_sb, kv):
        start = pl.multiple_of(kv * sb, sb)
        s_ref[...] = _dot(kfull_ref[pl.ds(start, sb), :], qaug_ref[q_sb])

    def absorb(kv, s_ref, diagonal):
        start = pl.multiple_of(kv * sb, sb)
        s_t = s_ref[...]
        if diagonal:
            s_t = jnp.where(causal, s_t, NEG_BIG)
        m_prev = m_ref[...]
        m_new = jnp.maximum(m_prev, jnp.max(s_t, axis=0, keepdims=True))
        m_ref[...] = m_new
        p_b = jnp.exp2(s_t - m_new).astype(BF16)
        alpha = jnp.exp2(m_prev - m_new)
        for a in range(2):
            cols = slice(a * sb, (a + 1) * sb)
            rows = slice(a * hd, (a + 1) * hd)
            v_aug = jnp.concatenate([vt_ref[rows, pl.ds(start, sb)], ones_rows], axis=0)
            r = _dot(v_aug, p_b[:, cols])
            l_ref[a:a + 1, :] = alpha[:, cols] * l_ref[a:a + 1, :] + r[hd:hd + 1, :]
            acc_ref[rows, :] = alpha[:, cols] * acc_ref[rows, :] + r[0:hd, :]

    def build_q_aug(i, carry):
        qt_sb = qt_ref[:, pl.ds(pl.multiple_of(i * sb, sb), sb)]
        own = 2 * i + jnp.where(q_local >= MOBA_BLOCK, 1, 0)
        valid = blk_row < own
        for a in range(2):
            qa = jnp.where(head_of_row == a, qt_sb, jnp.zeros_like(qt_sb))
            gate = _dot(km_hi, qa) + _dot(km_lo, qa)
            g = jnp.where(valid, gate, -jnp.inf)
            sel = jnp.zeros((nbr, sb), jnp.bool_)
            for _ in range(min(MOBA_TOPK, nb - 1)):
                cand = inrange & jnp.logical_not(sel)
                gm = jnp.max(jnp.where(cand, g, -jnp.inf), axis=0, keepdims=True)
                hit = cand & (g == gm)
                first = jnp.min(jnp.where(hit, blk_row, nb), axis=0, keepdims=True)
                sel = sel | (blk_row == first)
            chosen = (sel & valid) | (blk_row == own)
            flags = jnp.where(chosen, 0.0, NEG_BIG).astype(BF16)
            qaug_ref[i, :, a * sb:(a + 1) * sb] = jnp.concatenate([qa, flags, flag_pad], axis=0)
        return carry

    lax.fori_loop(0, nsb, build_q_aug, 0)
    scores_into(sc_ref, 0, 0)

    def q_super(i, carry):
        m_ref[...] = jnp.full(m_ref.shape, NEG_BIG, F32)
        l_ref[...] = jnp.zeros_like(l_ref)
        acc_ref[...] = jnp.zeros_like(acc_ref)
        nxt = jnp.minimum(i + 1, nsb - 1)

        @pl.when(i == 0)
        def _():
            absorb(0, sc_ref, True)

        @pl.when(i == 0)
        def _():
            scores_into(sc_ref, nxt, 0)

        @pl.when(i > 0)
        def _():
            scores_into(sa_ref, i, 1)
            absorb(0, sc_ref, False)

            def two_steps(t, c):
                scores_into(sb_ref, i, 2 * t + 2)
                absorb(2 * t + 1, sa_ref, False)
                scores_into(sa_ref, i, 2 * t + 3)
                absorb(2 * t + 2, sb_ref, False)
                return c

            lax.fori_loop(0, (i - 1) // 2, two_steps, 0)

            @pl.when(i % 2 == 1)
            def _():
                scores_into(sc_ref, nxt, 0)
                absorb(i, sa_ref, True)

            @pl.when(i % 2 == 0)
            def _():
                scores_into(sb_ref, i, i)
                absorb(i - 1, sa_ref, False)
                scores_into(sc_ref, nxt, 0)
                absorb(i, sb_ref, True)

        out_t = jnp.concatenate([acc_ref[a * hd:(a + 1) * hd, :] / l_ref[a:a + 1, :] for a in range(2)], axis=0)
        o_ref[pl.ds(pl.multiple_of(i * sb, sb), sb), :] = out_t.T.astype(BF16)
        return carry

    lax.fori_loop(0, nsb, q_super, 0)


def _moba(qt, k, vt, batch):
    t_tokens = k.shape[0]
    seq = t_tokens // batch
    assert seq % MOBA_SUPER == 0 and seq // MOBA_BLOCK <= MOBA_BLOCK_ROWS
    spec = pl.BlockSpec((seq, LANES), lambda b, hp: (b, hp))
    spec_t = pl.BlockSpec((LANES, seq), lambda b, hp: (hp, b))
    return pl.pallas_call(
        functools.partial(_moba_kernel, seq=seq),
        grid=(batch, ATT_INNER // LANES),
        in_specs=[spec_t, spec, spec_t],
        out_specs=spec,
        out_shape=jax.ShapeDtypeStruct((t_tokens, ATT_INNER), BF16),
        scratch_shapes=[pltpu.VMEM((seq, 2 * LANES), BF16),
                        pltpu.VMEM((seq // MOBA_SUPER, 2 * LANES, 2 * MOBA_SUPER), BF16),
                        pltpu.VMEM((MOBA_SUPER, 2 * MOBA_SUPER), F32),
                        pltpu.VMEM((MOBA_SUPER, 2 * MOBA_SUPER), F32),
                        pltpu.VMEM((MOBA_SUPER, 2 * MOBA_SUPER), F32),
                        pltpu.VMEM((1, 2 * MOBA_SUPER), F32),
                        pltpu.VMEM((SUBLANES, MOBA_SUPER), F32),
                        pltpu.VMEM((LANES, MOBA_SUPER), F32)],
        compiler_params=pltpu.CompilerParams(dimension_semantics=("arbitrary", "arbitrary"),
                                             vmem_limit_bytes=VMEM_LIMIT),
        name="moba",
    )(qt, k, vt)


ROUTER_ROWS = 40


def _merge_kernel(x_ref, yn_ref, att_ref, gs_ref, ga_ref, wso_ref, wao_ref, wo_ref, nw_ref,
                  wr_hi_ref, wr_lo_ref, br_ref, upper_ref, x1_ref, h2_ref, route_ref, jloc_ref, cnt_ref):
    y_s = _dot(yn_ref[...], wso_ref[...])
    y_a = _dot(att_ref[...], wao_ref[...])
    merged = _sigmoid(gs_ref[...].astype(F32)) * y_s + _sigmoid(ga_ref[...].astype(F32)) * y_a
    x1 = x_ref[...] + _dot(merged.astype(BF16), wo_ref[...])
    x1_ref[...] = x1
    h2 = _rms(x1, nw_ref[...])
    h2_ref[...] = h2.astype(BF16)

    h_hi = h2.astype(BF16)
    h_lo = (h2 - h_hi.astype(F32)).astype(BF16)
    wr_hi = wr_hi_ref[...]
    logits = _dot_nt(wr_hi, h_hi) + _dot_nt(wr_hi, h_lo) + _dot_nt(wr_lo_ref[...], h_hi) + br_ref[...]
    gl = logits[0:N_GROUPS, :]
    gidx = lax.broadcasted_iota(jnp.int32, gl.shape, 0)
    gmax = jnp.max(gl, axis=0, keepdims=True)
    gi = jnp.min(jnp.where(gl == gmax, gidx, N_GROUPS), axis=0, keepdims=True)
    g_w = 1.0 / jnp.sum(jnp.exp(gl - gmax), axis=0, keepdims=True)
    el = jnp.zeros((EXPERTS_PER_GROUP, gl.shape[1]), F32)
    for g in range(N_GROUPS):
        lo = N_GROUPS + g * EXPERTS_PER_GROUP
        el = el + jnp.where(gi == g, logits[lo:lo + EXPERTS_PER_GROUP, :], 0.0)
    eidx = lax.broadcasted_iota(jnp.int32, el.shape, 0)
    v0 = jnp.max(el, axis=0, keepdims=True)
    i0 = jnp.min(jnp.where(el == v0, eidx, EXPERTS_PER_GROUP), axis=0, keepdims=True)
    rest = jnp.where(eidx == i0, -jnp.inf, el)
    v1 = jnp.max(rest, axis=0, keepdims=True)
    i1 = jnp.min(jnp.where((rest == v1) & (eidx != i0), eidx, EXPERTS_PER_GROUP), axis=0, keepdims=True)
    t = jnp.exp(v1 - v0)
    w0 = g_w / (1.0 + t)
    w1 = g_w * t / (1.0 + t)
    base = gi * EXPERTS_PER_GROUP
    route_ref[...] = jnp.concatenate(
        [(base + i0).astype(F32), (base + i1).astype(F32), w0, w1], axis=0)

    tm = gl.shape[1]
    eids = jnp.concatenate([base + i0, base + i1], axis=1)
    onehot = lax.broadcasted_iota(jnp.int32, (N_EXPERTS, 2 * tm), 0) == eids
    csum = _dot(onehot.astype(BF16), upper_ref[...])
    oh = onehot.astype(F32)
    lrank = jnp.sum(csum * oh, axis=0, keepdims=True) - 1.0
    cnt = jnp.floor((csum[:, 2 * tm - 1:2 * tm] + (RUN_ALIGN - 1)) * (1.0 / RUN_ALIGN))
    er = lax.broadcasted_iota(jnp.int32, (N_EXPERTS, N_EXPERTS), 0)
    ec = lax.broadcasted_iota(jnp.int32, (N_EXPERTS, N_EXPERTS), 1)
    before = _dot((ec < er).astype(BF16), jnp.broadcast_to(cnt, (N_EXPERTS, LANES)).astype(BF16))
    off = before[:, 0:1] * RUN_ALIGN
    jloc = jnp.sum(oh * off, axis=0, keepdims=True) + lrank
    jloc_ref[...] = jnp.concatenate([jloc[:, :tm], jloc[:, tm:]], axis=0).astype(jnp.int32)
    cnt_ref[...] = jnp.broadcast_to(cnt * RUN_ALIGN, (N_EXPERTS, LANES))


def _merge(x2d, yn, att, gs, ga, wso, wao, wo, nw, wr_hi, wr_lo, br):
    t_tokens = x2d.shape[0]
    tm = TM_TOK
    upper = jnp.triu(jnp.ones((TILE_ASG, TILE_ASG), BF16))
    row = lambda: pl.BlockSpec((tm, D_MODEL), lambda i: (i, 0))
    whole = pl.BlockSpec(memory_space=pltpu.VMEM)
    return pl.pallas_call(
        _merge_kernel,
        grid=(t_tokens // tm,),
        in_specs=[row(), row(), row(), row(), row(), whole, whole, whole, whole, whole, whole, whole, whole],
        out_specs=[row(), row(), pl.BlockSpec((4, tm), lambda i: (0, i)), pl.BlockSpec((2, tm), lambda i: (0, i)),
                   pl.BlockSpec((N_EXPERTS, LANES), lambda i: (i, 0))],
        out_shape=[jax.ShapeDtypeStruct((t_tokens, D_MODEL), F32),
                   jax.ShapeDtypeStruct((t_tokens, D_MODEL), BF16),
                   jax.ShapeDtypeStruct((4, t_tokens), F32),
                   jax.ShapeDtypeStruct((2, t_tokens), jnp.int32),
                   jax.ShapeDtypeStruct((t_tokens // tm * N_EXPERTS, LANES), F32)],
        compiler_params=pltpu.CompilerParams(dimension_semantics=("arbitrary",), vmem_limit_bytes=VMEM_LIMIT),
        name="merge",
    )(x2d, yn, att, gs, ga, wso, wao, wo, nw, wr_hi, wr_lo, br, upper)


def _rows(ref, first, n_rows):
    start = first if isinstance(first, int) else pl.multiple_of(first, RUN_ALIGN)
    return ref.at[pl.ds(start, n_rows), :]


def _for_each_piece(count, fn, bits):
    for bit in range(bits - 1, ALIGN_BIT - 1, -1):
        size = 1 << bit

        @pl.when((count & size) != 0)
        def _():
            fn((count >> (bit + 1)) << (bit + 1), size)


def _run_copies(glob_ref, tot_ref, tile, unit_copy):
    def unit(v, carry):
        unit_copy(v * RUN_ALIGN, glob_ref[tile * UNITS + v]).start()
        return carry
    lax.fori_loop(0, tot_ref[tile] >> ALIGN_BIT, unit, 0)


def _wait_run_copies(tot_ref, tile, src, dst, sem):
    pltpu.make_async_copy(_rows(src, 0, TILE_ASG), _rows(dst, 0, TILE_ASG), sem).wait()

    def piece(o, size):
        pltpu.make_async_copy(_rows(src, 0, size), _rows(dst, 0, size), sem).wait()
    _for_each_piece(tot_ref[tile] - TILE_ASG, piece, EXTRA_BITS)


def _moe_max_tiles(t_tokens):
    rows = 2 * t_tokens + (t_tokens // TM_TOK) * N_EXPERTS * (RUN_ALIGN - 1)
    return -(-rows // TM_MOE) + N_EXPERTS


def _moe_plan(cnt_rows, max_tiles):
    cnt = cnt_rows[:, 0].astype(jnp.int32).reshape(-1, N_EXPERTS)
    off_end = jnp.cumsum(cnt, axis=1)
    counts = jnp.sum(cnt, axis=0)
    padded = ((counts + TM_MOE - 1) // TM_MOE) * TM_MOE
    ends = jnp.cumsum(padded)
    starts = ends - padded
    base = starts[None, :] + jnp.cumsum(cnt, axis=0) - cnt
    loc = jnp.arange(UNITS, dtype=jnp.int32) * RUN_ALIGN
    run = jnp.sum((loc[None, :, None] >= off_end[:, None, :]).astype(jnp.int32), axis=2)
    run = jnp.minimum(run, N_EXPERTS - 1)
    shift = jnp.take_along_axis(base - (off_end - cnt), run, axis=1)
    glob = loc[None, :] + shift
    tile_start = jnp.arange(max_tiles, dtype=jnp.int32) * TM_MOE
    tile_expert = jnp.sum((tile_start[:, None] >= ends[None, :]).astype(jnp.int32), axis=1)
    tile_expert = jnp.minimum(tile_expert, N_EXPERTS - 1)
    n_tiles = (ends[-1] // TM_MOE).astype(jnp.int32).reshape(1)
    flat = lambda a: a.reshape(-1).astype(jnp.int32)
    return flat(glob), flat(off_end[:, -1]), starts + counts, ends, tile_expert, n_tiles


def _dispatch_kernel(glob_ref, tot_ref, padlo_ref, padhi_ref, ntile_ref, h2_ref, jrow_ref, xs_hbm,
                     xbuf0, xbuf1, zeros_ref, sems, zsem, *, max_tiles):
    i = pl.program_id(0)
    n = pl.num_programs(0)
    xbufs = (xbuf0, xbuf1)

    def wait_runs(tile, slot):
        _wait_run_copies(tot_ref, tile, xbufs[slot], xs_hbm, sems.at[slot])

    @pl.when(i == 0)
    def _():
        zeros_ref[...] = jnp.zeros_like(zeros_ref)
        nt = ntile_ref[0]
        for wait in (False, True):
            def pad(e, carry):
                lo = padlo_ref[e]

                def piece(o, size):
                    cp = pltpu.make_async_copy(_rows(zeros_ref, 0, size), _rows(xs_hbm, lo + o, size), zsem)
                    cp.wait() if wait else cp.start()
                _for_each_piece(padhi_ref[e] - lo, piece, PAD_BITS)
                return carry
            lax.fori_loop(0, N_EXPERTS, pad, 0)

            def spare(tile, carry):
                cp = pltpu.make_async_copy(zeros_ref, _rows(xs_hbm, tile * TM_MOE, TM_MOE), zsem)
                cp.wait() if wait else cp.start()
                return carry
            lax.fori_loop(nt, max_tiles, spare, 0)

    for slot in range(2):
        @pl.when(i % 2 == slot)
        def _():
            @pl.when(i >= 2)
            def _():
                wait_runs(i - 2, slot)
            row = lax.broadcasted_iota(jnp.int32, (LOC_ROWS, TM_TOK), 0)
            j = jrow_ref[...]
            perm = ((row == j[0:1, :]) | (row == j[1:2, :])).astype(BF16)
            xbufs[slot][...] = _dot(perm, h2_ref[...]).astype(BF16)
            _run_copies(glob_ref, tot_ref, i, lambda loc, g: pltpu.make_async_copy(
                _rows(xbufs[slot], loc, RUN_ALIGN), _rows(xs_hbm, g, RUN_ALIGN), sems.at[slot]))

            @pl.when(i == n - 1)
            def _():
                @pl.when(i >= 1)
                def _():
                    wait_runs(i - 1, 1 - slot)
                wait_runs(i, slot)


def _dispatch(plan, jloc, h2, max_tiles):
    glob, tot, pad_lo, pad_hi, _, n_tiles = plan
    t_tokens = h2.shape[0]
    grid_spec = pltpu.PrefetchScalarGridSpec(
        num_scalar_prefetch=5,
        grid=(t_tokens // TM_TOK,),
        in_specs=[pl.BlockSpec((TM_TOK, D_MODEL), lambda i, *_: (i, 0)),
                  pl.BlockSpec((2, TM_TOK), lambda i, *_: (0, i))],
        out_specs=pl.BlockSpec(memory_space=pl.ANY),
        scratch_shapes=[pltpu.VMEM((LOC_ROWS, D_MODEL), BF16),
                        pltpu.VMEM((LOC_ROWS, D_MODEL), BF16),
                        pltpu.VMEM((TM_MOE, D_MODEL), BF16),
                        pltpu.SemaphoreType.DMA((2,)), pltpu.SemaphoreType.DMA(())],
    )
    return pl.pallas_call(
        functools.partial(_dispatch_kernel, max_tiles=max_tiles),
        grid_spec=grid_spec,
        out_shape=jax.ShapeDtypeStruct((max_tiles * TM_MOE, D_MODEL), BF16),
        compiler_params=pltpu.CompilerParams(dimension_semantics=("arbitrary",), vmem_limit_bytes=VMEM_LIMIT),
        name="dispatch",
    )(glob, tot, pad_lo, pad_hi, n_tiles, h2, jloc)


def _moe_kernel(texp_ref, ntile_ref, xs_ref, wg_ref, wu_ref, wd_ref, y_ref, wgu_s, wd_s):
    t = pl.program_id(0)
    live = t < ntile_ref[0]

    @pl.when(live & ((t == 0) | (texp_ref[t] != texp_ref[jnp.maximum(t - 1, 0)])))
    def _():
        wgu_s[:, 0:D_EXPERT] = wg_ref[0].astype(BF16)
        wgu_s[:, D_EXPERT:2 * D_EXPERT] = wu_ref[0].astype(BF16)
        wd_s[...] = wd_ref[0].astype(BF16)

    @pl.when(live)
    def _():
        gu = _dot(xs_ref[...], wgu_s[...])
        hid = (_silu(gu[:, :D_EXPERT]) * gu[:, D_EXPERT:]).astype(BF16)
        y_ref[...] = _dot(hid, wd_s[...]).astype(BF16)

    @pl.when(jnp.logical_not(live))
    def _():
        y_ref[...] = jnp.zeros_like(y_ref)


def _moe(xs, tile_expert, n_tiles, w_gate, w_up, w_down, max_tiles):
    live = lambda t, nt: jnp.minimum(t, nt[0] - 1)
    tile = lambda index: pl.BlockSpec((TM_MOE, D_MODEL), index)
    expert = lambda shape: pl.BlockSpec((1,) + shape, lambda t, te, nt: (te[live(t, nt)], 0, 0))
    grid_spec = pltpu.PrefetchScalarGridSpec(
        num_scalar_prefetch=2,
        grid=(max_tiles,),
        in_specs=[tile(lambda t, te, nt: (live(t, nt), 0)),
                  expert((D_MODEL, D_EXPERT)), expert((D_MODEL, D_EXPERT)), expert((D_EXPERT, D_MODEL))],
        out_specs=tile(lambda t, te, nt: (t, 0)),
        scratch_shapes=[pltpu.VMEM((D_MODEL, 2 * D_EXPERT), BF16), pltpu.VMEM((D_EXPERT, D_MODEL), BF16)],
    )
    return pl.pallas_call(
        _moe_kernel,
        grid_spec=grid_spec,
        out_shape=jax.ShapeDtypeStruct((max_tiles * TM_MOE, D_MODEL), BF16),
        compiler_params=pltpu.CompilerParams(dimension_semantics=("arbitrary",), vmem_limit_bytes=VMEM_LIMIT),
        name="moe",
    )(tile_expert, n_tiles, xs, w_gate, w_up, w_down)


def _tail_kernel(glob_ref, tot_ref, x1_ref, p_ref, jw_ref, y_hbm, pnw_ref, wpg_ref, wple_ref,
                 fnw_ref, o_ref, ybuf0, ybuf1, sems):
    i = pl.program_id(0)
    n = pl.num_programs(0)
    ybufs = (ybuf0, ybuf1)

    def fetch_runs(tile, slot):
        _run_copies(glob_ref, tot_ref, tile, lambda loc, g: pltpu.make_async_copy(
            _rows(y_hbm, g, RUN_ALIGN), _rows(ybufs[slot], loc, RUN_ALIGN), sems.at[slot]))

    @pl.when(i == 0)
    def _():
        ybuf0[...] = jnp.zeros_like(ybuf0)
        ybuf1[...] = jnp.zeros_like(ybuf1)
        fetch_runs(0, 0)

    for slot in range(2):
        @pl.when(i % 2 == slot)
        def _():
            @pl.when(i + 1 < n)
            def _():
                fetch_runs(i + 1, 1 - slot)
            _wait_run_copies(tot_ref, i, y_hbm, ybufs[slot], sems.at[slot])
            y_loc = ybufs[slot][...]
            jw = jw_ref[...]
            col = lax.broadcasted_iota(jnp.int32, (TM_TOK, LOC_ROWS), 1)
            pick = sum(jnp.where(col == jw[:, s:s + 1].astype(jnp.int32), jw[:, 2 + s:3 + s], 0.0) for s in range(2))
            moe = _dot(pick.astype(BF16), y_loc)
            x2 = x1_ref[...] + moe
            gate = _sigmoid(_dot(_rms(x2, pnw_ref[...]).astype(BF16), wpg_ref[...]))
            x3 = x2 + gate * _dot(p_ref[...].astype(BF16), wple_ref[...])
            o_ref[...] = _rms(x3, fnw_ref[...])


def _tail(plan, x1, p2d, y_sorted, jw, pnw, wpg, wple, fnw):
    glob, tot = plan[0:2]
    t_tokens = x1.shape[0]
    whole = pl.BlockSpec(memory_space=pltpu.VMEM)
    grid_spec = pltpu.PrefetchScalarGridSpec(
        num_scalar_prefetch=2,
        grid=(t_tokens // TM_TOK,),
        in_specs=[pl.BlockSpec((TM_TOK, D_MODEL), lambda i, *_: (i, 0)),
                  pl.BlockSpec((TM_TOK, PLE_DIM), lambda i, *_: (i, 0)),
                  pl.BlockSpec((TM_TOK, 4), lambda i, *_: (i, 0)),
                  pl.BlockSpec(memory_space=pl.ANY),
                  whole, whole, whole, whole],
        out_specs=pl.BlockSpec((TM_TOK, D_MODEL), lambda i, *_: (i, 0)),
        scratch_shapes=[pltpu.VMEM((LOC_ROWS, D_MODEL), BF16),
                        pltpu.VMEM((LOC_ROWS, D_MODEL), BF16),
                        pltpu.SemaphoreType.DMA((2,))],
    )
    return pl.pallas_call(
        _tail_kernel,
        grid_spec=grid_spec,
        out_shape=jax.ShapeDtypeStruct((t_tokens, D_MODEL), F32),
        compiler_params=pltpu.CompilerParams(dimension_semantics=("arbitrary",), vmem_limit_bytes=VMEM_LIMIT),
        name="tail",
    )(glob, tot, x1, p2d, jw, y_sorted, pnw, wpg, wple, fnw)


def _rope_tables(positions):
    half = ATT_HEAD_DIM // 2
    inv_freq = ROPE_THETA ** (-jnp.arange(half, dtype=F32) / half)
    ang = positions.astype(F32)[..., None] * inv_freq
    return jnp.cos(ang).reshape(-1, half).T, jnp.sin(ang).reshape(-1, half).T


def _layer(x2d, p2d, rope_tables, batch, attn_norm_w, w_in, conv_w, conv_b, dt_bias, a_log, d_skip, ssd_norm_w,
           w_ssd_out, w_attn_out, w_out, moe_norm_w, w_rg, b_rg, w_re, b_re, w_gate, w_up, w_down,
           ple_norm_w, w_ple, w_ple_gate, final_norm_w):
    t_tokens = x2d.shape[0]
    offs = np.cumsum((0,) + IN_SPLITS)
    cols = lambda i: w_in[:, offs[i]:offs[i + 1]]
    w_big = jnp.concatenate([cols(0), cols(1), cols(4), cols(6), cols(7)], axis=1).astype(BF16)
    w_t = jnp.concatenate([cols(3), cols(5), cols(2)], axis=1).T.astype(BF16)
    w_dt = jnp.pad(cols(2), ((0, 0), (0, LANES - SSD_HEADS))).astype(BF16)

    z, xbc, k, gs, ga, dt, qt, vt, dtt = _in_proj(x2d, attn_norm_w[None, :], *rope_tables, w_big, w_t, w_dt)

    yn = _ssd(z, xbc, dt, dtt, conv_w, conv_b, dt_bias, a_log, d_skip, ssd_norm_w, batch)

    att = _moba(qt, k, vt, batch)

    wr = jnp.concatenate([w_rg, w_re], axis=1).T
    wr = jnp.pad(wr, ((0, ROUTER_ROWS - wr.shape[0]), (0, 0)))
    wr_hi = wr.astype(BF16)
    wr_lo = (wr - wr_hi.astype(F32)).astype(BF16)
    br = jnp.pad(jnp.concatenate([b_rg, b_re]), (0, ROUTER_ROWS - N_GROUPS - N_EXPERTS))[:, None]
    x1, h2, route, jloc, cnt_rows = _merge(x2d, yn, att, gs, ga, w_ssd_out.astype(BF16), w_attn_out.astype(BF16),
                           w_out.astype(BF16), moe_norm_w[None, :], wr_hi, wr_lo, br)

    max_tiles = _moe_max_tiles(t_tokens)
    plan = _moe_plan(cnt_rows, max_tiles)
    x_sorted = _dispatch(plan, jloc, h2, max_tiles)
    y_sorted = _moe(x_sorted, plan[4], plan[5], w_gate.reshape(N_EXPERTS, D_MODEL, D_EXPERT),
                    w_up.reshape(N_EXPERTS, D_MODEL, D_EXPERT), w_down.reshape(N_EXPERTS, D_EXPERT, D_MODEL), max_tiles)
    jw = jnp.concatenate([jloc.astype(F32), route[2:4]], axis=0).T

    return _tail(plan, x1, p2d, y_sorted, jw, ple_norm_w[None, :],
                 w_ple_gate.astype(BF16), w_ple.astype(BF16), final_norm_w[None, :])


def kernel(x, p, positions, attn_norm_w, w_in, conv_w, conv_b, dt_bias, a_log, d_skip, ssd_norm_w, w_ssd_out,
           w_attn_out, w_out, moe_norm_w, w_router_group, b_router_group, w_router_expert, b_router_expert,
           w_exp_gate, w_exp_up, w_exp_down, ple_norm_w, w_ple, w_ple_gate, final_norm_w):
    batch, seq, d = x.shape
    depth = p.shape[0]
    assert depth == 1, "the final norm is fused into the last layer's tail kernel"
    rope_tables = _rope_tables(positions)
    i = 0
    out = _layer(x.reshape(batch * seq, d), p[i].reshape(batch * seq, PLE_DIM), rope_tables, batch,
                 attn_norm_w[i], w_in[i], conv_w[i], conv_b[i], dt_bias[i], a_log[i], d_skip[i], ssd_norm_w[i],
                 w_ssd_out[i], w_attn_out[i], w_out[i], moe_norm_w[i], w_router_group[i], b_router_group[i],
                 w_router_expert[i], b_router_expert[i], w_exp_gate[i], w_exp_up[i], w_exp_down[i],
                 ple_norm_w[i], w_ple[i], w_ple_gate[i], final_norm_w)
    return out.reshape(batch, seq, d)
```

```python
import functools

import jax
import jax.numpy as jnp
import numpy as np
from jax import lax
from jax.experimental import pallas as pl
from jax.experimental.pallas import tpu as pltpu

F32 = jnp.float32
BF16 = jnp.bfloat16

EPS = 1e-6
D_MODEL = 1024
SSD_HEADS = 16
SSD_HEAD_DIM = 64
SSD_INNER = SSD_HEADS * SSD_HEAD_DIM
SSD_GROUPS = 2
SSD_STATE = 64
SSD_CONV = 4
SSD_CHUNK = 128
SSD_CONV_CH = SSD_INNER + 2 * SSD_GROUPS * SSD_STATE
CONV_TAIL = 16
ATT_HEADS = 16
ATT_HEAD_DIM = 64
ATT_INNER = ATT_HEADS * ATT_HEAD_DIM
MOBA_BLOCK = 256
MOBA_TOPK = 3
ROPE_THETA = 10000.0
IN_SPLITS = (SSD_INNER, SSD_CONV_CH, SSD_HEADS, ATT_INNER, ATT_INNER, ATT_INNER, D_MODEL, D_MODEL)
N_GROUPS = 4
EXPERTS_PER_GROUP = 8
N_EXPERTS = N_GROUPS * EXPERTS_PER_GROUP
D_EXPERT = 256
PLE_DIM = 256

LANES = 128
SUBLANES = 8
NEG_BIG = -1e30
LOG2E = 1.4426950408889634
VMEM_LIMIT = 56 * 1024 * 1024

TM_PROJ = 512
TM_MOE = 512
TM_TOK = 512
TILE_ASG = 2 * TM_TOK
RUN_ALIGN = 2 * SUBLANES
LOC_ROWS = TILE_ASG + N_EXPERTS * RUN_ALIGN
ALIGN_BIT = RUN_ALIGN.bit_length() - 1
PAD_BITS = TM_MOE.bit_length() - 1
EXTRA_BITS = (LOC_ROWS - TILE_ASG).bit_length()
UNITS = LOC_ROWS // RUN_ALIGN


def _dot(a, b):
    return jnp.dot(a, b, preferred_element_type=F32)


def _dot_nt(a, b):
    return lax.dot_general(a, b, (((1,), (1,)), ((), ())), preferred_element_type=F32)


def _dot_tn(a, b):
    return lax.dot_general(a, b, (((0,), (0,)), ((), ())), preferred_element_type=F32)


def _split3(a):
    hi = a.astype(BF16)
    r1 = a - hi.astype(F32)
    mid = r1.astype(BF16)
    lo = (r1 - mid.astype(F32)).astype(BF16)
    return hi, mid, lo


def _dot_exact_rhs(sel_bf16, a):
    return _dot(jnp.concatenate([sel_bf16] * 3, axis=1), jnp.concatenate(_split3(a), axis=0))


def _dot_exact_tn(a, sel_bf16):
    return _dot_tn(jnp.concatenate(_split3(a), axis=0), jnp.concatenate([sel_bf16] * 3, axis=0))


def _dot_exact_lhs(a, sel_bf16):
    return _dot(jnp.concatenate(_split3(a), axis=1), jnp.concatenate([sel_bf16] * 3, axis=0))


def _sigmoid(x):
    return 0.5 * jnp.tanh(0.5 * x) + 0.5


def _silu(x):
    return x * _sigmoid(x)


def _softplus(x):
    return jnp.maximum(x, 0.0) + jnp.log1p(jnp.exp(-jnp.abs(x)))


def _rms(x, w):
    inv = lax.rsqrt(jnp.mean(x * x, axis=-1, keepdims=True) + EPS)
    return (x * inv) * w


def _rope_tile(t, cos, sin_signed):
    half = ATT_HEAD_DIM // 2
    lane = lax.broadcasted_iota(jnp.int32, t.shape, 1)
    first = (lane % ATT_HEAD_DIM) < half
    partner = jnp.where(first, pltpu.roll(t, LANES - half, 1), pltpu.roll(t, half, 1))
    return t * cos + partner * sin_signed


Q_SCALE = ATT_HEAD_DIM ** -0.5 * LOG2E


def _in_proj_kernel(x_ref, nw_ref, cost_ref, sint_ref, w_ref, wt_ref, wdt_ref,
                    z_ref, xbc_ref, k_ref, gs_ref, ga_ref, dt_ref, qt_ref, vt_ref, dtt_ref):
    h = _rms(x_ref[...], nw_ref[...]).astype(BF16)
    half = ATT_HEAD_DIM // 2
    lane = lax.broadcasted_iota(jnp.int32, (half, LANES), 1)
    same = lane % half == lax.broadcasted_iota(jnp.int32, (half, LANES), 0)
    first = lane % ATT_HEAD_DIM < half
    cost = cost_ref[...]
    sint = sint_ref[...]
    cos = _dot_exact_tn(cost, same.astype(BF16))
    sin = _dot_exact_tn(sint, jnp.where(same, jnp.where(first, -1.0, 1.0), 0.0).astype(BF16))

    def proj(lo, width):
        return _dot(h, w_ref[:, lo:lo + width])

    o = 0
    z_ref[...] = proj(o, SSD_INNER).astype(BF16)
    o += SSD_INNER
    xbc_ref[...] = proj(o, SSD_CONV_CH).astype(BF16)
    o += SSD_CONV_CH
    k_lin = proj(o, ATT_INNER)
    for g in range(ATT_INNER // LANES):
        cols = slice(g * LANES, (g + 1) * LANES)
        k_ref[:, cols] = _rope_tile(k_lin[:, cols], cos, sin).astype(BF16)
    o += ATT_INNER
    for ref in (gs_ref, ga_ref):
        ref[...] = proj(o, D_MODEL).astype(BF16)
        o += D_MODEL
    dt_ref[...] = _dot(h, wdt_ref[...])[:, :SSD_HEADS]

    dtt_ref[...] = _dot_nt(wt_ref[2 * ATT_INNER:2 * ATT_INNER + SSD_HEADS, :], h)
    vt_ref[...] = _dot_nt(wt_ref[ATT_INNER:2 * ATT_INNER, :], h).astype(BF16)
    qt = _dot_nt(wt_ref[0:ATT_INNER, :], h)
    for hd in range(ATT_HEADS):
        r0 = hd * ATT_HEAD_DIM
        x1 = qt[r0:r0 + half, :]
        x2 = qt[r0 + half:r0 + ATT_HEAD_DIM, :]
        qt_ref[r0:r0 + half, :] = ((x1 * cost - x2 * sint) * Q_SCALE).astype(BF16)
        qt_ref[r0 + half:r0 + ATT_HEAD_DIM, :] = ((x2 * cost + x1 * sint) * Q_SCALE).astype(BF16)


def _in_proj(x2d, nw, cost, sint, w_big, w_t, w_dt):
    t_tokens = x2d.shape[0]
    tm = min(TM_PROJ, t_tokens)
    grid = (t_tokens // tm,)
    row = lambda width: pl.BlockSpec((tm, width), lambda i: (i, 0))
    col = lambda height: pl.BlockSpec((height, tm), lambda i: (0, i))
    whole = pl.BlockSpec(memory_space=pltpu.VMEM)
    out_shapes = [
        jax.ShapeDtypeStruct((t_tokens, SSD_INNER), BF16),
        jax.ShapeDtypeStruct((t_tokens, SSD_CONV_CH), BF16),
        jax.ShapeDtypeStruct((t_tokens, ATT_INNER), BF16),
        jax.ShapeDtypeStruct((t_tokens, D_MODEL), BF16),
        jax.ShapeDtypeStruct((t_tokens, D_MODEL), BF16),
        jax.ShapeDtypeStruct((t_tokens, SSD_HEADS), F32),
        jax.ShapeDtypeStruct((ATT_INNER, t_tokens), BF16),
        jax.ShapeDtypeStruct((ATT_INNER, t_tokens), BF16),
        jax.ShapeDtypeStruct((SSD_HEADS, t_tokens), F32),
    ]
    out_specs = [row(SSD_INNER), row(SSD_CONV_CH), row(ATT_INNER), row(D_MODEL), row(D_MODEL), row(SSD_HEADS),
                 col(ATT_INNER), col(ATT_INNER), col(SSD_HEADS)]
    return pl.pallas_call(
        _in_proj_kernel,
        grid=grid,
        in_specs=[row(D_MODEL), whole, col(ATT_HEAD_DIM // 2), col(ATT_HEAD_DIM // 2),
                  whole, whole, whole],
        out_specs=out_specs,
        out_shape=out_shapes,
        compiler_params=pltpu.CompilerParams(dimension_semantics=("arbitrary",), vmem_limit_bytes=VMEM_LIMIT),
        name="in_proj",
    )(x2d, nw, cost, sint, w_big, w_t, w_dt)


def _ssd_kernel(z_ref, xbc_ref, xtail_ref, dt_ref, dtt_ref, cw_ref, cb_ref, dtb_ref, dtbc_ref, alog_ref, alogc_ref,
                dexp_ref, nw_ref, out_ref, state_ref):
    c = pl.program_id(1)
    cl = SSD_CHUNK
    heads_per_group = SSD_HEADS // SSD_GROUPS
    gw = heads_per_group * SSD_HEAD_DIM

    @pl.when(c == 0)
    def _():
        state_ref[...] = jnp.zeros_like(state_ref)

    xbc = xbc_ref[...]
    tail = xtail_ref[...]
    tail = jnp.where(c == 0, jnp.zeros_like(tail), tail)
    u_ext = jnp.concatenate([tail, xbc], axis=0)
    n_shift = SSD_CONV - 1
    srow = lax.broadcasted_iota(jnp.int32, (n_shift * cl, CONV_TAIL + cl), 0)
    scol = lax.broadcasted_iota(jnp.int32, (n_shift * cl, CONV_TAIL + cl), 1)
    shift_sel = (scol == CONV_TAIL + srow % cl - (n_shift - srow // cl)).astype(BF16)
    shifted = _dot(shift_sel, u_ext)
    acc = cb_ref[...] + cw_ref[n_shift:SSD_CONV, :] * xbc.astype(F32)
    for k in range(n_shift):
        acc = acc + cw_ref[k:k + 1, :] * shifted[k * cl:(k + 1) * cl, :]
    xc = _silu(acc)
    xs = xc[:, :SSD_INNER]
    bm = xc[:, SSD_INNER:SSD_INNER + SSD_GROUPS * SSD_STATE].astype(BF16)
    cm = xc[:, SSD_INNER + SSD_GROUPS * SSD_STATE:].astype(BF16)

    dt = _softplus(dt_ref[...] + dtb_ref[...])
    da = dt * (-jnp.exp(alog_ref[...]))
    dtt = _softplus(dtt_ref[...] + dtbc_ref[...])
    dat = dtt * (-jnp.exp(alogc_ref[...]))
    ri = lax.broadcasted_iota(jnp.int32, (cl, cl), 0)
    ci = lax.broadcasted_iota(jnp.int32, (cl, cl), 1)
    causal = ri >= ci
    tril = causal.astype(BF16)
    triu = (ri <= ci).astype(BF16)
    acum = _dot_exact_rhs(tril, da)
    acumt = _dot_exact_lhs(dat, triu)

    hrow = lax.broadcasted_iota(jnp.int32, (SSD_HEADS, SSD_INNER), 0)
    hcol = lax.broadcasted_iota(jnp.int32, (SSD_HEADS, SSD_INNER), 1) // SSD_HEAD_DIM
    expand = (hrow == hcol).astype(BF16)

    a_last = acum[cl - 1:cl, :]
    dt_e = _dot_exact_lhs(dt, expand)
    dec_e = _dot_exact_lhs(jnp.exp(a_last - acum), expand)
    ea_e = _dot_exact_lhs(jnp.exp(acum), expand)
    xdt = xs * dt_e
    xdt_b = xdt.astype(BF16)
    xdtdec_b = (xdt * dec_e).astype(BF16)
    chunk_decay_t = jnp.exp(acumt[:, cl - 1:cl])

    y_groups = []
    for g in range(SSD_GROUPS):
        b_g = bm[:, g * SSD_STATE:(g + 1) * SSD_STATE]
        c_g = cm[:, g * SSD_STATE:(g + 1) * SSD_STATE]
        prev = state_ref[g * gw:(g + 1) * gw, :]
        cb = _dot_nt(c_g, b_g)
        y_off = _dot_nt(c_g, prev.astype(BF16)) * ea_e[:, g * gw:(g + 1) * gw]
        new_states = _dot_tn(xdtdec_b[:, g * gw:(g + 1) * gw], b_g)
        y_heads = []
        decay_rows = []
        for e in range(heads_per_group):
            hd = g * heads_per_group + e
            lmat = jnp.where(causal, jnp.exp(acum[:, hd:hd + 1] - acumt[hd:hd + 1, :]), 0.0)
            m = (cb * lmat).astype(BF16)
            y_heads.append(_dot(m, xdt_b[:, hd * SSD_HEAD_DIM:(hd + 1) * SSD_HEAD_DIM]))
            decay_rows.append(jnp.broadcast_to(chunk_decay_t[hd:hd + 1, :], (SSD_HEAD_DIM, SSD_STATE)))
        y_groups.append(jnp.concatenate(y_heads, axis=1) + y_off)
        state_ref[g * gw:(g + 1) * gw, :] = prev * jnp.concatenate(decay_rows, axis=0) + new_states

    y = jnp.concatenate(y_groups, axis=1) + xs * dexp_ref[...]
    yg = y * _silu(z_ref[...].astype(F32))
    nw = nw_ref[...]
    outs = [_rms(yg[:, g * gw:(g + 1) * gw], nw[:, g * gw:(g + 1) * gw]) for g in range(SSD_GROUPS)]
    out_ref[...] = jnp.concatenate(outs, axis=1).astype(BF16)


def _ssd(z, xbc, dt, dtt, conv_w, conv_b, dt_bias, a_log, d_skip, norm_w, batch):
    t_tokens = z.shape[0]
    nc = t_tokens // batch // SSD_CHUNK
    tok = lambda width: pl.BlockSpec((SSD_CHUNK, width), lambda b, c: (b * nc + c, 0))
    whole = pl.BlockSpec(memory_space=pltpu.VMEM)
    dexp = jnp.repeat(d_skip, SSD_HEAD_DIM)[None, :]
    tails_per_chunk = SSD_CHUNK // CONV_TAIL
    prev_tail = pl.BlockSpec((CONV_TAIL, SSD_CONV_CH),
                             lambda b, c: (jnp.maximum((b * nc + c) * tails_per_chunk - 1, 0), 0))
    return pl.pallas_call(
        _ssd_kernel,
        grid=(batch, nc),
        in_specs=[tok(SSD_INNER), tok(SSD_CONV_CH), prev_tail, tok(SSD_HEADS),
                  pl.BlockSpec((SSD_HEADS, SSD_CHUNK), lambda b, c: (0, b * nc + c)),
                  whole, whole, whole, whole, whole, whole, whole, whole],
        out_specs=tok(SSD_INNER),
        out_shape=jax.ShapeDtypeStruct((t_tokens, SSD_INNER), BF16),
        scratch_shapes=[pltpu.VMEM((SSD_INNER, SSD_STATE), F32)],
        compiler_params=pltpu.CompilerParams(dimension_semantics=("arbitrary", "arbitrary"),
                                             vmem_limit_bytes=VMEM_LIMIT),
        name="ssd",
    )(z, xbc, xbc, dt, dtt, conv_w, conv_b[None, :], dt_bias[None, :], dt_bias[:, None],
      a_log[None, :], a_log[:, None], dexp, norm_w[None, :])


MOBA_SUPER = 2 * MOBA_BLOCK


MOBA_BLOCK_ROWS = 16
ONES_ROWS = 16


def _moba_kernel(qt_ref, k_ref, vt_ref, o_ref, kfull_ref, qaug_ref, sa_ref, sb_ref, sc_ref, m_ref, l_ref, acc_ref,
                 *, seq):
    nb = seq // MOBA_BLOCK
    sb = MOBA_SUPER
    nsb = seq // sb
    hd = ATT_HEAD_DIM
    nbr = MOBA_BLOCK_ROWS

    for c in range(nsb):
        rows = slice(c * sb, (c + 1) * sb)
        kfull_ref[rows, 0:LANES] = k_ref[rows, :]
        rblk = lax.broadcasted_iota(jnp.int32, (sb, LANES), 0) // MOBA_BLOCK + c * (sb // MOBA_BLOCK)
        rlane = lax.broadcasted_iota(jnp.int32, (sb, LANES), 1)
        kfull_ref[rows, LANES:2 * LANES] = jnp.where(rblk == rlane, 1.0, 0.0).astype(BF16)

    pr = lax.broadcasted_iota(jnp.int32, (nbr, seq), 0)
    pc = lax.broadcasted_iota(jnp.int32, (nbr, seq), 1) // MOBA_BLOCK
    pool = jnp.where(pr == pc, 1.0 / MOBA_BLOCK, 0.0).astype(BF16)
    kmean = _dot(pool, k_ref[...])
    km_hi = kmean.astype(BF16)
    km_lo = (kmean - km_hi.astype(F32)).astype(BF16)

    head_of_row = lax.broadcasted_iota(jnp.int32, (LANES, sb), 0) // hd
    blk_row = lax.broadcasted_iota(jnp.int32, (nbr, sb), 0)
    q_local = lax.broadcasted_iota(jnp.int32, (nbr, sb), 1)
    inrange = blk_row < nb
    krow = lax.broadcasted_iota(jnp.int32, (sb, 2 * sb), 0)
    qcol = lax.broadcasted_iota(jnp.int32, (sb, 2 * sb), 1) % sb
    causal = krow <= qcol
    ones_rows = jnp.ones((ONES_ROWS, sb), BF16)
    flag_pad = jnp.zeros((LANES - nbr, sb), BF16)

    def scores_into(s_ref, q_sb, kv):
        start = pl.multiple_of(kv * sb, sb)
        s_ref[...] = _dot(kfull_ref[pl.ds(start, sb), :], qaug_ref[q_sb])

    def absorb(kv, s_ref, diagonal):
        start = pl.multiple_of(kv * sb, sb)
        s_t = s_ref[...]
        if diagonal:
            s_t = jnp.where(causal, s_t, NEG_BIG)
        m_prev = m_ref[...]
        m_new = jnp.maximum(m_prev, jnp.max(s_t, axis=0, keepdims=True))
        m_ref[...] = m_new
        p_b = jnp.exp2(s_t - m_new).astype(BF16)
        alpha = jnp.exp2(m_prev - m_new)
        for a in range(2):
            cols = slice(a * sb, (a + 1) * sb)
            rows = slice(a * hd, (a + 1) * hd)
            v_aug = jnp.concatenate([vt_ref[rows, pl.ds(start, sb)], ones_rows], axis=0)
            r = _dot(v_aug, p_b[:, cols])
            l_ref[a:a + 1, :] = alpha[:, cols] * l_ref[a:a + 1, :] + r[hd:hd + 1, :]
            acc_ref[rows, :] = alpha[:, cols] * acc_ref[rows, :] + r[0:hd, :]

    def build_q_aug(i, carry):
        qt_sb = qt_ref[:, pl.ds(pl.multiple_of(i * sb, sb), sb)]
        own = 2 * i + jnp.where(q_local >= MOBA_BLOCK, 1, 0)
        valid = blk_row < own
        for a in range(2):
            qa = jnp.where(head_of_row == a, qt_sb, jnp.zeros_like(qt_sb))
            gate = _dot(km_hi, qa) + _dot(km_lo, qa)
            g = jnp.where(valid, gate, -jnp.inf)
            sel = jnp.zeros((nbr, sb), jnp.bool_)
            for _ in range(min(MOBA_TOPK, nb - 1)):
                cand = inrange & jnp.logical_not(sel)
                gm = jnp.max(jnp.where(cand, g, -jnp.inf), axis=0, keepdims=True)
                hit = cand & (g == gm)
                first = jnp.min(jnp.where(hit, blk_row, nb), axis=0, keepdims=True)
                sel = sel | (blk_row == first)
            chosen = (sel & valid) | (blk_row == own)
            flags = jnp.where(chosen, 0.0, NEG_BIG).astype(BF16)
            qaug_ref[i, :, a * sb:(a + 1) * sb] = jnp.concatenate([qa, flags, flag_pad], axis=0)
        return carry

    lax.fori_loop(0, nsb, build_q_aug, 0)
    scores_into(sc_ref, 0, 0)

    def q_super(i, carry):
        m_ref[...] = jnp.full(m_ref.shape, NEG_BIG, F32)
        l_ref[...] = jnp.zeros_like(l_ref)
        acc_ref[...] = jnp.zeros_like(acc_ref)
        nxt = jnp.minimum(i + 1, nsb - 1)

        @pl.when(i == 0)
        def _():
            absorb(0, sc_ref, True)

        @pl.when(i == 0)
        def _():
            scores_into(sc_ref, nxt, 0)

        @pl.when(i > 0)
        def _():
            scores_into(sa_ref, i, 1)
            absorb(0, sc_ref, False)

            def two_steps(t, c):
                scores_into(sb_ref, i, 2 * t + 2)
                absorb(2 * t + 1, sa_ref, False)
                scores_into(sa_ref, i, 2 * t + 3)
                absorb(2 * t + 2, sb_ref, False)
                return c

            lax.fori_loop(0, (i - 1) // 2, two_steps, 0)

            @pl.when(i % 2 == 1)
            def _():
                scores_into(sc_ref, nxt, 0)
                absorb(i, sa_ref, True)

            @pl.when(i % 2 == 0)
            def _():
                scores_into(sb_ref, i, i)
                absorb(i - 1, sa_ref, False)
                scores_into(sc_ref, nxt, 0)
                absorb(i, sb_ref, True)

        out_t = jnp.concatenate([acc_ref[a * hd:(a + 1) * hd, :] / l_ref[a:a + 1, :] for a in range(2)], axis=0)
        o_ref[pl.ds(pl.multiple_of(i * sb, sb), sb), :] = out_t.T.astype(BF16)
        return carry

    lax.fori_loop(0, nsb, q_super, 0)


def _moba(qt, k, vt, batch):
    t_tokens = k.shape[0]
    seq = t_tokens // batch
    assert seq % MOBA_SUPER == 0 and seq // MOBA_BLOCK <= MOBA_BLOCK_ROWS
    spec = pl.BlockSpec((seq, LANES), lambda b, hp: (b, hp))
    spec_t = pl.BlockSpec((LANES, seq), lambda b, hp: (hp, b))
    return pl.pallas_call(
        functools.partial(_moba_kernel, seq=seq),
        grid=(batch, ATT_INNER // LANES),
        in_specs=[spec_t, spec, spec_t],
        out_specs=spec,
        out_shape=jax.ShapeDtypeStruct((t_tokens, ATT_INNER), BF16),
        scratch_shapes=[pltpu.VMEM((seq, 2 * LANES), BF16),
                        pltpu.VMEM((seq // MOBA_SUPER, 2 * LANES, 2 * MOBA_SUPER), BF16),
                        pltpu.VMEM((MOBA_SUPER, 2 * MOBA_SUPER), F32),
                        pltpu.VMEM((MOBA_SUPER, 2 * MOBA_SUPER), F32),
                        pltpu.VMEM((MOBA_SUPER, 2 * MOBA_SUPER), F32),
                        pltpu.VMEM((1, 2 * MOBA_SUPER), F32),
                        pltpu.VMEM((SUBLANES, MOBA_SUPER), F32),
                        pltpu.VMEM((LANES, MOBA_SUPER), F32)],
        compiler_params=pltpu.CompilerParams(dimension_semantics=("arbitrary", "arbitrary"),
                                             vmem_limit_bytes=VMEM_LIMIT),
        name="moba",
    )(qt, k, vt)


ROUTER_ROWS = 40


def _merge_kernel(x_ref, yn_ref, att_ref, gs_ref, ga_ref, wso_ref, wao_ref, wo_ref, nw_ref,
                  wr_hi_ref, wr_lo_ref, br_ref, upper_ref, x1_ref, h2_ref, route_ref, jloc_ref, cnt_ref):
    y_s = _dot(yn_ref[...], wso_ref[...])
    y_a = _dot(att_ref[...], wao_ref[...])
    merged = _sigmoid(gs_ref[...].astype(F32)) * y_s + _sigmoid(ga_ref[...].astype(F32)) * y_a
    x1 = x_ref[...] + _dot(merged.astype(BF16), wo_ref[...])
    x1_ref[...] = x1
    h2 = _rms(x1, nw_ref[...])
    h2_ref[...] = h2.astype(BF16)

    h_hi = h2.astype(BF16)
    h_lo = (h2 - h_hi.astype(F32)).astype(BF16)
    wr_hi = wr_hi_ref[...]
    logits = _dot_nt(wr_hi, h_hi) + _dot_nt(wr_hi, h_lo) + _dot_nt(wr_lo_ref[...], h_hi) + br_ref[...]
    gl = logits[0:N_GROUPS, :]
    gidx = lax.broadcasted_iota(jnp.int32, gl.shape, 0)
    gmax = jnp.max(gl, axis=0, keepdims=True)
    gi = jnp.min(jnp.where(gl == gmax, gidx, N_GROUPS), axis=0, keepdims=True)
    g_w = 1.0 / jnp.sum(jnp.exp(gl - gmax), axis=0, keepdims=True)
    el = jnp.zeros((EXPERTS_PER_GROUP, gl.shape[1]), F32)
    for g in range(N_GROUPS):
        lo = N_GROUPS + g * EXPERTS_PER_GROUP
        el = el + jnp.where(gi == g, logits[lo:lo + EXPERTS_PER_GROUP, :], 0.0)
    eidx = lax.broadcasted_iota(jnp.int32, el.shape, 0)
    v0 = jnp.max(el, axis=0, keepdims=True)
    i0 = jnp.min(jnp.where(el == v0, eidx, EXPERTS_PER_GROUP), axis=0, keepdims=True)
    rest = jnp.where(eidx == i0, -jnp.inf, el)
    v1 = jnp.max(rest, axis=0, keepdims=True)
    i1 = jnp.min(jnp.where((rest == v1) & (eidx != i0), eidx, EXPERTS_PER_GROUP), axis=0, keepdims=True)
    t = jnp.exp(v1 - v0)
    w0 = g_w / (1.0 + t)
    w1 = g_w * t / (1.0 + t)
    base = gi * EXPERTS_PER_GROUP
    route_ref[...] = jnp.concatenate(
        [(base + i0).astype(F32), (base + i1).astype(F32), w0, w1], axis=0)

    tm = gl.shape[1]
    eids = jnp.concatenate([base + i0, base + i1], axis=1)
    onehot = lax.broadcasted_iota(jnp.int32, (N_EXPERTS, 2 * tm), 0) == eids
    csum = _dot(onehot.astype(BF16), upper_ref[...])
    oh = onehot.astype(F32)
    lrank = jnp.sum(csum * oh, axis=0, keepdims=True) - 1.0
    cnt = jnp.floor((csum[:, 2 * tm - 1:2 * tm] + (RUN_ALIGN - 1)) * (1.0 / RUN_ALIGN))
    er = lax.broadcasted_iota(jnp.int32, (N_EXPERTS, N_EXPERTS), 0)
    ec = lax.broadcasted_iota(jnp.int32, (N_EXPERTS, N_EXPERTS), 1)
    before = _dot((ec < er).astype(BF16), jnp.broadcast_to(cnt, (N_EXPERTS, LANES)).astype(BF16))
    off = before[:, 0:1] * RUN_ALIGN
    jloc = jnp.sum(oh * off, axis=0, keepdims=True) + lrank
    jloc_ref[...] = jnp.concatenate([jloc[:, :tm], jloc[:, tm:]], axis=0).astype(jnp.int32)
    cnt_ref[...] = jnp.broadcast_to(cnt * RUN_ALIGN, (N_EXPERTS, LANES))


def _merge(x2d, yn, att, gs, ga, wso, wao, wo, nw, wr_hi, wr_lo, br):
    t_tokens = x2d.shape[0]
    tm = TM_TOK
    upper = jnp.triu(jnp.ones((TILE_ASG, TILE_ASG), BF16))
    row = lambda: pl.BlockSpec((tm, D_MODEL), lambda i: (i, 0))
    whole = pl.BlockSpec(memory_space=pltpu.VMEM)
    return pl.pallas_call(
        _merge_kernel,
        grid=(t_tokens // tm,),
        in_specs=[row(), row(), row(), row(), row(), whole, whole, whole, whole, whole, whole, whole, whole],
        out_specs=[row(), row(), pl.BlockSpec((4, tm), lambda i: (0, i)), pl.BlockSpec((2, tm), lambda i: (0, i)),
                   pl.BlockSpec((N_EXPERTS, LANES), lambda i: (i, 0))],
        out_shape=[jax.ShapeDtypeStruct((t_tokens, D_MODEL), F32),
                   jax.ShapeDtypeStruct((t_tokens, D_MODEL), BF16),
                   jax.ShapeDtypeStruct((4, t_tokens), F32),
                   jax.ShapeDtypeStruct((2, t_tokens), jnp.int32),
                   jax.ShapeDtypeStruct((t_tokens // tm * N_EXPERTS, LANES), F32)],
        compiler_params=pltpu.CompilerParams(dimension_semantics=("arbitrary",), vmem_limit_bytes=VMEM_LIMIT),
        name="merge",
    )(x2d, yn, att, gs, ga, wso, wao, wo, nw, wr_hi, wr_lo, br, upper)


def _rows(ref, first, n_rows):
    start = first if isinstance(first, int) else pl.multiple_of(first, RUN_ALIGN)
    return ref.at[pl.ds(start, n_rows), :]


def _for_each_piece(count, fn, bits):
    for bit in range(bits - 1, ALIGN_BIT - 1, -1):
        size = 1 << bit

        @pl.when((count & size) != 0)
        def _():
            fn((count >> (bit + 1)) << (bit + 1), size)


def _run_copies(glob_ref, tot_ref, tile, unit_copy):
    def unit(v, carry):
        unit_copy(v * RUN_ALIGN, glob_ref[tile * UNITS + v]).start()
        return carry
    lax.fori_loop(0, tot_ref[tile] >> ALIGN_BIT, unit, 0)


def _wait_run_copies(tot_ref, tile, src, dst, sem):
    pltpu.make_async_copy(_rows(src, 0, TILE_ASG), _rows(dst, 0, TILE_ASG), sem).wait()

    def piece(o, size):
        pltpu.make_async_copy(_rows(src, 0, size), _rows(dst, 0, size), sem).wait()
    _for_each_piece(tot_ref[tile] - TILE_ASG, piece, EXTRA_BITS)


def _moe_max_tiles(t_tokens):
    rows = 2 * t_tokens + (t_tokens // TM_TOK) * N_EXPERTS * (RUN_ALIGN - 1)
    return -(-rows // TM_MOE) + N_EXPERTS


def _moe_plan(cnt_rows, max_tiles):
    cnt = cnt_rows[:, 0].astype(jnp.int32).reshape(-1, N_EXPERTS)
    off_end = jnp.cumsum(cnt, axis=1)
    counts = jnp.sum(cnt, axis=0)
    padded = ((counts + TM_MOE - 1) // TM_MOE) * TM_MOE
    ends = jnp.cumsum(padded)
    starts = ends - padded
    base = starts[None, :] + jnp.cumsum(cnt, axis=0) - cnt
    loc = jnp.arange(UNITS, dtype=jnp.int32) * RUN_ALIGN
    run = jnp.sum((loc[None, :, None] >= off_end[:, None, :]).astype(jnp.int32), axis=2)
    run = jnp.minimum(run, N_EXPERTS - 1)
    pick = (run[:, :, None] == jnp.arange(N_EXPERTS, dtype=jnp.int32)[None, None, :]).astype(jnp.int32)
    shift = jnp.sum(pick * (base - (off_end - cnt))[:, None, :], axis=2)
    glob = loc[None, :] + shift
    tile_start = jnp.arange(max_tiles, dtype=jnp.int32) * TM_MOE
    tile_expert = jnp.sum((tile_start[:, None] >= ends[None, :]).astype(jnp.int32), axis=1)
    tile_expert = jnp.minimum(tile_expert, N_EXPERTS - 1)
    n_tiles = (ends[-1] // TM_MOE).astype(jnp.int32).reshape(1)
    flat = lambda a: a.reshape(-1).astype(jnp.int32)
    return flat(glob), flat(off_end[:, -1]), starts + counts, ends, tile_expert, n_tiles


def _dispatch_kernel(glob_ref, tot_ref, padlo_ref, padhi_ref, ntile_ref, h2_ref, jrow_ref, xs_hbm,
                     xbuf0, xbuf1, zeros_ref, sems, zsem, *, max_tiles):
    i = pl.program_id(0)
    n = pl.num_programs(0)
    xbufs = (xbuf0, xbuf1)

    def wait_runs(tile, slot):
        _wait_run_copies(tot_ref, tile, xbufs[slot], xs_hbm, sems.at[slot])

    @pl.when(i == 0)
    def _():
        zeros_ref[...] = jnp.zeros_like(zeros_ref)
        nt = ntile_ref[0]
        for wait in (False, True):
            def pad(e, carry):
                lo = padlo_ref[e]

                def piece(o, size):
                    cp = pltpu.make_async_copy(_rows(zeros_ref, 0, size), _rows(xs_hbm, lo + o, size), zsem)
                    cp.wait() if wait else cp.start()
                _for_each_piece(padhi_ref[e] - lo, piece, PAD_BITS)
                return carry
            lax.fori_loop(0, N_EXPERTS, pad, 0)

            def spare(tile, carry):
                cp = pltpu.make_async_copy(zeros_ref, _rows(xs_hbm, tile * TM_MOE, TM_MOE), zsem)
                cp.wait() if wait else cp.start()
                return carry
            lax.fori_loop(nt, max_tiles, spare, 0)

    for slot in range(2):
        @pl.when(i % 2 == slot)
        def _():
            @pl.when(i >= 2)
            def _():
                wait_runs(i - 2, slot)
            row = lax.broadcasted_iota(jnp.int32, (LOC_ROWS, TM_TOK), 0)
            j = jrow_ref[...]
            perm = ((row == j[0:1, :]) | (row == j[1:2, :])).astype(BF16)
            xbufs[slot][...] = _dot(perm, h2_ref[...]).astype(BF16)
            _run_copies(glob_ref, tot_ref, i, lambda loc, g: pltpu.make_async_copy(
                _rows(xbufs[slot], loc, RUN_ALIGN), _rows(xs_hbm, g, RUN_ALIGN), sems.at[slot]))

            @pl.when(i == n - 1)
            def _():
                @pl.when(i >= 1)
                def _():
                    wait_runs(i - 1, 1 - slot)
                wait_runs(i, slot)


def _dispatch(plan, jloc, h2, max_tiles):
    glob, tot, pad_lo, pad_hi, _, n_tiles = plan
    t_tokens = h2.shape[0]
    grid_spec = pltpu.PrefetchScalarGridSpec(
        num_scalar_prefetch=5,
        grid=(t_tokens // TM_TOK,),
        in_specs=[pl.BlockSpec((TM_TOK, D_MODEL), lambda i, *_: (i, 0)),
                  pl.BlockSpec((2, TM_TOK), lambda i, *_: (0, i))],
        out_specs=pl.BlockSpec(memory_space=pl.ANY),
        scratch_shapes=[pltpu.VMEM((LOC_ROWS, D_MODEL), BF16),
                        pltpu.VMEM((LOC_ROWS, D_MODEL), BF16),
                        pltpu.VMEM((TM_MOE, D_MODEL), BF16),
                        pltpu.SemaphoreType.DMA((2,)), pltpu.SemaphoreType.DMA(())],
    )
    return pl.pallas_call(
        functools.partial(_dispatch_kernel, max_tiles=max_tiles),
        grid_spec=grid_spec,
        out_shape=jax.ShapeDtypeStruct((max_tiles * TM_MOE, D_MODEL), BF16),
        compiler_params=pltpu.CompilerParams(dimension_semantics=("arbitrary",), vmem_limit_bytes=VMEM_LIMIT),
        name="dispatch",
    )(glob, tot, pad_lo, pad_hi, n_tiles, h2, jloc)


def _moe_kernel(texp_ref, ntile_ref, xs_ref, wg_ref, wu_ref, wd_ref, y_ref, wgu_s, wd_s):
    t = pl.program_id(0)
    live = t < ntile_ref[0]

    @pl.when(live & ((t == 0) | (texp_ref[t] != texp_ref[jnp.maximum(t - 1, 0)])))
    def _():
        wgu_s[:, 0:D_EXPERT] = wg_ref[0].astype(BF16)
        wgu_s[:, D_EXPERT:2 * D_EXPERT] = wu_ref[0].astype(BF16)
        wd_s[...] = wd_ref[0].astype(BF16)

    @pl.when(live)
    def _():
        gu = _dot(xs_ref[...], wgu_s[...])
        hid = (_silu(gu[:, :D_EXPERT]) * gu[:, D_EXPERT:]).astype(BF16)
        y_ref[...] = _dot(hid, wd_s[...]).astype(BF16)

    @pl.when(jnp.logical_not(live))
    def _():
        y_ref[...] = jnp.zeros_like(y_ref)


def _moe(xs, tile_expert, n_tiles, w_gate, w_up, w_down, max_tiles):
    live = lambda t, nt: jnp.minimum(t, nt[0] - 1)
    tile = lambda index: pl.BlockSpec((TM_MOE, D_MODEL), index)
    expert = lambda shape: pl.BlockSpec((1,) + shape, lambda t, te, nt: (te[live(t, nt)], 0, 0))
    grid_spec = pltpu.PrefetchScalarGridSpec(
        num_scalar_prefetch=2,
        grid=(max_tiles,),
        in_specs=[tile(lambda t, te, nt: (live(t, nt), 0)),
                  expert((D_MODEL, D_EXPERT)), expert((D_MODEL, D_EXPERT)), expert((D_EXPERT, D_MODEL))],
        out_specs=tile(lambda t, te, nt: (t, 0)),
        scratch_shapes=[pltpu.VMEM((D_MODEL, 2 * D_EXPERT), BF16), pltpu.VMEM((D_EXPERT, D_MODEL), BF16)],
    )
    return pl.pallas_call(
        _moe_kernel,
        grid_spec=grid_spec,
        out_shape=jax.ShapeDtypeStruct((max_tiles * TM_MOE, D_MODEL), BF16),
        compiler_params=pltpu.CompilerParams(dimension_semantics=("arbitrary",), vmem_limit_bytes=VMEM_LIMIT),
        name="moe",
    )(tile_expert, n_tiles, xs, w_gate, w_up, w_down)


def _tail_kernel(glob_ref, tot_ref, x1_ref, p_ref, jw_ref, y_hbm, pnw_ref, wpg_ref, wple_ref,
                 fnw_ref, o_ref, ybuf0, ybuf1, sems):
    i = pl.program_id(0)
    n = pl.num_programs(0)
    ybufs = (ybuf0, ybuf1)

    def fetch_runs(tile, slot):
        _run_copies(glob_ref, tot_ref, tile, lambda loc, g: pltpu.make_async_copy(
            _rows(y_hbm, g, RUN_ALIGN), _rows(ybufs[slot], loc, RUN_ALIGN), sems.at[slot]))

    @pl.when(i == 0)
    def _():
        ybuf0[...] = jnp.zeros_like(ybuf0)
        ybuf1[...] = jnp.zeros_like(ybuf1)
        fetch_runs(0, 0)

    for slot in range(2):
        @pl.when(i % 2 == slot)
        def _():
            @pl.when(i + 1 < n)
            def _():
                fetch_runs(i + 1, 1 - slot)
            _wait_run_copies(tot_ref, i, y_hbm, ybufs[slot], sems.at[slot])
            y_loc = ybufs[slot][...]
            jw = jw_ref[...]
            col = lax.broadcasted_iota(jnp.int32, (TM_TOK, LOC_ROWS), 1)
            pick = sum(jnp.where(col == jw[:, s:s + 1].astype(jnp.int32), jw[:, 2 + s:3 + s], 0.0) for s in range(2))
            moe = _dot(pick.astype(BF16), y_loc)
            x2 = x1_ref[...] + moe
            gate = _sigmoid(_dot(_rms(x2, pnw_ref[...]).astype(BF16), wpg_ref[...]))
            x3 = x2 + gate * _dot(p_ref[...].astype(BF16), wple_ref[...])
            o_ref[...] = _rms(x3, fnw_ref[...])


def _tail(plan, x1, p2d, y_sorted, jw, pnw, wpg, wple, fnw):
    glob, tot = plan[0:2]
    t_tokens = x1.shape[0]
    whole = pl.BlockSpec(memory_space=pltpu.VMEM)
    grid_spec = pltpu.PrefetchScalarGridSpec(
        num_scalar_prefetch=2,
        grid=(t_tokens // TM_TOK,),
        in_specs=[pl.BlockSpec((TM_TOK, D_MODEL), lambda i, *_: (i, 0)),
                  pl.BlockSpec((TM_TOK, PLE_DIM), lambda i, *_: (i, 0)),
                  pl.BlockSpec((TM_TOK, 4), lambda i, *_: (i, 0)),
                  pl.BlockSpec(memory_space=pl.ANY),
                  whole, whole, whole, whole],
        out_specs=pl.BlockSpec((TM_TOK, D_MODEL), lambda i, *_: (i, 0)),
        scratch_shapes=[pltpu.VMEM((LOC_ROWS, D_MODEL), BF16),
                        pltpu.VMEM((LOC_ROWS, D_MODEL), BF16),
                        pltpu.SemaphoreType.DMA((2,))],
    )
    return pl.pallas_call(
        _tail_kernel,
        grid_spec=grid_spec,
        out_shape=jax.ShapeDtypeStruct((t_tokens, D_MODEL), F32),
        compiler_params=pltpu.CompilerParams(dimension_semantics=("arbitrary",), vmem_limit_bytes=VMEM_LIMIT),
        name="tail",
    )(glob, tot, x1, p2d, jw, y_sorted, pnw, wpg, wple, fnw)


def _rope_tables(positions):
    half = ATT_HEAD_DIM // 2
    inv_freq = ROPE_THETA ** (-jnp.arange(half, dtype=F32) / half)
    ang = positions.astype(F32)[..., None] * inv_freq
    return jnp.cos(ang).reshape(-1, half).T, jnp.sin(ang).reshape(-1, half).T


def _layer(x2d, p2d, rope_tables, batch, attn_norm_w, w_in, conv_w, conv_b, dt_bias, a_log, d_skip, ssd_norm_w,
           w_ssd_out, w_attn_out, w_out, moe_norm_w, w_rg, b_rg, w_re, b_re, w_gate, w_up, w_down,
           ple_norm_w, w_ple, w_ple_gate, final_norm_w):
    t_tokens = x2d.shape[0]
    offs = np.cumsum((0,) + IN_SPLITS)
    cols = lambda i: w_in[:, offs[i]:offs[i + 1]]
    w_big = jnp.concatenate([cols(0), cols(1), cols(4), cols(6), cols(7)], axis=1).astype(BF16)
    w_t = jnp.concatenate([cols(3), cols(5), cols(2)], axis=1).T.astype(BF16)
    w_dt = jnp.pad(cols(2), ((0, 0), (0, LANES - SSD_HEADS))).astype(BF16)

    z, xbc, k, gs, ga, dt, qt, vt, dtt = _in_proj(x2d, attn_norm_w[None, :], *rope_tables, w_big, w_t, w_dt)

    yn = _ssd(z, xbc, dt, dtt, conv_w, conv_b, dt_bias, a_log, d_skip, ssd_norm_w, batch)

    att = _moba(qt, k, vt, batch)

    wr = jnp.concatenate([w_rg, w_re], axis=1).T
    wr = jnp.pad(wr, ((0, ROUTER_ROWS - wr.shape[0]), (0, 0)))
    wr_hi = wr.astype(BF16)
    wr_lo = (wr - wr_hi.astype(F32)).astype(BF16)
    br = jnp.pad(jnp.concatenate([b_rg, b_re]), (0, ROUTER_ROWS - N_GROUPS - N_EXPERTS))[:, None]
    x1, h2, route, jloc, cnt_rows = _merge(x2d, yn, att, gs, ga, w_ssd_out.astype(BF16), w_attn_out.astype(BF16),
                           w_out.astype(BF16), moe_norm_w[None, :], wr_hi, wr_lo, br)

    max_tiles = _moe_max_tiles(t_tokens)
    plan = _moe_plan(cnt_rows, max_tiles)
    x_sorted = _dispatch(plan, jloc, h2, max_tiles)
    y_sorted = _moe(x_sorted, plan[4], plan[5], w_gate.reshape(N_EXPERTS, D_MODEL, D_EXPERT),
                    w_up.reshape(N_EXPERTS, D_MODEL, D_EXPERT), w_down.reshape(N_EXPERTS, D_EXPERT, D_MODEL), max_tiles)
    jw = jnp.concatenate([jloc.astype(F32), route[2:4]], axis=0).T

    return _tail(plan, x1, p2d, y_sorted, jw, ple_norm_w[None, :],
                 w_ple_gate.astype(BF16), w_ple.astype(BF16), final_norm_w[None, :])


def kernel(x, p, positions, attn_norm_w, w_in, conv_w, conv_b, dt_bias, a_log, d_skip, ssd_norm_w, w_ssd_out,
           w_attn_out, w_out, moe_norm_w, w_router_group, b_router_group, w_router_expert, b_router_expert,
           w_exp_gate, w_exp_up, w_exp_down, ple_norm_w, w_ple, w_ple_gate, final_norm_w):
    batch, seq, d = x.shape
    depth = p.shape[0]
    assert depth == 1, "the final norm is fused into the last layer's tail kernel"
    rope_tables = _rope_tables(positions)
    i = 0
    out = _layer(x.reshape(batch * seq, d), p[i].reshape(batch * seq, PLE_DIM), rope_tables, batch,
                 attn_norm_w[i], w_in[i], conv_w[i], conv_b[i], dt_bias[i], a_log[i], d_skip[i], ssd_norm_w[i],
                 w_ssd_out[i], w_attn_out[i], w_out[i], moe_norm_w[i], w_router_group[i], b_router_group[i],
                 w_router_expert[i], b_router_expert[i], w_exp_gate[i], w_exp_up[i], w_exp_down[i],
                 ple_norm_w[i], w_ple[i], w_ple_gate[i], final_norm_w)
    return out.reshape(batch, seq, d)
```

```python
import functools

import jax
import jax.numpy as jnp
import numpy as np
from jax import lax
from jax.experimental import pallas as pl
from jax.experimental.pallas import tpu as pltpu

F32 = jnp.float32
BF16 = jnp.bfloat16

EPS = 1e-6
D_MODEL = 1024
SSD_HEADS = 16
SSD_HEAD_DIM = 64
SSD_INNER = SSD_HEADS * SSD_HEAD_DIM
SSD_GROUPS = 2
SSD_STATE = 64
SSD_CONV = 4
SSD_CHUNK = 128
SSD_CONV_CH = SSD_INNER + 2 * SSD_GROUPS * SSD_STATE
CONV_TAIL = 16
ATT_HEADS = 16
ATT_HEAD_DIM = 64
ATT_INNER = ATT_HEADS * ATT_HEAD_DIM
MOBA_BLOCK = 256
MOBA_TOPK = 3
ROPE_THETA = 10000.0
IN_SPLITS = (SSD_INNER, SSD_CONV_CH, SSD_HEADS, ATT_INNER, ATT_INNER, ATT_INNER, D_MODEL, D_MODEL)
N_GROUPS = 4
EXPERTS_PER_GROUP = 8
N_EXPERTS = N_GROUPS * EXPERTS_PER_GROUP
D_EXPERT = 256
PLE_DIM = 256

LANES = 128
SUBLANES = 8
NEG_BIG = -1e30
LOG2E = 1.4426950408889634
VMEM_LIMIT = 56 * 1024 * 1024

TM_PROJ = 512
TM_MOE = 512
TM_TOK = 512
TILE_ASG = 2 * TM_TOK
RUN_ALIGN = 2 * SUBLANES
LOC_ROWS = TILE_ASG + N_EXPERTS * RUN_ALIGN
ALIGN_BIT = RUN_ALIGN.bit_length() - 1
PAD_BITS = TM_MOE.bit_length() - 1
EXTRA_BITS = (LOC_ROWS - TILE_ASG).bit_length()
UNITS = LOC_ROWS // RUN_ALIGN


def _dot(a, b):
    return jnp.dot(a, b, preferred_element_type=F32)


def _dot_nt(a, b):
    return lax.dot_general(a, b, (((1,), (1,)), ((), ())), preferred_element_type=F32)


def _dot_tn(a, b):
    return lax.dot_general(a, b, (((0,), (0,)), ((), ())), preferred_element_type=F32)


def _split3(a):
    hi = a.astype(BF16)
    r1 = a - hi.astype(F32)
    mid = r1.astype(BF16)
    lo = (r1 - mid.astype(F32)).astype(BF16)
    return hi, mid, lo


def _dot_exact_rhs(sel_bf16, a):
    return _dot(jnp.concatenate([sel_bf16] * 3, axis=1), jnp.concatenate(_split3(a), axis=0))


def _dot_exact_tn(a, sel_bf16):
    return _dot_tn(jnp.concatenate(_split3(a), axis=0), jnp.concatenate([sel_bf16] * 3, axis=0))


def _dot_exact_lhs(a, sel_bf16):
    return _dot(jnp.concatenate(_split3(a), axis=1), jnp.concatenate([sel_bf16] * 3, axis=0))


def _sigmoid(x):
    return 0.5 * jnp.tanh(0.5 * x) + 0.5


def _silu(x):
    return x * _sigmoid(x)


def _softplus(x):
    return jnp.maximum(x, 0.0) + jnp.log1p(jnp.exp(-jnp.abs(x)))


def _rms(x, w):
    inv = lax.rsqrt(jnp.mean(x * x, axis=-1, keepdims=True) + EPS)
    return (x * inv) * w


def _rope_tile(t, cos, sin_signed):
    half = ATT_HEAD_DIM // 2
    lane = lax.broadcasted_iota(jnp.int32, t.shape, 1)
    first = (lane % ATT_HEAD_DIM) < half
    partner = jnp.where(first, pltpu.roll(t, LANES - half, 1), pltpu.roll(t, half, 1))
    return t * cos + partner * sin_signed


Q_SCALE = ATT_HEAD_DIM ** -0.5 * LOG2E


def _in_proj_kernel(x_ref, nw_ref, cost_ref, sint_ref, w_ref, wt_ref, wdt_ref,
                    z_ref, xbc_ref, k_ref, gs_ref, ga_ref, dt_ref, qt_ref, vt_ref, dtt_ref):
    h = _rms(x_ref[...], nw_ref[...]).astype(BF16)
    half = ATT_HEAD_DIM // 2
    lane = lax.broadcasted_iota(jnp.int32, (half, LANES), 1)
    same = lane % half == lax.broadcasted_iota(jnp.int32, (half, LANES), 0)
    first = lane % ATT_HEAD_DIM < half
    cost = cost_ref[...]
    sint = sint_ref[...]
    cos = _dot_exact_tn(cost, same.astype(BF16))
    sin = _dot_exact_tn(sint, jnp.where(same, jnp.where(first, -1.0, 1.0), 0.0).astype(BF16))

    def proj(lo, width):
        return _dot(h, w_ref[:, lo:lo + width])

    o = 0
    z_ref[...] = proj(o, SSD_INNER).astype(BF16)
    o += SSD_INNER
    xbc_ref[...] = proj(o, SSD_CONV_CH).astype(BF16)
    o += SSD_CONV_CH
    k_lin = proj(o, ATT_INNER)
    for g in range(ATT_INNER // LANES):
        cols = slice(g * LANES, (g + 1) * LANES)
        k_ref[:, cols] = _rope_tile(k_lin[:, cols], cos, sin).astype(BF16)
    o += ATT_INNER
    for ref in (gs_ref, ga_ref):
        ref[...] = proj(o, D_MODEL).astype(BF16)
        o += D_MODEL
    dt_ref[...] = _dot(h, wdt_ref[...])[:, :SSD_HEADS]

    dtt_ref[...] = _dot_nt(wt_ref[2 * ATT_INNER:2 * ATT_INNER + SSD_HEADS, :], h)
    vt_ref[...] = _dot_nt(wt_ref[ATT_INNER:2 * ATT_INNER, :], h).astype(BF16)
    qt = _dot_nt(wt_ref[0:ATT_INNER, :], h)
    for hd in range(ATT_HEADS):
        r0 = hd * ATT_HEAD_DIM
        x1 = qt[r0:r0 + half, :]
        x2 = qt[r0 + half:r0 + ATT_HEAD_DIM, :]
        qt_ref[r0:r0 + half, :] = ((x1 * cost - x2 * sint) * Q_SCALE).astype(BF16)
        qt_ref[r0 + half:r0 + ATT_HEAD_DIM, :] = ((x2 * cost + x1 * sint) * Q_SCALE).astype(BF16)


def _in_proj(x2d, nw, cost, sint, w_big, w_t, w_dt):
    t_tokens = x2d.shape[0]
    tm = min(TM_PROJ, t_tokens)
    grid = (t_tokens // tm,)
    row = lambda width: pl.BlockSpec((tm, width), lambda i: (i, 0))
    col = lambda height: pl.BlockSpec((height, tm), lambda i: (0, i))
    whole = pl.BlockSpec(memory_space=pltpu.VMEM)
    out_shapes = [
        jax.ShapeDtypeStruct((t_tokens, SSD_INNER), BF16),
        jax.ShapeDtypeStruct((t_tokens, SSD_CONV_CH), BF16),
        jax.ShapeDtypeStruct((t_tokens, ATT_INNER), BF16),
        jax.ShapeDtypeStruct((t_tokens, D_MODEL), BF16),
        jax.ShapeDtypeStruct((t_tokens, D_MODEL), BF16),
        jax.ShapeDtypeStruct((t_tokens, SSD_HEADS), F32),
        jax.ShapeDtypeStruct((ATT_INNER, t_tokens), BF16),
        jax.ShapeDtypeStruct((ATT_INNER, t_tokens), BF16),
        jax.ShapeDtypeStruct((SSD_HEADS, t_tokens), F32),
    ]
    out_specs = [row(SSD_INNER), row(SSD_CONV_CH), row(ATT_INNER), row(D_MODEL), row(D_MODEL), row(SSD_HEADS),
                 col(ATT_INNER), col(ATT_INNER), col(SSD_HEADS)]
    return pl.pallas_call(
        _in_proj_kernel,
        grid=grid,
        in_specs=[row(D_MODEL), whole, col(ATT_HEAD_DIM // 2), col(ATT_HEAD_DIM // 2),
                  whole, whole, whole],
        out_specs=out_specs,
        out_shape=out_shapes,
        compiler_params=pltpu.CompilerParams(dimension_semantics=("arbitrary",), vmem_limit_bytes=VMEM_LIMIT),
        name="in_proj",
    )(x2d, nw, cost, sint, w_big, w_t, w_dt)


def _ssd_kernel(z_ref, xbc_ref, xtail_ref, dt_ref, dtt_ref, cw_ref, cb_ref, dtb_ref, dtbc_ref, alog_ref, alogc_ref,
                dexp_ref, nw_ref, out_ref, state_ref):
    c = pl.program_id(1)
    cl = SSD_CHUNK
    heads_per_group = SSD_HEADS // SSD_GROUPS
    gw = heads_per_group * SSD_HEAD_DIM

    @pl.when(c == 0)
    def _():
        state_ref[...] = jnp.zeros_like(state_ref)

    xbc = xbc_ref[...]
    tail = xtail_ref[...]
    tail = jnp.where(c == 0, jnp.zeros_like(tail), tail)
    u_ext = jnp.concatenate([tail, xbc], axis=0)
    n_shift = SSD_CONV - 1
    srow = lax.broadcasted_iota(jnp.int32, (n_shift * cl, CONV_TAIL + cl), 0)
    scol = lax.broadcasted_iota(jnp.int32, (n_shift * cl, CONV_TAIL + cl), 1)
    shift_sel = (scol == CONV_TAIL + srow % cl - (n_shift - srow // cl)).astype(BF16)
    shifted = _dot(shift_sel, u_ext)
    acc = cb_ref[...] + cw_ref[n_shift:SSD_CONV, :] * xbc.astype(F32)
    for k in range(n_shift):
        acc = acc + cw_ref[k:k + 1, :] * shifted[k * cl:(k + 1) * cl, :]
    xc = _silu(acc)
    xs = xc[:, :SSD_INNER]
    bm = xc[:, SSD_INNER:SSD_INNER + SSD_GROUPS * SSD_STATE].astype(BF16)
    cm = xc[:, SSD_INNER + SSD_GROUPS * SSD_STATE:].astype(BF16)

    dt = _softplus(dt_ref[...] + dtb_ref[...])
    da = dt * (-jnp.exp(alog_ref[...]))
    dtt = _softplus(dtt_ref[...] + dtbc_ref[...])
    dat = dtt * (-jnp.exp(alogc_ref[...]))
    ri = lax.broadcasted_iota(jnp.int32, (cl, cl), 0)
    ci = lax.broadcasted_iota(jnp.int32, (cl, cl), 1)
    causal = ri >= ci
    tril = causal.astype(BF16)
    triu = (ri <= ci).astype(BF16)
    acum = _dot_exact_rhs(tril, da)
    acumt = _dot_exact_lhs(dat, triu)

    hrow = lax.broadcasted_iota(jnp.int32, (SSD_HEADS, SSD_INNER), 0)
    hcol = lax.broadcasted_iota(jnp.int32, (SSD_HEADS, SSD_INNER), 1) // SSD_HEAD_DIM
    expand = (hrow == hcol).astype(BF16)

    a_last = acum[cl - 1:cl, :]
    dt_e = _dot_exact_lhs(dt, expand)
    dec_e = _dot_exact_lhs(jnp.exp(a_last - acum), expand)
    ea_e = _dot_exact_lhs(jnp.exp(acum), expand)
    xdt = xs * dt_e
    xdt_b = xdt.astype(BF16)
    xdtdec_b = (xdt * dec_e).astype(BF16)
    chunk_decay_t = jnp.exp(acumt[:, cl - 1:cl])

    y_groups = []
    for g in range(SSD_GROUPS):
        b_g = bm[:, g * SSD_STATE:(g + 1) * SSD_STATE]
        c_g = cm[:, g * SSD_STATE:(g + 1) * SSD_STATE]
        prev = state_ref[g * gw:(g + 1) * gw, :]
        cb = _dot_nt(c_g, b_g)
        y_off = _dot_nt(c_g, prev.astype(BF16)) * ea_e[:, g * gw:(g + 1) * gw]
        new_states = _dot_tn(xdtdec_b[:, g * gw:(g + 1) * gw], b_g)
        y_heads = []
        decay_rows = []
        for e in range(heads_per_group):
            hd = g * heads_per_group + e
            lmat = jnp.where(causal, jnp.exp(acum[:, hd:hd + 1] - acumt[hd:hd + 1, :]), 0.0)
            m = (cb * lmat).astype(BF16)
            y_heads.append(_dot(m, xdt_b[:, hd * SSD_HEAD_DIM:(hd + 1) * SSD_HEAD_DIM]))
            decay_rows.append(jnp.broadcast_to(chunk_decay_t[hd:hd + 1, :], (SSD_HEAD_DIM, SSD_STATE)))
        y_groups.append(jnp.concatenate(y_heads, axis=1) + y_off)
        state_ref[g * gw:(g + 1) * gw, :] = prev * jnp.concatenate(decay_rows, axis=0) + new_states

    y = jnp.concatenate(y_groups, axis=1) + xs * dexp_ref[...]
    yg = y * _silu(z_ref[...].astype(F32))
    nw = nw_ref[...]
    outs = [_rms(yg[:, g * gw:(g + 1) * gw], nw[:, g * gw:(g + 1) * gw]) for g in range(SSD_GROUPS)]
    out_ref[...] = jnp.concatenate(outs, axis=1).astype(BF16)


def _ssd(z, xbc, dt, dtt, conv_w, conv_b, dt_bias, a_log, d_skip, norm_w, batch):
    t_tokens = z.shape[0]
    nc = t_tokens // batch // SSD_CHUNK
    tok = lambda width: pl.BlockSpec((SSD_CHUNK, width), lambda b, c: (b * nc + c, 0))
    whole = pl.BlockSpec(memory_space=pltpu.VMEM)
    dexp = jnp.repeat(d_skip, SSD_HEAD_DIM)[None, :]
    tails_per_chunk = SSD_CHUNK // CONV_TAIL
    prev_tail = pl.BlockSpec((CONV_TAIL, SSD_CONV_CH),
                             lambda b, c: (jnp.maximum((b * nc + c) * tails_per_chunk - 1, 0), 0))
    return pl.pallas_call(
        _ssd_kernel,
        grid=(batch, nc),
        in_specs=[tok(SSD_INNER), tok(SSD_CONV_CH), prev_tail, tok(SSD_HEADS),
                  pl.BlockSpec((SSD_HEADS, SSD_CHUNK), lambda b, c: (0, b * nc + c)),
                  whole, whole, whole, whole, whole, whole, whole, whole],
        out_specs=tok(SSD_INNER),
        out_shape=jax.ShapeDtypeStruct((t_tokens, SSD_INNER), BF16),
        scratch_shapes=[pltpu.VMEM((SSD_INNER, SSD_STATE), F32)],
        compiler_params=pltpu.CompilerParams(dimension_semantics=("arbitrary", "arbitrary"),
                                             vmem_limit_bytes=VMEM_LIMIT),
        name="ssd",
    )(z, xbc, xbc, dt, dtt, conv_w, conv_b[None, :], dt_bias[None, :], dt_bias[:, None],
      a_log[None, :], a_log[:, None], dexp, norm_w[None, :])


MOBA_SUPER = 2 * MOBA_BLOCK


MOBA_BLOCK_ROWS = 16
ONES_ROWS = 16


def _moba_kernel(qt_ref, k_ref, vt_ref, o_ref, kfull_ref, qaug_ref, sa_ref, sb_ref, sc_ref, m_ref, l_ref, acc_ref,
                 *, seq):
    nb = seq // MOBA_BLOCK
    sb = MOBA_SUPER
    nsb = seq // sb
    hd = ATT_HEAD_DIM
    nbr = MOBA_BLOCK_ROWS

    for c in range(nsb):
        rows = slice(c * sb, (c + 1) * sb)
        kfull_ref[rows, 0:LANES] = k_ref[rows, :]
        rblk = lax.broadcasted_iota(jnp.int32, (sb, LANES), 0) // MOBA_BLOCK + c * (sb // MOBA_BLOCK)
        rlane = lax.broadcasted_iota(jnp.int32, (sb, LANES), 1)
        kfull_ref[rows, LANES:2 * LANES] = jnp.where(rblk == rlane, 1.0, 0.0).astype(BF16)

    pr = lax.broadcasted_iota(jnp.int32, (nbr, seq), 0)
    pc = lax.broadcasted_iota(jnp.int32, (nbr, seq), 1) // MOBA_BLOCK
    pool = jnp.where(pr == pc, 1.0 / MOBA_BLOCK, 0.0).astype(BF16)
    kmean = _dot(pool, k_ref[...])
    km_hi = kmean.astype(BF16)
    km_lo = (kmean - km_hi.astype(F32)).astype(BF16)

    head_of_row = lax.broadcasted_iota(jnp.int32, (LANES, sb), 0) // hd
    blk_row = lax.broadcasted_iota(jnp.int32, (nbr, sb), 0)
    q_local = lax.broadcasted_iota(jnp.int32, (nbr, sb), 1)
    inrange = blk_row < nb
    krow = lax.broadcasted_iota(jnp.int32, (sb, 2 * sb), 0)
    qcol = lax.broadcasted_iota(jnp.int32, (sb, 2 * sb), 1) % sb
    causal = krow <= qcol
    ones_rows = jnp.ones((ONES_ROWS, sb), BF16)
    flag_pad = jnp.zeros((LANES - nbr, sb), BF16)

    def scores_into(s_ref, q_sb, kv):
        start = pl.multiple_of(kv * sb, sb)
        s_ref[...] = _dot(kfull_ref[pl.ds(start, sb), :], qaug_ref[q_sb])

    def absorb(kv, s_ref, diagonal):
        start = pl.multiple_of(kv * sb, sb)
        s_t = s_ref[...]
        if diagonal:
            s_t = jnp.where(causal, s_t, NEG_BIG)
        m_prev = m_ref[...]
        m_new = jnp.maximum(m_prev, jnp.max(s_t, axis=0, keepdims=True))
        m_ref[...] = m_new
        p_b = jnp.exp2(s_t - m_new).astype(BF16)
        alpha = jnp.exp2(m_prev - m_new)
        for a in range(2):
            cols = slice(a * sb, (a + 1) * sb)
            rows = slice(a * hd, (a + 1) * hd)
            v_aug = jnp.concatenate([vt_ref[rows, pl.ds(start, sb)], ones_rows], axis=0)
            r = _dot(v_aug, p_b[:, cols])
            l_ref[a:a + 1, :] = alpha[:, cols] * l_ref[a:a + 1, :] + r[hd:hd + 1, :]
            acc_ref[rows, :] = alpha[:, cols] * acc_ref[rows, :] + r[0:hd, :]

    def build_q_aug(i, carry):
        qt_sb = qt_ref[:, pl.ds(pl.multiple_of(i * sb, sb), sb)]
        own = 2 * i + jnp.where(q_local >= MOBA_BLOCK, 1, 0)
        valid = blk_row < own
        for a in range(2):
            qa = jnp.where(head_of_row == a, qt_sb, jnp.zeros_like(qt_sb))
            gate = _dot(km_hi, qa) + _dot(km_lo, qa)
            g = jnp.where(valid, gate, -jnp.inf)
            sel = jnp.zeros((nbr, sb), jnp.bool_)
            for _ in range(min(MOBA_TOPK, nb - 1)):
                cand = inrange & jnp.logical_not(sel)
                gm = jnp.max(jnp.where(cand, g, -jnp.inf), axis=0, keepdims=True)
                hit = cand & (g == gm)
                first = jnp.min(jnp.where(hit, blk_row, nb), axis=0, keepdims=True)
                sel = sel | (blk_row == first)
            chosen = (sel & valid) | (blk_row == own)
            flags = jnp.where(chosen, 0.0, NEG_BIG).astype(BF16)
            qaug_ref[i, :, a * sb:(a + 1) * sb] = jnp.concatenate([qa, flags, flag_pad], axis=0)
        return carry

    lax.fori_loop(0, nsb, build_q_aug, 0)
    scores_into(sc_ref, 0, 0)

    def q_super(i, carry):
        m_ref[...] = jnp.full(m_ref.shape, NEG_BIG, F32)
        l_ref[...] = jnp.zeros_like(l_ref)
        acc_ref[...] = jnp.zeros_like(acc_ref)
        nxt = jnp.minimum(i + 1, nsb - 1)

        @pl.when(i == 0)
        def _():
            absorb(0, sc_ref, True)

        @pl.when(i == 0)
        def _():
            scores_into(sc_ref, nxt, 0)

        @pl.when(i > 0)
        def _():
            scores_into(sa_ref, i, 1)
            absorb(0, sc_ref, False)

            def two_steps(t, c):
                scores_into(sb_ref, i, 2 * t + 2)
                absorb(2 * t + 1, sa_ref, False)
                scores_into(sa_ref, i, 2 * t + 3)
                absorb(2 * t + 2, sb_ref, False)
                return c

            lax.fori_loop(0, (i - 1) // 2, two_steps, 0)

            @pl.when(i % 2 == 1)
            def _():
                scores_into(sc_ref, nxt, 0)
                absorb(i, sa_ref, True)

            @pl.when(i % 2 == 0)
            def _():
                scores_into(sb_ref, i, i)
                absorb(i - 1, sa_ref, False)
                scores_into(sc_ref, nxt, 0)
                absorb(i, sb_ref, True)

        out_t = jnp.concatenate([acc_ref[a * hd:(a + 1) * hd, :] / l_ref[a:a + 1, :] for a in range(2)], axis=0)
        o_ref[pl.ds(pl.multiple_of(i * sb, sb), sb), :] = out_t.T.astype(BF16)
        return carry

    lax.fori_loop(0, nsb, q_super, 0)


def _moba(qt, k, vt, batch):
    t_tokens = k.shape[0]
    seq = t_tokens // batch
    assert seq % MOBA_SUPER == 0 and seq // MOBA_BLOCK <= MOBA_BLOCK_ROWS
    spec = pl.BlockSpec((seq, LANES), lambda b, hp: (b, hp))
    spec_t = pl.BlockSpec((LANES, seq), lambda b, hp: (hp, b))
    return pl.pallas_call(
        functools.partial(_moba_kernel, seq=seq),
        grid=(batch, ATT_INNER // LANES),
        in_specs=[spec_t, spec, spec_t],
        out_specs=spec,
        out_shape=jax.ShapeDtypeStruct((t_tokens, ATT_INNER), BF16),
        scratch_shapes=[pltpu.VMEM((seq, 2 * LANES), BF16),
                        pltpu.VMEM((seq // MOBA_SUPER, 2 * LANES, 2 * MOBA_SUPER), BF16),
                        pltpu.VMEM((MOBA_SUPER, 2 * MOBA_SUPER), F32),
                        pltpu.VMEM((MOBA_SUPER, 2 * MOBA_SUPER), F32),
                        pltpu.VMEM((MOBA_SUPER, 2 * MOBA_SUPER), F32),
                        pltpu.VMEM((1, 2 * MOBA_SUPER), F32),
                        pltpu.VMEM((SUBLANES, MOBA_SUPER), F32),
                        pltpu.VMEM((LANES, MOBA_SUPER), F32)],
        compiler_params=pltpu.CompilerParams(dimension_semantics=("arbitrary", "arbitrary"),
                                             vmem_limit_bytes=VMEM_LIMIT),
        name="moba",
    )(qt, k, vt)


ROUTER_ROWS = 40


def _merge_kernel(x_ref, yn_ref, att_ref, gs_ref, ga_ref, wso_ref, wao_ref, wo_ref, nw_ref,
                  wr_hi_ref, wr_lo_ref, br_ref, upper_ref, x1_ref, h2_ref, route_ref, jloc_ref, cnt_ref):
    y_s = _dot(yn_ref[...], wso_ref[...])
    y_a = _dot(att_ref[...], wao_ref[...])
    merged = _sigmoid(gs_ref[...].astype(F32)) * y_s + _sigmoid(ga_ref[...].astype(F32)) * y_a
    x1 = x_ref[...] + _dot(merged.astype(BF16), wo_ref[...])
    x1_ref[...] = x1
    h2 = _rms(x1, nw_ref[...])
    h2_ref[...] = h2.astype(BF16)

    h_hi = h2.astype(BF16)
    h_lo = (h2 - h_hi.astype(F32)).astype(BF16)
    wr_hi = wr_hi_ref[...]
    logits = _dot_nt(wr_hi, h_hi) + _dot_nt(wr_hi, h_lo) + _dot_nt(wr_lo_ref[...], h_hi) + br_ref[...]
    gl = logits[0:N_GROUPS, :]
    gidx = lax.broadcasted_iota(jnp.int32, gl.shape, 0)
    gmax = jnp.max(gl, axis=0, keepdims=True)
    gi = jnp.min(jnp.where(gl == gmax, gidx, N_GROUPS), axis=0, keepdims=True)
    g_w = 1.0 / jnp.sum(jnp.exp(gl - gmax), axis=0, keepdims=True)
    el = jnp.zeros((EXPERTS_PER_GROUP, gl.shape[1]), F32)
    for g in range(N_GROUPS):
        lo = N_GROUPS + g * EXPERTS_PER_GROUP
        el = el + jnp.where(gi == g, logits[lo:lo + EXPERTS_PER_GROUP, :], 0.0)
    eidx = lax.broadcasted_iota(jnp.int32, el.shape, 0)
    v0 = jnp.max(el, axis=0, keepdims=True)
    i0 = jnp.min(jnp.where(el == v0, eidx, EXPERTS_PER_GROUP), axis=0, keepdims=True)
    rest = jnp.where(eidx == i0, -jnp.inf, el)
    v1 = jnp.max(rest, axis=0, keepdims=True)
    i1 = jnp.min(jnp.where((rest == v1) & (eidx != i0), eidx, EXPERTS_PER_GROUP), axis=0, keepdims=True)
    t = jnp.exp(v1 - v0)
    w0 = g_w / (1.0 + t)
    w1 = g_w * t / (1.0 + t)
    base = gi * EXPERTS_PER_GROUP
    route_ref[...] = jnp.concatenate(
        [(base + i0).astype(F32), (base + i1).astype(F32), w0, w1], axis=0)

    tm = gl.shape[1]
    eids = jnp.concatenate([base + i0, base + i1], axis=1)
    onehot = lax.broadcasted_iota(jnp.int32, (N_EXPERTS, 2 * tm), 0) == eids
    csum = _dot(onehot.astype(BF16), upper_ref[...])
    oh = onehot.astype(F32)
    lrank = jnp.sum(csum * oh, axis=0, keepdims=True) - 1.0
    cnt = jnp.floor((csum[:, 2 * tm - 1:2 * tm] + (RUN_ALIGN - 1)) * (1.0 / RUN_ALIGN))
    er = lax.broadcasted_iota(jnp.int32, (N_EXPERTS, N_EXPERTS), 0)
    ec = lax.broadcasted_iota(jnp.int32, (N_EXPERTS, N_EXPERTS), 1)
    before = _dot((ec < er).astype(BF16), jnp.broadcast_to(cnt, (N_EXPERTS, LANES)).astype(BF16))
    off = before[:, 0:1] * RUN_ALIGN
    jloc = jnp.sum(oh * off, axis=0, keepdims=True) + lrank
    jloc_ref[...] = jnp.concatenate([jloc[:, :tm], jloc[:, tm:]], axis=0).astype(jnp.int32)
    cnt_ref[...] = jnp.broadcast_to(cnt * RUN_ALIGN, (N_EXPERTS, LANES))


def _merge(x2d, yn, att, gs, ga, wso, wao, wo, nw, wr_hi, wr_lo, br):
    t_tokens = x2d.shape[0]
    tm = TM_TOK
    upper = jnp.triu(jnp.ones((TILE_ASG, TILE_ASG), BF16))
    row = lambda: pl.BlockSpec((tm, D_MODEL), lambda i: (i, 0))
    whole = pl.BlockSpec(memory_space=pltpu.VMEM)
    return pl.pallas_call(
        _merge_kernel,
        grid=(t_tokens // tm,),
        in_specs=[row(), row(), row(), row(), row(), whole, whole, whole, whole, whole, whole, whole, whole],
        out_specs=[row(), row(), pl.BlockSpec((4, tm), lambda i: (0, i)), pl.BlockSpec((2, tm), lambda i: (0, i)),
                   pl.BlockSpec((N_EXPERTS, LANES), lambda i: (i, 0))],
        out_shape=[jax.ShapeDtypeStruct((t_tokens, D_MODEL), F32),
                   jax.ShapeDtypeStruct((t_tokens, D_MODEL), BF16),
                   jax.ShapeDtypeStruct((4, t_tokens), F32),
                   jax.ShapeDtypeStruct((2, t_tokens), jnp.int32),
                   jax.ShapeDtypeStruct((t_tokens // tm * N_EXPERTS, LANES), F32)],
        compiler_params=pltpu.CompilerParams(dimension_semantics=("arbitrary",), vmem_limit_bytes=VMEM_LIMIT),
        name="merge",
    )(x2d, yn, att, gs, ga, wso, wao, wo, nw, wr_hi, wr_lo, br, upper)


def _rows(ref, first, n_rows):
    start = first if isinstance(first, int) else pl.multiple_of(first, RUN_ALIGN)
    return ref.at[pl.ds(start, n_rows), :]


def _for_each_piece(count, fn, bits):
    for bit in range(bits - 1, ALIGN_BIT - 1, -1):
        size = 1 << bit

        @pl.when((count & size) != 0)
        def _():
            fn((count >> (bit + 1)) << (bit + 1), size)


def _run_copies(glob_ref, tot_ref, tile, unit_copy):
    def start(v, priority):
        unit_copy(v * RUN_ALIGN, glob_ref[tile * UNITS + v]).start(priority=priority)

    n_units = tot_ref[tile] >> ALIGN_BIT

    def pair(u, carry):
        start(2 * u, 0)
        start(2 * u + 1, 1)
        return carry
    lax.fori_loop(0, n_units >> 1, pair, 0)

    @pl.when((n_units & 1) == 1)
    def _():
        start(n_units - 1, 0)


def _wait_run_copies(tot_ref, tile, src, dst, sem):
    pltpu.make_async_copy(_rows(src, 0, TILE_ASG), _rows(dst, 0, TILE_ASG), sem).wait()

    def piece(o, size):
        pltpu.make_async_copy(_rows(src, 0, size), _rows(dst, 0, size), sem).wait()
    _for_each_piece(tot_ref[tile] - TILE_ASG, piece, EXTRA_BITS)


def _moe_max_tiles(t_tokens):
    rows = 2 * t_tokens + (t_tokens // TM_TOK) * N_EXPERTS * (RUN_ALIGN - 1)
    return -(-rows // TM_MOE) + N_EXPERTS


def _moe_plan(cnt_rows, max_tiles):
    cnt = cnt_rows[:, 0].astype(jnp.int32).reshape(-1, N_EXPERTS)
    off_end = jnp.cumsum(cnt, axis=1)
    counts = jnp.sum(cnt, axis=0)
    padded = ((counts + TM_MOE - 1) // TM_MOE) * TM_MOE
    ends = jnp.cumsum(padded)
    starts = ends - padded
    base = starts[None, :] + jnp.cumsum(cnt, axis=0) - cnt
    loc = jnp.arange(UNITS, dtype=jnp.int32) * RUN_ALIGN
    run = jnp.sum((loc[None, :, None] >= off_end[:, None, :]).astype(jnp.int32), axis=2)
    run = jnp.minimum(run, N_EXPERTS - 1)
    pick = (run[:, :, None] == jnp.arange(N_EXPERTS, dtype=jnp.int32)[None, None, :]).astype(jnp.int32)
    shift = jnp.sum(pick * (base - (off_end - cnt))[:, None, :], axis=2)
    glob = loc[None, :] + shift
    tile_start = jnp.arange(max_tiles, dtype=jnp.int32) * TM_MOE
    tile_expert = jnp.sum((tile_start[:, None] >= ends[None, :]).astype(jnp.int32), axis=1)
    tile_expert = jnp.minimum(tile_expert, N_EXPERTS - 1)
    n_tiles = (ends[-1] // TM_MOE).astype(jnp.int32).reshape(1)
    flat = lambda a: a.reshape(-1).astype(jnp.int32)
    return flat(glob), flat(off_end[:, -1]), starts + counts, ends, tile_expert, n_tiles


def _dispatch_kernel(glob_ref, tot_ref, padlo_ref, padhi_ref, ntile_ref, h2_ref, jrow_ref, xs_hbm,
                     xbuf0, xbuf1, zeros_ref, sems, zsem, *, max_tiles):
    i = pl.program_id(0)
    n = pl.num_programs(0)
    xbufs = (xbuf0, xbuf1)

    def wait_runs(tile, slot):
        _wait_run_copies(tot_ref, tile, xbufs[slot], xs_hbm, sems.at[slot])

    @pl.when(i == 0)
    def _():
        zeros_ref[...] = jnp.zeros_like(zeros_ref)
        nt = ntile_ref[0]
        for wait in (False, True):
            def pad(e, carry):
                lo = padlo_ref[e]

                def piece(o, size):
                    cp = pltpu.make_async_copy(_rows(zeros_ref, 0, size), _rows(xs_hbm, lo + o, size), zsem)
                    cp.wait() if wait else cp.start()
                _for_each_piece(padhi_ref[e] - lo, piece, PAD_BITS)
                return carry
            lax.fori_loop(0, N_EXPERTS, pad, 0)

            def spare(tile, carry):
                cp = pltpu.make_async_copy(zeros_ref, _rows(xs_hbm, tile * TM_MOE, TM_MOE), zsem)
                cp.wait() if wait else cp.start()
                return carry
            lax.fori_loop(nt, max_tiles, spare, 0)

    for slot in range(2):
        @pl.when(i % 2 == slot)
        def _():
            @pl.when(i >= 2)
            def _():
                wait_runs(i - 2, slot)
            row = lax.broadcasted_iota(jnp.int32, (LOC_ROWS, TM_TOK), 0)
            j = jrow_ref[...]
            perm = ((row == j[0:1, :]) | (row == j[1:2, :])).astype(BF16)
            xbufs[slot][...] = _dot(perm, h2_ref[...]).astype(BF16)
            _run_copies(glob_ref, tot_ref, i, lambda loc, g: pltpu.make_async_copy(
                _rows(xbufs[slot], loc, RUN_ALIGN), _rows(xs_hbm, g, RUN_ALIGN), sems.at[slot]))

            @pl.when(i == n - 1)
            def _():
                @pl.when(i >= 1)
                def _():
                    wait_runs(i - 1, 1 - slot)
                wait_runs(i, slot)


def _dispatch(plan, jloc, h2, max_tiles):
    glob, tot, pad_lo, pad_hi, _, n_tiles = plan
    t_tokens = h2.shape[0]
    grid_spec = pltpu.PrefetchScalarGridSpec(
        num_scalar_prefetch=5,
        grid=(t_tokens // TM_TOK,),
        in_specs=[pl.BlockSpec((TM_TOK, D_MODEL), lambda i, *_: (i, 0)),
                  pl.BlockSpec((2, TM_TOK), lambda i, *_: (0, i))],
        out_specs=pl.BlockSpec(memory_space=pl.ANY),
        scratch_shapes=[pltpu.VMEM((LOC_ROWS, D_MODEL), BF16),
                        pltpu.VMEM((LOC_ROWS, D_MODEL), BF16),
                        pltpu.VMEM((TM_MOE, D_MODEL), BF16),
                        pltpu.SemaphoreType.DMA((2,)), pltpu.SemaphoreType.DMA(())],
    )
    return pl.pallas_call(
        functools.partial(_dispatch_kernel, max_tiles=max_tiles),
        grid_spec=grid_spec,
        out_shape=jax.ShapeDtypeStruct((max_tiles * TM_MOE, D_MODEL), BF16),
        compiler_params=pltpu.CompilerParams(dimension_semantics=("arbitrary",), vmem_limit_bytes=VMEM_LIMIT),
        name="dispatch",
    )(glob, tot, pad_lo, pad_hi, n_tiles, h2, jloc)


def _moe_kernel(texp_ref, ntile_ref, xs_ref, wg_ref, wu_ref, wd_ref, y_ref, wgu_s, wd_s):
    t = pl.program_id(0)
    live = t < ntile_ref[0]

    @pl.when(live & ((t == 0) | (texp_ref[t] != texp_ref[jnp.maximum(t - 1, 0)])))
    def _():
        wgu_s[:, 0:D_EXPERT] = wg_ref[0].astype(BF16)
        wgu_s[:, D_EXPERT:2 * D_EXPERT] = wu_ref[0].astype(BF16)
        wd_s[...] = wd_ref[0].astype(BF16)

    @pl.when(live)
    def _():
        gu = _dot(xs_ref[...], wgu_s[...])
        hid = (_silu(gu[:, :D_EXPERT]) * gu[:, D_EXPERT:]).astype(BF16)
        y_ref[...] = _dot(hid, wd_s[...]).astype(BF16)

    @pl.when(jnp.logical_not(live))
    def _():
        y_ref[...] = jnp.zeros_like(y_ref)


def _moe(xs, tile_expert, n_tiles, w_gate, w_up, w_down, max_tiles):
    live = lambda t, nt: jnp.minimum(t, nt[0] - 1)
    tile = lambda index: pl.BlockSpec((TM_MOE, D_MODEL), index)
    expert = lambda shape: pl.BlockSpec((1,) + shape, lambda t, te, nt: (te[live(t, nt)], 0, 0))
    grid_spec = pltpu.PrefetchScalarGridSpec(
        num_scalar_prefetch=2,
        grid=(max_tiles,),
        in_specs=[tile(lambda t, te, nt: (live(t, nt), 0)),
                  expert((D_MODEL, D_EXPERT)), expert((D_MODEL, D_EXPERT)), expert((D_EXPERT, D_MODEL))],
        out_specs=tile(lambda t, te, nt: (t, 0)),
        scratch_shapes=[pltpu.VMEM((D_MODEL, 2 * D_EXPERT), BF16), pltpu.VMEM((D_EXPERT, D_MODEL), BF16)],
    )
    return pl.pallas_call(
        _moe_kernel,
        grid_spec=grid_spec,
        out_shape=jax.ShapeDtypeStruct((max_tiles * TM_MOE, D_MODEL), BF16),
        compiler_params=pltpu.CompilerParams(dimension_semantics=("arbitrary",), vmem_limit_bytes=VMEM_LIMIT),
        name="moe",
    )(tile_expert, n_tiles, xs, w_gate, w_up, w_down)


def _tail_kernel(glob_ref, tot_ref, x1_ref, p_ref, jw_ref, y_hbm, pnw_ref, wpg_ref, wple_ref,
                 fnw_ref, o_ref, ybuf0, ybuf1, sems):
    i = pl.program_id(0)
    n = pl.num_programs(0)
    ybufs = (ybuf0, ybuf1)

    def fetch_runs(tile, slot):
        _run_copies(glob_ref, tot_ref, tile, lambda loc, g: pltpu.make_async_copy(
            _rows(y_hbm, g, RUN_ALIGN), _rows(ybufs[slot], loc, RUN_ALIGN), sems.at[slot]))

    @pl.when(i == 0)
    def _():
        ybuf0[...] = jnp.zeros_like(ybuf0)
        ybuf1[...] = jnp.zeros_like(ybuf1)
        fetch_runs(0, 0)

    for slot in range(2):
        @pl.when(i % 2 == slot)
        def _():
            @pl.when(i + 1 < n)
            def _():
                fetch_runs(i + 1, 1 - slot)
            _wait_run_copies(tot_ref, i, y_hbm, ybufs[slot], sems.at[slot])
            y_loc = ybufs[slot][...]
            jw = jw_ref[...]
            col = lax.broadcasted_iota(jnp.int32, (TM_TOK, LOC_ROWS), 1)
            pick = sum(jnp.where(col == jw[:, s:s + 1].astype(jnp.int32), jw[:, 2 + s:3 + s], 0.0) for s in range(2))
            moe = _dot(pick.astype(BF16), y_loc)
            x2 = x1_ref[...] + moe
            gate = _sigmoid(_dot(_rms(x2, pnw_ref[...]).astype(BF16), wpg_ref[...]))
            x3 = x2 + gate * _dot(p_ref[...].astype(BF16), wple_ref[...])
            o_ref[...] = _rms(x3, fnw_ref[...])


def _tail(plan, x1, p2d, y_sorted, jw, pnw, wpg, wple, fnw):
    glob, tot = plan[0:2]
    t_tokens = x1.shape[0]
    whole = pl.BlockSpec(memory_space=pltpu.VMEM)
    grid_spec = pltpu.PrefetchScalarGridSpec(
        num_scalar_prefetch=2,
        grid=(t_tokens // TM_TOK,),
        in_specs=[pl.BlockSpec((TM_TOK, D_MODEL), lambda i, *_: (i, 0)),
                  pl.BlockSpec((TM_TOK, PLE_DIM), lambda i, *_: (i, 0)),
                  pl.BlockSpec((TM_TOK, 4), lambda i, *_: (i, 0)),
                  pl.BlockSpec(memory_space=pl.ANY),
                  whole, whole, whole, whole],
        out_specs=pl.BlockSpec((TM_TOK, D_MODEL), lambda i, *_: (i, 0)),
        scratch_shapes=[pltpu.VMEM((LOC_ROWS, D_MODEL), BF16),
                        pltpu.VMEM((LOC_ROWS, D_MODEL), BF16),
                        pltpu.SemaphoreType.DMA((2,))],
    )
    return pl.pallas_call(
        _tail_kernel,
        grid_spec=grid_spec,
        out_shape=jax.ShapeDtypeStruct((t_tokens, D_MODEL), F32),
        compiler_params=pltpu.CompilerParams(dimension_semantics=("arbitrary",), vmem_limit_bytes=VMEM_LIMIT),
        name="tail",
    )(glob, tot, x1, p2d, jw, y_sorted, pnw, wpg, wple, fnw)


def _rope_tables(positions):
    half = ATT_HEAD_DIM // 2
    inv_freq = ROPE_THETA ** (-jnp.arange(half, dtype=F32) / half)
    ang = positions.astype(F32)[..., None] * inv_freq
    return jnp.cos(ang).reshape(-1, half).T, jnp.sin(ang).reshape(-1, half).T


def _layer(x2d, p2d, rope_tables, batch, attn_norm_w, w_in, conv_w, conv_b, dt_bias, a_log, d_skip, ssd_norm_w,
           w_ssd_out, w_attn_out, w_out, moe_norm_w, w_rg, b_rg, w_re, b_re, w_gate, w_up, w_down,
           ple_norm_w, w_ple, w_ple_gate, final_norm_w):
    t_tokens = x2d.shape[0]
    offs = np.cumsum((0,) + IN_SPLITS)
    cols = lambda i: w_in[:, offs[i]:offs[i + 1]]
    w_big = jnp.concatenate([cols(0), cols(1), cols(4), cols(6), cols(7)], axis=1).astype(BF16)
    w_t = jnp.concatenate([cols(3), cols(5), cols(2)], axis=1).T.astype(BF16)
    w_dt = jnp.pad(cols(2), ((0, 0), (0, LANES - SSD_HEADS))).astype(BF16)

    z, xbc, k, gs, ga, dt, qt, vt, dtt = _in_proj(x2d, attn_norm_w[None, :], *rope_tables, w_big, w_t, w_dt)

    yn = _ssd(z, xbc, dt, dtt, conv_w, conv_b, dt_bias, a_log, d_skip, ssd_norm_w, batch)

    att = _moba(qt, k, vt, batch)

    wr = jnp.concatenate([w_rg, w_re], axis=1).T
    wr = jnp.pad(wr, ((0, ROUTER_ROWS - wr.shape[0]), (0, 0)))
    wr_hi = wr.astype(BF16)
    wr_lo = (wr - wr_hi.astype(F32)).astype(BF16)
    br = jnp.pad(jnp.concatenate([b_rg, b_re]), (0, ROUTER_ROWS - N_GROUPS - N_EXPERTS))[:, None]
    x1, h2, route, jloc, cnt_rows = _merge(x2d, yn, att, gs, ga, w_ssd_out.astype(BF16), w_attn_out.astype(BF16),
                           w_out.astype(BF16), moe_norm_w[None, :], wr_hi, wr_lo, br)

    max_tiles = _moe_max_tiles(t_tokens)
    plan = _moe_plan(cnt_rows, max_tiles)
    x_sorted = _dispatch(plan, jloc, h2, max_tiles)
    y_sorted = _moe(x_sorted, plan[4], plan[5], w_gate.reshape(N_EXPERTS, D_MODEL, D_EXPERT),
                    w_up.reshape(N_EXPERTS, D_MODEL, D_EXPERT), w_down.reshape(N_EXPERTS, D_EXPERT, D_MODEL), max_tiles)
    jw = jnp.concatenate([jloc.astype(F32), route[2:4]], axis=0).T

    return _tail(plan, x1, p2d, y_sorted, jw, ple_norm_w[None, :],
                 w_ple_gate.astype(BF16), w_ple.astype(BF16), final_norm_w[None, :])


def kernel(x, p, positions, attn_norm_w, w_in, conv_w, conv_b, dt_bias, a_log, d_skip, ssd_norm_w, w_ssd_out,
           w_attn_out, w_out, moe_norm_w, w_router_group, b_router_group, w_router_expert, b_router_expert,
           w_exp_gate, w_exp_up, w_exp_down, ple_norm_w, w_ple, w_ple_gate, final_norm_w):
    batch, seq, d = x.shape
    depth = p.shape[0]
    assert depth == 1, "the final norm is fused into the last layer's tail kernel"
    rope_tables = _rope_tables(positions)
    i = 0
    out = _layer(x.reshape(batch * seq, d), p[i].reshape(batch * seq, PLE_DIM), rope_tables, batch,
                 attn_norm_w[i], w_in[i], conv_w[i], conv_b[i], dt_bias[i], a_log[i], d_skip[i], ssd_norm_w[i],
                 w_ssd_out[i], w_attn_out[i], w_out[i], moe_norm_w[i], w_router_group[i], b_router_group[i],
                 w_router_expert[i], b_router_expert[i], w_exp_gate[i], w_exp_up[i], w_exp_down[i],
                 ple_norm_w[i], w_ple[i], w_ple_gate[i], final_norm_w)
    return out.reshape(batch, seq, d)
```
